```python
import jax
import jax.numpy as jnp
from jax import lax
import numpy as np

D_MODEL = 1024
BATCH = 8
SEQ = 2048
DEPTH = 1

CTX_LEN = 256
GRID_W = 64
FOURIER_W = 512
FOURIER_GROUPS = 8
RWKV_W = D_MODEL - FOURIER_W
HEAD = 64
N_HEADS = RWKV_W // HEAD
N_DIR = 2
DECAY_RANK = 64
ICLR_RANK = 64
GATE_RANK = 128
D_FF = 2816
RWKV_PROJ_W = 3 * RWKV_W + N_DIR * DECAY_RANK + N_DIR * ICLR_RANK + GATE_RANK
PROJ_W = FOURIER_W + RWKV_PROJ_W
NORM_EPS = 1e-6
GN_EPS = 64e-5
KK_EPS = 1e-12

kernel_name = "hybrid_fourier_rwkv7_convffn_prefix"


def rms_norm(x, g):
    xf = x.astype(jnp.float32)
    y = xf * lax.rsqrt(jnp.mean(xf * xf, axis=-1, keepdims=True) + NORM_EPS)
    return (y * g.astype(jnp.float32)).astype(x.dtype)


def modulate(h, shift, scale):
    return h * (1 + scale) + shift


def conv1d_centred(u, w):
    up = jnp.pad(u, ((0, 0), (1, 1), (0, 0)))
    return up[:, :-2] * w[0] + up[:, 1:-1] * w[1] + up[:, 2:] * w[2]


def conv2d_grid(u, w, rows):
    b, t, ch = u.shape
    g = u.reshape(b, rows, GRID_W, ch)
    y = lax.conv_general_dilated(g, w[:, :, None, :].astype(u.dtype), (1, 1), "SAME",
                                 dimension_numbers=("NHWC", "HWIO", "NHWC"),
                                 feature_group_count=ch)
    return y.reshape(b, t, ch)


def fourier_mix(u, g):
    b, t, _ = u.shape
    uf = u.astype(jnp.float32).reshape(b, t, FOURIER_GROUPS, FOURIER_W // FOURIER_GROUPS)
    y = jnp.fft.fft2(uf, axes=(1, 3), norm="ortho").real
    return rms_norm(y.reshape(b, t, FOURIER_W), g).astype(u.dtype)


def rwkv_inputs(u, conv_w, w0, w2, a0, a2, k_k, k_a):
    b, t, _ = u.shape
    rkv = conv1d_centred(u[..., :3 * RWKV_W], conv_w)
    r, k, v = jnp.split(rkv, 3, axis=-1)
    o = 3 * RWKV_W
    wd = u[..., o:o + N_DIR * DECAY_RANK].reshape(b, t, N_DIR, DECAY_RANK)
    o += N_DIR * DECAY_RANK
    ad = u[..., o:o + N_DIR * ICLR_RANK].reshape(b, t, N_DIR, ICLR_RANK)
    o += N_DIR * ICLR_RANK
    gd = u[..., o:]
    w_lora = (w0 + jnp.einsum("btdr,drc->btdc", jnp.tanh(wd), w2)).astype(jnp.float32)
    decay = jnp.exp(-jnp.exp(-jax.nn.softplus(-w_lora) - 0.5))
    a = jax.nn.sigmoid((a0 + jnp.einsum("btdr,drc->btdc", ad, a2)).astype(jnp.float32))
    kk = (k * k_k).astype(jnp.float32).reshape(b, t, N_HEADS, HEAD)
    kk = kk * lax.rsqrt(jnp.sum(kk * kk, axis=-1, keepdims=True) + KK_EPS)
    kk = kk.reshape(b, t, RWKV_W)
    kd = k.astype(jnp.float32)[:, :, None] * (1 + (a - 1) * k_a)
    return r, kd, v, kk, a, decay, gd


def _heads_time_major(x2):
    d, b, t, _ = x2.shape
    return x2.astype(jnp.float32).reshape(d, b, t, N_HEADS, HEAD).transpose(2, 0, 1, 3, 4)


def shared_dirs(x):
    return _heads_time_major(jnp.stack([x, jnp.flip(x, axis=1)], axis=0))


def per_dir(x):
    return _heads_time_major(jnp.stack([x[:, :, 0], jnp.flip(x[:, :, 1], axis=1)], axis=0))


def merge_dirs(y):
    y = y.transpose(1, 2, 0, 3, 4)
    return y[0] + jnp.flip(y[1], axis=1)


def rwkv_scan(prep, state0, readout):
    r, kd, v, kk, a, decay, _ = prep
    xs = (shared_dirs(r), per_dir(kd), shared_dirs(v), shared_dirs(kk), per_dir(a), per_dir(decay))

    def step(s, inp):
        r_t, k_t, v_t, kk_t, a_t, w_t = inp
        sa = jnp.einsum("dbhij,dbhj->dbhi", s, kk_t)
        s = (s * w_t[..., None, :] - sa[..., :, None] * (kk_t * a_t)[..., None, :]
             + v_t[..., :, None] * k_t[..., None, :])
        y = jnp.einsum("dbhij,dbhj->dbhi", s, r_t) if readout else None
        return s, y

    return lax.scan(step, state0, xs)


def rwkv_output(y, prep, g2, r_k, gn_g, gn_b):
    r, kd, v, _, _, _, gd = prep
    b, t = r.shape[:2]
    mu = jnp.mean(y, axis=-1, keepdims=True)
    var = jnp.mean(jnp.square(y - mu), axis=-1, keepdims=True)
    yn = ((y - mu) * lax.rsqrt(var + GN_EPS)).reshape(b, t, RWKV_W)
    yn = yn * gn_g.astype(jnp.float32) + gn_b.astype(jnp.float32)
    rh = r.astype(jnp.float32).reshape(b, t, N_HEADS, HEAD)
    vh = v.astype(jnp.float32).reshape(b, t, N_HEADS, HEAD)
    kdh = kd.reshape(b, t, N_DIR, N_HEADS, HEAD)
    bonus = jnp.einsum("bthn,btdhn,hn->bth", rh, kdh, r_k.astype(jnp.float32))[..., None] * vh
    gate = jax.nn.sigmoid(gd.astype(jnp.float32)) @ g2.astype(jnp.float32)
    return (yn + bonus.reshape(b, t, RWKV_W)) * gate


def token_mixer(h, hc, w_in, conv_w, w0, w2, a0, a2, g2, k_k, k_a, r_k, gn_g, gn_b,
                f_g, w_out, ctx_out):
    u = h @ w_in
    uc = hc @ (w_in if ctx_out else w_in[:, FOURIER_W:])
    uc_r = uc[..., FOURIER_W:] if ctx_out else uc
    prep_l = rwkv_inputs(u[..., FOURIER_W:], conv_w, w0, w2, a0, a2, k_k, k_a)
    prep_c = rwkv_inputs(uc_r, conv_w, w0, w2, a0, a2, k_k, k_a)
    state0 = jnp.zeros((N_DIR, hc.shape[0], N_HEADS, HEAD, HEAD), jnp.float32)
    state_c, y_c = rwkv_scan(prep_c, state0, ctx_out)
    _, y_l = rwkv_scan(prep_l, state_c, True)
    rw_l = rwkv_output(merge_dirs(y_l), prep_l, g2, r_k, gn_g, gn_b).astype(h.dtype)
    out = jnp.concatenate([fourier_mix(u[..., :FOURIER_W], f_g).astype(h.dtype), rw_l], axis=-1) @ w_out
    out_c = None
    if ctx_out:
        rw_c = rwkv_output(merge_dirs(y_c), prep_c, g2, r_k, gn_g, gn_b).astype(hc.dtype)
        out_c = jnp.concatenate([fourier_mix(uc[..., :FOURIER_W], f_g).astype(hc.dtype), rw_c], axis=-1) @ w_out
    return out, out_c


def conv_ffn(h, w_up, conv_w, conv_b, w_down, rows):
    u = h @ w_up
    if rows is None:
        u = conv1d_centred(u, conv_w[1]) + conv_b
    else:
        u = conv2d_grid(u, conv_w, rows) + conv_b
    gate, val = jnp.split(u, 2, axis=-1)
    return (jax.nn.silu(gate) * val) @ w_down


def setup_inputs(seed: int = 0) -> dict:
    key = jax.random.key(seed)
    ks = jax.random.split(key, 27)
    L, D, C, F2 = DEPTH, D_MODEL, RWKV_W, 2 * D_FF

    def nrm(i, shape, s=1.0):
        return s * jax.random.normal(ks[i], shape, jnp.float32)

    def near_one(i, shape):
        return 1.0 + nrm(i, shape, 0.05)

    centre3 = jnp.zeros((3,), jnp.float32).at[1].set(1.0)
    centre33 = jnp.zeros((3, 3), jnp.float32).at[1, 1].set(1.0)
    return {
        "x": nrm(0, (BATCH, SEQ, D)),
        "c": nrm(1, (BATCH, D)),
        "ctx": nrm(2, (BATCH, CTX_LEN, D)),
        "c_ctx": nrm(3, (D,)),
        "ada_w": nrm(4, (L, D, 6 * D), 0.5 * D ** -0.5),
        "ada_b": nrm(5, (L, 6 * D), 0.02),
        "norm1_g": near_one(6, (L, D)),
        "norm2_g": near_one(7, (L, D)),
        "w_in": nrm(8, (L, D, PROJ_W), D ** -0.5),
        "rwkv_conv_w": centre3[None, :, None] + nrm(9, (L, 3, 3 * C), 0.1),
        "decay_w0": -1.0 + nrm(10, (L, N_DIR, C), 1.5),
        "decay_w2": nrm(11, (L, N_DIR, DECAY_RANK, C), 0.1 * DECAY_RANK ** -0.5),
        "iclr_a0": nrm(12, (L, N_DIR, C), 0.5),
        "iclr_a2": nrm(13, (L, N_DIR, ICLR_RANK, C), 0.1 * ICLR_RANK ** -0.5),
        "gate_g2": nrm(14, (L, GATE_RANK, C), GATE_RANK ** -0.5),
        "k_k": 0.85 + nrm(15, (L, C), 0.05),
        "k_a": near_one(16, (L, C)),
        "r_k": nrm(17, (L, N_HEADS, HEAD), 0.1),
        "gn_g": near_one(18, (L, C)),
        "gn_b": nrm(19, (L, C), 0.02),
        "fourier_g": near_one(20, (L, FOURIER_W)),
        "w_out": nrm(21, (L, D, D), D ** -0.5),
        "ffn_w_up": nrm(22, (L, D, F2), D ** -0.5),
        "ffn_conv_w": centre33[None, :, :, None] + nrm(23, (L, 3, 3, F2), 0.1),
        "ffn_conv_b": nrm(24, (L, F2), 0.02),
        "ffn_w_down": nrm(25, (L, D_FF, D), D_FF ** -0.5),
        "final_g": near_one(26, (D,)),
    }


def reference(x, c, ctx, c_ctx, ada_w, ada_b, norm1_g, norm2_g, w_in, rwkv_conv_w,
              decay_w0, decay_w2, iclr_a0, iclr_a2, gate_g2, k_k, k_a, r_k, gn_g, gn_b,
              fourier_g, w_out, ffn_w_up, ffn_conv_w, ffn_conv_b, ffn_w_down, final_g):
    rows = x.shape[1] // GRID_W
    for l in range(DEPTH):
        last = l == DEPTH - 1
        mod_x = jax.nn.silu(c) @ ada_w[l] + ada_b[l]
        mod_c = jax.nn.silu(c_ctx) @ ada_w[l] + ada_b[l]
        sh1, sc1, g1, sh2, sc2, g2 = [m[:, None, :] for m in jnp.split(mod_x, 6, axis=-1)]
        sh1c, sc1c, g1c, sh2c, sc2c, g2c = jnp.split(mod_c, 6, axis=-1)

        h = modulate(rms_norm(x, norm1_g[l]), sh1, sc1)
        hc = modulate(rms_norm(ctx, norm1_g[l]), sh1c, sc1c)
        y, yc = token_mixer(h, hc, w_in[l], rwkv_conv_w[l], decay_w0[l], decay_w2[l],
                            iclr_a0[l], iclr_a2[l], gate_g2[l], k_k[l], k_a[l], r_k[l],
                            gn_g[l], gn_b[l], fourier_g[l], w_out[l], not last)
        x = x + g1 * y
        h = modulate(rms_norm(x, norm2_g[l]), sh2, sc2)
        x = x + g2 * conv_ffn(h, ffn_w_up[l], ffn_conv_w[l], ffn_conv_b[l], ffn_w_down[l], rows)
        if not last:
            ctx = ctx + g1c * yc
            hc = modulate(rms_norm(ctx, norm2_g[l]), sh2c, sc2c)
            ctx = ctx + g2c * conv_ffn(hc, ffn_w_up[l], ffn_conv_w[l], ffn_conv_b[l],
                                       ffn_w_down[l], None)
    return rms_norm(x, final_g)
```

```python
import functools
import math

import numpy as np
import jax
import jax.numpy as jnp
from jax import lax
from jax.experimental import pallas as pl
from jax.experimental.pallas import tpu as pltpu

F32 = jnp.float32
BF16 = jnp.bfloat16

D_MODEL = 1024
CTX_LEN = 256
GRID_W = 64
FOURIER_W = 512
FOURIER_GROUPS = 8
RWKV_W = 512
HEAD = 64
N_HEADS = 8
DECAY_RANK = 64
ICLR_RANK = 64
GATE_RANK = 128
D_FF = 2816
NORM_EPS = 1e-6
GN_EPS = 64e-5
KK_EPS = 1e-12

CHUNK = 64
TILE = 256
CHUNKS_PER_TILE = TILE // CHUNK
PAIR = 2 * HEAD
N_PAIRS = RWKV_W // PAIR
FF_TILE = 256
VMEM_LIMIT = 56 * 1024 * 1024

RKV_LO = FOURIER_W
RKV_HI = FOURIER_W + 3 * RWKV_W
WD_LO = RKV_HI
AD_LO = WD_LO + 2 * DECAY_RANK
GD_LO = AD_LO + 2 * ICLR_RANK
PROJ_W = GD_LO + GATE_RANK


def _bdot(a, b):
    return jnp.dot(a.astype(BF16), b.astype(BF16), preferred_element_type=F32)


def _bdot_nt(a, b):
    return lax.dot_general(a.astype(BF16), b.astype(BF16), (((1,), (1,)), ((), ())),
                           preferred_element_type=F32)


def _bdot_tn(a, b):
    return lax.dot_general(a.astype(BF16), b.astype(BF16), (((0,), (0,)), ((), ())),
                           preferred_element_type=F32)


def _split_terms(a, terms):
    out, rem = [], a
    for _ in range(terms):
        hi = rem.astype(BF16)
        out.append(hi)
        rem = rem - hi.astype(F32)
    return out


def _dot_exact_rhs(a, b_exact, terms=2):
    acc = None
    for piece in _split_terms(a, terms):
        t = jnp.dot(piece, b_exact, preferred_element_type=F32)
        acc = t if acc is None else acc + t
    return acc


def _dot_exact_lhs(a_exact, b, terms=3):
    acc = None
    for piece in _split_terms(b, terms):
        t = jnp.dot(a_exact, piece, preferred_element_type=F32)
        acc = t if acc is None else acc + t
    return acc


def _sigmoid(x):
    return 1.0 / (1.0 + jnp.exp(-x))


def _rms(x, g):
    return x * lax.rsqrt(jnp.mean(x * x, axis=-1, keepdims=True) + NORM_EPS) * g


def _mod_kernel(c_ref, w_ref, b_ref, o_ref):
    cc = c_ref[...]
    s = cc * _sigmoid(cc)
    acc = None
    w = w_ref[...]
    w_parts = _split_terms(w, 3)
    for sp in _split_terms(s, 3):
        for wp in w_parts:
            t = jnp.dot(sp, wp, preferred_element_type=F32)
            acc = t if acc is None else acc + t
    o_ref[...] = acc + b_ref[...]


def _mod_call(cc, ada_w, ada_b):
    rows, d = cc.shape
    n = ada_w.shape[1]
    tn = 512
    return pl.pallas_call(
        _mod_kernel,
        grid=(n // tn,),
        in_specs=[pl.BlockSpec((rows, d), lambda i: (0, 0)),
                  pl.BlockSpec((d, tn), lambda i: (0, i)),
                  pl.BlockSpec((1, tn), lambda i: (0, i))],
        out_specs=pl.BlockSpec((rows, tn), lambda i: (0, i)),
        out_shape=jax.ShapeDtypeStruct((rows, n), F32),
        compiler_params=pltpu.CompilerParams(dimension_semantics=("arbitrary",),
                                             vmem_limit_bytes=VMEM_LIMIT),
        name="mod",
    )(cc, ada_w, ada_b)


def _prep_kernel(x_ref, xp_ref, xn_ref, ctx_ref, ms_ref, g1_ref, win_ref, cw_ref,
                 w0_ref, w2_ref, a0_ref, a2_ref, g2_ref, kk_ref, ka_ref, rk_ref,
                 cs_ref, ones_ref, tri_ref,
                 at_ref, bt_ref, kt_ref, rt_ref, v_ref, pl_ref, bonus_ref, gate_ref, z_ref):
    j = pl.program_id(1)
    n_j = pl.num_programs(1)
    is_ctx = j == 0
    shift = ms_ref[0, 0, 0:1, :]
    scale = ms_ref[0, 0, 1:2, :]
    g = g1_ref[...]

    def norm_mod(xr):
        return _rms(xr, g) * (1.0 + scale) + shift

    xin = jnp.where(is_ctx, ctx_ref[0], x_ref[0])
    h = norm_mod(xin).astype(BF16)
    u = jnp.dot(h, win_ref[...], preferred_element_type=F32)

    halo = jnp.concatenate([xp_ref[0], xn_ref[0]], axis=0)
    uh = jnp.dot(norm_mod(halo).astype(BF16), win_ref[:, RKV_LO:RKV_HI],
                 preferred_element_type=F32)
    prev_row = jnp.where(j >= 2, uh[7:8], 0.0)
    next_row = jnp.where(jnp.logical_and(j >= 1, j < n_j - 1), uh[8:9], 0.0)

    rkv = u[:, RKV_LO:RKV_HI]
    row = lax.broadcasted_iota(jnp.int32, rkv.shape, 0)
    u_m1 = jnp.where(row == 0, prev_row, pltpu.roll(rkv, 1, 0))
    u_p1 = jnp.where(row == TILE - 1, next_row, pltpu.roll(rkv, TILE - 1, 0))
    rkv = cw_ref[0:1, :] * u_m1 + cw_ref[1:2, :] * rkv + cw_ref[2:3, :] * u_p1
    r = rkv[:, 0:RWKV_W]
    k = rkv[:, RWKV_W:2 * RWKV_W]
    v = rkv[:, 2 * RWKV_W:3 * RWKV_W]

    wd = u[:, WD_LO:AD_LO]
    ad = u[:, AD_LO:GD_LO]
    gd = u[:, GD_LO:PROJ_W]
    w_lora = w0_ref[...] + _bdot(jnp.tanh(wd), w2_ref[...])
    lw = -math.exp(-0.5) * _sigmoid(w_lora)
    a = _sigmoid(a0_ref[...] + _bdot(ad, a2_ref[...]))

    ones_blk = ones_ref[...]
    kraw = k * kk_ref[...]
    kk = kraw * lax.rsqrt(_dot_exact_rhs(kraw * kraw, ones_blk) + KK_EPS)
    ka = ka_ref[...]
    kd = [k * (1.0 + (a[:, d * RWKV_W:(d + 1) * RWKV_W] - 1.0) * ka) for d in range(2)]

    v_ref[0] = v.astype(BF16)
    for d in range(2):
        lwd = lw[:, d * RWKV_W:(d + 1) * RWKV_W]
        ad_ = a[:, d * RWKV_W:(d + 1) * RWKV_W]
        c = _dot_exact_lhs(tri_ref[d], lwd)
        e_pos = jnp.exp(c)
        e_neg = jnp.exp(-c)
        e_prev = jnp.exp(c - lwd)
        at_ref[0, d] = (kk * e_prev).astype(BF16)
        bt_ref[0, d] = (kk * ad_ * e_neg).astype(BF16)
        kt_ref[0, d] = (kd[d] * e_neg).astype(BF16)
        rt_ref[0, d] = (r * e_pos).astype(BF16)
        for q in range(CHUNKS_PER_TILE):
            last = q * CHUNK + (CHUNK - 1 if d == 0 else 0)
            pl_ref[0, d, q] = e_pos[last:last + 1, :]

    @pl.when(j >= 1)
    def _():
        bs = _dot_exact_rhs(r * rk_ref[...] * (kd[0] + kd[1]), ones_blk)
        bonus_ref[0] = bs * v
        gate_ref[0] = _bdot(_sigmoid(gd), g2_ref[...])
        z_ref[0] = _bdot(u[:, 0:FOURIER_W], cs_ref[...]).astype(BF16)


def _prep_call(x, ctx, modsel, norm1_g, w_in, conv_w, w0, w2bd, a0, a2bd, g2, k_k, k_a, r_k,
               cs, ones_blk, tri):
    b, s, d = x.shape
    n_j = 1 + s // TILE
    n_chunks = (CTX_LEN + s) // CHUNK
    t_all = CTX_LEN + s
    c2 = 2 * RWKV_W

    def full(arr):
        nd = arr.ndim
        return pl.BlockSpec(arr.shape, lambda bi, j, _n=nd: (0,) * _n)

    rows8 = TILE // 8
    in_specs = [
        pl.BlockSpec((1, TILE, d), lambda bi, j: (bi, jnp.maximum(j - 1, 0), 0)),
        pl.BlockSpec((1, 8, d), lambda bi, j: (bi, jnp.maximum((j - 1) * rows8 - 1, 0), 0)),
        pl.BlockSpec((1, 8, d), lambda bi, j: (bi, jnp.minimum(j * rows8, s // 8 - 1), 0)),
        pl.BlockSpec((1, CTX_LEN, d), lambda bi, j: (bi, 0, 0)),
        pl.BlockSpec((1, 1, 2, d), lambda bi, j: (bi, jnp.minimum(j, 1), 0, 0)),
        full(norm1_g), full(w_in), full(conv_w), full(w0), full(w2bd), full(a0), full(a2bd),
        full(g2), full(k_k), full(k_a), full(r_k), full(cs), full(ones_blk), full(tri),
    ]
    lat = lambda bi, j: (bi, jnp.maximum(j - 1, 0), 0)
    out_specs = [
        pl.BlockSpec((1, 2, TILE, RWKV_W), lambda bi, j: (bi, 0, j, 0)),
        pl.BlockSpec((1, 2, TILE, RWKV_W), lambda bi, j: (bi, 0, j, 0)),
        pl.BlockSpec((1, 2, TILE, RWKV_W), lambda bi, j: (bi, 0, j, 0)),
        pl.BlockSpec((1, 2, TILE, RWKV_W), lambda bi, j: (bi, 0, j, 0)),
        pl.BlockSpec((1, TILE, RWKV_W), lambda bi, j: (bi, j, 0)),
        pl.BlockSpec((1, 2, CHUNKS_PER_TILE, 1, RWKV_W), lambda bi, j: (bi, 0, j, 0, 0)),
        pl.BlockSpec((1, TILE, RWKV_W), lat),
        pl.BlockSpec((1, TILE, RWKV_W), lat),
        pl.BlockSpec((1, TILE, 2 * FOURIER_W), lat),
    ]
    out_shape = [
        jax.ShapeDtypeStruct((b, 2, t_all, RWKV_W), BF16),
        jax.ShapeDtypeStruct((b, 2, t_all, RWKV_W), BF16),
        jax.ShapeDtypeStruct((b, 2, t_all, RWKV_W), BF16),
        jax.ShapeDtypeStruct((b, 2, t_all, RWKV_W), BF16),
        jax.ShapeDtypeStruct((b, t_all, RWKV_W), BF16),
        jax.ShapeDtypeStruct((b, 2, n_chunks, 1, RWKV_W), F32),
        jax.ShapeDtypeStruct((b, s, RWKV_W), F32),
        jax.ShapeDtypeStruct((b, s, RWKV_W), F32),
        jax.ShapeDtypeStruct((b, s, 2 * FOURIER_W), BF16),
    ]
    return pl.pallas_call(
        _prep_kernel,
        grid=(b, n_j),
        in_specs=in_specs,
        out_specs=out_specs,
        out_shape=out_shape,
        compiler_params=pltpu.CompilerParams(dimension_semantics=("arbitrary", "arbitrary"),
                                             vmem_limit_bytes=VMEM_LIMIT),
        name="prep",
    )(x, x, x, ctx, modsel, norm1_g, w_in, conv_w, w0, w2bd, a0, a2bd, g2, k_k, k_a, r_k,
      cs, ones_blk, tri)


def _pair_chunk(a, bm, km, r, v, plrow, g_state, strict, incl, eye, lane_lo):
    def dup(xv):
        z = jnp.zeros_like(xv)
        return jnp.concatenate([jnp.where(lane_lo, xv, z), jnp.where(lane_lo, z, xv)], axis=0)

    a2, b2, k2, r2, v2 = dup(a), dup(bm), dup(km), dup(r), dup(v)
    lhs = jnp.concatenate([a2, r2], axis=0)
    rhs = jnp.concatenate([b2, k2], axis=0)
    s = _bdot_nt(lhs, rhs)
    yield
    n = PAIR
    a_ab = jnp.where(strict, s[:n, :n], 0.0)
    a_ak = jnp.where(strict, s[:n, n:], 0.0)
    m_rb = jnp.where(incl, s[n:, :n], 0.0)
    m_rk = jnp.where(incl, s[n:, n:], 0.0)

    npow = (-a_ab).astype(BF16)
    t_inv = jnp.where(eye, 1.0, 0.0) - a_ab
    av = _bdot(a_ak, v2)
    for _ in range(int(math.log2(CHUNK)) - 1):
        npow = jnp.dot(npow, npow, preferred_element_type=F32).astype(BF16)
        yield
        t_inv = t_inv + jnp.dot(t_inv.astype(BF16), npow, preferred_element_type=F32)
    yield
    tx = _bdot(t_inv, jnp.concatenate([a2, av.astype(BF16)], axis=1))
    yield
    a_w = tx[:, :n]
    u_v = -tx[:, n:]
    zmat = jnp.concatenate(
        [jnp.concatenate([a_w, u_v], axis=1).astype(BF16),
         jnp.concatenate([jnp.zeros_like(v2), v2], axis=1)], axis=0)
    rhs_end = (rhs.astype(F32) * plrow).astype(BF16)
    bk = _bdot_tn(rhs_end, zmat)
    my = _bdot(jnp.concatenate([m_rb, m_rk], axis=1), zmat)
    yield
    phi = jnp.where(eye, plrow, 0.0) - bk[:, :n]
    dmat = bk[:, n:]
    r_y = r2.astype(F32) - my[:, :n]
    y2 = _bdot(r_y, g_state) + my[:, n:]
    y = y2[:CHUNK] + y2[CHUNK:]
    g_new = _bdot(phi, g_state) + dmat
    return y, g_new


def _run_interleaved(gens):
    results = [None] * len(gens)
    active = list(range(len(gens)))
    while active:
        for i in list(active):
            try:
                next(gens[i])
            except StopIteration as stop:
                results[i] = stop.value
                active.remove(i)
    return results


def _scan_kernel(atf, btf, ktf, rtf, vf, plf, atb, btb, ktb, rtb, vb, plb,
                 yf_ref, yb_ref, g_ref):
    s = pl.program_id(1)

    @pl.when(s == 0)
    def _():
        g_ref[...] = jnp.zeros_like(g_ref)

    ri = lax.broadcasted_iota(jnp.int32, (PAIR, PAIR), 0)
    ci = lax.broadcasted_iota(jnp.int32, (PAIR, PAIR), 1)
    same = (ri >= HEAD) == (ci >= HEAD)
    ti = ri & (HEAD - 1)
    tj = ci & (HEAD - 1)
    eye = ri == ci
    lane_lo = lax.broadcasted_iota(jnp.int32, (CHUNK, PAIR), 1) < HEAD
    masks = [(same & (tj < ti), same & (tj <= ti)),
             (same & (tj > ti), same & (tj >= ti))]
    refs = [(atf, btf, ktf, rtf, vf, plf, yf_ref), (atb, btb, ktb, rtb, vb, plb, yb_ref)]
    gens, dests = [], []
    for d in range(2):
        at, bt, kt, rt, vv, plr, y_ref = refs[d]
        strict, incl = masks[d]
        for p in range(N_PAIRS):
            sl = slice(p * PAIR, (p + 1) * PAIR)
            gens.append(_pair_chunk(at[0, 0, :, sl], bt[0, 0, :, sl], kt[0, 0, :, sl],
                                    rt[0, 0, :, sl], vv[0, :, sl], plr[0, 0, 0, :, sl],
                                    g_ref[d, p], strict, incl, eye, lane_lo))
            dests.append((y_ref, d, p, sl))
    for (y, g_new), (y_ref, d, p, sl) in zip(_run_interleaved(gens), dests):
        g_ref[d, p] = g_new
        y_ref[0, :, sl] = y


def _scan_call(at, bt, kt, rt, v, plast, s_lat):
    b = at.shape[0]
    n_chunks = at.shape[2] // CHUNK
    n_ctx = CTX_LEN // CHUNK
    n_lat = s_lat // CHUNK

    def fwd_c(s):
        return s

    def bwd_c(s):
        return jnp.where(s < n_ctx, n_ctx - 1 - s, n_chunks + n_ctx - 1 - s)

    def dir_spec(d, cfun):
        return pl.BlockSpec((1, 1, CHUNK, RWKV_W), lambda bi, s: (bi, d, cfun(s), 0))

    def v_spec(cfun):
        return pl.BlockSpec((1, CHUNK, RWKV_W), lambda bi, s: (bi, cfun(s), 0))

    def pl_spec(d, cfun):
        return pl.BlockSpec((1, 1, 1, 1, RWKV_W), lambda bi, s: (bi, d, cfun(s), 0, 0))

    in_specs = ([dir_spec(0, fwd_c)] * 4 + [v_spec(fwd_c), pl_spec(0, fwd_c)]
                + [dir_spec(1, bwd_c)] * 4 + [v_spec(bwd_c), pl_spec(1, bwd_c)])
    out_specs = [
        pl.BlockSpec((1, CHUNK, RWKV_W), lambda bi, s: (bi, jnp.maximum(s - n_ctx, 0), 0)),
        pl.BlockSpec((1, CHUNK, RWKV_W),
                     lambda bi, s: (bi, jnp.minimum(n_chunks - 1 - s, n_lat - 1), 0)),
    ]
    out_shape = [jax.ShapeDtypeStruct((b, s_lat, RWKV_W), F32)] * 2
    return pl.pallas_call(
        _scan_kernel,
        grid=(b, n_chunks),
        in_specs=in_specs,
        out_specs=out_specs,
        out_shape=out_shape,
        scratch_shapes=[pltpu.VMEM((2, N_PAIRS, PAIR, PAIR), F32)],
        compiler_params=pltpu.CompilerParams(dimension_semantics=("arbitrary", "arbitrary"),
                                             vmem_limit_bytes=VMEM_LIMIT),
        name="scan",
    )(at, bt, kt, rt, v, plast, at, bt, kt, rt, v, plast)


def _mix_kernel(dft_ref, z_ref, yf_ref, yb_ref, bonus_ref, gate_ref, x_ref, mod_ref,
                gng_ref, gnb_ref, fg_ref, wout_ref, n2g_ref, ones_ref, x1_ref, h2_ref, *, fscale):
    seq = z_ref.shape[1]
    yfour = (jnp.dot(dft_ref[:, :seq], z_ref[0, :, :FOURIER_W], preferred_element_type=F32)
             + jnp.dot(dft_ref[:, seq:], z_ref[0, :, FOURIER_W:], preferred_element_type=F32))
    four = _rms(yfour * fscale, fg_ref[...])

    ones_blk = ones_ref[...]
    y = yf_ref[0] + yb_ref[0]
    mu = _dot_exact_rhs(y, ones_blk) * (1.0 / HEAD)
    dy = y - mu
    var = _dot_exact_rhs(dy * dy, ones_blk) * (1.0 / HEAD)
    yn = dy * lax.rsqrt(var + GN_EPS) * gng_ref[...] + gnb_ref[...]
    rw = (yn + bonus_ref[0]) * gate_ref[0]

    out = _bdot(four, wout_ref[:FOURIER_W, :]) + _bdot(rw, wout_ref[FOURIER_W:, :])
    x1 = x_ref[0] + mod_ref[0, 2:3, :] * out
    x1_ref[0] = x1
    h2 = _rms(x1, n2g_ref[...]) * (1.0 + mod_ref[0, 4:5, :]) + mod_ref[0, 3:4, :]
    h2_ref[0] = h2.astype(BF16)


def _mix_call(dft, z, yf, yb, bonus, gate, x, modx, gn_g, gn_b, f_g, w_out, norm2_g, ones_blk):
    b, s, d = x.shape
    tm = TILE

    def full(arr):
        nd = arr.ndim
        return pl.BlockSpec(arr.shape, lambda bi, m, _n=nd: (0,) * _n)

    tok = lambda w: pl.BlockSpec((1, tm, w), lambda bi, m: (bi, m, 0))
    in_specs = [
        pl.BlockSpec((tm, 2 * s), lambda bi, m: (m, 0)),
        pl.BlockSpec((1, s, 2 * FOURIER_W), lambda bi, m: (bi, 0, 0)),
        tok(RWKV_W), tok(RWKV_W), tok(RWKV_W), tok(RWKV_W), tok(d),
        pl.BlockSpec((1, 6, d), lambda bi, m: (bi, 0, 0)),
        full(gn_g), full(gn_b), full(f_g), full(w_out), full(norm2_g), full(ones_blk),
    ]
    fscale = 1.0 / math.sqrt(s * (FOURIER_W // FOURIER_GROUPS))
    return pl.pallas_call(
        functools.partial(_mix_kernel, fscale=fscale),
        grid=(b, s // tm),
        in_specs=in_specs,
        out_specs=[tok(d), tok(d)],
        out_shape=[jax.ShapeDtypeStruct((b, s, d), F32), jax.ShapeDtypeStruct((b, s, d), BF16)],
        compiler_params=pltpu.CompilerParams(dimension_semantics=("arbitrary", "arbitrary"),
                                             vmem_limit_bytes=VMEM_LIMIT),
        name="mix",
    )(dft, z, yf, yb, bonus, gate, x, modx, gn_g, gn_b, f_g, w_out, norm2_g, ones_blk)


def _conv3x3(u, w):
    t = u.shape[0]
    col = lax.broadcasted_iota(jnp.int32, u.shape, 0) & (GRID_W - 1)
    u_l = jnp.where(col > 0, pltpu.roll(u, 1, 0), 0.0)
    u_r = jnp.where(col < GRID_W - 1, pltpu.roll(u, t - 1, 0), 0.0)

    def rowmix(kh):
        return w[3 * kh:3 * kh + 1] * u_l + w[3 * kh + 1:3 * kh + 2] * u + w[3 * kh + 2:3 * kh + 3] * u_r

    zeros = jnp.zeros((GRID_W, u.shape[1]), F32)
    up = jnp.concatenate([zeros, rowmix(0)[:t - GRID_W]], axis=0)
    down = jnp.concatenate([rowmix(2)[GRID_W:], zeros], axis=0)
    return rowmix(1) + up + down


def _ffn_kernel(h2_ref, wg_ref, wv_ref, cwg_ref, cwv_ref, cbg_ref, cbv_ref, wd_ref,
                x1_ref, mod_ref, fg_ref, out_ref):
    f = pl.program_id(1)
    h2 = h2_ref[0]
    ug = jnp.dot(h2, wg_ref[...], preferred_element_type=F32)
    cg = _conv3x3(ug, cwg_ref[...]) + cbg_ref[...]
    uv = jnp.dot(h2, wv_ref[...], preferred_element_type=F32)
    cv = _conv3x3(uv, cwv_ref[...]) + cbv_ref[...]
    act = cg * _sigmoid(cg) * cv
    part = _bdot(act, wd_ref[...])

    @pl.when(f == 0)
    def _():
        out_ref[0] = part

    @pl.when(f > 0)
    def _():
        out_ref[0] += part

    @pl.when(f == pl.num_programs(1) - 1)
    def _():
        xx = x1_ref[0] + mod_ref[0, 5:6, :] * out_ref[0]
        out_ref[0] = _rms(xx, fg_ref[...])


def _ffn_call(h2, w_up, conv_w9, conv_b, w_down, x1, modx, final_g):
    b, s, d = x1.shape
    n_f = D_FF // FF_TILE
    single = pl.Buffered(1)
    in_specs = [
        pl.BlockSpec((1, s, d), lambda bi, f: (bi, 0, 0), pipeline_mode=single),
        pl.BlockSpec((d, FF_TILE), lambda bi, f: (0, f)),
        pl.BlockSpec((d, FF_TILE), lambda bi, f: (0, n_f + f)),
        pl.BlockSpec((9, FF_TILE), lambda bi, f: (0, f)),
        pl.BlockSpec((9, FF_TILE), lambda bi, f: (0, n_f + f)),
        pl.BlockSpec((1, FF_TILE), lambda bi, f: (0, f)),
        pl.BlockSpec((1, FF_TILE), lambda bi, f: (0, n_f + f)),
        pl.BlockSpec((FF_TILE, d), lambda bi, f: (f, 0)),
        pl.BlockSpec((1, s, d), lambda bi, f: (bi, 0, 0), pipeline_mode=single),
        pl.BlockSpec((1, 6, d), lambda bi, f: (bi, 0, 0)),
        pl.BlockSpec((1, d), lambda bi, f: (0, 0)),
    ]
    return pl.pallas_call(
        _ffn_kernel,
        grid=(b, n_f),
        in_specs=in_specs,
        out_specs=pl.BlockSpec((1, s, d), lambda bi, f: (bi, 0, 0)),
        out_shape=jax.ShapeDtypeStruct((b, s, d), F32),
        compiler_params=pltpu.CompilerParams(dimension_semantics=("arbitrary", "arbitrary"),
                                             vmem_limit_bytes=VMEM_LIMIT),
        name="ffn",
    )(h2, w_up, w_up, conv_w9, conv_w9, conv_b, conv_b, w_down, x1, modx, final_g)


@functools.lru_cache(maxsize=None)
def _constants(seq):
    gw = FOURIER_W // FOURIER_GROUPS
    nm = np.outer(np.arange(gw), np.arange(gw)) % gw
    ang = 2.0 * np.pi * nm / gw
    cs = np.zeros((FOURIER_W, 2 * FOURIER_W), np.float64)
    for gi in range(FOURIER_GROUPS):
        sl = slice(gi * gw, (gi + 1) * gw)
        cs[sl, sl] = np.cos(ang)
        cs[sl, FOURIER_W + gi * gw:FOURIER_W + (gi + 1) * gw] = np.sin(ang)
    kt = np.outer(np.arange(seq, dtype=np.int64), np.arange(seq, dtype=np.int64)) % seq
    ang_t = 2.0 * np.pi * kt / seq
    dft = np.concatenate([np.cos(ang_t), -np.sin(ang_t)], axis=1)
    head_id = np.arange(RWKV_W) // HEAD
    ones_blk = (head_id[:, None] == head_id[None, :]).astype(np.float64)
    ti = np.arange(TILE)
    same_chunk = (ti[:, None] // CHUNK) == (ti[None, :] // CHUNK)
    tri = np.stack([same_chunk & (ti[None, :] <= ti[:, None]),
                    same_chunk & (ti[None, :] >= ti[:, None])]).astype(np.float64)
    to_bf16 = lambda arr: np.asarray(arr, dtype=np.float32).astype(BF16)
    return (np.asarray(cs, dtype=np.float32), np.asarray(dft, dtype=np.float32),
            to_bf16(ones_blk), to_bf16(tri))


def _block_diag_dirs(w):
    z = jnp.zeros_like(w[0])
    return jnp.concatenate([jnp.concatenate([w[0], z], axis=1),
                            jnp.concatenate([z, w[1]], axis=1)], axis=0)


def kernel(x, c, ctx, c_ctx, ada_w, ada_b, norm1_g, norm2_g, w_in, rwkv_conv_w, decay_w0, decay_w2, iclr_a0, iclr_a2, gate_g2, k_k, k_a, r_k, gn_g, gn_b, fourier_g, w_out, ffn_w_up, ffn_conv_w, ffn_conv_b, ffn_w_down, final_g):
    b, s, d = x.shape
    assert ada_w.shape[0] == 1, "single-layer configuration"
    assert (b, s, d) == (c.shape[0], s, D_MODEL) and ctx.shape == (b, CTX_LEN, d)
    assert s % TILE == 0 and CTX_LEN == TILE
    cs, dft, ones_blk, tri = (jnp.asarray(t) for t in _constants(s))
    cs, dft = cs.astype(BF16), dft.astype(BF16)

    cc = jnp.concatenate([c, c_ctx[None, :], jnp.zeros((16 - b - 1, d), F32)], axis=0)
    mod = _mod_call(cc, ada_w[0], ada_b[0][None, :])
    modx = mod[:b].reshape(b, 6, d)
    modc = jnp.broadcast_to(mod[b].reshape(1, 6, d)[:, :2], (b, 2, d))
    modsel = jnp.stack([modc, modx[:, :2]], axis=1)

    row = lambda t: t.reshape(1, -1)
    at, bt, kt, rt, v, plast, bonus, gate, z = _prep_call(
        x, ctx, modsel, row(norm1_g[0]), w_in[0].astype(BF16), rwkv_conv_w[0],
        row(decay_w0[0]), _block_diag_dirs(decay_w2[0]).astype(BF16),
        row(iclr_a0[0]), _block_diag_dirs(iclr_a2[0]).astype(BF16),
        gate_g2[0].astype(BF16), row(k_k[0]), row(k_a[0]), row(r_k[0]), cs, ones_blk, tri)

    yf, yb = _scan_call(at, bt, kt, rt, v, plast, s)

    x1, h2 = _mix_call(dft, z, yf, yb, bonus, gate, x, modx, row(gn_g[0]), row(gn_b[0]),
                       row(fourier_g[0]), w_out[0].astype(BF16), row(norm2_g[0]), ones_blk)

    return _ffn_call(h2, ffn_w_up[0].astype(BF16), ffn_conv_w[0].reshape(9, 2 * D_FF),
                     row(ffn_conv_b[0]), ffn_w_down[0].astype(BF16), x1, modx, row(final_g))
```

```python
import functools
import math

import numpy as np
import jax
import jax.numpy as jnp
from jax import lax
from jax.experimental import pallas as pl
from jax.experimental.pallas import tpu as pltpu

F32 = jnp.float32
BF16 = jnp.bfloat16

D_MODEL = 1024
CTX_LEN = 256
GRID_W = 64
FOURIER_W = 512
FOURIER_GROUPS = 8
RWKV_W = 512
HEAD = 64
N_HEADS = 8
DECAY_RANK = 64
ICLR_RANK = 64
GATE_RANK = 128
D_FF = 2816
NORM_EPS = 1e-6
GN_EPS = 64e-5
KK_EPS = 1e-12

CHUNK = 64
TILE = 256
CHUNKS_PER_TILE = TILE // CHUNK
PAIR = 2 * HEAD
N_PAIRS = RWKV_W // PAIR
QUAD_HEADS = 4
SCAN_BATCH = 2
FF_TILE = 256
DOWN_TILE = 512
VMEM_LIMIT = 56 * 1024 * 1024

RKV_LO = FOURIER_W
RKV_HI = FOURIER_W + 3 * RWKV_W
WD_LO = RKV_HI
AD_LO = WD_LO + 2 * DECAY_RANK
GD_LO = AD_LO + 2 * ICLR_RANK
PROJ_W = GD_LO + GATE_RANK


def _bdot(a, b):
    return jnp.dot(a.astype(BF16), b.astype(BF16), preferred_element_type=F32)


def _bdot_nt(a, b):
    return lax.dot_general(a.astype(BF16), b.astype(BF16), (((1,), (1,)), ((), ())),
                           preferred_element_type=F32)


def _bdot_tn(a, b):
    return lax.dot_general(a.astype(BF16), b.astype(BF16), (((0,), (0,)), ((), ())),
                           preferred_element_type=F32)


def _split_terms(a, terms):
    out, rem = [], a
    for _ in range(terms):
        hi = rem.astype(BF16)
        out.append(hi)
        rem = rem - hi.astype(F32)
    return out


def _dot_exact_rhs(a, b_exact, terms=2):
    acc = None
    for piece in _split_terms(a, terms):
        t = jnp.dot(piece, b_exact, preferred_element_type=F32)
        acc = t if acc is None else acc + t
    return acc


def _dot_exact_lhs(a_exact, b, terms=3):
    acc = None
    for piece in _split_terms(b, terms):
        t = jnp.dot(a_exact, piece, preferred_element_type=F32)
        acc = t if acc is None else acc + t
    return acc


def _sigmoid(x):
    return 1.0 / (1.0 + jnp.exp(-x))


def _rms(x, g):
    return x * lax.rsqrt(jnp.mean(x * x, axis=-1, keepdims=True) + NORM_EPS) * g


def _mod_kernel(c_ref, w_ref, b_ref, o_ref):
    cc = c_ref[...]
    s = cc * _sigmoid(cc)
    acc = None
    w = w_ref[...]
    w_parts = _split_terms(w, 3)
    for sp in _split_terms(s, 3):
        for wp in w_parts:
            t = jnp.dot(sp, wp, preferred_element_type=F32)
            acc = t if acc is None else acc + t
    o_ref[...] = acc + b_ref[...]


def _mod_call(cc, ada_w, ada_b):
    rows, d = cc.shape
    n = ada_w.shape[1]
    tn = 512
    return pl.pallas_call(
        _mod_kernel,
        grid=(n // tn,),
        in_specs=[pl.BlockSpec((rows, d), lambda i: (0, 0)),
                  pl.BlockSpec((d, tn), lambda i: (0, i)),
                  pl.BlockSpec((1, tn), lambda i: (0, i))],
        out_specs=pl.BlockSpec((rows, tn), lambda i: (0, i)),
        out_shape=jax.ShapeDtypeStruct((rows, n), F32),
        compiler_params=pltpu.CompilerParams(dimension_semantics=("arbitrary",),
                                             vmem_limit_bytes=VMEM_LIMIT),
        name="mod",
    )(cc, ada_w, ada_b)


def _prep_kernel(x_ref, xp_ref, xn_ref, ctx_ref, ms_ref, g1_ref, win_ref, cw_ref,
                 w0_ref, w2_ref, a0_ref, a2_ref, g2_ref, kk_ref, ka_ref, rk_ref,
                 cs_ref, ones_ref, tri_ref,
                 at_ref, bt_ref, kt_ref, rt_ref, v_ref, pl_ref, bonus_ref, gate_ref, z_ref):
    j = pl.program_id(1)
    n_j = pl.num_programs(1)
    is_ctx = j == 0
    shift = ms_ref[0, 0, 0:1, :]
    scale = ms_ref[0, 0, 1:2, :]
    g = g1_ref[...]

    def norm_mod(xr):
        return _rms(xr, g) * (1.0 + scale) + shift

    xin = jnp.where(is_ctx, ctx_ref[0], x_ref[0])
    h = norm_mod(xin).astype(BF16)
    u = jnp.dot(h, win_ref[...], preferred_element_type=F32)

    halo = jnp.concatenate([xp_ref[0], xn_ref[0]], axis=0)
    uh = jnp.dot(norm_mod(halo).astype(BF16), win_ref[:, RKV_LO:RKV_HI],
                 preferred_element_type=F32)
    prev_row = jnp.where(j >= 2, uh[7:8], 0.0)
    next_row = jnp.where(jnp.logical_and(j >= 1, j < n_j - 1), uh[8:9], 0.0)

    rkv = u[:, RKV_LO:RKV_HI]
    row = lax.broadcasted_iota(jnp.int32, rkv.shape, 0)
    u_m1 = jnp.where(row == 0, prev_row, pltpu.roll(rkv, 1, 0))
    u_p1 = jnp.where(row == TILE - 1, next_row, pltpu.roll(rkv, TILE - 1, 0))
    rkv = cw_ref[0:1, :] * u_m1 + cw_ref[1:2, :] * rkv + cw_ref[2:3, :] * u_p1
    r = rkv[:, 0:RWKV_W]
    k = rkv[:, RWKV_W:2 * RWKV_W]
    v = rkv[:, 2 * RWKV_W:3 * RWKV_W]

    wd = u[:, WD_LO:AD_LO]
    ad = u[:, AD_LO:GD_LO]
    gd = u[:, GD_LO:PROJ_W]
    w_lora = w0_ref[...] + _bdot(jnp.tanh(wd), w2_ref[...])
    lw = -math.exp(-0.5) * _sigmoid(w_lora)
    a = _sigmoid(a0_ref[...] + _bdot(ad, a2_ref[...]))

    ones_blk = ones_ref[...]
    kraw = k * kk_ref[...]
    kk = kraw * lax.rsqrt(_dot_exact_rhs(kraw * kraw, ones_blk) + KK_EPS)
    ka = ka_ref[...]
    kd = [k * (1.0 + (a[:, d * RWKV_W:(d + 1) * RWKV_W] - 1.0) * ka) for d in range(2)]

    v_ref[0] = v.astype(BF16)
    for d in range(2):
        lwd = lw[:, d * RWKV_W:(d + 1) * RWKV_W]
        ad_ = a[:, d * RWKV_W:(d + 1) * RWKV_W]
        c = _dot_exact_lhs(tri_ref[d], lwd)
        e_pos = jnp.exp(c)
        e_neg = jnp.exp(-c)
        e_prev = jnp.exp(c - lwd)
        at_ref[0, d] = (kk * e_prev).astype(BF16)
        bt_ref[0, d] = (kk * ad_ * e_neg).astype(BF16)
        kt_ref[0, d] = (kd[d] * e_neg).astype(BF16)
        rt_ref[0, d] = (r * e_pos).astype(BF16)
        for q in range(CHUNKS_PER_TILE):
            last = q * CHUNK + (CHUNK - 1 if d == 0 else 0)
            pl_ref[0, d, q] = e_pos[last:last + 1, :]

    @pl.when(j >= 1)
    def _():
        bs = _dot_exact_rhs(r * rk_ref[...] * (kd[0] + kd[1]), ones_blk)
        bonus_ref[0] = bs * v
        gate_ref[0] = _bdot(_sigmoid(gd), g2_ref[...])
        z_ref[0] = _bdot(u[:, 0:FOURIER_W], cs_ref[...]).astype(BF16)


def _prep_call(x, ctx, modsel, norm1_g, w_in, conv_w, w0, w2bd, a0, a2bd, g2, k_k, k_a, r_k,
               cs, ones_blk, tri):
    b, s, d = x.shape
    n_j = 1 + s // TILE
    n_chunks = (CTX_LEN + s) // CHUNK
    t_all = CTX_LEN + s
    c2 = 2 * RWKV_W

    def full(arr):
        nd = arr.ndim
        return pl.BlockSpec(arr.shape, lambda bi, j, _n=nd: (0,) * _n)

    rows8 = TILE // 8
    in_specs = [
        pl.BlockSpec((1, TILE, d), lambda bi, j: (bi, jnp.maximum(j - 1, 0), 0)),
        pl.BlockSpec((1, 8, d), lambda bi, j: (bi, jnp.maximum((j - 1) * rows8 - 1, 0), 0)),
        pl.BlockSpec((1, 8, d), lambda bi, j: (bi, jnp.minimum(j * rows8, s // 8 - 1), 0)),
        pl.BlockSpec((1, CTX_LEN, d), lambda bi, j: (bi, 0, 0)),
        pl.BlockSpec((1, 1, 2, d), lambda bi, j: (bi, jnp.minimum(j, 1), 0, 0)),
        full(norm1_g), full(w_in), full(conv_w), full(w0), full(w2bd), full(a0), full(a2bd),
        full(g2), full(k_k), full(k_a), full(r_k), full(cs), full(ones_blk), full(tri),
    ]
    lat = lambda bi, j: (bi, jnp.maximum(j - 1, 0), 0)
    out_specs = [
        pl.BlockSpec((1, 2, TILE, RWKV_W), lambda bi, j: (bi, 0, j, 0)),
        pl.BlockSpec((1, 2, TILE, RWKV_W), lambda bi, j: (bi, 0, j, 0)),
        pl.BlockSpec((1, 2, TILE, RWKV_W), lambda bi, j: (bi, 0, j, 0)),
        pl.BlockSpec((1, 2, TILE, RWKV_W), lambda bi, j: (bi, 0, j, 0)),
        pl.BlockSpec((1, TILE, RWKV_W), lambda bi, j: (bi, j, 0)),
        pl.BlockSpec((1, 2, CHUNKS_PER_TILE, 1, RWKV_W), lambda bi, j: (bi, 0, j, 0, 0)),
        pl.BlockSpec((1, TILE, RWKV_W), lat),
        pl.BlockSpec((1, TILE, RWKV_W), lat),
        pl.BlockSpec((1, TILE, 2 * FOURIER_W), lat),
    ]
    out_shape = [
        jax.ShapeDtypeStruct((b, 2, t_all, RWKV_W), BF16),
        jax.ShapeDtypeStruct((b, 2, t_all, RWKV_W), BF16),
        jax.ShapeDtypeStruct((b, 2, t_all, RWKV_W), BF16),
        jax.ShapeDtypeStruct((b, 2, t_all, RWKV_W), BF16),
        jax.ShapeDtypeStruct((b, t_all, RWKV_W), BF16),
        jax.ShapeDtypeStruct((b, 2, n_chunks, 1, RWKV_W), F32),
        jax.ShapeDtypeStruct((b, s, RWKV_W), F32),
        jax.ShapeDtypeStruct((b, s, RWKV_W), F32),
        jax.ShapeDtypeStruct((b, s, 2 * FOURIER_W), BF16),
    ]
    return pl.pallas_call(
        _prep_kernel,
        grid=(b, n_j),
        in_specs=in_specs,
        out_specs=out_specs,
        out_shape=out_shape,
        compiler_params=pltpu.CompilerParams(dimension_semantics=("arbitrary", "arbitrary"),
                                             vmem_limit_bytes=VMEM_LIMIT),
        name="prep",
    )(x, x, x, ctx, modsel, norm1_g, w_in, conv_w, w0, w2bd, a0, a2bd, g2, k_k, k_a, r_k,
      cs, ones_blk, tri)


def _quad_chunk(a, bm, km, r, v, plrow, g_states, strict, incl, eye, lane_lo, mask_bd, eye_row):
    n = PAIR

    def dup(xv):
        z = jnp.zeros_like(xv)
        return jnp.concatenate([jnp.where(lane_lo, xv, z), jnp.where(lane_lo, z, xv)], axis=0)

    def to_row(x2):
        return x2[:CHUNK] + x2[CHUNK:]

    def bdiag(xr):
        return jnp.where(mask_bd, jnp.concatenate([xr] * QUAD_HEADS, axis=0), jnp.zeros((), xr.dtype))

    pairs = []
    for p in range(2):
        sl = slice(p * n, (p + 1) * n)
        a2, b2, k2, r2, v2 = (dup(t[:, sl]) for t in (a, bm, km, r, v))
        rhs = jnp.concatenate([b2, k2], axis=0)
        s = _bdot_nt(jnp.concatenate([a2, r2], axis=0), rhs)
        pairs.append((sl, r2, v2, rhs, s))
    yield

    a_ab = jnp.concatenate([to_row(jnp.where(strict, s[:n, :n], 0.0)) for *_, s in pairs], axis=1)
    a_ak = jnp.concatenate([to_row(jnp.where(strict, s[:n, n:], 0.0)) for *_, s in pairs], axis=1)
    m_rbk = [jnp.concatenate([jnp.where(incl, s[n:, :n], 0.0), jnp.where(incl, s[n:, n:], 0.0)],
                             axis=1).astype(BF16) for *_, s in pairs]

    pk = (-a_ab).astype(BF16)
    t_row = jnp.where(eye_row, 1.0, 0.0) - a_ab
    av = _bdot(a_ak, bdiag(v))
    pk = jnp.dot(pk, bdiag(pk), preferred_element_type=F32).astype(BF16)
    yield
    for _ in range(int(math.log2(CHUNK)) - 2):
        both = jnp.dot(jnp.concatenate([pk, t_row.astype(BF16)], axis=0), bdiag(pk),
                       preferred_element_type=F32)
        yield
        pk = both[:CHUNK].astype(BF16)
        t_row = t_row + both[CHUNK:]
    t_row = t_row + jnp.dot(t_row.astype(BF16), bdiag(pk), preferred_element_type=F32)
    yield
    tx = _bdot(t_row, jnp.concatenate([bdiag(a), bdiag(av.astype(BF16))], axis=1))
    yield
    a_w = tx[:, :2 * n].astype(BF16)
    u_v = (-tx[:, 2 * n:]).astype(BF16)
    stage5 = []
    for (sl, r2, v2, rhs, _), m2 in zip(pairs, m_rbk):
        zmat = jnp.concatenate(
            [jnp.concatenate([dup(a_w[:, sl]), dup(u_v[:, sl])], axis=1),
             jnp.concatenate([jnp.zeros_like(v2), v2], axis=1)], axis=0)
        rhs_end = rhs.astype(F32) * plrow[:, sl]
        lhs = jnp.concatenate([rhs_end.T.astype(BF16), m2], axis=0)
        stage5.append(jnp.dot(lhs, zmat, preferred_element_type=F32))
    yield
    ys, g_new = [], []
    for (sl, r2, v2, rhs, _), o5, g_state in zip(pairs, stage5, g_states):
        phi = jnp.where(eye, plrow[:, sl], 0.0) - o5[:n, :n]
        r_y = r2.astype(F32) - o5[n:, :n]
        o6 = _bdot(jnp.concatenate([r_y, phi], axis=0), g_state)
        ys.append(to_row(o6[:n] + o5[n:, n:]))
        g_new.append(o6[n:] + o5[:n, n:])
    return jnp.concatenate(ys, axis=1), g_new


def _run_interleaved(gens):
    results = [None] * len(gens)
    active = list(range(len(gens)))
    while active:
        for i in list(active):
            try:
                next(gens[i])
            except StopIteration as stop:
                results[i] = stop.value
                active.remove(i)
    return results


def _scan_kernel(atf, btf, ktf, rtf, vf, plf, atb, btb, ktb, rtb, vb, plb,
                 yf_ref, yb_ref, g_ref):
    s = pl.program_id(1)

    @pl.when(s == 0)
    def _():
        g_ref[...] = jnp.zeros_like(g_ref)

    ri = lax.broadcasted_iota(jnp.int32, (PAIR, PAIR), 0)
    ci = lax.broadcasted_iota(jnp.int32, (PAIR, PAIR), 1)
    same = (ri >= HEAD) == (ci >= HEAD)
    ti = ri & (HEAD - 1)
    tj = ci & (HEAD - 1)
    eye = ri == ci
    lane_lo = lax.broadcasted_iota(jnp.int32, (CHUNK, PAIR), 1) < HEAD
    masks = [(same & (tj < ti), same & (tj <= ti)),
             (same & (tj > ti), same & (tj >= ti))]
    quad_w = QUAD_HEADS * HEAD
    rq = lax.broadcasted_iota(jnp.int32, (quad_w, quad_w), 0)
    cq = lax.broadcasted_iota(jnp.int32, (quad_w, quad_w), 1)
    mask_bd = (rq // HEAD) == (cq // HEAD)
    eye_row = (lax.broadcasted_iota(jnp.int32, (CHUNK, quad_w), 1) & (HEAD - 1)
               ) == lax.broadcasted_iota(jnp.int32, (CHUNK, quad_w), 0)
    refs = [(atf, btf, ktf, rtf, vf, plf, yf_ref), (atb, btb, ktb, rtb, vb, plb, yb_ref)]
    gens, dests = [], []
    for bi in range(yf_ref.shape[0]):
        for d in range(2):
            at, bt, kt, rt, vv, plr, y_ref = refs[d]
            strict, incl = masks[d]
            for q in range(RWKV_W // quad_w):
                sl = slice(q * quad_w, (q + 1) * quad_w)
                gens.append(_quad_chunk(at[bi, 0, :, sl], bt[bi, 0, :, sl], kt[bi, 0, :, sl],
                                        rt[bi, 0, :, sl], vv[bi, :, sl], plr[bi, 0, 0, :, sl],
                                        [g_ref[bi, d, 2 * q], g_ref[bi, d, 2 * q + 1]],
                                        strict, incl, eye, lane_lo, mask_bd, eye_row))
                dests.append((y_ref, bi, d, q, sl))
    for (y, g_new), (y_ref, bi, d, q, sl) in zip(_run_interleaved(gens), dests):
        g_ref[bi, d, 2 * q] = g_new[0]
        g_ref[bi, d, 2 * q + 1] = g_new[1]
        y_ref[bi, :, sl] = y


def _scan_call(at, bt, kt, rt, v, plast, s_lat):
    b = at.shape[0]
    n_chunks = at.shape[2] // CHUNK
    n_ctx = CTX_LEN // CHUNK
    n_lat = s_lat // CHUNK

    def fwd_c(s):
        return s

    def bwd_c(s):
        return jnp.where(s < n_ctx, n_ctx - 1 - s, n_chunks + n_ctx - 1 - s)

    nb = SCAN_BATCH

    def dir_spec(d, cfun):
        return pl.BlockSpec((nb, 1, CHUNK, RWKV_W), lambda bi, s: (bi, d, cfun(s), 0))

    def v_spec(cfun):
        return pl.BlockSpec((nb, CHUNK, RWKV_W), lambda bi, s: (bi, cfun(s), 0))

    def pl_spec(d, cfun):
        return pl.BlockSpec((nb, 1, 1, 1, RWKV_W), lambda bi, s: (bi, d, cfun(s), 0, 0))

    in_specs = ([dir_spec(0, fwd_c)] * 4 + [v_spec(fwd_c), pl_spec(0, fwd_c)]
                + [dir_spec(1, bwd_c)] * 4 + [v_spec(bwd_c), pl_spec(1, bwd_c)])
    out_specs = [
        pl.BlockSpec((nb, CHUNK, RWKV_W), lambda bi, s: (bi, jnp.maximum(s - n_ctx, 0), 0)),
        pl.BlockSpec((nb, CHUNK, RWKV_W),
                     lambda bi, s: (bi, jnp.minimum(n_chunks - 1 - s, n_lat - 1), 0)),
    ]
    out_shape = [jax.ShapeDtypeStruct((b, s_lat, RWKV_W), F32)] * 2
    return pl.pallas_call(
        _scan_kernel,
        grid=(b // nb, n_chunks),
        in_specs=in_specs,
        out_specs=out_specs,
        out_shape=out_shape,
        scratch_shapes=[pltpu.VMEM((nb, 2, N_PAIRS, PAIR, PAIR), F32)],
        compiler_params=pltpu.CompilerParams(dimension_semantics=("arbitrary", "arbitrary"),
                                             vmem_limit_bytes=VMEM_LIMIT),
        name="scan",
    )(at, bt, kt, rt, v, plast, at, bt, kt, rt, v, plast)


def _mix_kernel(dft_ref, z_ref, yf_ref, yb_ref, bonus_ref, gate_ref, x_ref, mod_ref,
                gng_ref, gnb_ref, fg_ref, wout_ref, n2g_ref, ones_ref, x1_ref, h2_ref, *, fscale):
    seq = z_ref.shape[1]
    yfour = (jnp.dot(dft_ref[:, :seq], z_ref[0, :, :FOURIER_W], preferred_element_type=F32)
             + jnp.dot(dft_ref[:, seq:], z_ref[0, :, FOURIER_W:], preferred_element_type=F32))
    four = _rms(yfour * fscale, fg_ref[...])

    ones_blk = ones_ref[...]
    y = yf_ref[0] + yb_ref[0]
    mu = _dot_exact_rhs(y, ones_blk) * (1.0 / HEAD)
    dy = y - mu
    var = _dot_exact_rhs(dy * dy, ones_blk) * (1.0 / HEAD)
    yn = dy * lax.rsqrt(var + GN_EPS) * gng_ref[...] + gnb_ref[...]
    rw = (yn + bonus_ref[0]) * gate_ref[0]

    out = _bdot(four, wout_ref[:FOURIER_W, :]) + _bdot(rw, wout_ref[FOURIER_W:, :])
    x1 = x_ref[0] + mod_ref[0, 2:3, :] * out
    x1_ref[0] = x1
    h2 = _rms(x1, n2g_ref[...]) * (1.0 + mod_ref[0, 4:5, :]) + mod_ref[0, 3:4, :]
    h2_ref[0] = h2.astype(BF16)


def _mix_call(dft, z, yf, yb, bonus, gate, x, modx, gn_g, gn_b, f_g, w_out, norm2_g, ones_blk):
    b, s, d = x.shape
    tm = TILE

    def full(arr):
        nd = arr.ndim
        return pl.BlockSpec(arr.shape, lambda bi, m, _n=nd: (0,) * _n)

    tok = lambda w: pl.BlockSpec((1, tm, w), lambda bi, m: (bi, m, 0))
    in_specs = [
        pl.BlockSpec((tm, 2 * s), lambda bi, m: (m, 0)),
        pl.BlockSpec((1, s, 2 * FOURIER_W), lambda bi, m: (bi, 0, 0)),
        tok(RWKV_W), tok(RWKV_W), tok(RWKV_W), tok(RWKV_W), tok(d),
        pl.BlockSpec((1, 6, d), lambda bi, m: (bi, 0, 0)),
        full(gn_g), full(gn_b), full(f_g), full(w_out), full(norm2_g), full(ones_blk),
    ]
    fscale = 1.0 / math.sqrt(s * (FOURIER_W // FOURIER_GROUPS))
    return pl.pallas_call(
        functools.partial(_mix_kernel, fscale=fscale),
        grid=(b, s // tm),
        in_specs=in_specs,
        out_specs=[tok(d), tok(d)],
        out_shape=[jax.ShapeDtypeStruct((b, s, d), F32), jax.ShapeDtypeStruct((b, s, d), BF16)],
        compiler_params=pltpu.CompilerParams(dimension_semantics=("arbitrary", "arbitrary"),
                                             vmem_limit_bytes=VMEM_LIMIT),
        name="mix",
    )(dft, z, yf, yb, bonus, gate, x, modx, gn_g, gn_b, f_g, w_out, norm2_g, ones_blk)


def _conv3x3(u, w):
    t = u.shape[0]
    col = lax.broadcasted_iota(jnp.int32, u.shape, 0) & (GRID_W - 1)
    u_l = jnp.where(col > 0, pltpu.roll(u, 1, 0), 0.0)
    u_r = jnp.where(col < GRID_W - 1, pltpu.roll(u, t - 1, 0), 0.0)

    def rowmix(kh):
        return w[3 * kh:3 * kh + 1] * u_l + w[3 * kh + 1:3 * kh + 2] * u + w[3 * kh + 2:3 * kh + 3] * u_r

    zeros = jnp.zeros((GRID_W, u.shape[1]), F32)
    up = jnp.concatenate([zeros, rowmix(0)[:t - GRID_W]], axis=0)
    down = jnp.concatenate([rowmix(2)[GRID_W:], zeros], axis=0)
    return rowmix(1) + up + down


def _ffn_up_kernel(h2_ref, wg_ref, wv_ref, cwg_ref, cwv_ref, cbg_ref, cbv_ref, act_ref):
    h2 = h2_ref[0]
    ug = jnp.dot(h2, wg_ref[...], preferred_element_type=F32)
    uv = jnp.dot(h2, wv_ref[...], preferred_element_type=F32)
    cg = _conv3x3(ug, cwg_ref[...]) + cbg_ref[...]
    cv = _conv3x3(uv, cwv_ref[...]) + cbv_ref[...]
    act_ref[0] = (cg * _sigmoid(cg) * cv).astype(BF16)


def _ffn_up_call(h2, w_up, conv_w9, conv_b):
    b, s, d = h2.shape
    n_f = D_FF // FF_TILE
    in_specs = [
        pl.BlockSpec((1, s, d), lambda bi, f: (bi, 0, 0)),
        pl.BlockSpec((d, FF_TILE), lambda bi, f: (0, f)),
        pl.BlockSpec((d, FF_TILE), lambda bi, f: (0, n_f + f)),
        pl.BlockSpec((9, FF_TILE), lambda bi, f: (0, f)),
        pl.BlockSpec((9, FF_TILE), lambda bi, f: (0, n_f + f)),
        pl.BlockSpec((1, FF_TILE), lambda bi, f: (0, f)),
        pl.BlockSpec((1, FF_TILE), lambda bi, f: (0, n_f + f)),
    ]
    return pl.pallas_call(
        _ffn_up_kernel,
        grid=(b, n_f),
        in_specs=in_specs,
        out_specs=pl.BlockSpec((1, s, FF_TILE), lambda bi, f: (bi, 0, f)),
        out_shape=jax.ShapeDtypeStruct((b, s, D_FF), BF16),
        compiler_params=pltpu.CompilerParams(dimension_semantics=("arbitrary", "arbitrary"),
                                             vmem_limit_bytes=VMEM_LIMIT),
        name="ffn_up",
    )(h2, w_up, w_up, conv_w9, conv_w9, conv_b, conv_b)


def _ffn_down_kernel(act_ref, wd_ref, x1_ref, mod_ref, fg_ref, out_ref):
    y = jnp.dot(act_ref[0], wd_ref[...], preferred_element_type=F32)
    out_ref[0] = _rms(x1_ref[0] + mod_ref[0, 5:6, :] * y, fg_ref[...])


def _ffn_down_call(act, w_down, x1, modx, final_g):
    b, s, d = x1.shape
    tm = DOWN_TILE
    in_specs = [
        pl.BlockSpec((1, tm, D_FF), lambda bi, m: (bi, m, 0)),
        pl.BlockSpec((D_FF, d), lambda bi, m: (0, 0)),
        pl.BlockSpec((1, tm, d), lambda bi, m: (bi, m, 0)),
        pl.BlockSpec((1, 6, d), lambda bi, m: (bi, 0, 0)),
        pl.BlockSpec((1, d), lambda bi, m: (0, 0)),
    ]
    return pl.pallas_call(
        _ffn_down_kernel,
        grid=(b, s // tm),
        in_specs=in_specs,
        out_specs=pl.BlockSpec((1, tm, d), lambda bi, m: (bi, m, 0)),
        out_shape=jax.ShapeDtypeStruct((b, s, d), F32),
        compiler_params=pltpu.CompilerParams(dimension_semantics=("arbitrary", "arbitrary"),
                                             vmem_limit_bytes=VMEM_LIMIT),
        name="ffn_down",
    )(act, w_down, x1, modx, final_g)


@functools.lru_cache(maxsize=None)
def _constants(seq):
    gw = FOURIER_W // FOURIER_GROUPS
    nm = np.outer(np.arange(gw), np.arange(gw)) % gw
    ang = 2.0 * np.pi * nm / gw
    cs = np.zeros((FOURIER_W, 2 * FOURIER_W), np.float64)
    for gi in range(FOURIER_GROUPS):
        sl = slice(gi * gw, (gi + 1) * gw)
        cs[sl, sl] = np.cos(ang)
        cs[sl, FOURIER_W + gi * gw:FOURIER_W + (gi + 1) * gw] = np.sin(ang)
    kt = np.outer(np.arange(seq, dtype=np.int64), np.arange(seq, dtype=np.int64)) % seq
    ang_t = 2.0 * np.pi * kt / seq
    dft = np.concatenate([np.cos(ang_t), -np.sin(ang_t)], axis=1)
    head_id = np.arange(RWKV_W) // HEAD
    ones_blk = (head_id[:, None] == head_id[None, :]).astype(np.float64)
    ti = np.arange(TILE)
    same_chunk = (ti[:, None] // CHUNK) == (ti[None, :] // CHUNK)
    tri = np.stack([same_chunk & (ti[None, :] <= ti[:, None]),
                    same_chunk & (ti[None, :] >= ti[:, None])]).astype(np.float64)
    to_bf16 = lambda arr: np.asarray(arr, dtype=np.float32).astype(BF16)
    return (np.asarray(cs, dtype=np.float32), np.asarray(dft, dtype=np.float32),
            to_bf16(ones_blk), to_bf16(tri))


def _block_diag_dirs(w):
    z = jnp.zeros_like(w[0])
    return jnp.concatenate([jnp.concatenate([w[0], z], axis=1),
                            jnp.concatenate([z, w[1]], axis=1)], axis=0)


def kernel(x, c, ctx, c_ctx, ada_w, ada_b, norm1_g, norm2_g, w_in, rwkv_conv_w, decay_w0, decay_w2, iclr_a0, iclr_a2, gate_g2, k_k, k_a, r_k, gn_g, gn_b, fourier_g, w_out, ffn_w_up, ffn_conv_w, ffn_conv_b, ffn_w_down, final_g):
    b, s, d = x.shape
    assert ada_w.shape[0] == 1, "single-layer configuration"
    assert (b, s, d) == (c.shape[0], s, D_MODEL) and ctx.shape == (b, CTX_LEN, d)
    assert s % TILE == 0 and CTX_LEN == TILE
    cs, dft, ones_blk, tri = (jnp.asarray(t) for t in _constants(s))
    cs, dft = cs.astype(BF16), dft.astype(BF16)

    cc = jnp.concatenate([c, c_ctx[None, :], jnp.zeros((16 - b - 1, d), F32)], axis=0)
    mod = _mod_call(cc, ada_w[0], ada_b[0][None, :])
    modx = mod[:b].reshape(b, 6, d)
    modc = jnp.broadcast_to(mod[b].reshape(1, 6, d)[:, :2], (b, 2, d))
    modsel = jnp.stack([modc, modx[:, :2]], axis=1)

    row = lambda t: t.reshape(1, -1)
    at, bt, kt, rt, v, plast, bonus, gate, z = _prep_call(
        x, ctx, modsel, row(norm1_g[0]), w_in[0].astype(BF16), rwkv_conv_w[0],
        row(decay_w0[0]), _block_diag_dirs(decay_w2[0]).astype(BF16),
        row(iclr_a0[0]), _block_diag_dirs(iclr_a2[0]).astype(BF16),
        gate_g2[0].astype(BF16), row(k_k[0]), row(k_a[0]), row(r_k[0]), cs, ones_blk, tri)

    yf, yb = _scan_call(at, bt, kt, rt, v, plast, s)

    x1, h2 = _mix_call(dft, z, yf, yb, bonus, gate, x, modx, row(gn_g[0]), row(gn_b[0]),
                       row(fourier_g[0]), w_out[0].astype(BF16), row(norm2_g[0]), ones_blk)

    act = _ffn_up_call(h2, ffn_w_up[0].astype(BF16), ffn_conv_w[0].reshape(9, 2 * D_FF),
                       row(ffn_conv_b[0]))
    return _ffn_down_call(act, ffn_w_down[0].astype(BF16), x1, modx, row(final_g))
```

```python
import functools
import math

import numpy as np
import jax
import jax.numpy as jnp
from jax import lax
from jax.experimental import pallas as pl
from jax.experimental.pallas import tpu as pltpu

F32 = jnp.float32
BF16 = jnp.bfloat16

D_MODEL = 1024
CTX_LEN = 256
GRID_W = 64
FOURIER_W = 512
FOURIER_GROUPS = 8
RWKV_W = 512
HEAD = 64
N_HEADS = 8
DECAY_RANK = 64
ICLR_RANK = 64
GATE_RANK = 128
D_FF = 2816
NORM_EPS = 1e-6
GN_EPS = 64e-5
KK_EPS = 1e-12

CHUNK = 64
TILE = 256
CHUNKS_PER_TILE = TILE // CHUNK
PAIR = 2 * HEAD
N_PAIRS = RWKV_W // PAIR
QUAD_HEADS = 4
SCAN_BATCH = 2
FF_TILE = 256
FFN_ROWS = 128
DOWN_TILE = 512
VMEM_LIMIT = 56 * 1024 * 1024

RKV_LO = FOURIER_W
RKV_HI = FOURIER_W + 3 * RWKV_W
WD_LO = RKV_HI
AD_LO = WD_LO + 2 * DECAY_RANK
GD_LO = AD_LO + 2 * ICLR_RANK
PROJ_W = GD_LO + GATE_RANK


def _bdot(a, b):
    return jnp.dot(a.astype(BF16), b.astype(BF16), preferred_element_type=F32)


def _bdot_nt(a, b):
    return lax.dot_general(a.astype(BF16), b.astype(BF16), (((1,), (1,)), ((), ())),
                           preferred_element_type=F32)


def _bdot_tn(a, b):
    return lax.dot_general(a.astype(BF16), b.astype(BF16), (((0,), (0,)), ((), ())),
                           preferred_element_type=F32)


def _split_terms(a, terms):
    out, rem = [], a
    for _ in range(terms):
        hi = rem.astype(BF16)
        out.append(hi)
        rem = rem - hi.astype(F32)
    return out


def _dot_exact_rhs(a, b_exact, terms=2):
    acc = None
    for piece in _split_terms(a, terms):
        t = jnp.dot(piece, b_exact, preferred_element_type=F32)
        acc = t if acc is None else acc + t
    return acc


def _dot_exact_lhs(a_exact, b, terms=3):
    acc = None
    for piece in _split_terms(b, terms):
        t = jnp.dot(a_exact, piece, preferred_element_type=F32)
        acc = t if acc is None else acc + t
    return acc


def _sigmoid(x):
    return 1.0 / (1.0 + jnp.exp(-x))


def _rms(x, g):
    return x * lax.rsqrt(jnp.mean(x * x, axis=-1, keepdims=True) + NORM_EPS) * g


def _mod_kernel(c_ref, w_ref, b_ref, o_ref):
    cc = c_ref[...]
    s = cc * _sigmoid(cc)
    rows = s.shape[0]
    s_parts = jnp.concatenate(_split_terms(s, 3), axis=0)
    acc = b_ref[...]
    for wp in _split_terms(w_ref[...], 3):
        t = jnp.dot(s_parts, wp, preferred_element_type=F32)
        acc = acc + (t[:rows] + t[rows:2 * rows] + t[2 * rows:])
    o_ref[...] = acc


def _mod_call(cc, ada_w, ada_b):
    rows, d = cc.shape
    n = ada_w.shape[1]
    tn = 512
    return pl.pallas_call(
        _mod_kernel,
        grid=(n // tn,),
        in_specs=[pl.BlockSpec((rows, d), lambda i: (0, 0)),
                  pl.BlockSpec((d, tn), lambda i: (0, i)),
                  pl.BlockSpec((1, tn), lambda i: (0, i))],
        out_specs=pl.BlockSpec((rows, tn), lambda i: (0, i)),
        out_shape=jax.ShapeDtypeStruct((rows, n), F32),
        compiler_params=pltpu.CompilerParams(dimension_semantics=("arbitrary",),
                                             vmem_limit_bytes=VMEM_LIMIT),
        name="mod",
    )(cc, ada_w, ada_b)


def _prep_kernel(x_ref, xp_ref, xn_ref, ctx_ref, ms_ref, g1_ref, win_ref, cw_ref,
                 w0_ref, w2_ref, a0_ref, a2_ref, g2_ref, kk_ref, ka_ref, rk_ref,
                 cs_ref, ones_ref, tri_ref,
                 at_ref, bt_ref, kt_ref, rt_ref, v_ref, pl_ref, bonus_ref, gate_ref, z_ref):
    j = pl.program_id(1)
    n_j = pl.num_programs(1)
    is_ctx = j == 0
    shift = ms_ref[0, 0, 0:1, :]
    scale = ms_ref[0, 0, 1:2, :]
    g = g1_ref[...]

    def norm_mod(xr):
        return _rms(xr, g) * (1.0 + scale) + shift

    xin = jnp.where(is_ctx, ctx_ref[0], x_ref[0])
    h = norm_mod(xin).astype(BF16)
    u = jnp.dot(h, win_ref[...], preferred_element_type=F32)

    halo = jnp.concatenate([xp_ref[0], xn_ref[0]], axis=0)
    uh = jnp.dot(norm_mod(halo).astype(BF16), win_ref[:, RKV_LO:RKV_HI],
                 preferred_element_type=F32)
    prev_row = jnp.where(j >= 2, uh[7:8], 0.0)
    next_row = jnp.where(jnp.logical_and(j >= 1, j < n_j - 1), uh[8:9], 0.0)

    rkv = u[:, RKV_LO:RKV_HI]
    row = lax.broadcasted_iota(jnp.int32, rkv.shape, 0)
    u_m1 = jnp.where(row == 0, prev_row, pltpu.roll(rkv, 1, 0))
    u_p1 = jnp.where(row == TILE - 1, next_row, pltpu.roll(rkv, TILE - 1, 0))
    rkv = cw_ref[0:1, :] * u_m1 + cw_ref[1:2, :] * rkv + cw_ref[2:3, :] * u_p1
    r = rkv[:, 0:RWKV_W]
    k = rkv[:, RWKV_W:2 * RWKV_W]
    v = rkv[:, 2 * RWKV_W:3 * RWKV_W]

    wd = u[:, WD_LO:AD_LO]
    ad = u[:, AD_LO:GD_LO]
    gd = u[:, GD_LO:PROJ_W]
    w_lora = w0_ref[...] + _bdot(jnp.tanh(wd), w2_ref[...])
    lw = -math.exp(-0.5) * _sigmoid(w_lora)
    a = _sigmoid(a0_ref[...] + _bdot(ad, a2_ref[...]))

    ones_blk = ones_ref[...]
    kraw = k * kk_ref[...]
    kk = kraw * lax.rsqrt(_dot_exact_rhs(kraw * kraw, ones_blk) + KK_EPS)
    ka = ka_ref[...]
    kd = [k * (1.0 + (a[:, d * RWKV_W:(d + 1) * RWKV_W] - 1.0) * ka) for d in range(2)]

    v_ref[0] = v.astype(BF16)
    for d in range(2):
        lwd = lw[:, d * RWKV_W:(d + 1) * RWKV_W]
        ad_ = a[:, d * RWKV_W:(d + 1) * RWKV_W]
        c = _dot_exact_lhs(tri_ref[d], lwd)
        e_pos = jnp.exp(c)
        e_neg = jnp.exp(-c)
        e_prev = jnp.exp(c - lwd)
        at_ref[0, d] = (kk * e_prev).astype(BF16)
        bt_ref[0, d] = (kk * ad_ * e_neg).astype(BF16)
        kt_ref[0, d] = (kd[d] * e_neg).astype(BF16)
        rt_ref[0, d] = (r * e_pos).astype(BF16)
        for q in range(CHUNKS_PER_TILE):
            last = q * CHUNK + (CHUNK - 1 if d == 0 else 0)
            pl_ref[0, d, q] = e_pos[last:last + 1, :]

    @pl.when(j >= 1)
    def _():
        bs = _dot_exact_rhs(r * rk_ref[...] * (kd[0] + kd[1]), ones_blk)
        bonus_ref[0] = bs * v
        gate_ref[0] = _bdot(_sigmoid(gd), g2_ref[...])
        z_ref[0] = _bdot(u[:, 0:FOURIER_W], cs_ref[...]).astype(BF16)


def _prep_call(x, ctx, modsel, norm1_g, w_in, conv_w, w0, w2bd, a0, a2bd, g2, k_k, k_a, r_k,
               cs, ones_blk, tri):
    b, s, d = x.shape
    n_j = 1 + s // TILE
    n_chunks = (CTX_LEN + s) // CHUNK
    t_all = CTX_LEN + s
    c2 = 2 * RWKV_W

    def full(arr):
        nd = arr.ndim
        return pl.BlockSpec(arr.shape, lambda bi, j, _n=nd: (0,) * _n)

    rows8 = TILE // 8
    in_specs = [
        pl.BlockSpec((1, TILE, d), lambda bi, j: (bi, jnp.maximum(j - 1, 0), 0)),
        pl.BlockSpec((1, 8, d), lambda bi, j: (bi, jnp.maximum((j - 1) * rows8 - 1, 0), 0)),
        pl.BlockSpec((1, 8, d), lambda bi, j: (bi, jnp.minimum(j * rows8, s // 8 - 1), 0)),
        pl.BlockSpec((1, CTX_LEN, d), lambda bi, j: (bi, 0, 0)),
        pl.BlockSpec((1, 1, 2, d), lambda bi, j: (bi, jnp.minimum(j, 1), 0, 0)),
        full(norm1_g), full(w_in), full(conv_w), full(w0), full(w2bd), full(a0), full(a2bd),
        full(g2), full(k_k), full(k_a), full(r_k), full(cs), full(ones_blk), full(tri),
    ]
    lat = lambda bi, j: (bi, jnp.maximum(j - 1, 0), 0)
    out_specs = [
        pl.BlockSpec((1, 2, TILE, RWKV_W), lambda bi, j: (bi, 0, j, 0)),
        pl.BlockSpec((1, 2, TILE, RWKV_W), lambda bi, j: (bi, 0, j, 0)),
        pl.BlockSpec((1, 2, TILE, RWKV_W), lambda bi, j: (bi, 0, j, 0)),
        pl.BlockSpec((1, 2, TILE, RWKV_W), lambda bi, j: (bi, 0, j, 0)),
        pl.BlockSpec((1, TILE, RWKV_W), lambda bi, j: (bi, j, 0)),
        pl.BlockSpec((1, 2, CHUNKS_PER_TILE, 1, RWKV_W), lambda bi, j: (bi, 0, j, 0, 0)),
        pl.BlockSpec((1, TILE, RWKV_W), lat),
        pl.BlockSpec((1, TILE, RWKV_W), lat),
        pl.BlockSpec((1, TILE, 2 * FOURIER_W), lat),
    ]
    out_shape = [
        jax.ShapeDtypeStruct((b, 2, t_all, RWKV_W), BF16),
        jax.ShapeDtypeStruct((b, 2, t_all, RWKV_W), BF16),
        jax.ShapeDtypeStruct((b, 2, t_all, RWKV_W), BF16),
        jax.ShapeDtypeStruct((b, 2, t_all, RWKV_W), BF16),
        jax.ShapeDtypeStruct((b, t_all, RWKV_W), BF16),
        jax.ShapeDtypeStruct((b, 2, n_chunks, 1, RWKV_W), F32),
        jax.ShapeDtypeStruct((b, s, RWKV_W), F32),
        jax.ShapeDtypeStruct((b, s, RWKV_W), F32),
        jax.ShapeDtypeStruct((b, s, 2 * FOURIER_W), BF16),
    ]
    return pl.pallas_call(
        _prep_kernel,
        grid=(b, n_j),
        in_specs=in_specs,
        out_specs=out_specs,
        out_shape=out_shape,
        compiler_params=pltpu.CompilerParams(dimension_semantics=("arbitrary", "arbitrary"),
                                             vmem_limit_bytes=VMEM_LIMIT),
        name="prep",
    )(x, x, x, ctx, modsel, norm1_g, w_in, conv_w, w0, w2bd, a0, a2bd, g2, k_k, k_a, r_k,
      cs, ones_blk, tri)


def _quad_chunk(a, bm, km, r, v, plrow, g_states, strict, incl, eye, lane_lo, mask_bd, eye_row):
    n = PAIR

    def dup(xv):
        z = jnp.zeros_like(xv)
        return jnp.concatenate([jnp.where(lane_lo, xv, z), jnp.where(lane_lo, z, xv)], axis=0)

    def to_row(x2):
        return x2[:CHUNK] + x2[CHUNK:]

    def bdiag(xr):
        return jnp.where(mask_bd, jnp.concatenate([xr] * QUAD_HEADS, axis=0), jnp.zeros((), xr.dtype))

    pairs = []
    for p in range(2):
        sl = slice(p * n, (p + 1) * n)
        a2, b2, k2, r2, v2 = (dup(t[:, sl]) for t in (a, bm, km, r, v))
        rhs = jnp.concatenate([b2, k2], axis=0)
        s = _bdot_nt(jnp.concatenate([a2, r2], axis=0), rhs)
        pairs.append((sl, r2, v2, rhs, s))
    yield

    a_ab = jnp.concatenate([to_row(jnp.where(strict, s[:n, :n], 0.0)) for *_, s in pairs], axis=1)
    a_ak = jnp.concatenate([to_row(jnp.where(strict, s[:n, n:], 0.0)) for *_, s in pairs], axis=1)
    m_rbk = [jnp.concatenate([jnp.where(incl, s[n:, :n], 0.0), jnp.where(incl, s[n:, n:], 0.0)],
                             axis=1).astype(BF16) for *_, s in pairs]

    pk = (-a_ab).astype(BF16)
    t_row = jnp.where(eye_row, 1.0, 0.0) - a_ab
    av = _bdot(a_ak, bdiag(v))
    pk = jnp.dot(pk, bdiag(pk), preferred_element_type=F32).astype(BF16)
    yield
    for _ in range(int(math.log2(CHUNK)) - 2):
        both = jnp.dot(jnp.concatenate([pk, t_row.astype(BF16)], axis=0), bdiag(pk),
                       preferred_element_type=F32)
        yield
        pk = both[:CHUNK].astype(BF16)
        t_row = t_row + both[CHUNK:]
    t_row = t_row + jnp.dot(t_row.astype(BF16), bdiag(pk), preferred_element_type=F32)
    yield
    tx = _bdot(t_row, jnp.concatenate([bdiag(a), bdiag(av.astype(BF16))], axis=1))
    yield
    a_w = tx[:, :2 * n].astype(BF16)
    u_v = (-tx[:, 2 * n:]).astype(BF16)
    stage5 = []
    for (sl, r2, v2, rhs, _), m2 in zip(pairs, m_rbk):
        zmat = jnp.concatenate(
            [jnp.concatenate([dup(a_w[:, sl]), dup(u_v[:, sl])], axis=1),
             jnp.concatenate([jnp.zeros_like(v2), v2], axis=1)], axis=0)
        rhs_end = rhs.astype(F32) * plrow[:, sl]
        lhs = jnp.concatenate([rhs_end.T.astype(BF16), m2], axis=0)
        stage5.append(jnp.dot(lhs, zmat, preferred_element_type=F32))
    yield
    ys, g_new = [], []
    for (sl, r2, v2, rhs, _), o5, g_state in zip(pairs, stage5, g_states):
        phi = jnp.where(eye, plrow[:, sl], 0.0) - o5[:n, :n]
        r_y = r2.astype(F32) - o5[n:, :n]
        o6 = _bdot(jnp.concatenate([r_y, phi], axis=0), g_state)
        ys.append(to_row(o6[:n] + o5[n:, n:]))
        g_new.append(o6[n:] + o5[:n, n:])
    return jnp.concatenate(ys, axis=1), g_new


def _run_interleaved(gens):
    results = [None] * len(gens)
    active = list(range(len(gens)))
    while active:
        for i in list(active):
            try:
                next(gens[i])
            except StopIteration as stop:
                results[i] = stop.value
                active.remove(i)
    return results


def _scan_kernel(atf, btf, ktf, rtf, vf, plf, atb, btb, ktb, rtb, vb, plb,
                 yf_ref, yb_ref, g_ref):
    s = pl.program_id(1)

    @pl.when(s == 0)
    def _():
        g_ref[...] = jnp.zeros_like(g_ref)

    ri = lax.broadcasted_iota(jnp.int32, (PAIR, PAIR), 0)
    ci = lax.broadcasted_iota(jnp.int32, (PAIR, PAIR), 1)
    same = (ri >= HEAD) == (ci >= HEAD)
    ti = ri & (HEAD - 1)
    tj = ci & (HEAD - 1)
    eye = ri == ci
    lane_lo = lax.broadcasted_iota(jnp.int32, (CHUNK, PAIR), 1) < HEAD
    masks = [(same & (tj < ti), same & (tj <= ti)),
             (same & (tj > ti), same & (tj >= ti))]
    quad_w = QUAD_HEADS * HEAD
    rq = lax.broadcasted_iota(jnp.int32, (quad_w, quad_w), 0)
    cq = lax.broadcasted_iota(jnp.int32, (quad_w, quad_w), 1)
    mask_bd = (rq // HEAD) == (cq // HEAD)
    eye_row = (lax.broadcasted_iota(jnp.int32, (CHUNK, quad_w), 1) & (HEAD - 1)
               ) == lax.broadcasted_iota(jnp.int32, (CHUNK, quad_w), 0)
    refs = [(atf, btf, ktf, rtf, vf, plf, yf_ref), (atb, btb, ktb, rtb, vb, plb, yb_ref)]
    gens, dests = [], []
    for bi in range(yf_ref.shape[0]):
        for d in range(2):
            at, bt, kt, rt, vv, plr, y_ref = refs[d]
            strict, incl = masks[d]
            for q in range(RWKV_W // quad_w):
                sl = slice(q * quad_w, (q + 1) * quad_w)
                gens.append(_quad_chunk(at[bi, 0, :, sl], bt[bi, 0, :, sl], kt[bi, 0, :, sl],
                                        rt[bi, 0, :, sl], vv[bi, :, sl], plr[bi, 0, 0, :, sl],
                                        [g_ref[bi, d, 2 * q], g_ref[bi, d, 2 * q + 1]],
                                        strict, incl, eye, lane_lo, mask_bd, eye_row))
                dests.append((y_ref, bi, d, q, sl))
    for (y, g_new), (y_ref, bi, d, q, sl) in zip(_run_interleaved(gens), dests):
        g_ref[bi, d, 2 * q] = g_new[0]
        g_ref[bi, d, 2 * q + 1] = g_new[1]
        y_ref[bi, :, sl] = y


def _scan_call(at, bt, kt, rt, v, plast, s_lat):
    b = at.shape[0]
    n_chunks = at.shape[2] // CHUNK
    n_ctx = CTX_LEN // CHUNK
    n_lat = s_lat // CHUNK

    def fwd_c(s):
        return s

    def bwd_c(s):
        return jnp.where(s < n_ctx, n_ctx - 1 - s, n_chunks + n_ctx - 1 - s)

    nb = SCAN_BATCH

    def dir_spec(d, cfun):
        return pl.BlockSpec((nb, 1, CHUNK, RWKV_W), lambda bi, s: (bi, d, cfun(s), 0))

    def v_spec(cfun):
        return pl.BlockSpec((nb, CHUNK, RWKV_W), lambda bi, s: (bi, cfun(s), 0))

    def pl_spec(d, cfun):
        return pl.BlockSpec((nb, 1, 1, 1, RWKV_W), lambda bi, s: (bi, d, cfun(s), 0, 0))

    in_specs = ([dir_spec(0, fwd_c)] * 4 + [v_spec(fwd_c), pl_spec(0, fwd_c)]
                + [dir_spec(1, bwd_c)] * 4 + [v_spec(bwd_c), pl_spec(1, bwd_c)])
    out_specs = [
        pl.BlockSpec((nb, CHUNK, RWKV_W), lambda bi, s: (bi, jnp.maximum(s - n_ctx, 0), 0)),
        pl.BlockSpec((nb, CHUNK, RWKV_W),
                     lambda bi, s: (bi, jnp.minimum(n_chunks - 1 - s, n_lat - 1), 0)),
    ]
    out_shape = [jax.ShapeDtypeStruct((b, s_lat, RWKV_W), F32)] * 2
    return pl.pallas_call(
        _scan_kernel,
        grid=(b // nb, n_chunks),
        in_specs=in_specs,
        out_specs=out_specs,
        out_shape=out_shape,
        scratch_shapes=[pltpu.VMEM((nb, 2, N_PAIRS, PAIR, PAIR), F32)],
        compiler_params=pltpu.CompilerParams(dimension_semantics=("arbitrary", "arbitrary"),
                                             vmem_limit_bytes=VMEM_LIMIT),
        name="scan",
    )(at, bt, kt, rt, v, plast, at, bt, kt, rt, v, plast)


def _mix_kernel(dft_ref, z_ref, yf_ref, yb_ref, bonus_ref, gate_ref, x_ref, mod_ref,
                gng_ref, gnb_ref, fg_ref, wout_ref, n2g_ref, ones_ref, x1_ref, h2_ref, *, fscale):
    seq = z_ref.shape[1]
    yfour = (jnp.dot(dft_ref[:, :seq], z_ref[0, :, :FOURIER_W], preferred_element_type=F32)
             + jnp.dot(dft_ref[:, seq:], z_ref[0, :, FOURIER_W:], preferred_element_type=F32))
    four = _rms(yfour * fscale, fg_ref[...])

    ones_blk = ones_ref[...]
    y = yf_ref[0] + yb_ref[0]
    mu = _dot_exact_rhs(y, ones_blk) * (1.0 / HEAD)
    dy = y - mu
    var = _dot_exact_rhs(dy * dy, ones_blk) * (1.0 / HEAD)
    yn = dy * lax.rsqrt(var + GN_EPS) * gng_ref[...] + gnb_ref[...]
    rw = (yn + bonus_ref[0]) * gate_ref[0]

    out = _bdot(four, wout_ref[:FOURIER_W, :]) + _bdot(rw, wout_ref[FOURIER_W:, :])
    x1 = x_ref[0] + mod_ref[0, 2:3, :] * out
    x1_ref[0] = x1
    h2 = _rms(x1, n2g_ref[...]) * (1.0 + mod_ref[0, 4:5, :]) + mod_ref[0, 3:4, :]
    h2_ref[0] = h2.astype(BF16)


def _mix_call(dft, z, yf, yb, bonus, gate, x, modx, gn_g, gn_b, f_g, w_out, norm2_g, ones_blk):
    b, s, d = x.shape
    tm = TILE

    def full(arr):
        nd = arr.ndim
        return pl.BlockSpec(arr.shape, lambda bi, m, _n=nd: (0,) * _n)

    tok = lambda w: pl.BlockSpec((1, tm, w), lambda bi, m: (bi, m, 0))
    in_specs = [
        pl.BlockSpec((tm, 2 * s), lambda bi, m: (m, 0)),
        pl.BlockSpec((1, s, 2 * FOURIER_W), lambda bi, m: (bi, 0, 0)),
        tok(RWKV_W), tok(RWKV_W), tok(RWKV_W), tok(RWKV_W), tok(d),
        pl.BlockSpec((1, 6, d), lambda bi, m: (bi, 0, 0)),
        full(gn_g), full(gn_b), full(f_g), full(w_out), full(norm2_g), full(ones_blk),
    ]
    fscale = 1.0 / math.sqrt(s * (FOURIER_W // FOURIER_GROUPS))
    return pl.pallas_call(
        functools.partial(_mix_kernel, fscale=fscale),
        grid=(b, s // tm),
        in_specs=in_specs,
        out_specs=[tok(d), tok(d)],
        out_shape=[jax.ShapeDtypeStruct((b, s, d), F32), jax.ShapeDtypeStruct((b, s, d), BF16)],
        compiler_params=pltpu.CompilerParams(dimension_semantics=("arbitrary", "arbitrary"),
                                             vmem_limit_bytes=VMEM_LIMIT),
        name="mix",
    )(dft, z, yf, yb, bonus, gate, x, modx, gn_g, gn_b, f_g, w_out, norm2_g, ones_blk)


def _ffn_up_kernel(h2_ref, w_ref, cw_ref, cb_ref, act_ref):
    t = h2_ref.shape[1]
    rows = FFN_ROWS
    n_blk = t // rows
    w = w_ref[...]
    cw = cw_ref[...].astype(BF16)
    cb = cb_ref[...].astype(BF16)
    col = lax.broadcasted_iota(jnp.int32, (rows, 2 * FF_TILE), 0) & (GRID_W - 1)
    has_left = col > 0
    has_right = col < GRID_W - 1
    zeros = jnp.zeros((GRID_W, 2 * FF_TILE), BF16)

    def row_mixes(i):
        u = jnp.dot(h2_ref[0, i * rows:(i + 1) * rows, :], w, preferred_element_type=F32)
        u_l = jnp.where(has_left, pltpu.roll(u, 1, 0), 0.0).astype(BF16)
        u_r = jnp.where(has_right, pltpu.roll(u, rows - 1, 0), 0.0).astype(BF16)
        u_c = u.astype(BF16)
        return [cw[3 * kh:3 * kh + 1] * u_l + cw[3 * kh + 1:3 * kh + 2] * u_c
                + cw[3 * kh + 2:3 * kh + 3] * u_r for kh in range(3)]

    def finish(i, prev, cur, nxt):
        above = zeros if prev is None else prev[0][rows - GRID_W:]
        below = zeros if nxt is None else nxt[2][:GRID_W]
        c = (cur[1] + jnp.concatenate([above, cur[0][:rows - GRID_W]], axis=0)
             + jnp.concatenate([cur[2][GRID_W:], below], axis=0) + cb)
        cg = c[:, :FF_TILE]
        act_ref[0, i * rows:(i + 1) * rows, :] = cg * _sigmoid(cg) * c[:, FF_TILE:]

    mixes = [None] * (n_blk + 1)
    for i in range(n_blk):
        mixes[i] = row_mixes(i)
        if i >= 1:
            finish(i - 1, mixes[i - 2] if i >= 2 else None, mixes[i - 1], mixes[i])
    finish(n_blk - 1, mixes[n_blk - 2] if n_blk >= 2 else None, mixes[n_blk - 1], None)


def _gate_value_tiles(w):
    lead = w.shape[:-1]
    n_f = D_FF // FF_TILE
    w = w.reshape(lead + (2, n_f, FF_TILE))
    return jnp.swapaxes(w, -3, -2).reshape(lead + (2 * D_FF,))


def _ffn_up_call(h2, w_up, conv_w9, conv_b):
    b, s, d = h2.shape
    n_f = D_FF // FF_TILE
    in_specs = [
        pl.BlockSpec((1, s, d), lambda bi, f: (bi, 0, 0)),
        pl.BlockSpec((d, 2 * FF_TILE), lambda bi, f: (0, f)),
        pl.BlockSpec((9, 2 * FF_TILE), lambda bi, f: (0, f)),
        pl.BlockSpec((1, 2 * FF_TILE), lambda bi, f: (0, f)),
    ]
    return pl.pallas_call(
        _ffn_up_kernel,
        grid=(b, n_f),
        in_specs=in_specs,
        out_specs=pl.BlockSpec((1, s, FF_TILE), lambda bi, f: (bi, 0, f)),
        out_shape=jax.ShapeDtypeStruct((b, s, D_FF), BF16),
        compiler_params=pltpu.CompilerParams(dimension_semantics=("arbitrary", "arbitrary"),
                                             vmem_limit_bytes=VMEM_LIMIT),
        name="ffn_up",
    )(h2, _gate_value_tiles(w_up), _gate_value_tiles(conv_w9), _gate_value_tiles(conv_b))


def _ffn_down_kernel(act_ref, wd_ref, x1_ref, mod_ref, fg_ref, out_ref):
    y = jnp.dot(act_ref[0], wd_ref[...], preferred_element_type=F32)
    out_ref[0] = _rms(x1_ref[0] + mod_ref[0, 5:6, :] * y, fg_ref[...])


def _ffn_down_call(act, w_down, x1, modx, final_g):
    b, s, d = x1.shape
    tm = DOWN_TILE
    in_specs = [
        pl.BlockSpec((1, tm, D_FF), lambda bi, m: (bi, m, 0)),
        pl.BlockSpec((D_FF, d), lambda bi, m: (0, 0)),
        pl.BlockSpec((1, tm, d), lambda bi, m: (bi, m, 0)),
        pl.BlockSpec((1, 6, d), lambda bi, m: (bi, 0, 0)),
        pl.BlockSpec((1, d), lambda bi, m: (0, 0)),
    ]
    return pl.pallas_call(
        _ffn_down_kernel,
        grid=(b, s // tm),
        in_specs=in_specs,
        out_specs=pl.BlockSpec((1, tm, d), lambda bi, m: (bi, m, 0)),
        out_shape=jax.ShapeDtypeStruct((b, s, d), F32),
        compiler_params=pltpu.CompilerParams(dimension_semantics=("arbitrary", "arbitrary"),
                                             vmem_limit_bytes=VMEM_LIMIT),
        name="ffn_down",
    )(act, w_down, x1, modx, final_g)


@functools.lru_cache(maxsize=None)
def _constants(seq):
    gw = FOURIER_W // FOURIER_GROUPS
    nm = np.outer(np.arange(gw), np.arange(gw)) % gw
    ang = 2.0 * np.pi * nm / gw
    cs = np.zeros((FOURIER_W, 2 * FOURIER_W), np.float64)
    for gi in range(FOURIER_GROUPS):
        sl = slice(gi * gw, (gi + 1) * gw)
        cs[sl, sl] = np.cos(ang)
        cs[sl, FOURIER_W + gi * gw:FOURIER_W + (gi + 1) * gw] = np.sin(ang)
    kt = np.outer(np.arange(seq, dtype=np.int64), np.arange(seq, dtype=np.int64)) % seq
    ang_t = 2.0 * np.pi * kt / seq
    dft = np.concatenate([np.cos(ang_t), -np.sin(ang_t)], axis=1)
    head_id = np.arange(RWKV_W) // HEAD
    ones_blk = (head_id[:, None] == head_id[None, :]).astype(np.float64)
    ti = np.arange(TILE)
    same_chunk = (ti[:, None] // CHUNK) == (ti[None, :] // CHUNK)
    tri = np.stack([same_chunk & (ti[None, :] <= ti[:, None]),
                    same_chunk & (ti[None, :] >= ti[:, None])]).astype(np.float64)
    to_bf16 = lambda arr: np.asarray(arr, dtype=np.float32).astype(BF16)
    return (np.asarray(cs, dtype=np.float32), np.asarray(dft, dtype=np.float32),
            to_bf16(ones_blk), to_bf16(tri))


def _block_diag_dirs(w):
    z = jnp.zeros_like(w[0])
    return jnp.concatenate([jnp.concatenate([w[0], z], axis=1),
                            jnp.concatenate([z, w[1]], axis=1)], axis=0)


def kernel(x, c, ctx, c_ctx, ada_w, ada_b, norm1_g, norm2_g, w_in, rwkv_conv_w, decay_w0, decay_w2, iclr_a0, iclr_a2, gate_g2, k_k, k_a, r_k, gn_g, gn_b, fourier_g, w_out, ffn_w_up, ffn_conv_w, ffn_conv_b, ffn_w_down, final_g):
    b, s, d = x.shape
    assert ada_w.shape[0] == 1, "single-layer configuration"
    assert (b, s, d) == (c.shape[0], s, D_MODEL) and ctx.shape == (b, CTX_LEN, d)
    assert s % TILE == 0 and CTX_LEN == TILE
    cs, dft, ones_blk, tri = (jnp.asarray(t) for t in _constants(s))
    cs, dft = cs.astype(BF16), dft.astype(BF16)

    cc = jnp.concatenate([c, c_ctx[None, :], jnp.zeros((16 - b - 1, d), F32)], axis=0)
    mod = _mod_call(cc, ada_w[0], ada_b[0][None, :])
    modx = mod[:b].reshape(b, 6, d)
    modc = jnp.broadcast_to(mod[b].reshape(1, 6, d)[:, :2], (b, 2, d))
    modsel = jnp.stack([modc, modx[:, :2]], axis=1)

    row = lambda t: t.reshape(1, -1)
    at, bt, kt, rt, v, plast, bonus, gate, z = _prep_call(
        x, ctx, modsel, row(norm1_g[0]), w_in[0].astype(BF16), rwkv_conv_w[0],
        row(decay_w0[0]), _block_diag_dirs(decay_w2[0]).astype(BF16),
        row(iclr_a0[0]), _block_diag_dirs(iclr_a2[0]).astype(BF16),
        gate_g2[0].astype(BF16), row(k_k[0]), row(k_a[0]), row(r_k[0]), cs, ones_blk, tri)

    yf, yb = _scan_call(at, bt, kt, rt, v, plast, s)

    x1, h2 = _mix_call(dft, z, yf, yb, bonus, gate, x, modx, row(gn_g[0]), row(gn_b[0]),
                       row(fourier_g[0]), w_out[0].astype(BF16), row(norm2_g[0]), ones_blk)

    act = _ffn_up_call(h2, ffn_w_up[0].astype(BF16), ffn_conv_w[0].reshape(9, 2 * D_FF),
                       row(ffn_conv_b[0]))
    return _ffn_down_call(act, ffn_w_down[0].astype(BF16), x1, modx, row(final_g))
```

```python
import functools
import math

import numpy as np
import jax
import jax.numpy as jnp
from jax import lax
from jax.experimental import pallas as pl
from jax.experimental.pallas import tpu as pltpu

F32 = jnp.float32
BF16 = jnp.bfloat16

D_MODEL = 1024
CTX_LEN = 256
GRID_W = 64
FOURIER_W = 512
FOURIER_GROUPS = 8
RWKV_W = 512
HEAD = 64
N_HEADS = 8
DECAY_RANK = 64
ICLR_RANK = 64
GATE_RANK = 128
D_FF = 2816
NORM_EPS = 1e-6
GN_EPS = 64e-5
KK_EPS = 1e-12

CHUNK = 64
TILE = 256
CHUNKS_PER_TILE = TILE // CHUNK
PREP_BATCH = 2
PAIR = 2 * HEAD
N_PAIRS = RWKV_W // PAIR
QUAD_HEADS = 4
SCAN_BATCH = 2
MOD_TILE = 1536
FF_TILE = 256
FFN_ROWS = 128
DOWN_TILE = 512
VMEM_LIMIT = 56 * 1024 * 1024
HEAD_SUM_TERMS = 1
CUMSUM_TERMS = 2

RKV_LO = FOURIER_W
RKV_HI = FOURIER_W + 3 * RWKV_W
WD_LO = RKV_HI
AD_LO = WD_LO + 2 * DECAY_RANK
GD_LO = AD_LO + 2 * ICLR_RANK
PROJ_W = GD_LO + GATE_RANK


def _bdot(a, b):
    return jnp.dot(a.astype(BF16), b.astype(BF16), preferred_element_type=F32)


def _bdot_nt(a, b):
    return lax.dot_general(a.astype(BF16), b.astype(BF16), (((1,), (1,)), ((), ())),
                           preferred_element_type=F32)


def _bdot_tn(a, b):
    return lax.dot_general(a.astype(BF16), b.astype(BF16), (((0,), (0,)), ((), ())),
                           preferred_element_type=F32)


def _split_terms(a, terms):
    out, rem = [], a
    for _ in range(terms):
        hi = rem.astype(BF16)
        out.append(hi)
        rem = rem - hi.astype(F32)
    return out


def _dot_exact_rhs(a, b_exact, terms=HEAD_SUM_TERMS):
    acc = None
    for piece in _split_terms(a, terms):
        t = jnp.dot(piece, b_exact, preferred_element_type=F32)
        acc = t if acc is None else acc + t
    return acc


def _dot_exact_lhs(a_exact, b, terms=CUMSUM_TERMS):
    acc = None
    for piece in _split_terms(b, terms):
        t = jnp.dot(a_exact, piece, preferred_element_type=F32)
        acc = t if acc is None else acc + t
    return acc


def _sigmoid(x):
    return 1.0 / (1.0 + jnp.exp(-x))


def _rms(x, g):
    return x * lax.rsqrt(jnp.mean(x * x, axis=-1, keepdims=True) + NORM_EPS) * g


def _mod_kernel(c_ref, w_ref, b_ref, o_ref):
    cc = c_ref[...]
    s = cc * _sigmoid(cc)
    rows = s.shape[0]
    s_parts = jnp.concatenate(_split_terms(s, 3), axis=0)
    acc = b_ref[...]
    for wp in _split_terms(w_ref[...], 3):
        t = jnp.dot(s_parts, wp, preferred_element_type=F32)
        acc = acc + (t[:rows] + t[rows:2 * rows] + t[2 * rows:])
    o_ref[...] = acc


def _mod_call(cc, ada_w, ada_b):
    rows, d = cc.shape
    n = ada_w.shape[1]
    tn = MOD_TILE
    return pl.pallas_call(
        _mod_kernel,
        grid=(n // tn,),
        in_specs=[pl.BlockSpec((rows, d), lambda i: (0, 0)),
                  pl.BlockSpec((d, tn), lambda i: (0, i)),
                  pl.BlockSpec((1, tn), lambda i: (0, i))],
        out_specs=pl.BlockSpec((rows, tn), lambda i: (0, i)),
        out_shape=jax.ShapeDtypeStruct((rows, n), F32),
        compiler_params=pltpu.CompilerParams(dimension_semantics=("arbitrary",),
                                             vmem_limit_bytes=VMEM_LIMIT),
        name="mod",
    )(cc, ada_w, ada_b)


def _prep_kernel(x_ref, xp_ref, xn_ref, ctx_ref, ms_ref, g1_ref, win_ref, cw_ref,
                 w0_ref, w2_ref, a0_ref, a2_ref, g2_ref, kk_ref, ka_ref, rk_ref,
                 cs_ref, ones_ref, tri_ref,
                 at_ref, bt_ref, kt_ref, rt_ref, v_ref, pl_ref, bonus_ref, gate_ref, z_ref):
    j = pl.program_id(1)
    refs = (x_ref, xp_ref, xn_ref, ctx_ref, ms_ref, g1_ref, win_ref, cw_ref,
            w0_ref, w2_ref, a0_ref, a2_ref, kk_ref, ka_ref, ones_ref, tri_ref,
            at_ref, bt_ref, kt_ref, rt_ref, v_ref, pl_ref)
    tiles = _run_interleaved([_prep_tile(bi, *refs) for bi in range(x_ref.shape[0])])

    @pl.when(j >= 1)
    def _():
        ones_blk = ones_ref[...]
        for bi, (u_four, r, v, gd, kd_sum) in enumerate(tiles):
            bs = _dot_exact_rhs(r * rk_ref[...] * kd_sum, ones_blk)
            bonus_ref[bi] = bs * v
            gate_ref[bi] = _bdot(_sigmoid(gd), g2_ref[...])
            z_ref[bi] = _bdot(u_four, cs_ref[...]).astype(BF16)


def _prep_tile(bi, x_ref, xp_ref, xn_ref, ctx_ref, ms_ref, g1_ref, win_ref, cw_ref,
               w0_ref, w2_ref, a0_ref, a2_ref, kk_ref, ka_ref, ones_ref, tri_ref,
               at_ref, bt_ref, kt_ref, rt_ref, v_ref, pl_ref):
    j = pl.program_id(1)
    n_j = pl.num_programs(1)
    is_ctx = j == 0
    shift = ms_ref[bi, 0, 0:1, :]
    scale = ms_ref[bi, 0, 1:2, :]
    g = g1_ref[...]

    def norm_mod(xr):
        return _rms(xr, g) * (1.0 + scale) + shift

    xin = jnp.where(is_ctx, ctx_ref[bi], x_ref[bi])
    h = norm_mod(xin).astype(BF16)
    u = jnp.dot(h, win_ref[...], preferred_element_type=F32)

    halo = jnp.concatenate([xp_ref[bi], xn_ref[bi]], axis=0)
    uh = jnp.dot(norm_mod(halo).astype(BF16), win_ref[:, RKV_LO:RKV_HI],
                 preferred_element_type=F32)
    yield
    prev_row = jnp.where(j >= 2, uh[7:8], 0.0)
    next_row = jnp.where(jnp.logical_and(j >= 1, j < n_j - 1), uh[8:9], 0.0)

    rkv = u[:, RKV_LO:RKV_HI]
    row = lax.broadcasted_iota(jnp.int32, rkv.shape, 0)
    u_m1 = jnp.where(row == 0, prev_row, pltpu.roll(rkv, 1, 0))
    u_p1 = jnp.where(row == TILE - 1, next_row, pltpu.roll(rkv, TILE - 1, 0))
    rkv = cw_ref[0:1, :] * u_m1 + cw_ref[1:2, :] * rkv + cw_ref[2:3, :] * u_p1
    r = rkv[:, 0:RWKV_W]
    k = rkv[:, RWKV_W:2 * RWKV_W]
    v = rkv[:, 2 * RWKV_W:3 * RWKV_W]

    wd = u[:, WD_LO:AD_LO]
    ad = u[:, AD_LO:GD_LO]
    gd = u[:, GD_LO:PROJ_W]
    w_lora = w0_ref[...] + _bdot(jnp.tanh(wd), w2_ref[...])
    lw = -math.exp(-0.5) * _sigmoid(w_lora)
    a = _sigmoid(a0_ref[...] + _bdot(ad, a2_ref[...]))

    ones_blk = ones_ref[...]
    kraw = k * kk_ref[...]
    kk = kraw * lax.rsqrt(_dot_exact_rhs(kraw * kraw, ones_blk) + KK_EPS)
    ka = ka_ref[...]
    kd = [k * (1.0 + (a[:, d * RWKV_W:(d + 1) * RWKV_W] - 1.0) * ka) for d in range(2)]

    v_ref[bi] = v.astype(BF16)
    for d in range(2):
        lwd = lw[:, d * RWKV_W:(d + 1) * RWKV_W]
        ad_ = a[:, d * RWKV_W:(d + 1) * RWKV_W]
        c = _dot_exact_lhs(tri_ref[d], lwd)
        e_pos = jnp.exp(c)
        e_neg = jnp.exp(-c)
        e_prev = jnp.exp(c - lwd)
        at_ref[bi, d] = (kk * e_prev).astype(BF16)
        bt_ref[bi, d] = (kk * ad_ * e_neg).astype(BF16)
        kt_ref[bi, d] = (kd[d] * e_neg).astype(BF16)
        rt_ref[bi, d] = (r * e_pos).astype(BF16)
        for q in range(CHUNKS_PER_TILE):
            last = q * CHUNK + (CHUNK - 1 if d == 0 else 0)
            pl_ref[bi, d, q] = e_pos[last:last + 1, :]
    return u[:, 0:FOURIER_W], r, v, gd, kd[0] + kd[1]


def _prep_call(x, ctx, modsel, norm1_g, w_in, conv_w, w0, w2bd, a0, a2bd, g2, k_k, k_a, r_k,
               cs, ones_blk, tri):
    b, s, d = x.shape
    n_j = 1 + s // TILE
    n_chunks = (CTX_LEN + s) // CHUNK
    t_all = CTX_LEN + s
    c2 = 2 * RWKV_W

    def full(arr):
        nd = arr.ndim
        return pl.BlockSpec(arr.shape, lambda bi, j, _n=nd: (0,) * _n)

    rows8 = TILE // 8
    nb = PREP_BATCH
    in_specs = [
        pl.BlockSpec((nb, TILE, d), lambda bi, j: (bi, jnp.maximum(j - 1, 0), 0)),
        pl.BlockSpec((nb, 8, d), lambda bi, j: (bi, jnp.maximum((j - 1) * rows8 - 1, 0), 0)),
        pl.BlockSpec((nb, 8, d), lambda bi, j: (bi, jnp.minimum(j * rows8, s // 8 - 1), 0)),
        pl.BlockSpec((nb, CTX_LEN, d), lambda bi, j: (bi, 0, 0)),
        pl.BlockSpec((nb, 1, 2, d), lambda bi, j: (bi, jnp.minimum(j, 1), 0, 0)),
        full(norm1_g), full(w_in), full(conv_w), full(w0), full(w2bd), full(a0), full(a2bd),
        full(g2), full(k_k), full(k_a), full(r_k), full(cs), full(ones_blk), full(tri),
    ]
    lat = lambda bi, j: (bi, jnp.maximum(j - 1, 0), 0)
    out_specs = [
        pl.BlockSpec((nb, 2, TILE, RWKV_W), lambda bi, j: (bi, 0, j, 0)),
        pl.BlockSpec((nb, 2, TILE, RWKV_W), lambda bi, j: (bi, 0, j, 0)),
        pl.BlockSpec((nb, 2, TILE, RWKV_W), lambda bi, j: (bi, 0, j, 0)),
        pl.BlockSpec((nb, 2, TILE, RWKV_W), lambda bi, j: (bi, 0, j, 0)),
        pl.BlockSpec((nb, TILE, RWKV_W), lambda bi, j: (bi, j, 0)),
        pl.BlockSpec((nb, 2, CHUNKS_PER_TILE, 1, RWKV_W), lambda bi, j: (bi, 0, j, 0, 0)),
        pl.BlockSpec((nb, TILE, RWKV_W), lat),
        pl.BlockSpec((nb, TILE, RWKV_W), lat),
        pl.BlockSpec((nb, TILE, 2 * FOURIER_W), lat),
    ]
    out_shape = [
        jax.ShapeDtypeStruct((b, 2, t_all, RWKV_W), BF16),
        jax.ShapeDtypeStruct((b, 2, t_all, RWKV_W), BF16),
        jax.ShapeDtypeStruct((b, 2, t_all, RWKV_W), BF16),
        jax.ShapeDtypeStruct((b, 2, t_all, RWKV_W), BF16),
        jax.ShapeDtypeStruct((b, t_all, RWKV_W), BF16),
        jax.ShapeDtypeStruct((b, 2, n_chunks, 1, RWKV_W), F32),
        jax.ShapeDtypeStruct((b, s, RWKV_W), F32),
        jax.ShapeDtypeStruct((b, s, RWKV_W), F32),
        jax.ShapeDtypeStruct((b, s, 2 * FOURIER_W), BF16),
    ]
    return pl.pallas_call(
        _prep_kernel,
        grid=(b // nb, n_j),
        in_specs=in_specs,
        out_specs=out_specs,
        out_shape=out_shape,
        compiler_params=pltpu.CompilerParams(dimension_semantics=("arbitrary", "arbitrary"),
                                             vmem_limit_bytes=VMEM_LIMIT),
        name="prep",
    )(x, x, x, ctx, modsel, norm1_g, w_in, conv_w, w0, w2bd, a0, a2bd, g2, k_k, k_a, r_k,
      cs, ones_blk, tri)


def _quad_chunk(a, bm, km, r, v, plrow, g_states, strict, incl, eye, lane_lo, mask_bd, eye_row):
    n = PAIR

    def dup(xv):
        z = jnp.zeros_like(xv)
        return jnp.concatenate([jnp.where(lane_lo, xv, z), jnp.where(lane_lo, z, xv)], axis=0)

    def to_row(x2):
        return x2[:CHUNK] + x2[CHUNK:]

    def bdiag(xr):
        return jnp.where(mask_bd, jnp.concatenate([xr] * QUAD_HEADS, axis=0), jnp.zeros((), xr.dtype))

    pairs = []
    for p in range(2):
        sl = slice(p * n, (p + 1) * n)
        a2, b2, k2, r2, v2 = (dup(t[:, sl]) for t in (a, bm, km, r, v))
        rhs = jnp.concatenate([b2, k2], axis=0)
        s = _bdot_nt(jnp.concatenate([a2, r2], axis=0), rhs)
        pairs.append((sl, r2, v2, rhs, s))
    yield

    a_ab = jnp.concatenate([to_row(jnp.where(strict, s[:n, :n], 0.0)) for *_, s in pairs], axis=1)
    a_ak = jnp.concatenate([to_row(jnp.where(strict, s[:n, n:], 0.0)) for *_, s in pairs], axis=1)
    m_rbk = [jnp.concatenate([jnp.where(incl, s[n:, :n], 0.0), jnp.where(incl, s[n:, n:], 0.0)],
                             axis=1).astype(BF16) for *_, s in pairs]

    pk = (-a_ab).astype(BF16)
    t_row = jnp.where(eye_row, 1.0, 0.0) - a_ab
    av = _bdot(a_ak, bdiag(v))
    pk = jnp.dot(pk, bdiag(pk), preferred_element_type=F32).astype(BF16)
    yield
    for _ in range(int(math.log2(CHUNK)) - 2):
        both = jnp.dot(jnp.concatenate([pk, t_row.astype(BF16)], axis=0), bdiag(pk),
                       preferred_element_type=F32)
        yield
        pk = both[:CHUNK].astype(BF16)
        t_row = t_row + both[CHUNK:]
    t_row = t_row + jnp.dot(t_row.astype(BF16), bdiag(pk), preferred_element_type=F32)
    yield
    tx = _bdot(t_row, jnp.concatenate([bdiag(a), bdiag(av.astype(BF16))], axis=1))
    yield
    a_w = tx[:, :2 * n].astype(BF16)
    u_v = (-tx[:, 2 * n:]).astype(BF16)
    stage5 = []
    for (sl, r2, v2, rhs, _), m2 in zip(pairs, m_rbk):
        zmat = jnp.concatenate(
            [jnp.concatenate([dup(a_w[:, sl]), dup(u_v[:, sl])], axis=1),
             jnp.concatenate([jnp.zeros_like(v2), v2], axis=1)], axis=0)
        rhs_end = rhs.astype(F32) * plrow[:, sl]
        lhs = jnp.concatenate([rhs_end.T.astype(BF16), m2], axis=0)
        stage5.append(jnp.dot(lhs, zmat, preferred_element_type=F32))
    yield
    ys, g_new = [], []
    for (sl, r2, v2, rhs, _), o5, g_state in zip(pairs, stage5, g_states):
        phi = jnp.where(eye, plrow[:, sl], 0.0) - o5[:n, :n]
        r_y = r2.astype(F32) - o5[n:, :n]
        o6 = _bdot(jnp.concatenate([r_y, phi], axis=0), g_state)
        ys.append(to_row(o6[:n] + o5[n:, n:]))
        g_new.append(o6[n:] + o5[:n, n:])
    return jnp.concatenate(ys, axis=1), g_new


def _run_interleaved(gens):
    results = [None] * len(gens)
    active = list(range(len(gens)))
    while active:
        for i in list(active):
            try:
                next(gens[i])
            except StopIteration as stop:
                results[i] = stop.value
                active.remove(i)
    return results


def _scan_kernel(atf, btf, ktf, rtf, vf, plf, atb, btb, ktb, rtb, vb, plb,
                 yf_ref, yb_ref, g_ref):
    s = pl.program_id(1)

    @pl.when(s == 0)
    def _():
        g_ref[...] = jnp.zeros_like(g_ref)

    ri = lax.broadcasted_iota(jnp.int32, (PAIR, PAIR), 0)
    ci = lax.broadcasted_iota(jnp.int32, (PAIR, PAIR), 1)
    same = (ri >= HEAD) == (ci >= HEAD)
    ti = ri & (HEAD - 1)
    tj = ci & (HEAD - 1)
    eye = ri == ci
    lane_lo = lax.broadcasted_iota(jnp.int32, (CHUNK, PAIR), 1) < HEAD
    masks = [(same & (tj < ti), same & (tj <= ti)),
             (same & (tj > ti), same & (tj >= ti))]
    quad_w = QUAD_HEADS * HEAD
    rq = lax.broadcasted_iota(jnp.int32, (quad_w, quad_w), 0)
    cq = lax.broadcasted_iota(jnp.int32, (quad_w, quad_w), 1)
    mask_bd = (rq // HEAD) == (cq // HEAD)
    eye_row = (lax.broadcasted_iota(jnp.int32, (CHUNK, quad_w), 1) & (HEAD - 1)
               ) == lax.broadcasted_iota(jnp.int32, (CHUNK, quad_w), 0)
    refs = [(atf, btf, ktf, rtf, vf, plf, yf_ref), (atb, btb, ktb, rtb, vb, plb, yb_ref)]
    gens, dests = [], []
    for bi in range(yf_ref.shape[0]):
        for d in range(2):
            at, bt, kt, rt, vv, plr, y_ref = refs[d]
            strict, incl = masks[d]
            for q in range(RWKV_W // quad_w):
                sl = slice(q * quad_w, (q + 1) * quad_w)
                gens.append(_quad_chunk(at[bi, 0, :, sl], bt[bi, 0, :, sl], kt[bi, 0, :, sl],
                                        rt[bi, 0, :, sl], vv[bi, :, sl], plr[bi, 0, 0, :, sl],
                                        [g_ref[bi, d, 2 * q], g_ref[bi, d, 2 * q + 1]],
                                        strict, incl, eye, lane_lo, mask_bd, eye_row))
                dests.append((y_ref, bi, d, q, sl))
    for (y, g_new), (y_ref, bi, d, q, sl) in zip(_run_interleaved(gens), dests):
        g_ref[bi, d, 2 * q] = g_new[0]
        g_ref[bi, d, 2 * q + 1] = g_new[1]
        y_ref[bi, :, sl] = y


def _scan_call(at, bt, kt, rt, v, plast, s_lat):
    b = at.shape[0]
    n_chunks = at.shape[2] // CHUNK
    n_ctx = CTX_LEN // CHUNK
    n_lat = s_lat // CHUNK

    def fwd_c(s):
        return s

    def bwd_c(s):
        return jnp.where(s < n_ctx, n_ctx - 1 - s, n_chunks + n_ctx - 1 - s)

    nb = SCAN_BATCH

    def dir_spec(d, cfun):
        return pl.BlockSpec((nb, 1, CHUNK, RWKV_W), lambda bi, s: (bi, d, cfun(s), 0))

    def v_spec(cfun):
        return pl.BlockSpec((nb, CHUNK, RWKV_W), lambda bi, s: (bi, cfun(s), 0))

    def pl_spec(d, cfun):
        return pl.BlockSpec((nb, 1, 1, 1, RWKV_W), lambda bi, s: (bi, d, cfun(s), 0, 0))

    in_specs = ([dir_spec(0, fwd_c)] * 4 + [v_spec(fwd_c), pl_spec(0, fwd_c)]
                + [dir_spec(1, bwd_c)] * 4 + [v_spec(bwd_c), pl_spec(1, bwd_c)])
    out_specs = [
        pl.BlockSpec((nb, CHUNK, RWKV_W), lambda bi, s: (bi, jnp.maximum(s - n_ctx, 0), 0)),
        pl.BlockSpec((nb, CHUNK, RWKV_W),
                     lambda bi, s: (bi, jnp.minimum(n_chunks - 1 - s, n_lat - 1), 0)),
    ]
    out_shape = [jax.ShapeDtypeStruct((b, s_lat, RWKV_W), F32)] * 2
    return pl.pallas_call(
        _scan_kernel,
        grid=(b // nb, n_chunks),
        in_specs=in_specs,
        out_specs=out_specs,
        out_shape=out_shape,
        scratch_shapes=[pltpu.VMEM((nb, 2, N_PAIRS, PAIR, PAIR), F32)],
        compiler_params=pltpu.CompilerParams(dimension_semantics=("arbitrary", "arbitrary"),
                                             vmem_limit_bytes=VMEM_LIMIT),
        name="scan",
    )(at, bt, kt, rt, v, plast, at, bt, kt, rt, v, plast)


def _mix_kernel(dft_ref, z_ref, yf_ref, yb_ref, bonus_ref, gate_ref, x_ref, mod_ref,
                gng_ref, gnb_ref, fg_ref, wout_ref, n2g_ref, ones_ref, x1_ref, h2_ref, *, fscale):
    seq = z_ref.shape[1]
    yfour = (jnp.dot(dft_ref[:, :seq], z_ref[0, :, :FOURIER_W], preferred_element_type=F32)
             + jnp.dot(dft_ref[:, seq:], z_ref[0, :, FOURIER_W:], preferred_element_type=F32))
    four = _rms(yfour * fscale, fg_ref[...])

    ones_blk = ones_ref[...]
    y = yf_ref[0] + yb_ref[0]
    mu = _dot_exact_rhs(y, ones_blk) * (1.0 / HEAD)
    dy = y - mu
    var = _dot_exact_rhs(dy * dy, ones_blk) * (1.0 / HEAD)
    yn = dy * lax.rsqrt(var + GN_EPS) * gng_ref[...] + gnb_ref[...]
    rw = (yn + bonus_ref[0]) * gate_ref[0]

    out = _bdot(four, wout_ref[:FOURIER_W, :]) + _bdot(rw, wout_ref[FOURIER_W:, :])
    x1 = x_ref[0] + mod_ref[0, 2:3, :] * out
    x1_ref[0] = x1
    h2 = _rms(x1, n2g_ref[...]) * (1.0 + mod_ref[0, 4:5, :]) + mod_ref[0, 3:4, :]
    h2_ref[0] = h2.astype(BF16)


def _mix_call(dft, z, yf, yb, bonus, gate, x, modx, gn_g, gn_b, f_g, w_out, norm2_g, ones_blk):
    b, s, d = x.shape
    tm = TILE

    def full(arr):
        nd = arr.ndim
        return pl.BlockSpec(arr.shape, lambda bi, m, _n=nd: (0,) * _n)

    tok = lambda w: pl.BlockSpec((1, tm, w), lambda bi, m: (bi, m, 0))
    in_specs = [
        pl.BlockSpec((tm, 2 * s), lambda bi, m: (m, 0)),
        pl.BlockSpec((1, s, 2 * FOURIER_W), lambda bi, m: (bi, 0, 0)),
        tok(RWKV_W), tok(RWKV_W), tok(RWKV_W), tok(RWKV_W), tok(d),
        pl.BlockSpec((1, 6, d), lambda bi, m: (bi, 0, 0)),
        full(gn_g), full(gn_b), full(f_g), full(w_out), full(norm2_g), full(ones_blk),
    ]
    fscale = 1.0 / math.sqrt(s * (FOURIER_W // FOURIER_GROUPS))
    return pl.pallas_call(
        functools.partial(_mix_kernel, fscale=fscale),
        grid=(b, s // tm),
        in_specs=in_specs,
        out_specs=[tok(d), tok(d)],
        out_shape=[jax.ShapeDtypeStruct((b, s, d), F32), jax.ShapeDtypeStruct((b, s, d), BF16)],
        compiler_params=pltpu.CompilerParams(dimension_semantics=("arbitrary", "arbitrary"),
                                             vmem_limit_bytes=VMEM_LIMIT),
        name="mix",
    )(dft, z, yf, yb, bonus, gate, x, modx, gn_g, gn_b, f_g, w_out, norm2_g, ones_blk)


def _ffn_up_kernel(h2_ref, wg_ref, wv_ref, cwg_ref, cwv_ref, cbg_ref, cbv_ref, act_ref):
    t = h2_ref.shape[1]
    rows = FFN_ROWS
    n_blk = t // rows
    w = jnp.concatenate([wg_ref[...], wv_ref[...]], axis=1)
    cw = jnp.concatenate([cwg_ref[...], cwv_ref[...]], axis=1).astype(BF16)
    cb = jnp.concatenate([cbg_ref[...], cbv_ref[...]], axis=1).astype(BF16)
    col = lax.broadcasted_iota(jnp.int32, (rows, 2 * FF_TILE), 0) & (GRID_W - 1)
    has_left = col > 0
    has_right = col < GRID_W - 1
    zeros = jnp.zeros((GRID_W, 2 * FF_TILE), BF16)

    def row_mixes(i):
        u = jnp.dot(h2_ref[0, i * rows:(i + 1) * rows, :], w, preferred_element_type=F32)
        u_l = jnp.where(has_left, pltpu.roll(u, 1, 0), 0.0).astype(BF16)
        u_r = jnp.where(has_right, pltpu.roll(u, rows - 1, 0), 0.0).astype(BF16)
        u_c = u.astype(BF16)
        return [cw[3 * kh:3 * kh + 1] * u_l + cw[3 * kh + 1:3 * kh + 2] * u_c
                + cw[3 * kh + 2:3 * kh + 3] * u_r for kh in range(3)]

    def finish(i, prev, cur, nxt):
        above = zeros if prev is None else prev[0][rows - GRID_W:]
        below = zeros if nxt is None else nxt[2][:GRID_W]
        c = (cur[1] + jnp.concatenate([above, cur[0][:rows - GRID_W]], axis=0)
             + jnp.concatenate([cur[2][GRID_W:], below], axis=0) + cb)
        cg = c[:, :FF_TILE]
        act_ref[0, i * rows:(i + 1) * rows, :] = cg * _sigmoid(cg) * c[:, FF_TILE:]

    mixes = [None] * (n_blk + 1)
    for i in range(n_blk):
        mixes[i] = row_mixes(i)
        if i >= 1:
            finish(i - 1, mixes[i - 2] if i >= 2 else None, mixes[i - 1], mixes[i])
    finish(n_blk - 1, mixes[n_blk - 2] if n_blk >= 2 else None, mixes[n_blk - 1], None)


def _ffn_up_call(h2, w_up, conv_w9, conv_b):
    b, s, d = h2.shape
    n_f = D_FF // FF_TILE
    in_specs = [
        pl.BlockSpec((1, s, d), lambda bi, f: (bi, 0, 0)),
        pl.BlockSpec((d, FF_TILE), lambda bi, f: (0, f)),
        pl.BlockSpec((d, FF_TILE), lambda bi, f: (0, n_f + f)),
        pl.BlockSpec((9, FF_TILE), lambda bi, f: (0, f)),
        pl.BlockSpec((9, FF_TILE), lambda bi, f: (0, n_f + f)),
        pl.BlockSpec((1, FF_TILE), lambda bi, f: (0, f)),
        pl.BlockSpec((1, FF_TILE), lambda bi, f: (0, n_f + f)),
    ]
    return pl.pallas_call(
        _ffn_up_kernel,
        grid=(b, n_f),
        in_specs=in_specs,
        out_specs=pl.BlockSpec((1, s, FF_TILE), lambda bi, f: (bi, 0, f)),
        out_shape=jax.ShapeDtypeStruct((b, s, D_FF), BF16),
        compiler_params=pltpu.CompilerParams(dimension_semantics=("arbitrary", "arbitrary"),
                                             vmem_limit_bytes=VMEM_LIMIT),
        name="ffn_up",
    )(h2, w_up, w_up, conv_w9, conv_w9, conv_b, conv_b)


def _ffn_down_kernel(act_ref, wd_ref, x1_ref, mod_ref, fg_ref, out_ref):
    y = jnp.dot(act_ref[0], wd_ref[...], preferred_element_type=F32)
    out_ref[0] = _rms(x1_ref[0] + mod_ref[0, 5:6, :] * y, fg_ref[...])


def _ffn_down_call(act, w_down, x1, modx, final_g):
    b, s, d = x1.shape
    tm = DOWN_TILE
    in_specs = [
        pl.BlockSpec((1, tm, D_FF), lambda bi, m: (bi, m, 0)),
        pl.BlockSpec((D_FF, d), lambda bi, m: (0, 0)),
        pl.BlockSpec((1, tm, d), lambda bi, m: (bi, m, 0)),
        pl.BlockSpec((1, 6, d), lambda bi, m: (bi, 0, 0)),
        pl.BlockSpec((1, d), lambda bi, m: (0, 0)),
    ]
    return pl.pallas_call(
        _ffn_down_kernel,
        grid=(b, s // tm),
        in_specs=in_specs,
        out_specs=pl.BlockSpec((1, tm, d), lambda bi, m: (bi, m, 0)),
        out_shape=jax.ShapeDtypeStruct((b, s, d), F32),
        compiler_params=pltpu.CompilerParams(dimension_semantics=("arbitrary", "arbitrary"),
                                             vmem_limit_bytes=VMEM_LIMIT),
        name="ffn_down",
    )(act, w_down, x1, modx, final_g)


@functools.lru_cache(maxsize=None)
def _constants(seq):
    gw = FOURIER_W // FOURIER_GROUPS
    nm = np.outer(np.arange(gw), np.arange(gw)) % gw
    ang = 2.0 * np.pi * nm / gw
    cs = np.zeros((FOURIER_W, 2 * FOURIER_W), np.float64)
    for gi in range(FOURIER_GROUPS):
        sl = slice(gi * gw, (gi + 1) * gw)
        cs[sl, sl] = np.cos(ang)
        cs[sl, FOURIER_W + gi * gw:FOURIER_W + (gi + 1) * gw] = np.sin(ang)
    kt = np.outer(np.arange(seq, dtype=np.int64), np.arange(seq, dtype=np.int64)) % seq
    ang_t = 2.0 * np.pi * kt / seq
    dft = np.concatenate([np.cos(ang_t), -np.sin(ang_t)], axis=1)
    head_id = np.arange(RWKV_W) // HEAD
    ones_blk = (head_id[:, None] == head_id[None, :]).astype(np.float64)
    ti = np.arange(TILE)
    same_chunk = (ti[:, None] // CHUNK) == (ti[None, :] // CHUNK)
    tri = np.stack([same_chunk & (ti[None, :] <= ti[:, None]),
                    same_chunk & (ti[None, :] >= ti[:, None])]).astype(np.float64)
    to_bf16 = lambda arr: np.asarray(arr, dtype=np.float32).astype(BF16)
    return (np.asarray(cs, dtype=np.float32), np.asarray(dft, dtype=np.float32),
            to_bf16(ones_blk), to_bf16(tri))


def _block_diag_dirs(w):
    z = jnp.zeros_like(w[0])
    return jnp.concatenate([jnp.concatenate([w[0], z], axis=1),
                            jnp.concatenate([z, w[1]], axis=1)], axis=0)


def kernel(x, c, ctx, c_ctx, ada_w, ada_b, norm1_g, norm2_g, w_in, rwkv_conv_w, decay_w0, decay_w2, iclr_a0, iclr_a2, gate_g2, k_k, k_a, r_k, gn_g, gn_b, fourier_g, w_out, ffn_w_up, ffn_conv_w, ffn_conv_b, ffn_w_down, final_g):
    b, s, d = x.shape
    assert ada_w.shape[0] == 1, "single-layer configuration"
    assert (b, s, d) == (c.shape[0], s, D_MODEL) and ctx.shape == (b, CTX_LEN, d)
    assert s % TILE == 0 and CTX_LEN == TILE
    cs, dft, ones_blk, tri = (jnp.asarray(t) for t in _constants(s))
    cs, dft = cs.astype(BF16), dft.astype(BF16)

    cc = jnp.concatenate([c, c_ctx[None, :], jnp.zeros((16 - b - 1, d), F32)], axis=0)
    mod = _mod_call(cc, ada_w[0], ada_b[0][None, :])
    modx = mod[:b].reshape(b, 6, d)
    modc = jnp.broadcast_to(mod[b].reshape(1, 6, d)[:, :2], (b, 2, d))
    modsel = jnp.stack([modc, modx[:, :2]], axis=1)

    row = lambda t: t.reshape(1, -1)
    at, bt, kt, rt, v, plast, bonus, gate, z = _prep_call(
        x, ctx, modsel, row(norm1_g[0]), w_in[0].astype(BF16), rwkv_conv_w[0],
        row(decay_w0[0]), _block_diag_dirs(decay_w2[0]).astype(BF16),
        row(iclr_a0[0]), _block_diag_dirs(iclr_a2[0]).astype(BF16),
        gate_g2[0].astype(BF16), row(k_k[0]), row(k_a[0]), row(r_k[0]), cs, ones_blk, tri)

    yf, yb = _scan_call(at, bt, kt, rt, v, plast, s)

    x1, h2 = _mix_call(dft, z, yf, yb, bonus, gate, x, modx, row(gn_g[0]), row(gn_b[0]),
                       row(fourier_g[0]), w_out[0].astype(BF16), row(norm2_g[0]), ones_blk)

    act = _ffn_up_call(h2, ffn_w_up[0].astype(BF16), ffn_conv_w[0].reshape(9, 2 * D_FF),
                       row(ffn_conv_b[0]))
    return _ffn_down_call(act, ffn_w_down[0].astype(BF16), x1, modx, row(final_g))
```

```python
import functools
import math

import numpy as np
import jax
import jax.numpy as jnp
from jax import lax
from jax.experimental import pallas as pl
from jax.experimental.pallas import tpu as pltpu

F32 = jnp.float32
BF16 = jnp.bfloat16

D_MODEL = 1024
CTX_LEN = 256
GRID_W = 64
FOURIER_W = 512
FOURIER_GROUPS = 8
RWKV_W = 512
HEAD = 64
N_HEADS = 8
DECAY_RANK = 64
ICLR_RANK = 64
GATE_RANK = 128
D_FF = 2816
NORM_EPS = 1e-6
GN_EPS = 64e-5
KK_EPS = 1e-12

CHUNK = 64
TILE = 256
CHUNKS_PER_TILE = TILE // CHUNK
PREP_BATCH = 2
PAIR = 2 * HEAD
N_PAIRS = RWKV_W // PAIR
QUAD_HEADS = 4
SCAN_BATCH = 4
MOD_TILE = 1536
FF_TILE = 1408
FFN_ROWS = 128
DOWN_TILE = 512
VMEM_LIMIT = 56 * 1024 * 1024
HEAD_SUM_TERMS = 1
CUMSUM_TERMS = 2

RKV_LO = FOURIER_W
RKV_HI = FOURIER_W + 3 * RWKV_W
WD_LO = RKV_HI
AD_LO = WD_LO + 2 * DECAY_RANK
GD_LO = AD_LO + 2 * ICLR_RANK
PROJ_W = GD_LO + GATE_RANK


def _bdot(a, b):
    return jnp.dot(a.astype(BF16), b.astype(BF16), preferred_element_type=F32)


def _bdot_nt(a, b):
    return lax.dot_general(a.astype(BF16), b.astype(BF16), (((1,), (1,)), ((), ())),
                           preferred_element_type=F32)


def _bdot_tn(a, b):
    return lax.dot_general(a.astype(BF16), b.astype(BF16), (((0,), (0,)), ((), ())),
                           preferred_element_type=F32)


def _split_terms(a, terms):
    out, rem = [], a
    for _ in range(terms):
        hi = rem.astype(BF16)
        out.append(hi)
        rem = rem - hi.astype(F32)
    return out


def _dot_exact_rhs(a, b_exact, terms=HEAD_SUM_TERMS):
    acc = None
    for piece in _split_terms(a, terms):
        t = jnp.dot(piece, b_exact, preferred_element_type=F32)
        acc = t if acc is None else acc + t
    return acc


def _dot_exact_lhs(a_exact, b, terms=CUMSUM_TERMS):
    acc = None
    for piece in _split_terms(b, terms):
        t = jnp.dot(a_exact, piece, preferred_element_type=F32)
        acc = t if acc is None else acc + t
    return acc


def _sigmoid(x):
    return 1.0 / (1.0 + jnp.exp(-x))


def _rms(x, g):
    return x * lax.rsqrt(jnp.mean(x * x, axis=-1, keepdims=True) + NORM_EPS) * g


def _mod_kernel(c_ref, w_ref, b_ref, o_ref):
    cc = c_ref[...]
    s = cc * _sigmoid(cc)
    rows = s.shape[0]
    s_parts = jnp.concatenate(_split_terms(s, 3), axis=0)
    acc = b_ref[...]
    for wp in _split_terms(w_ref[...], 3):
        t = jnp.dot(s_parts, wp, preferred_element_type=F32)
        acc = acc + (t[:rows] + t[rows:2 * rows] + t[2 * rows:])
    o_ref[...] = acc


def _mod_call(cc, ada_w, ada_b):
    rows, d = cc.shape
    n = ada_w.shape[1]
    tn = MOD_TILE
    return pl.pallas_call(
        _mod_kernel,
        grid=(n // tn,),
        in_specs=[pl.BlockSpec((rows, d), lambda i: (0, 0)),
                  pl.BlockSpec((d, tn), lambda i: (0, i)),
                  pl.BlockSpec((1, tn), lambda i: (0, i))],
        out_specs=pl.BlockSpec((rows, tn), lambda i: (0, i)),
        out_shape=jax.ShapeDtypeStruct((rows, n), F32),
        compiler_params=pltpu.CompilerParams(dimension_semantics=("arbitrary",),
                                             vmem_limit_bytes=VMEM_LIMIT),
        name="mod",
    )(cc, ada_w, ada_b)


def _prep_kernel(x_ref, xp_ref, xn_ref, ctx_ref, ms_ref, g1_ref, win_ref, cw_ref,
                 w0_ref, w2_ref, a0_ref, a2_ref, g2_ref, kk_ref, ka_ref, rk_ref,
                 cs_ref, ones_ref, tri_ref,
                 at_ref, bt_ref, kt_ref, rt_ref, v_ref, pl_ref, bonus_ref, gate_ref, z_ref):
    j = pl.program_id(1)
    refs = (x_ref, xp_ref, xn_ref, ctx_ref, ms_ref, g1_ref, win_ref, cw_ref,
            w0_ref, w2_ref, a0_ref, a2_ref, kk_ref, ka_ref, ones_ref, tri_ref,
            at_ref, bt_ref, kt_ref, rt_ref, v_ref, pl_ref)
    tiles = _run_interleaved([_prep_tile(bi, *refs) for bi in range(x_ref.shape[0])])

    @pl.when(j >= 1)
    def _():
        ones_blk = ones_ref[...]
        for bi, (u_four, r, v, gd, kd_sum) in enumerate(tiles):
            bs = _dot_exact_rhs(r * rk_ref[...] * kd_sum, ones_blk)
            bonus_ref[bi] = bs * v
            gate_ref[bi] = _bdot(_sigmoid(gd), g2_ref[...])
            z_ref[bi] = _bdot(u_four, cs_ref[...]).astype(BF16)


def _prep_tile(bi, x_ref, xp_ref, xn_ref, ctx_ref, ms_ref, g1_ref, win_ref, cw_ref,
               w0_ref, w2_ref, a0_ref, a2_ref, kk_ref, ka_ref, ones_ref, tri_ref,
               at_ref, bt_ref, kt_ref, rt_ref, v_ref, pl_ref):
    j = pl.program_id(1)
    n_j = pl.num_programs(1)
    is_ctx = j == 0
    shift = ms_ref[bi, 0, 0:1, :]
    scale = ms_ref[bi, 0, 1:2, :]
    g = g1_ref[...]

    def norm_mod(xr):
        return _rms(xr, g) * (1.0 + scale) + shift

    xin = jnp.where(is_ctx, ctx_ref[bi], x_ref[bi])
    h = norm_mod(xin).astype(BF16)
    u = jnp.dot(h, win_ref[...], preferred_element_type=F32)

    halo = jnp.concatenate([xp_ref[bi], xn_ref[bi]], axis=0)
    uh = jnp.dot(norm_mod(halo).astype(BF16), win_ref[:, RKV_LO:RKV_HI],
                 preferred_element_type=F32)
    yield
    prev_row = jnp.where(j >= 2, uh[7:8], 0.0)
    next_row = jnp.where(jnp.logical_and(j >= 1, j < n_j - 1), uh[8:9], 0.0)

    rkv = u[:, RKV_LO:RKV_HI]
    row = lax.broadcasted_iota(jnp.int32, rkv.shape, 0)
    u_m1 = jnp.where(row == 0, prev_row, pltpu.roll(rkv, 1, 0))
    u_p1 = jnp.where(row == TILE - 1, next_row, pltpu.roll(rkv, TILE - 1, 0))
    rkv = cw_ref[0:1, :] * u_m1 + cw_ref[1:2, :] * rkv + cw_ref[2:3, :] * u_p1
    r = rkv[:, 0:RWKV_W]
    k = rkv[:, RWKV_W:2 * RWKV_W]
    v = rkv[:, 2 * RWKV_W:3 * RWKV_W]

    wd = u[:, WD_LO:AD_LO]
    ad = u[:, AD_LO:GD_LO]
    gd = u[:, GD_LO:PROJ_W]
    w_lora = w0_ref[...] + _bdot(jnp.tanh(wd), w2_ref[...])
    lw = -math.exp(-0.5) * _sigmoid(w_lora)
    a = _sigmoid(a0_ref[...] + _bdot(ad, a2_ref[...]))

    ones_blk = ones_ref[...]
    kraw = k * kk_ref[...]
    kk = kraw * lax.rsqrt(_dot_exact_rhs(kraw * kraw, ones_blk) + KK_EPS)
    ka = ka_ref[...]
    kd = [k * (1.0 + (a[:, d * RWKV_W:(d + 1) * RWKV_W] - 1.0) * ka) for d in range(2)]

    v_ref[bi] = v.astype(BF16)
    for d in range(2):
        lwd = lw[:, d * RWKV_W:(d + 1) * RWKV_W]
        ad_ = a[:, d * RWKV_W:(d + 1) * RWKV_W]
        c = _dot_exact_lhs(tri_ref[d], lwd)
        e_pos = jnp.exp(c)
        e_neg = jnp.exp(-c)
        e_prev = jnp.exp(c - lwd)
        at_ref[bi, d] = (kk * e_prev).astype(BF16)
        bt_ref[bi, d] = (kk * ad_ * e_neg).astype(BF16)
        kt_ref[bi, d] = (kd[d] * e_neg).astype(BF16)
        rt_ref[bi, d] = (r * e_pos).astype(BF16)
        for q in range(CHUNKS_PER_TILE):
            last = q * CHUNK + (CHUNK - 1 if d == 0 else 0)
            pl_ref[bi, d, q] = e_pos[last:last + 1, :]
    return u[:, 0:FOURIER_W], r, v, gd, kd[0] + kd[1]


def _prep_call(x, ctx, modsel, norm1_g, w_in, conv_w, w0, w2bd, a0, a2bd, g2, k_k, k_a, r_k,
               cs, ones_blk, tri):
    b, s, d = x.shape
    n_j = 1 + s // TILE
    n_chunks = (CTX_LEN + s) // CHUNK
    t_all = CTX_LEN + s
    c2 = 2 * RWKV_W

    def full(arr):
        nd = arr.ndim
        return pl.BlockSpec(arr.shape, lambda bi, j, _n=nd: (0,) * _n)

    rows8 = TILE // 8
    nb = PREP_BATCH
    in_specs = [
        pl.BlockSpec((nb, TILE, d), lambda bi, j: (bi, jnp.maximum(j - 1, 0), 0)),
        pl.BlockSpec((nb, 8, d), lambda bi, j: (bi, jnp.maximum((j - 1) * rows8 - 1, 0), 0)),
        pl.BlockSpec((nb, 8, d), lambda bi, j: (bi, jnp.minimum(j * rows8, s // 8 - 1), 0)),
        pl.BlockSpec((nb, CTX_LEN, d), lambda bi, j: (bi, 0, 0)),
        pl.BlockSpec((nb, 1, 2, d), lambda bi, j: (bi, jnp.minimum(j, 1), 0, 0)),
        full(norm1_g), full(w_in), full(conv_w), full(w0), full(w2bd), full(a0), full(a2bd),
        full(g2), full(k_k), full(k_a), full(r_k), full(cs), full(ones_blk), full(tri),
    ]
    lat = lambda bi, j: (bi, jnp.maximum(j - 1, 0), 0)
    out_specs = [
        pl.BlockSpec((nb, 2, TILE, RWKV_W), lambda bi, j: (bi, 0, j, 0)),
        pl.BlockSpec((nb, 2, TILE, RWKV_W), lambda bi, j: (bi, 0, j, 0)),
        pl.BlockSpec((nb, 2, TILE, RWKV_W), lambda bi, j: (bi, 0, j, 0)),
        pl.BlockSpec((nb, 2, TILE, RWKV_W), lambda bi, j: (bi, 0, j, 0)),
        pl.BlockSpec((nb, TILE, RWKV_W), lambda bi, j: (bi, j, 0)),
        pl.BlockSpec((nb, 2, CHUNKS_PER_TILE, 1, RWKV_W), lambda bi, j: (bi, 0, j, 0, 0)),
        pl.BlockSpec((nb, TILE, RWKV_W), lat),
        pl.BlockSpec((nb, TILE, RWKV_W), lat),
        pl.BlockSpec((nb, TILE, 2 * FOURIER_W), lat),
    ]
    out_shape = [
        jax.ShapeDtypeStruct((b, 2, t_all, RWKV_W), BF16),
        jax.ShapeDtypeStruct((b, 2, t_all, RWKV_W), BF16),
        jax.ShapeDtypeStruct((b, 2, t_all, RWKV_W), BF16),
        jax.ShapeDtypeStruct((b, 2, t_all, RWKV_W), BF16),
        jax.ShapeDtypeStruct((b, t_all, RWKV_W), BF16),
        jax.ShapeDtypeStruct((b, 2, n_chunks, 1, RWKV_W), F32),
        jax.ShapeDtypeStruct((b, s, RWKV_W), F32),
        jax.ShapeDtypeStruct((b, s, RWKV_W), F32),
        jax.ShapeDtypeStruct((b, s, 2 * FOURIER_W), BF16),
    ]
    return pl.pallas_call(
        _prep_kernel,
        grid=(b // nb, n_j),
        in_specs=in_specs,
        out_specs=out_specs,
        out_shape=out_shape,
        compiler_params=pltpu.CompilerParams(dimension_semantics=("arbitrary", "arbitrary"),
                                             vmem_limit_bytes=VMEM_LIMIT),
        name="prep",
    )(x, x, x, ctx, modsel, norm1_g, w_in, conv_w, w0, w2bd, a0, a2bd, g2, k_k, k_a, r_k,
      cs, ones_blk, tri)


def _quad_chunk(a, bm, km, r, v, plrow, g_states, strict, incl, eye, lane_lo, mask_bd, eye_row):
    n = PAIR

    def dup(xv):
        z = jnp.zeros_like(xv)
        return jnp.concatenate([jnp.where(lane_lo, xv, z), jnp.where(lane_lo, z, xv)], axis=0)

    def to_row(x2):
        return x2[:CHUNK] + x2[CHUNK:]

    def bdiag(xr):
        return jnp.where(mask_bd, jnp.concatenate([xr] * QUAD_HEADS, axis=0), jnp.zeros((), xr.dtype))

    pairs = []
    for p in range(2):
        sl = slice(p * n, (p + 1) * n)
        a2, b2, k2, r2, v2 = (dup(t[:, sl]) for t in (a, bm, km, r, v))
        rhs = jnp.concatenate([b2, k2], axis=0)
        s = _bdot_nt(jnp.concatenate([a2, r2], axis=0), rhs)
        pairs.append((sl, r2, v2, rhs, s))
    yield

    a_ab = jnp.concatenate([to_row(jnp.where(strict, s[:n, :n], 0.0)) for *_, s in pairs], axis=1)
    a_ak = jnp.concatenate([to_row(jnp.where(strict, s[:n, n:], 0.0)) for *_, s in pairs], axis=1)
    m_rbk = [jnp.concatenate([jnp.where(incl, s[n:, :n], 0.0), jnp.where(incl, s[n:, n:], 0.0)],
                             axis=1).astype(BF16) for *_, s in pairs]

    pk = (-a_ab).astype(BF16)
    t_row = jnp.where(eye_row, 1.0, 0.0) - a_ab
    av = _bdot(a_ak, bdiag(v))
    pk = jnp.dot(pk, bdiag(pk), preferred_element_type=F32).astype(BF16)
    yield
    for _ in range(int(math.log2(CHUNK)) - 2):
        both = jnp.dot(jnp.concatenate([pk, t_row.astype(BF16)], axis=0), bdiag(pk),
                       preferred_element_type=F32)
        yield
        pk = both[:CHUNK].astype(BF16)
        t_row = t_row + both[CHUNK:]
    t_row = t_row + jnp.dot(t_row.astype(BF16), bdiag(pk), preferred_element_type=F32)
    yield
    tx = _bdot(t_row, jnp.concatenate([bdiag(a), bdiag(av.astype(BF16))], axis=1))
    yield
    a_w = tx[:, :2 * n].astype(BF16)
    u_v = (-tx[:, 2 * n:]).astype(BF16)
    stage5 = []
    for (sl, r2, v2, rhs, _), m2 in zip(pairs, m_rbk):
        zmat = jnp.concatenate(
            [jnp.concatenate([dup(a_w[:, sl]), dup(u_v[:, sl])], axis=1),
             jnp.concatenate([jnp.zeros_like(v2), v2], axis=1)], axis=0)
        rhs_end = rhs.astype(F32) * plrow[:, sl]
        lhs = jnp.concatenate([rhs_end.T.astype(BF16), m2], axis=0)
        stage5.append(jnp.dot(lhs, zmat, preferred_element_type=F32))
    yield
    ys, g_new = [], []
    for (sl, r2, v2, rhs, _), o5, g_state in zip(pairs, stage5, g_states):
        phi = jnp.where(eye, plrow[:, sl], 0.0) - o5[:n, :n]
        r_y = r2.astype(F32) - o5[n:, :n]
        o6 = _bdot(jnp.concatenate([r_y, phi], axis=0), g_state)
        ys.append(to_row(o6[:n] + o5[n:, n:]))
        g_new.append(o6[n:] + o5[:n, n:])
    return jnp.concatenate(ys, axis=1), g_new


def _run_interleaved(gens):
    results = [None] * len(gens)
    active = list(range(len(gens)))
    while active:
        for i in list(active):
            try:
                next(gens[i])
            except StopIteration as stop:
                results[i] = stop.value
                active.remove(i)
    return results


def _scan_kernel(atf, btf, ktf, rtf, vf, plf, atb, btb, ktb, rtb, vb, plb,
                 yf_ref, yb_ref, g_ref):
    s = pl.program_id(1)

    @pl.when(s == 0)
    def _():
        g_ref[...] = jnp.zeros_like(g_ref)

    ri = lax.broadcasted_iota(jnp.int32, (PAIR, PAIR), 0)
    ci = lax.broadcasted_iota(jnp.int32, (PAIR, PAIR), 1)
    same = (ri >= HEAD) == (ci >= HEAD)
    ti = ri & (HEAD - 1)
    tj = ci & (HEAD - 1)
    eye = ri == ci
    lane_lo = lax.broadcasted_iota(jnp.int32, (CHUNK, PAIR), 1) < HEAD
    masks = [(same & (tj < ti), same & (tj <= ti)),
             (same & (tj > ti), same & (tj >= ti))]
    quad_w = QUAD_HEADS * HEAD
    rq = lax.broadcasted_iota(jnp.int32, (quad_w, quad_w), 0)
    cq = lax.broadcasted_iota(jnp.int32, (quad_w, quad_w), 1)
    mask_bd = (rq // HEAD) == (cq // HEAD)
    eye_row = (lax.broadcasted_iota(jnp.int32, (CHUNK, quad_w), 1) & (HEAD - 1)
               ) == lax.broadcasted_iota(jnp.int32, (CHUNK, quad_w), 0)
    refs = [(atf, btf, ktf, rtf, vf, plf, yf_ref), (atb, btb, ktb, rtb, vb, plb, yb_ref)]
    gens, dests = [], []
    for bi in range(yf_ref.shape[0]):
        for d in range(2):
            at, bt, kt, rt, vv, plr, y_ref = refs[d]
            strict, incl = masks[d]
            for q in range(RWKV_W // quad_w):
                sl = slice(q * quad_w, (q + 1) * quad_w)
                gens.append(_quad_chunk(at[bi, 0, :, sl], bt[bi, 0, :, sl], kt[bi, 0, :, sl],
                                        rt[bi, 0, :, sl], vv[bi, :, sl], plr[bi, 0, 0, :, sl],
                                        [g_ref[bi, d, 2 * q], g_ref[bi, d, 2 * q + 1]],
                                        strict, incl, eye, lane_lo, mask_bd, eye_row))
                dests.append((y_ref, bi, d, q, sl))
    for (y, g_new), (y_ref, bi, d, q, sl) in zip(_run_interleaved(gens), dests):
        g_ref[bi, d, 2 * q] = g_new[0]
        g_ref[bi, d, 2 * q + 1] = g_new[1]
        y_ref[bi, :, sl] = y


def _scan_call(at, bt, kt, rt, v, plast, s_lat):
    b = at.shape[0]
    n_chunks = at.shape[2] // CHUNK
    n_ctx = CTX_LEN // CHUNK
    n_lat = s_lat // CHUNK

    def fwd_c(s):
        return s

    def bwd_c(s):
        return jnp.where(s < n_ctx, n_ctx - 1 - s, n_chunks + n_ctx - 1 - s)

    nb = SCAN_BATCH

    def dir_spec(d, cfun):
        return pl.BlockSpec((nb, 1, CHUNK, RWKV_W), lambda bi, s: (bi, d, cfun(s), 0))

    def v_spec(cfun):
        return pl.BlockSpec((nb, CHUNK, RWKV_W), lambda bi, s: (bi, cfun(s), 0))

    def pl_spec(d, cfun):
        return pl.BlockSpec((nb, 1, 1, 1, RWKV_W), lambda bi, s: (bi, d, cfun(s), 0, 0))

    in_specs = ([dir_spec(0, fwd_c)] * 4 + [v_spec(fwd_c), pl_spec(0, fwd_c)]
                + [dir_spec(1, bwd_c)] * 4 + [v_spec(bwd_c), pl_spec(1, bwd_c)])
    out_specs = [
        pl.BlockSpec((nb, CHUNK, RWKV_W), lambda bi, s: (bi, jnp.maximum(s - n_ctx, 0), 0)),
        pl.BlockSpec((nb, CHUNK, RWKV_W),
                     lambda bi, s: (bi, jnp.minimum(n_chunks - 1 - s, n_lat - 1), 0)),
    ]
    out_shape = [jax.ShapeDtypeStruct((b, s_lat, RWKV_W), F32)] * 2
    return pl.pallas_call(
        _scan_kernel,
        grid=(b // nb, n_chunks),
        in_specs=in_specs,
        out_specs=out_specs,
        out_shape=out_shape,
        scratch_shapes=[pltpu.VMEM((nb, 2, N_PAIRS, PAIR, PAIR), F32)],
        compiler_params=pltpu.CompilerParams(dimension_semantics=("arbitrary", "arbitrary"),
                                             vmem_limit_bytes=VMEM_LIMIT),
        name="scan",
    )(at, bt, kt, rt, v, plast, at, bt, kt, rt, v, plast)


def _mix_kernel(dft_ref, z_ref, yf_ref, yb_ref, bonus_ref, gate_ref, x_ref, mod_ref,
                gng_ref, gnb_ref, fg_ref, wout_ref, n2g_ref, ones_ref, x1_ref, h2_ref, *, fscale):
    seq = z_ref.shape[1]
    yfour = (jnp.dot(dft_ref[:, :seq], z_ref[0, :, :FOURIER_W], preferred_element_type=F32)
             + jnp.dot(dft_ref[:, seq:], z_ref[0, :, FOURIER_W:], preferred_element_type=F32))
    four = _rms(yfour * fscale, fg_ref[...])

    ones_blk = ones_ref[...]
    y = yf_ref[0] + yb_ref[0]
    mu = _dot_exact_rhs(y, ones_blk) * (1.0 / HEAD)
    dy = y - mu
    var = _dot_exact_rhs(dy * dy, ones_blk) * (1.0 / HEAD)
    yn = dy * lax.rsqrt(var + GN_EPS) * gng_ref[...] + gnb_ref[...]
    rw = (yn + bonus_ref[0]) * gate_ref[0]

    out = _bdot(four, wout_ref[:FOURIER_W, :]) + _bdot(rw, wout_ref[FOURIER_W:, :])
    x1 = x_ref[0] + mod_ref[0, 2:3, :] * out
    x1_ref[0] = x1
    h2 = _rms(x1, n2g_ref[...]) * (1.0 + mod_ref[0, 4:5, :]) + mod_ref[0, 3:4, :]
    h2_ref[0] = h2.astype(BF16)


def _mix_call(dft, z, yf, yb, bonus, gate, x, modx, gn_g, gn_b, f_g, w_out, norm2_g, ones_blk):
    b, s, d = x.shape
    tm = TILE

    def full(arr):
        nd = arr.ndim
        return pl.BlockSpec(arr.shape, lambda bi, m, _n=nd: (0,) * _n)

    tok = lambda w: pl.BlockSpec((1, tm, w), lambda bi, m: (bi, m, 0))
    in_specs = [
        pl.BlockSpec((tm, 2 * s), lambda bi, m: (m, 0)),
        pl.BlockSpec((1, s, 2 * FOURIER_W), lambda bi, m: (bi, 0, 0)),
        tok(RWKV_W), tok(RWKV_W), tok(RWKV_W), tok(RWKV_W), tok(d),
        pl.BlockSpec((1, 6, d), lambda bi, m: (bi, 0, 0)),
        full(gn_g), full(gn_b), full(f_g), full(w_out), full(norm2_g), full(ones_blk),
    ]
    fscale = 1.0 / math.sqrt(s * (FOURIER_W // FOURIER_GROUPS))
    return pl.pallas_call(
        functools.partial(_mix_kernel, fscale=fscale),
        grid=(b, s // tm),
        in_specs=in_specs,
        out_specs=[tok(d), tok(d)],
        out_shape=[jax.ShapeDtypeStruct((b, s, d), F32), jax.ShapeDtypeStruct((b, s, d), BF16)],
        compiler_params=pltpu.CompilerParams(dimension_semantics=("arbitrary", "arbitrary"),
                                             vmem_limit_bytes=VMEM_LIMIT),
        name="mix",
    )(dft, z, yf, yb, bonus, gate, x, modx, gn_g, gn_b, f_g, w_out, norm2_g, ones_blk)


def _ffn_up_kernel(h2_ref, wg_ref, wv_ref, cwg_ref, cwv_ref, cbg_ref, cbv_ref, act_ref):
    t = h2_ref.shape[1]
    rows = FFN_ROWS
    n_blk = t // rows
    w = jnp.concatenate([wg_ref[...], wv_ref[...]], axis=1)
    cw = jnp.concatenate([cwg_ref[...], cwv_ref[...]], axis=1).astype(BF16)
    cb = jnp.concatenate([cbg_ref[...], cbv_ref[...]], axis=1).astype(BF16)
    col = lax.broadcasted_iota(jnp.int32, (rows, 2 * FF_TILE), 0) & (GRID_W - 1)
    has_left = col > 0
    has_right = col < GRID_W - 1
    zeros = jnp.zeros((GRID_W, 2 * FF_TILE), BF16)

    def row_mixes(i):
        u = jnp.dot(h2_ref[0, i * rows:(i + 1) * rows, :], w, preferred_element_type=F32)
        u_l = jnp.where(has_left, pltpu.roll(u, 1, 0), 0.0).astype(BF16)
        u_r = jnp.where(has_right, pltpu.roll(u, rows - 1, 0), 0.0).astype(BF16)
        u_c = u.astype(BF16)
        return [cw[3 * kh:3 * kh + 1] * u_l + cw[3 * kh + 1:3 * kh + 2] * u_c
                + cw[3 * kh + 2:3 * kh + 3] * u_r for kh in range(3)]

    def finish(i, prev, cur, nxt):
        above = zeros if prev is None else prev[0][rows - GRID_W:]
        below = zeros if nxt is None else nxt[2][:GRID_W]
        c = (cur[1] + jnp.concatenate([above, cur[0][:rows - GRID_W]], axis=0)
             + jnp.concatenate([cur[2][GRID_W:], below], axis=0) + cb)
        cg = c[:, :FF_TILE]
        act_ref[0, i * rows:(i + 1) * rows, :] = cg * _sigmoid(cg) * c[:, FF_TILE:]

    mixes = [None] * (n_blk + 1)
    for i in range(n_blk):
        mixes[i] = row_mixes(i)
        if i >= 1:
            finish(i - 1, mixes[i - 2] if i >= 2 else None, mixes[i - 1], mixes[i])
    finish(n_blk - 1, mixes[n_blk - 2] if n_blk >= 2 else None, mixes[n_blk - 1], None)


def _ffn_up_call(h2, w_up, conv_w9, conv_b):
    b, s, d = h2.shape
    n_f = D_FF // FF_TILE
    in_specs = [
        pl.BlockSpec((1, s, d), lambda bi, f: (bi, 0, 0)),
        pl.BlockSpec((d, FF_TILE), lambda bi, f: (0, f)),
        pl.BlockSpec((d, FF_TILE), lambda bi, f: (0, n_f + f)),
        pl.BlockSpec((9, FF_TILE), lambda bi, f: (0, f)),
        pl.BlockSpec((9, FF_TILE), lambda bi, f: (0, n_f + f)),
        pl.BlockSpec((1, FF_TILE), lambda bi, f: (0, f)),
        pl.BlockSpec((1, FF_TILE), lambda bi, f: (0, n_f + f)),
    ]
    return pl.pallas_call(
        _ffn_up_kernel,
        grid=(b, n_f),
        in_specs=in_specs,
        out_specs=pl.BlockSpec((1, s, FF_TILE), lambda bi, f: (bi, 0, f)),
        out_shape=jax.ShapeDtypeStruct((b, s, D_FF), BF16),
        compiler_params=pltpu.CompilerParams(dimension_semantics=("arbitrary", "arbitrary"),
                                             vmem_limit_bytes=VMEM_LIMIT),
        name="ffn_up",
    )(h2, w_up, w_up, conv_w9, conv_w9, conv_b, conv_b)


def _ffn_down_kernel(act_ref, wd_ref, x1_ref, mod_ref, fg_ref, out_ref):
    y = jnp.dot(act_ref[0], wd_ref[...], preferred_element_type=F32)
    out_ref[0] = _rms(x1_ref[0] + mod_ref[0, 5:6, :] * y, fg_ref[...])


def _ffn_down_call(act, w_down, x1, modx, final_g):
    b, s, d = x1.shape
    tm = DOWN_TILE
    in_specs = [
        pl.BlockSpec((1, tm, D_FF), lambda bi, m: (bi, m, 0)),
        pl.BlockSpec((D_FF, d), lambda bi, m: (0, 0)),
        pl.BlockSpec((1, tm, d), lambda bi, m: (bi, m, 0)),
        pl.BlockSpec((1, 6, d), lambda bi, m: (bi, 0, 0)),
        pl.BlockSpec((1, d), lambda bi, m: (0, 0)),
    ]
    return pl.pallas_call(
        _ffn_down_kernel,
        grid=(b, s // tm),
        in_specs=in_specs,
        out_specs=pl.BlockSpec((1, tm, d), lambda bi, m: (bi, m, 0)),
        out_shape=jax.ShapeDtypeStruct((b, s, d), F32),
        compiler_params=pltpu.CompilerParams(dimension_semantics=("arbitrary", "arbitrary"),
                                             vmem_limit_bytes=VMEM_LIMIT),
        name="ffn_down",
    )(act, w_down, x1, modx, final_g)


@functools.lru_cache(maxsize=None)
def _constants(seq):
    gw = FOURIER_W // FOURIER_GROUPS
    nm = np.outer(np.arange(gw), np.arange(gw)) % gw
    ang = 2.0 * np.pi * nm / gw
    cs = np.zeros((FOURIER_W, 2 * FOURIER_W), np.float64)
    for gi in range(FOURIER_GROUPS):
        sl = slice(gi * gw, (gi + 1) * gw)
        cs[sl, sl] = np.cos(ang)
        cs[sl, FOURIER_W + gi * gw:FOURIER_W + (gi + 1) * gw] = np.sin(ang)
    kt = np.outer(np.arange(seq, dtype=np.int64), np.arange(seq, dtype=np.int64)) % seq
    ang_t = 2.0 * np.pi * kt / seq
    dft = np.concatenate([np.cos(ang_t), -np.sin(ang_t)], axis=1)
    head_id = np.arange(RWKV_W) // HEAD
    ones_blk = (head_id[:, None] == head_id[None, :]).astype(np.float64)
    ti = np.arange(TILE)
    same_chunk = (ti[:, None] // CHUNK) == (ti[None, :] // CHUNK)
    tri = np.stack([same_chunk & (ti[None, :] <= ti[:, None]),
                    same_chunk & (ti[None, :] >= ti[:, None])]).astype(np.float64)
    to_bf16 = lambda arr: np.asarray(arr, dtype=np.float32).astype(BF16)
    return (np.asarray(cs, dtype=np.float32), np.asarray(dft, dtype=np.float32),
            to_bf16(ones_blk), to_bf16(tri))


def _block_diag_dirs(w):
    z = jnp.zeros_like(w[0])
    return jnp.concatenate([jnp.concatenate([w[0], z], axis=1),
                            jnp.concatenate([z, w[1]], axis=1)], axis=0)


def kernel(x, c, ctx, c_ctx, ada_w, ada_b, norm1_g, norm2_g, w_in, rwkv_conv_w, decay_w0, decay_w2, iclr_a0, iclr_a2, gate_g2, k_k, k_a, r_k, gn_g, gn_b, fourier_g, w_out, ffn_w_up, ffn_conv_w, ffn_conv_b, ffn_w_down, final_g):
    b, s, d = x.shape
    assert ada_w.shape[0] == 1, "single-layer configuration"
    assert (b, s, d) == (c.shape[0], s, D_MODEL) and ctx.shape == (b, CTX_LEN, d)
    assert s % TILE == 0 and CTX_LEN == TILE
    cs, dft, ones_blk, tri = (jnp.asarray(t) for t in _constants(s))
    cs, dft = cs.astype(BF16), dft.astype(BF16)

    cc = jnp.concatenate([c, c_ctx[None, :], jnp.zeros((16 - b - 1, d), F32)], axis=0)
    mod = _mod_call(cc, ada_w[0], ada_b[0][None, :])
    modx = mod[:b].reshape(b, 6, d)
    modc = jnp.broadcast_to(mod[b].reshape(1, 6, d)[:, :2], (b, 2, d))
    modsel = jnp.stack([modc, modx[:, :2]], axis=1)

    row = lambda t: t.reshape(1, -1)
    at, bt, kt, rt, v, plast, bonus, gate, z = _prep_call(
        x, ctx, modsel, row(norm1_g[0]), w_in[0].astype(BF16), rwkv_conv_w[0],
        row(decay_w0[0]), _block_diag_dirs(decay_w2[0]).astype(BF16),
        row(iclr_a0[0]), _block_diag_dirs(iclr_a2[0]).astype(BF16),
        gate_g2[0].astype(BF16), row(k_k[0]), row(k_a[0]), row(r_k[0]), cs, ones_blk, tri)

    yf, yb = _scan_call(at, bt, kt, rt, v, plast, s)

    x1, h2 = _mix_call(dft, z, yf, yb, bonus, gate, x, modx, row(gn_g[0]), row(gn_b[0]),
                       row(fourier_g[0]), w_out[0].astype(BF16), row(norm2_g[0]), ones_blk)

    act = _ffn_up_call(h2, ffn_w_up[0].astype(BF16), ffn_conv_w[0].reshape(9, 2 * D_FF),
                       row(ffn_conv_b[0]))
    return _ffn_down_call(act, ffn_w_down[0].astype(BF16), x1, modx, row(final_g))
```

```python
import functools
import math

import numpy as np
import jax
import jax.numpy as jnp
from jax import lax
from jax.experimental import pallas as pl
from jax.experimental.pallas import tpu as pltpu

F32 = jnp.float32
BF16 = jnp.bfloat16

D_MODEL = 1024
CTX_LEN = 256
GRID_W = 64
FOURIER_W = 512
FOURIER_GROUPS = 8
RWKV_W = 512
HEAD = 64
N_HEADS = 8
DECAY_RANK = 64
ICLR_RANK = 64
GATE_RANK = 128
D_FF = 2816
NORM_EPS = 1e-6
GN_EPS = 64e-5
KK_EPS = 1e-12

CHUNK = 64
TILE = 256
MIX_TILE = 512
REV_TILE = 256
CHUNKS_PER_TILE = TILE // CHUNK
PREP_BATCH = 2
PAIR = 2 * HEAD
N_PAIRS = RWKV_W // PAIR
QUAD_HEADS = 4
SCAN_BATCH = 4
MOD_TILE = 1536
FF_TILE = 1408
FFN_ROWS = 256
DOWN_TILE = 512
VMEM_LIMIT = 56 * 1024 * 1024
HEAD_SUM_TERMS = 1
CUMSUM_TERMS = 2

RKV_LO = FOURIER_W
RKV_HI = FOURIER_W + 3 * RWKV_W
WD_LO = RKV_HI
AD_LO = WD_LO + 2 * DECAY_RANK
GD_LO = AD_LO + 2 * ICLR_RANK
PROJ_W = GD_LO + GATE_RANK


def _bdot(a, b):
    return jnp.dot(a.astype(BF16), b.astype(BF16), preferred_element_type=F32)


def _bdot_nt(a, b):
    return lax.dot_general(a.astype(BF16), b.astype(BF16), (((1,), (1,)), ((), ())),
                           preferred_element_type=F32)


def _bdot_tn(a, b):
    return lax.dot_general(a.astype(BF16), b.astype(BF16), (((0,), (0,)), ((), ())),
                           preferred_element_type=F32)


def _split_terms(a, terms):
    out, rem = [], a
    for _ in range(terms):
        hi = rem.astype(BF16)
        out.append(hi)
        rem = rem - hi.astype(F32)
    return out


def _dot_exact_rhs(a, b_exact, terms=HEAD_SUM_TERMS):
    acc = None
    for piece in _split_terms(a, terms):
        t = jnp.dot(piece, b_exact, preferred_element_type=F32)
        acc = t if acc is None else acc + t
    return acc


def _dot_exact_lhs(a_exact, b, terms=CUMSUM_TERMS):
    acc = None
    for piece in _split_terms(b, terms):
        t = jnp.dot(a_exact, piece, preferred_element_type=F32)
        acc = t if acc is None else acc + t
    return acc


def _sigmoid(x):
    return 1.0 / (1.0 + jnp.exp(-x))


def _rms(x, g):
    return x * lax.rsqrt(jnp.mean(x * x, axis=-1, keepdims=True) + NORM_EPS) * g


def _mod_kernel(c_ref, w_ref, b_ref, o_ref):
    cc = c_ref[...]
    s = cc * _sigmoid(cc)
    rows = s.shape[0]
    s_parts = jnp.concatenate(_split_terms(s, 3), axis=0)
    acc = b_ref[...]
    for wp in _split_terms(w_ref[...], 3):
        t = jnp.dot(s_parts, wp, preferred_element_type=F32)
        acc = acc + (t[:rows] + t[rows:2 * rows] + t[2 * rows:])
    o_ref[...] = acc


def _mod_call(cc, ada_w, ada_b):
    rows, d = cc.shape
    n = ada_w.shape[1]
    tn = MOD_TILE
    return pl.pallas_call(
        _mod_kernel,
        grid=(n // tn,),
        in_specs=[pl.BlockSpec((rows, d), lambda i: (0, 0)),
                  pl.BlockSpec((d, tn), lambda i: (0, i)),
                  pl.BlockSpec((1, tn), lambda i: (0, i))],
        out_specs=pl.BlockSpec((rows, tn), lambda i: (0, i)),
        out_shape=jax.ShapeDtypeStruct((rows, n), F32),
        compiler_params=pltpu.CompilerParams(dimension_semantics=("arbitrary",),
                                             vmem_limit_bytes=VMEM_LIMIT),
        name="mod",
    )(cc, ada_w, ada_b)


def _prep_kernel(x_ref, xp_ref, xn_ref, ctx_ref, ms_ref, g1_ref, win_ref, cw_ref,
                 w0_ref, w2_ref, a0_ref, a2_ref, g2_ref, kk_ref, ka_ref, rk_ref,
                 cs_ref, ones_ref, tri_ref,
                 at_ref, bt_ref, kt_ref, rt_ref, v_ref, pl_ref, bonus_ref, gate_ref, z_ref):
    j = pl.program_id(1)
    refs = (x_ref, xp_ref, xn_ref, ctx_ref, ms_ref, g1_ref, win_ref, cw_ref,
            w0_ref, w2_ref, a0_ref, a2_ref, kk_ref, ka_ref, ones_ref, tri_ref,
            at_ref, bt_ref, kt_ref, rt_ref, v_ref, pl_ref)
    tiles = _run_interleaved([_prep_tile(bi, *refs) for bi in range(x_ref.shape[0])])

    @pl.when(j >= 1)
    def _():
        ones_blk = ones_ref[...]
        for bi, (u_four, r, v, gd, kd_sum) in enumerate(tiles):
            bs = _dot_exact_rhs(r * rk_ref[...] * kd_sum, ones_blk)
            bonus_ref[bi] = bs * v
            gate_ref[bi] = _bdot(_sigmoid(gd), g2_ref[...])
            z_ref[bi] = _bdot(u_four, cs_ref[...]).astype(BF16)


def _prep_tile(bi, x_ref, xp_ref, xn_ref, ctx_ref, ms_ref, g1_ref, win_ref, cw_ref,
               w0_ref, w2_ref, a0_ref, a2_ref, kk_ref, ka_ref, ones_ref, tri_ref,
               at_ref, bt_ref, kt_ref, rt_ref, v_ref, pl_ref):
    j = pl.program_id(1)
    n_j = pl.num_programs(1)
    is_ctx = j == 0
    shift = ms_ref[bi, 0, 0:1, :]
    scale = ms_ref[bi, 0, 1:2, :]
    g = g1_ref[...]

    def norm_mod(xr):
        return _rms(xr, g) * (1.0 + scale) + shift

    xin = jnp.where(is_ctx, ctx_ref[bi], x_ref[bi])
    h = norm_mod(xin).astype(BF16)
    u = jnp.dot(h, win_ref[...], preferred_element_type=F32)

    halo = jnp.concatenate([xp_ref[bi], xn_ref[bi]], axis=0)
    uh = jnp.dot(norm_mod(halo).astype(BF16), win_ref[:, RKV_LO:RKV_HI],
                 preferred_element_type=F32)
    yield
    prev_row = jnp.where(j >= 2, uh[7:8], 0.0)
    next_row = jnp.where(jnp.logical_and(j >= 1, j < n_j - 1), uh[8:9], 0.0)

    rkv = u[:, RKV_LO:RKV_HI]
    row = lax.broadcasted_iota(jnp.int32, rkv.shape, 0)
    u_m1 = jnp.where(row == 0, prev_row, pltpu.roll(rkv, 1, 0))
    u_p1 = jnp.where(row == TILE - 1, next_row, pltpu.roll(rkv, TILE - 1, 0))
    rkv = cw_ref[0:1, :] * u_m1 + cw_ref[1:2, :] * rkv + cw_ref[2:3, :] * u_p1
    r = rkv[:, 0:RWKV_W]
    k = rkv[:, RWKV_W:2 * RWKV_W]
    v = rkv[:, 2 * RWKV_W:3 * RWKV_W]

    wd = u[:, WD_LO:AD_LO]
    ad = u[:, AD_LO:GD_LO]
    gd = u[:, GD_LO:PROJ_W]
    w_lora = w0_ref[...] + _bdot(jnp.tanh(wd), w2_ref[...])
    lw = -math.exp(-0.5) * _sigmoid(w_lora)
    a = _sigmoid(a0_ref[...] + _bdot(ad, a2_ref[...]))

    ones_blk = ones_ref[...]
    kraw = k * kk_ref[...]
    kk = kraw * lax.rsqrt(_dot_exact_rhs(kraw * kraw, ones_blk) + KK_EPS)
    ka = ka_ref[...]
    kd = [k * (1.0 + (a[:, d * RWKV_W:(d + 1) * RWKV_W] - 1.0) * ka) for d in range(2)]

    v_ref[bi] = v.astype(BF16)
    for d in range(2):
        lwd = lw[:, d * RWKV_W:(d + 1) * RWKV_W]
        ad_ = a[:, d * RWKV_W:(d + 1) * RWKV_W]
        c = _dot_exact_lhs(tri_ref[d], lwd)
        e_pos = jnp.exp(c)
        e_neg = jnp.exp(-c)
        e_prev = jnp.exp(c - lwd)
        at_ref[bi, d] = (kk * e_prev).astype(BF16)
        bt_ref[bi, d] = (kk * ad_ * e_neg).astype(BF16)
        kt_ref[bi, d] = (kd[d] * e_neg).astype(BF16)
        rt_ref[bi, d] = (r * e_pos).astype(BF16)
        for q in range(CHUNKS_PER_TILE):
            last = q * CHUNK + (CHUNK - 1 if d == 0 else 0)
            pl_ref[bi, d, q] = e_pos[last:last + 1, :]
    return u[:, 0:FOURIER_W], r, v, gd, kd[0] + kd[1]


def _prep_call(x, ctx, modsel, norm1_g, w_in, conv_w, w0, w2bd, a0, a2bd, g2, k_k, k_a, r_k,
               cs, ones_blk, tri):
    b, s, d = x.shape
    n_j = 1 + s // TILE
    n_chunks = (CTX_LEN + s) // CHUNK
    t_all = CTX_LEN + s
    c2 = 2 * RWKV_W

    def full(arr):
        nd = arr.ndim
        return pl.BlockSpec(arr.shape, lambda bi, j, _n=nd: (0,) * _n)

    rows8 = TILE // 8
    nb = PREP_BATCH
    in_specs = [
        pl.BlockSpec((nb, TILE, d), lambda bi, j: (bi, jnp.maximum(j - 1, 0), 0)),
        pl.BlockSpec((nb, 8, d), lambda bi, j: (bi, jnp.maximum((j - 1) * rows8 - 1, 0), 0)),
        pl.BlockSpec((nb, 8, d), lambda bi, j: (bi, jnp.minimum(j * rows8, s // 8 - 1), 0)),
        pl.BlockSpec((nb, CTX_LEN, d), lambda bi, j: (bi, 0, 0)),
        pl.BlockSpec((nb, 1, 2, d), lambda bi, j: (bi, jnp.minimum(j, 1), 0, 0)),
        full(norm1_g), full(w_in), full(conv_w), full(w0), full(w2bd), full(a0), full(a2bd),
        full(g2), full(k_k), full(k_a), full(r_k), full(cs), full(ones_blk), full(tri),
    ]
    lat = lambda bi, j: (bi, jnp.maximum(j - 1, 0), 0)
    out_specs = [
        pl.BlockSpec((nb, 2, TILE, RWKV_W), lambda bi, j: (bi, 0, j, 0)),
        pl.BlockSpec((nb, 2, TILE, RWKV_W), lambda bi, j: (bi, 0, j, 0)),
        pl.BlockSpec((nb, 2, TILE, RWKV_W), lambda bi, j: (bi, 0, j, 0)),
        pl.BlockSpec((nb, 2, TILE, RWKV_W), lambda bi, j: (bi, 0, j, 0)),
        pl.BlockSpec((nb, TILE, RWKV_W), lambda bi, j: (bi, j, 0)),
        pl.BlockSpec((nb, 2, CHUNKS_PER_TILE, 1, RWKV_W), lambda bi, j: (bi, 0, j, 0, 0)),
        pl.BlockSpec((nb, TILE, RWKV_W), lat),
        pl.BlockSpec((nb, TILE, RWKV_W), lat),
        pl.BlockSpec((nb, TILE, 2 * FOURIER_W), lat),
    ]
    out_shape = [
        jax.ShapeDtypeStruct((b, 2, t_all, RWKV_W), BF16),
        jax.ShapeDtypeStruct((b, 2, t_all, RWKV_W), BF16),
        jax.ShapeDtypeStruct((b, 2, t_all, RWKV_W), BF16),
        jax.ShapeDtypeStruct((b, 2, t_all, RWKV_W), BF16),
        jax.ShapeDtypeStruct((b, t_all, RWKV_W), BF16),
        jax.ShapeDtypeStruct((b, 2, n_chunks, 1, RWKV_W), F32),
        jax.ShapeDtypeStruct((b, s, RWKV_W), F32),
        jax.ShapeDtypeStruct((b, s, RWKV_W), F32),
        jax.ShapeDtypeStruct((b, s, 2 * FOURIER_W), BF16),
    ]
    return pl.pallas_call(
        _prep_kernel,
        grid=(b // nb, n_j),
        in_specs=in_specs,
        out_specs=out_specs,
        out_shape=out_shape,
        compiler_params=pltpu.CompilerParams(dimension_semantics=("arbitrary", "arbitrary"),
                                             vmem_limit_bytes=VMEM_LIMIT),
        name="prep",
    )(x, x, x, ctx, modsel, norm1_g, w_in, conv_w, w0, w2bd, a0, a2bd, g2, k_k, k_a, r_k,
      cs, ones_blk, tri)


def _quad_chunk(a, bm, km, r, v, plrow, g_states, strict, incl, eye, lane_lo, mask_bd, eye_row):
    n = PAIR

    def dup(xv):
        z = jnp.zeros_like(xv)
        return jnp.concatenate([jnp.where(lane_lo, xv, z), jnp.where(lane_lo, z, xv)], axis=0)

    def to_row(x2):
        return x2[:CHUNK] + x2[CHUNK:]

    def bdiag(xr):
        return jnp.where(mask_bd, jnp.concatenate([xr] * QUAD_HEADS, axis=0), jnp.zeros((), xr.dtype))

    pairs = []
    for p in range(2):
        sl = slice(p * n, (p + 1) * n)
        a2, b2, k2, r2, v2 = (dup(t[:, sl]) for t in (a, bm, km, r, v))
        rhs = jnp.concatenate([b2, k2], axis=0)
        s = _bdot_nt(jnp.concatenate([a2, r2], axis=0), rhs)
        pairs.append((sl, r2, v2, rhs, s))
    yield

    a_ab = jnp.concatenate([to_row(jnp.where(strict, s[:n, :n], 0.0)) for *_, s in pairs], axis=1)
    a_ak = jnp.concatenate([to_row(jnp.where(strict, s[:n, n:], 0.0)) for *_, s in pairs], axis=1)
    m_rbk = [jnp.concatenate([jnp.where(incl, s[n:, :n], 0.0), jnp.where(incl, s[n:, n:], 0.0)],
                             axis=1).astype(BF16) for *_, s in pairs]

    pk = (-a_ab).astype(BF16)
    t_row = jnp.where(eye_row, 1.0, 0.0) - a_ab
    av = _bdot(a_ak, bdiag(v))
    pk = jnp.dot(pk, bdiag(pk), preferred_element_type=F32).astype(BF16)
    yield
    for _ in range(int(math.log2(CHUNK)) - 2):
        both = jnp.dot(jnp.concatenate([pk, t_row.astype(BF16)], axis=0), bdiag(pk),
                       preferred_element_type=F32)
        yield
        pk = both[:CHUNK].astype(BF16)
        t_row = t_row + both[CHUNK:]
    t_row = t_row + jnp.dot(t_row.astype(BF16), bdiag(pk), preferred_element_type=F32)
    yield
    tx = _bdot(t_row, jnp.concatenate([bdiag(a), bdiag(av.astype(BF16))], axis=1))
    yield
    a_w = tx[:, :2 * n].astype(BF16)
    u_v = (-tx[:, 2 * n:]).astype(BF16)
    stage5 = []
    for (sl, r2, v2, rhs, _), m2 in zip(pairs, m_rbk):
        zmat = jnp.concatenate(
            [jnp.concatenate([dup(a_w[:, sl]), dup(u_v[:, sl])], axis=1),
             jnp.concatenate([jnp.zeros_like(v2), v2], axis=1)], axis=0)
        rhs_end = rhs.astype(F32) * plrow[:, sl]
        lhs = jnp.concatenate([rhs_end.T.astype(BF16), m2], axis=0)
        stage5.append(jnp.dot(lhs, zmat, preferred_element_type=F32))
    yield
    ys, g_new = [], []
    for (sl, r2, v2, rhs, _), o5, g_state in zip(pairs, stage5, g_states):
        phi = jnp.where(eye, plrow[:, sl], 0.0) - o5[:n, :n]
        r_y = r2.astype(F32) - o5[n:, :n]
        o6 = _bdot(jnp.concatenate([r_y, phi], axis=0), g_state)
        ys.append(to_row(o6[:n] + o5[n:, n:]))
        g_new.append(o6[n:] + o5[:n, n:])
    return jnp.concatenate(ys, axis=1), g_new


def _run_interleaved(gens):
    results = [None] * len(gens)
    active = list(range(len(gens)))
    while active:
        for i in list(active):
            try:
                next(gens[i])
            except StopIteration as stop:
                results[i] = stop.value
                active.remove(i)
    return results


def _scan_kernel(atf, btf, ktf, rtf, vf, plf, atb, btb, ktb, rtb, vb, plb,
                 yf_ref, yb_ref, g_ref):
    s = pl.program_id(1)

    @pl.when(s == 0)
    def _():
        g_ref[...] = jnp.zeros_like(g_ref)

    ri = lax.broadcasted_iota(jnp.int32, (PAIR, PAIR), 0)
    ci = lax.broadcasted_iota(jnp.int32, (PAIR, PAIR), 1)
    same = (ri >= HEAD) == (ci >= HEAD)
    ti = ri & (HEAD - 1)
    tj = ci & (HEAD - 1)
    eye = ri == ci
    lane_lo = lax.broadcasted_iota(jnp.int32, (CHUNK, PAIR), 1) < HEAD
    masks = [(same & (tj < ti), same & (tj <= ti)),
             (same & (tj > ti), same & (tj >= ti))]
    quad_w = QUAD_HEADS * HEAD
    rq = lax.broadcasted_iota(jnp.int32, (quad_w, quad_w), 0)
    cq = lax.broadcasted_iota(jnp.int32, (quad_w, quad_w), 1)
    mask_bd = (rq // HEAD) == (cq // HEAD)
    eye_row = (lax.broadcasted_iota(jnp.int32, (CHUNK, quad_w), 1) & (HEAD - 1)
               ) == lax.broadcasted_iota(jnp.int32, (CHUNK, quad_w), 0)
    refs = [(atf, btf, ktf, rtf, vf, plf, yf_ref), (atb, btb, ktb, rtb, vb, plb, yb_ref)]
    gens, dests = [], []
    for bi in range(yf_ref.shape[0]):
        for d in range(2):
            at, bt, kt, rt, vv, plr, y_ref = refs[d]
            strict, incl = masks[d]
            for q in range(RWKV_W // quad_w):
                sl = slice(q * quad_w, (q + 1) * quad_w)
                gens.append(_quad_chunk(at[bi, 0, :, sl], bt[bi, 0, :, sl], kt[bi, 0, :, sl],
                                        rt[bi, 0, :, sl], vv[bi, :, sl], plr[bi, 0, 0, :, sl],
                                        [g_ref[bi, d, 2 * q], g_ref[bi, d, 2 * q + 1]],
                                        strict, incl, eye, lane_lo, mask_bd, eye_row))
                dests.append((y_ref, bi, d, q, sl))
    for (y, g_new), (y_ref, bi, d, q, sl) in zip(_run_interleaved(gens), dests):
        g_ref[bi, d, 2 * q] = g_new[0]
        g_ref[bi, d, 2 * q + 1] = g_new[1]
        y_ref[bi, :, sl] = y


def _scan_call(at, bt, kt, rt, v, plast, s_lat):
    b = at.shape[0]
    n_chunks = at.shape[2] // CHUNK
    n_ctx = CTX_LEN // CHUNK
    n_lat = s_lat // CHUNK

    def fwd_c(s):
        return s

    def bwd_c(s):
        return jnp.where(s < n_ctx, n_ctx - 1 - s, n_chunks + n_ctx - 1 - s)

    nb = SCAN_BATCH

    def dir_spec(d, cfun):
        return pl.BlockSpec((nb, 1, CHUNK, RWKV_W), lambda bi, s: (bi, d, cfun(s), 0))

    def v_spec(cfun):
        return pl.BlockSpec((nb, CHUNK, RWKV_W), lambda bi, s: (bi, cfun(s), 0))

    def pl_spec(d, cfun):
        return pl.BlockSpec((nb, 1, 1, 1, RWKV_W), lambda bi, s: (bi, d, cfun(s), 0, 0))

    in_specs = ([dir_spec(0, fwd_c)] * 4 + [v_spec(fwd_c), pl_spec(0, fwd_c)]
                + [dir_spec(1, bwd_c)] * 4 + [v_spec(bwd_c), pl_spec(1, bwd_c)])
    out_specs = [
        pl.BlockSpec((nb, CHUNK, RWKV_W), lambda bi, s: (bi, jnp.maximum(s - n_ctx, 0), 0)),
        pl.BlockSpec((nb, CHUNK, RWKV_W),
                     lambda bi, s: (bi, jnp.minimum(n_chunks - 1 - s, n_lat - 1), 0)),
    ]
    out_shape = [jax.ShapeDtypeStruct((b, s_lat, RWKV_W), F32)] * 2
    return pl.pallas_call(
        _scan_kernel,
        grid=(b // nb, n_chunks),
        in_specs=in_specs,
        out_specs=out_specs,
        out_shape=out_shape,
        scratch_shapes=[pltpu.VMEM((nb, 2, N_PAIRS, PAIR, PAIR), F32)],
        compiler_params=pltpu.CompilerParams(dimension_semantics=("arbitrary", "arbitrary"),
                                             vmem_limit_bytes=VMEM_LIMIT),
        name="scan",
    )(at, bt, kt, rt, v, plast, at, bt, kt, rt, v, plast)


def _mix_kernel(dft_ref, rev_ref, z_ref, yf_ref, yb_ref, bonus_ref, gate_ref, x_ref, mod_ref,
                gng_ref, gnb_ref, fg_ref, wout_ref, n2g_ref, ones_ref, x1_ref, h2_ref, zf_ref,
                *, fscale):
    seq = z_ref.shape[1]
    half = seq // 2
    rt = rev_ref.shape[0]

    @pl.when(pl.program_id(1) == 0)
    def _():
        mid = z_ref[0, half:half + 1, :FOURIER_W].astype(F32)
        row0 = lax.broadcasted_iota(jnp.int32, (rt, FOURIER_W), 0) == 0
        for part, sign in ((0, 1.0), (1, -1.0)):
            lanes = slice(part * FOURIER_W, (part + 1) * FOURIER_W)
            for r in range(half // rt):
                lo = seq - rt * (r + 1)
                nxt = (z_ref[0, lo + rt:lo + 2 * rt, lanes] if r > 0
                       else jnp.zeros((rt, FOURIER_W), BF16))
                rev = jnp.dot(rev_ref[...], jnp.concatenate([z_ref[0, lo:lo + rt, lanes], nxt], axis=0),
                              preferred_element_type=F32)
                folded = z_ref[0, r * rt:(r + 1) * rt, lanes].astype(F32) + sign * rev
                if part == 1 and r == 0:
                    folded = jnp.where(row0, mid, folded)
                zf_ref[part * half + r * rt:part * half + (r + 1) * rt, :] = folded.astype(BF16)

    yfour = jnp.dot(dft_ref[...], zf_ref[...], preferred_element_type=F32)
    four = _rms(yfour * fscale, fg_ref[...])

    ones_blk = ones_ref[...]
    y = yf_ref[0] + yb_ref[0]
    mu = _dot_exact_rhs(y, ones_blk) * (1.0 / HEAD)
    dy = y - mu
    var = _dot_exact_rhs(dy * dy, ones_blk) * (1.0 / HEAD)
    yn = dy * lax.rsqrt(var + GN_EPS) * gng_ref[...] + gnb_ref[...]
    rw = (yn + bonus_ref[0]) * gate_ref[0]

    out = _bdot(four, wout_ref[:FOURIER_W, :]) + _bdot(rw, wout_ref[FOURIER_W:, :])
    x1 = x_ref[0] + mod_ref[0, 2:3, :] * out
    x1_ref[0] = x1
    h2 = _rms(x1, n2g_ref[...]) * (1.0 + mod_ref[0, 4:5, :]) + mod_ref[0, 3:4, :]
    h2_ref[0] = h2.astype(BF16)


def _mix_call(dft, rev, z, yf, yb, bonus, gate, x, modx, gn_g, gn_b, f_g, w_out, norm2_g,
              ones_blk):
    b, s, d = x.shape
    tm = MIX_TILE

    def full(arr):
        nd = arr.ndim
        return pl.BlockSpec(arr.shape, lambda bi, m, _n=nd: (0,) * _n)

    tok = lambda w: pl.BlockSpec((1, tm, w), lambda bi, m: (bi, m, 0))
    in_specs = [
        pl.BlockSpec((tm, s), lambda bi, m: (m, 0)),
        full(rev),
        pl.BlockSpec((1, s, 2 * FOURIER_W), lambda bi, m: (bi, 0, 0)),
        tok(RWKV_W), tok(RWKV_W), tok(RWKV_W), tok(RWKV_W), tok(d),
        pl.BlockSpec((1, 6, d), lambda bi, m: (bi, 0, 0)),
        full(gn_g), full(gn_b), full(f_g), full(w_out), full(norm2_g), full(ones_blk),
    ]
    fscale = 1.0 / math.sqrt(s * (FOURIER_W // FOURIER_GROUPS))
    return pl.pallas_call(
        functools.partial(_mix_kernel, fscale=fscale),
        grid=(b, s // tm),
        in_specs=in_specs,
        out_specs=[tok(d), tok(d)],
        out_shape=[jax.ShapeDtypeStruct((b, s, d), F32), jax.ShapeDtypeStruct((b, s, d), BF16)],
        scratch_shapes=[pltpu.VMEM((s, FOURIER_W), BF16)],
        compiler_params=pltpu.CompilerParams(dimension_semantics=("arbitrary", "arbitrary"),
                                             vmem_limit_bytes=VMEM_LIMIT),
        name="mix",
    )(dft, rev, z, yf, yb, bonus, gate, x, modx, gn_g, gn_b, f_g, w_out, norm2_g, ones_blk)


def _ffn_up_kernel(h2_ref, wg_ref, wv_ref, cwg_ref, cwv_ref, cbg_ref, cbv_ref, act_ref):
    t = h2_ref.shape[1]
    rows = FFN_ROWS
    n_blk = t // rows
    w = jnp.concatenate([wg_ref[...], wv_ref[...]], axis=1)
    cw = jnp.concatenate([cwg_ref[...], cwv_ref[...]], axis=1).astype(BF16)
    cb = jnp.concatenate([cbg_ref[...], cbv_ref[...]], axis=1).astype(BF16)
    col = lax.broadcasted_iota(jnp.int32, (rows, 2 * FF_TILE), 0) & (GRID_W - 1)
    has_left = col > 0
    has_right = col < GRID_W - 1
    zeros = jnp.zeros((GRID_W, 2 * FF_TILE), BF16)

    def row_mixes(i):
        u = jnp.dot(h2_ref[0, i * rows:(i + 1) * rows, :], w, preferred_element_type=F32)
        u_l = jnp.where(has_left, pltpu.roll(u, 1, 0), 0.0).astype(BF16)
        u_r = jnp.where(has_right, pltpu.roll(u, rows - 1, 0), 0.0).astype(BF16)
        u_c = u.astype(BF16)
        return [cw[3 * kh:3 * kh + 1] * u_l + cw[3 * kh + 1:3 * kh + 2] * u_c
                + cw[3 * kh + 2:3 * kh + 3] * u_r for kh in range(3)]

    def finish(i, prev, cur, nxt):
        above = zeros if prev is None else prev[0][rows - GRID_W:]
        below = zeros if nxt is None else nxt[2][:GRID_W]
        c = (cur[1] + jnp.concatenate([above, cur[0][:rows - GRID_W]], axis=0)
             + jnp.concatenate([cur[2][GRID_W:], below], axis=0) + cb)
        cg = c[:, :FF_TILE]
        act_ref[0, i * rows:(i + 1) * rows, :] = cg * _sigmoid(cg) * c[:, FF_TILE:]

    mixes = [None] * (n_blk + 1)
    for i in range(n_blk):
        mixes[i] = row_mixes(i)
        if i >= 1:
            finish(i - 1, mixes[i - 2] if i >= 2 else None, mixes[i - 1], mixes[i])
    finish(n_blk - 1, mixes[n_blk - 2] if n_blk >= 2 else None, mixes[n_blk - 1], None)


def _ffn_up_call(h2, w_up, conv_w9, conv_b):
    b, s, d = h2.shape
    n_f = D_FF // FF_TILE
    in_specs = [
        pl.BlockSpec((1, s, d), lambda bi, f: (bi, 0, 0)),
        pl.BlockSpec((d, FF_TILE), lambda bi, f: (0, f)),
        pl.BlockSpec((d, FF_TILE), lambda bi, f: (0, n_f + f)),
        pl.BlockSpec((9, FF_TILE), lambda bi, f: (0, f)),
        pl.BlockSpec((9, FF_TILE), lambda bi, f: (0, n_f + f)),
        pl.BlockSpec((1, FF_TILE), lambda bi, f: (0, f)),
        pl.BlockSpec((1, FF_TILE), lambda bi, f: (0, n_f + f)),
    ]
    return pl.pallas_call(
        _ffn_up_kernel,
        grid=(b, n_f),
        in_specs=in_specs,
        out_specs=pl.BlockSpec((1, s, FF_TILE), lambda bi, f: (bi, 0, f)),
        out_shape=jax.ShapeDtypeStruct((b, s, D_FF), BF16),
        compiler_params=pltpu.CompilerParams(dimension_semantics=("arbitrary", "arbitrary"),
                                             vmem_limit_bytes=VMEM_LIMIT),
        name="ffn_up",
    )(h2, w_up, w_up, conv_w9, conv_w9, conv_b, conv_b)


def _ffn_down_kernel(act_ref, wd_ref, x1_ref, mod_ref, fg_ref, out_ref):
    y = jnp.dot(act_ref[0], wd_ref[...], preferred_element_type=F32)
    out_ref[0] = _rms(x1_ref[0] + mod_ref[0, 5:6, :] * y, fg_ref[...])


def _ffn_down_call(act, w_down, x1, modx, final_g):
    b, s, d = x1.shape
    tm = DOWN_TILE
    in_specs = [
        pl.BlockSpec((1, tm, D_FF), lambda bi, m: (bi, m, 0)),
        pl.BlockSpec((D_FF, d), lambda bi, m: (0, 0)),
        pl.BlockSpec((1, tm, d), lambda bi, m: (bi, m, 0)),
        pl.BlockSpec((1, 6, d), lambda bi, m: (bi, 0, 0)),
        pl.BlockSpec((1, d), lambda bi, m: (0, 0)),
    ]
    return pl.pallas_call(
        _ffn_down_kernel,
        grid=(b, s // tm),
        in_specs=in_specs,
        out_specs=pl.BlockSpec((1, tm, d), lambda bi, m: (bi, m, 0)),
        out_shape=jax.ShapeDtypeStruct((b, s, d), F32),
        compiler_params=pltpu.CompilerParams(dimension_semantics=("arbitrary", "arbitrary"),
                                             vmem_limit_bytes=VMEM_LIMIT),
        name="ffn_down",
    )(act, w_down, x1, modx, final_g)


@functools.lru_cache(maxsize=None)
def _constants(seq):
    gw = FOURIER_W // FOURIER_GROUPS
    nm = np.outer(np.arange(gw), np.arange(gw)) % gw
    ang = 2.0 * np.pi * nm / gw
    cs = np.zeros((FOURIER_W, 2 * FOURIER_W), np.float64)
    for gi in range(FOURIER_GROUPS):
        sl = slice(gi * gw, (gi + 1) * gw)
        cs[sl, sl] = np.cos(ang)
        cs[sl, FOURIER_W + gi * gw:FOURIER_W + (gi + 1) * gw] = np.sin(ang)
    half = seq // 2
    k_idx = np.arange(seq, dtype=np.int64)[:, None]
    ang_c = 2.0 * np.pi * ((k_idx * np.arange(0, half + 1, dtype=np.int64)[None, :]) % seq) / seq
    ang_s = 2.0 * np.pi * ((k_idx * np.arange(1, half, dtype=np.int64)[None, :]) % seq) / seq
    dft = np.concatenate([np.cos(ang_c), -np.sin(ang_s)], axis=1)
    ri = np.arange(REV_TILE)
    rev = (np.arange(2 * REV_TILE)[None, :] == (REV_TILE - ri)[:, None]).astype(np.float64)
    head_id = np.arange(RWKV_W) // HEAD
    ones_blk = (head_id[:, None] == head_id[None, :]).astype(np.float64)
    ti = np.arange(TILE)
    same_chunk = (ti[:, None] // CHUNK) == (ti[None, :] // CHUNK)
    tri = np.stack([same_chunk & (ti[None, :] <= ti[:, None]),
                    same_chunk & (ti[None, :] >= ti[:, None])]).astype(np.float64)
    to_bf16 = lambda arr: np.asarray(arr, dtype=np.float32).astype(BF16)
    return (np.asarray(cs, dtype=np.float32), np.asarray(dft, dtype=np.float32),
            to_bf16(ones_blk), to_bf16(tri), to_bf16(rev))


def _block_diag_dirs(w):
    z = jnp.zeros_like(w[0])
    return jnp.concatenate([jnp.concatenate([w[0], z], axis=1),
                            jnp.concatenate([z, w[1]], axis=1)], axis=0)


def kernel(x, c, ctx, c_ctx, ada_w, ada_b, norm1_g, norm2_g, w_in, rwkv_conv_w, decay_w0, decay_w2, iclr_a0, iclr_a2, gate_g2, k_k, k_a, r_k, gn_g, gn_b, fourier_g, w_out, ffn_w_up, ffn_conv_w, ffn_conv_b, ffn_w_down, final_g):
    b, s, d = x.shape
    assert ada_w.shape[0] == 1, "single-layer configuration"
    assert (b, s, d) == (c.shape[0], s, D_MODEL) and ctx.shape == (b, CTX_LEN, d)
    assert s % TILE == 0 and CTX_LEN == TILE
    cs, dft, ones_blk, tri, rev = (jnp.asarray(t) for t in _constants(s))
    cs, dft = cs.astype(BF16), dft.astype(BF16)

    cc = jnp.concatenate([c, c_ctx[None, :], jnp.zeros((16 - b - 1, d), F32)], axis=0)
    mod = _mod_call(cc, ada_w[0], ada_b[0][None, :])
    modx = mod[:b].reshape(b, 6, d)
    modc = jnp.broadcast_to(mod[b].reshape(1, 6, d)[:, :2], (b, 2, d))
    modsel = jnp.stack([modc, modx[:, :2]], axis=1)

    row = lambda t: t.reshape(1, -1)
    at, bt, kt, rt, v, plast, bonus, gate, z = _prep_call(
        x, ctx, modsel, row(norm1_g[0]), w_in[0].astype(BF16), rwkv_conv_w[0],
        row(decay_w0[0]), _block_diag_dirs(decay_w2[0]).astype(BF16),
        row(iclr_a0[0]), _block_diag_dirs(iclr_a2[0]).astype(BF16),
        gate_g2[0].astype(BF16), row(k_k[0]), row(k_a[0]), row(r_k[0]), cs, ones_blk, tri)

    yf, yb = _scan_call(at, bt, kt, rt, v, plast, s)

    x1, h2 = _mix_call(dft, rev, z, yf, yb, bonus, gate, x, modx, row(gn_g[0]), row(gn_b[0]),
                       row(fourier_g[0]), w_out[0].astype(BF16), row(norm2_g[0]), ones_blk)

    act = _ffn_up_call(h2, ffn_w_up[0].astype(BF16), ffn_conv_w[0].reshape(9, 2 * D_FF),
                       row(ffn_conv_b[0]))
    return _ffn_down_call(act, ffn_w_down[0].astype(BF16), x1, modx, row(final_g))
```

```python
import functools
import math

import numpy as np
import jax
import jax.numpy as jnp
from jax import lax
from jax.experimental import pallas as pl
from jax.experimental.pallas import tpu as pltpu

F32 = jnp.float32
BF16 = jnp.bfloat16

D_MODEL = 1024
CTX_LEN = 256
GRID_W = 64
FOURIER_W = 512
FOURIER_GROUPS = 8
RWKV_W = 512
HEAD = 64
N_HEADS = 8
DECAY_RANK = 64
ICLR_RANK = 64
GATE_RANK = 128
D_FF = 2816
NORM_EPS = 1e-6
GN_EPS = 64e-5
KK_EPS = 1e-12

CHUNK = 64
TILE = 256
MIX_TILE = 512
REV_TILE = 256
CHUNKS_PER_TILE = TILE // CHUNK
PREP_BATCH = 2
PAIR = 2 * HEAD
N_PAIRS = RWKV_W // PAIR
QUAD_HEADS = 4
SCAN_BATCH = 8
MOD_TILE = 1536
FF_TILE = 1408
FFN_ROWS = 256
DOWN_TILE = 1024
VMEM_LIMIT = 56 * 1024 * 1024
HEAD_SUM_TERMS = 1
CUMSUM_TERMS = 2
MOD_WEIGHT_TERMS = 2

RKV_LO = FOURIER_W
RKV_HI = FOURIER_W + 3 * RWKV_W
WD_LO = RKV_HI
AD_LO = WD_LO + 2 * DECAY_RANK
GD_LO = AD_LO + 2 * ICLR_RANK
PROJ_W = GD_LO + GATE_RANK


def _bdot(a, b):
    return jnp.dot(a.astype(BF16), b.astype(BF16), preferred_element_type=F32)


def _bdot_nt(a, b):
    return lax.dot_general(a.astype(BF16), b.astype(BF16), (((1,), (1,)), ((), ())),
                           preferred_element_type=F32)


def _bdot_tn(a, b):
    return lax.dot_general(a.astype(BF16), b.astype(BF16), (((0,), (0,)), ((), ())),
                           preferred_element_type=F32)


def _split_terms(a, terms):
    out, rem = [], a
    for _ in range(terms):
        hi = rem.astype(BF16)
        out.append(hi)
        rem = rem - hi.astype(F32)
    return out


def _dot_exact_rhs(a, b_exact, terms=HEAD_SUM_TERMS):
    acc = None
    for piece in _split_terms(a, terms):
        t = jnp.dot(piece, b_exact, preferred_element_type=F32)
        acc = t if acc is None else acc + t
    return acc


def _dot_exact_lhs(a_exact, b, terms=CUMSUM_TERMS):
    acc = None
    for piece in _split_terms(b, terms):
        t = jnp.dot(a_exact, piece, preferred_element_type=F32)
        acc = t if acc is None else acc + t
    return acc


def _sigmoid(x):
    return 1.0 / (1.0 + jnp.exp(-x))


def _rms(x, g):
    return x * lax.rsqrt(jnp.mean(x * x, axis=-1, keepdims=True) + NORM_EPS) * g


def _mod_kernel(c_ref, w_ref, b_ref, o_ref):
    cc = c_ref[...]
    s = cc * _sigmoid(cc)
    rows = s.shape[0]
    s_parts = jnp.concatenate(_split_terms(s, 3), axis=0)
    acc = b_ref[...]
    for wp in _split_terms(w_ref[...], MOD_WEIGHT_TERMS):
        t = jnp.dot(s_parts, wp, preferred_element_type=F32)
        acc = acc + (t[:rows] + t[rows:2 * rows] + t[2 * rows:])
    o_ref[...] = acc


def _mod_call(cc, ada_w, ada_b):
    rows, d = cc.shape
    n = ada_w.shape[1]
    tn = MOD_TILE
    return pl.pallas_call(
        _mod_kernel,
        grid=(n // tn,),
        in_specs=[pl.BlockSpec((rows, d), lambda i: (0, 0)),
                  pl.BlockSpec((d, tn), lambda i: (0, i)),
                  pl.BlockSpec((1, tn), lambda i: (0, i))],
        out_specs=pl.BlockSpec((rows, tn), lambda i: (0, i)),
        out_shape=jax.ShapeDtypeStruct((rows, n), F32),
        compiler_params=pltpu.CompilerParams(dimension_semantics=("arbitrary",),
                                             vmem_limit_bytes=VMEM_LIMIT),
        name="mod",
    )(cc, ada_w, ada_b)


def _prep_kernel(x_ref, xp_ref, xn_ref, ctx_ref, ms_ref, g1_ref, win_ref, cw_ref,
                 w0_ref, w2_ref, a0_ref, a2_ref, g2_ref, kk_ref, ka_ref, rk_ref,
                 cs_ref, ones_ref, tri_ref,
                 at_ref, bt_ref, kt_ref, rt_ref, v_ref, pl_ref, bonus_ref, gate_ref, z_ref):
    j = pl.program_id(1)
    refs = (x_ref, xp_ref, xn_ref, ctx_ref, ms_ref, g1_ref, win_ref, cw_ref,
            w0_ref, w2_ref, a0_ref, a2_ref, kk_ref, ka_ref, ones_ref, tri_ref,
            at_ref, bt_ref, kt_ref, rt_ref, v_ref, pl_ref)
    tiles = _run_interleaved([_prep_tile(bi, *refs) for bi in range(x_ref.shape[0])])

    @pl.when(j >= 1)
    def _():
        ones_blk = ones_ref[...]
        for bi, (u_four, r, v, gd, kd_sum) in enumerate(tiles):
            bs = _dot_exact_rhs(r * rk_ref[...] * kd_sum, ones_blk)
            bonus_ref[bi] = bs * v
            gate_ref[bi] = _bdot(_sigmoid(gd), g2_ref[...])
            z_ref[bi] = _bdot(u_four, cs_ref[...]).astype(BF16)


def _prep_tile(bi, x_ref, xp_ref, xn_ref, ctx_ref, ms_ref, g1_ref, win_ref, cw_ref,
               w0_ref, w2_ref, a0_ref, a2_ref, kk_ref, ka_ref, ones_ref, tri_ref,
               at_ref, bt_ref, kt_ref, rt_ref, v_ref, pl_ref):
    j = pl.program_id(1)
    n_j = pl.num_programs(1)
    is_ctx = j == 0
    shift = ms_ref[bi, 0, 0:1, :]
    scale = ms_ref[bi, 0, 1:2, :]
    g = g1_ref[...]

    def norm_mod(xr):
        return _rms(xr, g) * (1.0 + scale) + shift

    xin = jnp.where(is_ctx, ctx_ref[bi], x_ref[bi])
    h = norm_mod(xin).astype(BF16)
    u = jnp.dot(h, win_ref[...], preferred_element_type=F32)

    halo = jnp.concatenate([xp_ref[bi], xn_ref[bi]], axis=0)
    uh = jnp.dot(norm_mod(halo).astype(BF16), win_ref[:, RKV_LO:RKV_HI],
                 preferred_element_type=F32)
    yield
    prev_row = jnp.where(j >= 2, uh[7:8], 0.0)
    next_row = jnp.where(jnp.logical_and(j >= 1, j < n_j - 1), uh[8:9], 0.0)

    rkv = u[:, RKV_LO:RKV_HI]
    row = lax.broadcasted_iota(jnp.int32, rkv.shape, 0)
    u_m1 = jnp.where(row == 0, prev_row, pltpu.roll(rkv, 1, 0))
    u_p1 = jnp.where(row == TILE - 1, next_row, pltpu.roll(rkv, TILE - 1, 0))
    rkv = cw_ref[0:1, :] * u_m1 + cw_ref[1:2, :] * rkv + cw_ref[2:3, :] * u_p1
    r = rkv[:, 0:RWKV_W]
    k = rkv[:, RWKV_W:2 * RWKV_W]
    v = rkv[:, 2 * RWKV_W:3 * RWKV_W]

    wd = u[:, WD_LO:AD_LO]
    ad = u[:, AD_LO:GD_LO]
    gd = u[:, GD_LO:PROJ_W]
    w_lora = w0_ref[...] + _bdot(jnp.tanh(wd), w2_ref[...])
    lw = -math.exp(-0.5) * _sigmoid(w_lora)
    a = _sigmoid(a0_ref[...] + _bdot(ad, a2_ref[...]))

    ones_blk = ones_ref[...]
    kraw = k * kk_ref[...]
    kk = kraw * lax.rsqrt(_dot_exact_rhs(kraw * kraw, ones_blk) + KK_EPS)
    ka = ka_ref[...]
    kd = [k * (1.0 + (a[:, d * RWKV_W:(d + 1) * RWKV_W] - 1.0) * ka) for d in range(2)]

    v_ref[bi] = v.astype(BF16)
    for d in range(2):
        lwd = lw[:, d * RWKV_W:(d + 1) * RWKV_W]
        ad_ = a[:, d * RWKV_W:(d + 1) * RWKV_W]
        c = _dot_exact_lhs(tri_ref[d], lwd)
        e_pos = jnp.exp(c)
        e_neg = jnp.exp(-c)
        e_prev = jnp.exp(c - lwd)
        at_ref[bi, d] = (kk * e_prev).astype(BF16)
        bt_ref[bi, d] = (kk * ad_ * e_neg).astype(BF16)
        kt_ref[bi, d] = (kd[d] * e_neg).astype(BF16)
        rt_ref[bi, d] = (r * e_pos).astype(BF16)
        for q in range(CHUNKS_PER_TILE):
            last = q * CHUNK + (CHUNK - 1 if d == 0 else 0)
            pl_ref[bi, d, q] = e_pos[last:last + 1, :]
    return u[:, 0:FOURIER_W], r, v, gd, kd[0] + kd[1]


def _prep_call(x, ctx, modsel, norm1_g, w_in, conv_w, w0, w2bd, a0, a2bd, g2, k_k, k_a, r_k,
               cs, ones_blk, tri):
    b, s, d = x.shape
    n_j = 1 + s // TILE
    n_chunks = (CTX_LEN + s) // CHUNK
    t_all = CTX_LEN + s
    c2 = 2 * RWKV_W

    def full(arr):
        nd = arr.ndim
        return pl.BlockSpec(arr.shape, lambda bi, j, _n=nd: (0,) * _n)

    rows8 = TILE // 8
    nb = PREP_BATCH
    in_specs = [
        pl.BlockSpec((nb, TILE, d), lambda bi, j: (bi, jnp.maximum(j - 1, 0), 0)),
        pl.BlockSpec((nb, 8, d), lambda bi, j: (bi, jnp.maximum((j - 1) * rows8 - 1, 0), 0)),
        pl.BlockSpec((nb, 8, d), lambda bi, j: (bi, jnp.minimum(j * rows8, s // 8 - 1), 0)),
        pl.BlockSpec((nb, CTX_LEN, d), lambda bi, j: (bi, 0, 0)),
        pl.BlockSpec((nb, 1, 2, d), lambda bi, j: (bi, jnp.minimum(j, 1), 0, 0)),
        full(norm1_g), full(w_in), full(conv_w), full(w0), full(w2bd), full(a0), full(a2bd),
        full(g2), full(k_k), full(k_a), full(r_k), full(cs), full(ones_blk), full(tri),
    ]
    lat = lambda bi, j: (bi, jnp.maximum(j - 1, 0), 0)
    out_specs = [
        pl.BlockSpec((nb, 2, TILE, RWKV_W), lambda bi, j: (bi, 0, j, 0)),
        pl.BlockSpec((nb, 2, TILE, RWKV_W), lambda bi, j: (bi, 0, j, 0)),
        pl.BlockSpec((nb, 2, TILE, RWKV_W), lambda bi, j: (bi, 0, j, 0)),
        pl.BlockSpec((nb, 2, TILE, RWKV_W), lambda bi, j: (bi, 0, j, 0)),
        pl.BlockSpec((nb, TILE, RWKV_W), lambda bi, j: (bi, j, 0)),
        pl.BlockSpec((nb, 2, CHUNKS_PER_TILE, 1, RWKV_W), lambda bi, j: (bi, 0, j, 0, 0)),
        pl.BlockSpec((nb, TILE, RWKV_W), lat),
        pl.BlockSpec((nb, TILE, RWKV_W), lat),
        pl.BlockSpec((nb, TILE, 2 * FOURIER_W), lat),
    ]
    out_shape = [
        jax.ShapeDtypeStruct((b, 2, t_all, RWKV_W), BF16),
        jax.ShapeDtypeStruct((b, 2, t_all, RWKV_W), BF16),
        jax.ShapeDtypeStruct((b, 2, t_all, RWKV_W), BF16),
        jax.ShapeDtypeStruct((b, 2, t_all, RWKV_W), BF16),
        jax.ShapeDtypeStruct((b, t_all, RWKV_W), BF16),
        jax.ShapeDtypeStruct((b, 2, n_chunks, 1, RWKV_W), F32),
        jax.ShapeDtypeStruct((b, s, RWKV_W), F32),
        jax.ShapeDtypeStruct((b, s, RWKV_W), F32),
        jax.ShapeDtypeStruct((b, s, 2 * FOURIER_W), BF16),
    ]
    return pl.pallas_call(
        _prep_kernel,
        grid=(b // nb, n_j),
        in_specs=in_specs,
        out_specs=out_specs,
        out_shape=out_shape,
        compiler_params=pltpu.CompilerParams(dimension_semantics=("arbitrary", "arbitrary"),
                                             vmem_limit_bytes=VMEM_LIMIT),
        name="prep",
    )(x, x, x, ctx, modsel, norm1_g, w_in, conv_w, w0, w2bd, a0, a2bd, g2, k_k, k_a, r_k,
      cs, ones_blk, tri)


def _quad_chunk(a, bm, km, r, v, plrow, g_states, strict, incl, eye, lane_lo, mask_bd, eye_row):
    n = PAIR

    def dup(xv):
        z = jnp.zeros_like(xv)
        return jnp.concatenate([jnp.where(lane_lo, xv, z), jnp.where(lane_lo, z, xv)], axis=0)

    def to_row(x2):
        return x2[:CHUNK] + x2[CHUNK:]

    def bdiag(xr):
        return jnp.where(mask_bd, jnp.concatenate([xr] * QUAD_HEADS, axis=0), jnp.zeros((), xr.dtype))

    pairs = []
    for p in range(2):
        sl = slice(p * n, (p + 1) * n)
        a2, b2, k2, r2, v2 = (dup(t[:, sl]) for t in (a, bm, km, r, v))
        rhs = jnp.concatenate([b2, k2], axis=0)
        s = _bdot_nt(jnp.concatenate([a2, r2], axis=0), rhs)
        pairs.append((sl, r2, v2, rhs, s))
    yield

    a_ab = jnp.concatenate([to_row(jnp.where(strict, s[:n, :n], 0.0)) for *_, s in pairs], axis=1)
    a_ak = jnp.concatenate([to_row(jnp.where(strict, s[:n, n:], 0.0)) for *_, s in pairs], axis=1)
    m_rbk = [jnp.concatenate([jnp.where(incl, s[n:, :n], 0.0), jnp.where(incl, s[n:, n:], 0.0)],
                             axis=1).astype(BF16) for *_, s in pairs]

    pk = (-a_ab).astype(BF16)
    t_row = jnp.where(eye_row, 1.0, 0.0) - a_ab
    av = _bdot(a_ak, bdiag(v))
    pk = jnp.dot(pk, bdiag(pk), preferred_element_type=F32).astype(BF16)
    yield
    for _ in range(int(math.log2(CHUNK)) - 2):
        both = jnp.dot(jnp.concatenate([pk, t_row.astype(BF16)], axis=0), bdiag(pk),
                       preferred_element_type=F32)
        yield
        pk = both[:CHUNK].astype(BF16)
        t_row = t_row + both[CHUNK:]
    t_row = t_row + jnp.dot(t_row.astype(BF16), bdiag(pk), preferred_element_type=F32)
    yield
    tx = _bdot(t_row, jnp.concatenate([bdiag(a), bdiag(av.astype(BF16))], axis=1))
    yield
    a_w = tx[:, :2 * n].astype(BF16)
    u_v = (-tx[:, 2 * n:]).astype(BF16)
    stage5 = []
    for (sl, r2, v2, rhs, _), m2 in zip(pairs, m_rbk):
        zmat = jnp.concatenate(
            [jnp.concatenate([dup(a_w[:, sl]), dup(u_v[:, sl])], axis=1),
             jnp.concatenate([jnp.zeros_like(v2), v2], axis=1)], axis=0)
        rhs_end = rhs.astype(F32) * plrow[:, sl]
        lhs = jnp.concatenate([rhs_end.T.astype(BF16), m2], axis=0)
        stage5.append(jnp.dot(lhs, zmat, preferred_element_type=F32))
    yield
    ys, g_new = [], []
    for (sl, r2, v2, rhs, _), o5, g_state in zip(pairs, stage5, g_states):
        phi = jnp.where(eye, plrow[:, sl], 0.0) - o5[:n, :n]
        r_y = r2.astype(F32) - o5[n:, :n]
        o6 = _bdot(jnp.concatenate([r_y, phi], axis=0), g_state)
        ys.append(to_row(o6[:n] + o5[n:, n:]))
        g_new.append(o6[n:] + o5[:n, n:])
    return jnp.concatenate(ys, axis=1), g_new


def _run_interleaved(gens):
    results = [None] * len(gens)
    active = list(range(len(gens)))
    while active:
        for i in list(active):
            try:
                next(gens[i])
            except StopIteration as stop:
                results[i] = stop.value
                active.remove(i)
    return results


def _scan_kernel(atf, btf, ktf, rtf, vf, plf, atb, btb, ktb, rtb, vb, plb,
                 yf_ref, yb_ref, g_ref):
    s = pl.program_id(1)

    @pl.when(s == 0)
    def _():
        g_ref[...] = jnp.zeros_like(g_ref)

    ri = lax.broadcasted_iota(jnp.int32, (PAIR, PAIR), 0)
    ci = lax.broadcasted_iota(jnp.int32, (PAIR, PAIR), 1)
    same = (ri >= HEAD) == (ci >= HEAD)
    ti = ri & (HEAD - 1)
    tj = ci & (HEAD - 1)
    eye = ri == ci
    lane_lo = lax.broadcasted_iota(jnp.int32, (CHUNK, PAIR), 1) < HEAD
    masks = [(same & (tj < ti), same & (tj <= ti)),
             (same & (tj > ti), same & (tj >= ti))]
    quad_w = QUAD_HEADS * HEAD
    rq = lax.broadcasted_iota(jnp.int32, (quad_w, quad_w), 0)
    cq = lax.broadcasted_iota(jnp.int32, (quad_w, quad_w), 1)
    mask_bd = (rq // HEAD) == (cq // HEAD)
    eye_row = (lax.broadcasted_iota(jnp.int32, (CHUNK, quad_w), 1) & (HEAD - 1)
               ) == lax.broadcasted_iota(jnp.int32, (CHUNK, quad_w), 0)
    refs = [(atf, btf, ktf, rtf, vf, plf, yf_ref), (atb, btb, ktb, rtb, vb, plb, yb_ref)]
    gens, dests = [], []
    for bi in range(yf_ref.shape[0]):
        for d in range(2):
            at, bt, kt, rt, vv, plr, y_ref = refs[d]
            strict, incl = masks[d]
            for q in range(RWKV_W // quad_w):
                sl = slice(q * quad_w, (q + 1) * quad_w)
                gens.append(_quad_chunk(at[bi, 0, :, sl], bt[bi, 0, :, sl], kt[bi, 0, :, sl],
                                        rt[bi, 0, :, sl], vv[bi, :, sl], plr[bi, 0, 0, :, sl],
                                        [g_ref[bi, d, 2 * q], g_ref[bi, d, 2 * q + 1]],
                                        strict, incl, eye, lane_lo, mask_bd, eye_row))
                dests.append((y_ref, bi, d, q, sl))
    for (y, g_new), (y_ref, bi, d, q, sl) in zip(_run_interleaved(gens), dests):
        g_ref[bi, d, 2 * q] = g_new[0]
        g_ref[bi, d, 2 * q + 1] = g_new[1]
        y_ref[bi, :, sl] = y


def _scan_call(at, bt, kt, rt, v, plast, s_lat):
    b = at.shape[0]
    n_chunks = at.shape[2] // CHUNK
    n_ctx = CTX_LEN // CHUNK
    n_lat = s_lat // CHUNK

    def fwd_c(s):
        return s

    def bwd_c(s):
        return jnp.where(s < n_ctx, n_ctx - 1 - s, n_chunks + n_ctx - 1 - s)

    nb = SCAN_BATCH

    def dir_spec(d, cfun):
        return pl.BlockSpec((nb, 1, CHUNK, RWKV_W), lambda bi, s: (bi, d, cfun(s), 0))

    def v_spec(cfun):
        return pl.BlockSpec((nb, CHUNK, RWKV_W), lambda bi, s: (bi, cfun(s), 0))

    def pl_spec(d, cfun):
        return pl.BlockSpec((nb, 1, 1, 1, RWKV_W), lambda bi, s: (bi, d, cfun(s), 0, 0))

    in_specs = ([dir_spec(0, fwd_c)] * 4 + [v_spec(fwd_c), pl_spec(0, fwd_c)]
                + [dir_spec(1, bwd_c)] * 4 + [v_spec(bwd_c), pl_spec(1, bwd_c)])
    out_specs = [
        pl.BlockSpec((nb, CHUNK, RWKV_W), lambda bi, s: (bi, jnp.maximum(s - n_ctx, 0), 0)),
        pl.BlockSpec((nb, CHUNK, RWKV_W),
                     lambda bi, s: (bi, jnp.minimum(n_chunks - 1 - s, n_lat - 1), 0)),
    ]
    out_shape = [jax.ShapeDtypeStruct((b, s_lat, RWKV_W), F32)] * 2
    return pl.pallas_call(
        _scan_kernel,
        grid=(b // nb, n_chunks),
        in_specs=in_specs,
        out_specs=out_specs,
        out_shape=out_shape,
        scratch_shapes=[pltpu.VMEM((nb, 2, N_PAIRS, PAIR, PAIR), F32)],
        compiler_params=pltpu.CompilerParams(dimension_semantics=("arbitrary", "arbitrary"),
                                             vmem_limit_bytes=VMEM_LIMIT),
        name="scan",
    )(at, bt, kt, rt, v, plast, at, bt, kt, rt, v, plast)


def _mix_kernel(dft_ref, rev_ref, z_ref, yf_ref, yb_ref, bonus_ref, gate_ref, x_ref, mod_ref,
                gng_ref, gnb_ref, fg_ref, wout_ref, n2g_ref, ones_ref, x1_ref, h2_ref, zf_ref,
                *, fscale):
    seq = z_ref.shape[1]
    half = seq // 2
    rt = rev_ref.shape[0]

    @pl.when(pl.program_id(1) == 0)
    def _():
        mid = z_ref[0, half:half + 1, :FOURIER_W].astype(F32)
        row0 = lax.broadcasted_iota(jnp.int32, (rt, FOURIER_W), 0) == 0
        for part, sign in ((0, 1.0), (1, -1.0)):
            lanes = slice(part * FOURIER_W, (part + 1) * FOURIER_W)
            for r in range(half // rt):
                lo = seq - rt * (r + 1)
                nxt = (z_ref[0, lo + rt:lo + 2 * rt, lanes] if r > 0
                       else jnp.zeros((rt, FOURIER_W), BF16))
                rev = jnp.dot(rev_ref[...], jnp.concatenate([z_ref[0, lo:lo + rt, lanes], nxt], axis=0),
                              preferred_element_type=F32)
                folded = z_ref[0, r * rt:(r + 1) * rt, lanes].astype(F32) + sign * rev
                if part == 1 and r == 0:
                    folded = jnp.where(row0, mid, folded)
                zf_ref[part * half + r * rt:part * half + (r + 1) * rt, :] = folded.astype(BF16)

    yfour = jnp.dot(dft_ref[...], zf_ref[...], preferred_element_type=F32)
    four = _rms(yfour * fscale, fg_ref[...])

    ones_blk = ones_ref[...]
    y = yf_ref[0] + yb_ref[0]
    mu = _dot_exact_rhs(y, ones_blk) * (1.0 / HEAD)
    dy = y - mu
    var = _dot_exact_rhs(dy * dy, ones_blk) * (1.0 / HEAD)
    yn = dy * lax.rsqrt(var + GN_EPS) * gng_ref[...] + gnb_ref[...]
    rw = (yn + bonus_ref[0]) * gate_ref[0]

    out = _bdot(four, wout_ref[:FOURIER_W, :]) + _bdot(rw, wout_ref[FOURIER_W:, :])
    x1 = x_ref[0] + mod_ref[0, 2:3, :] * out
    x1_ref[0] = x1
    h2 = _rms(x1, n2g_ref[...]) * (1.0 + mod_ref[0, 4:5, :]) + mod_ref[0, 3:4, :]
    h2_ref[0] = h2.astype(BF16)


def _mix_call(dft, rev, z, yf, yb, bonus, gate, x, modx, gn_g, gn_b, f_g, w_out, norm2_g,
              ones_blk):
    b, s, d = x.shape
    tm = MIX_TILE

    def full(arr):
        nd = arr.ndim
        return pl.BlockSpec(arr.shape, lambda bi, m, _n=nd: (0,) * _n)

    tok = lambda w: pl.BlockSpec((1, tm, w), lambda bi, m: (bi, m, 0))
    in_specs = [
        pl.BlockSpec((tm, s), lambda bi, m: (m, 0)),
        full(rev),
        pl.BlockSpec((1, s, 2 * FOURIER_W), lambda bi, m: (bi, 0, 0)),
        tok(RWKV_W), tok(RWKV_W), tok(RWKV_W), tok(RWKV_W), tok(d),
        pl.BlockSpec((1, 6, d), lambda bi, m: (bi, 0, 0)),
        full(gn_g), full(gn_b), full(f_g), full(w_out), full(norm2_g), full(ones_blk),
    ]
    fscale = 1.0 / math.sqrt(s * (FOURIER_W // FOURIER_GROUPS))
    return pl.pallas_call(
        functools.partial(_mix_kernel, fscale=fscale),
        grid=(b, s // tm),
        in_specs=in_specs,
        out_specs=[tok(d), tok(d)],
        out_shape=[jax.ShapeDtypeStruct((b, s, d), F32), jax.ShapeDtypeStruct((b, s, d), BF16)],
        scratch_shapes=[pltpu.VMEM((s, FOURIER_W), BF16)],
        compiler_params=pltpu.CompilerParams(dimension_semantics=("arbitrary", "arbitrary"),
                                             vmem_limit_bytes=VMEM_LIMIT),
        name="mix",
    )(dft, rev, z, yf, yb, bonus, gate, x, modx, gn_g, gn_b, f_g, w_out, norm2_g, ones_blk)


def _ffn_up_kernel(h2_ref, wg_ref, wv_ref, cwg_ref, cwv_ref, cbg_ref, cbv_ref, act_ref):
    t = h2_ref.shape[1]
    rows = FFN_ROWS
    n_blk = t // rows
    w = jnp.concatenate([wg_ref[...], wv_ref[...]], axis=1)
    cw = jnp.concatenate([cwg_ref[...], cwv_ref[...]], axis=1).astype(BF16)
    cb = jnp.concatenate([cbg_ref[...], cbv_ref[...]], axis=1).astype(BF16)
    col = lax.broadcasted_iota(jnp.int32, (rows, 2 * FF_TILE), 0) & (GRID_W - 1)
    has_left = col > 0
    has_right = col < GRID_W - 1
    zeros = jnp.zeros((GRID_W, 2 * FF_TILE), BF16)

    def row_mixes(i):
        u = jnp.dot(h2_ref[0, i * rows:(i + 1) * rows, :], w, preferred_element_type=F32)
        u_l = jnp.where(has_left, pltpu.roll(u, 1, 0), 0.0).astype(BF16)
        u_r = jnp.where(has_right, pltpu.roll(u, rows - 1, 0), 0.0).astype(BF16)
        u_c = u.astype(BF16)
        return [cw[3 * kh:3 * kh + 1] * u_l + cw[3 * kh + 1:3 * kh + 2] * u_c
                + cw[3 * kh + 2:3 * kh + 3] * u_r for kh in range(3)]

    def finish(i, prev, cur, nxt):
        above = zeros if prev is None else prev[0][rows - GRID_W:]
        below = zeros if nxt is None else nxt[2][:GRID_W]
        c = (cur[1] + jnp.concatenate([above, cur[0][:rows - GRID_W]], axis=0)
             + jnp.concatenate([cur[2][GRID_W:], below], axis=0) + cb)
        cg = c[:, :FF_TILE]
        act_ref[0, i * rows:(i + 1) * rows, :] = cg * _sigmoid(cg) * c[:, FF_TILE:]

    mixes = [None] * (n_blk + 1)
    for i in range(n_blk):
        mixes[i] = row_mixes(i)
        if i >= 1:
            finish(i - 1, mixes[i - 2] if i >= 2 else None, mixes[i - 1], mixes[i])
    finish(n_blk - 1, mixes[n_blk - 2] if n_blk >= 2 else None, mixes[n_blk - 1], None)


def _ffn_up_call(h2, w_up, conv_w9, conv_b):
    b, s, d = h2.shape
    n_f = D_FF // FF_TILE
    in_specs = [
        pl.BlockSpec((1, s, d), lambda bi, f: (bi, 0, 0)),
        pl.BlockSpec((d, FF_TILE), lambda bi, f: (0, f)),
        pl.BlockSpec((d, FF_TILE), lambda bi, f: (0, n_f + f)),
        pl.BlockSpec((9, FF_TILE), lambda bi, f: (0, f)),
        pl.BlockSpec((9, FF_TILE), lambda bi, f: (0, n_f + f)),
        pl.BlockSpec((1, FF_TILE), lambda bi, f: (0, f)),
        pl.BlockSpec((1, FF_TILE), lambda bi, f: (0, n_f + f)),
    ]
    return pl.pallas_call(
        _ffn_up_kernel,
        grid=(b, n_f),
        in_specs=in_specs,
        out_specs=pl.BlockSpec((1, s, FF_TILE), lambda bi, f: (bi, 0, f)),
        out_shape=jax.ShapeDtypeStruct((b, s, D_FF), BF16),
        compiler_params=pltpu.CompilerParams(dimension_semantics=("arbitrary", "arbitrary"),
                                             vmem_limit_bytes=VMEM_LIMIT),
        name="ffn_up",
    )(h2, w_up, w_up, conv_w9, conv_w9, conv_b, conv_b)


def _ffn_down_kernel(act_ref, wd_ref, x1_ref, mod_ref, fg_ref, out_ref):
    y = jnp.dot(act_ref[0], wd_ref[...], preferred_element_type=F32)
    out_ref[0] = _rms(x1_ref[0] + mod_ref[0, 5:6, :] * y, fg_ref[...])


def _ffn_down_call(act, w_down, x1, modx, final_g):
    b, s, d = x1.shape
    tm = DOWN_TILE
    in_specs = [
        pl.BlockSpec((1, tm, D_FF), lambda bi, m: (bi, m, 0)),
        pl.BlockSpec((D_FF, d), lambda bi, m: (0, 0)),
        pl.BlockSpec((1, tm, d), lambda bi, m: (bi, m, 0)),
        pl.BlockSpec((1, 6, d), lambda bi, m: (bi, 0, 0)),
        pl.BlockSpec((1, d), lambda bi, m: (0, 0)),
    ]
    return pl.pallas_call(
        _ffn_down_kernel,
        grid=(b, s // tm),
        in_specs=in_specs,
        out_specs=pl.BlockSpec((1, tm, d), lambda bi, m: (bi, m, 0)),
        out_shape=jax.ShapeDtypeStruct((b, s, d), F32),
        compiler_params=pltpu.CompilerParams(dimension_semantics=("arbitrary", "arbitrary"),
                                             vmem_limit_bytes=VMEM_LIMIT),
        name="ffn_down",
    )(act, w_down, x1, modx, final_g)


@functools.lru_cache(maxsize=None)
def _constants(seq):
    gw = FOURIER_W // FOURIER_GROUPS
    nm = np.outer(np.arange(gw), np.arange(gw)) % gw
    ang = 2.0 * np.pi * nm / gw
    cs = np.zeros((FOURIER_W, 2 * FOURIER_W), np.float64)
    for gi in range(FOURIER_GROUPS):
        sl = slice(gi * gw, (gi + 1) * gw)
        cs[sl, sl] = np.cos(ang)
        cs[sl, FOURIER_W + gi * gw:FOURIER_W + (gi + 1) * gw] = np.sin(ang)
    half = seq // 2
    k_idx = np.arange(seq, dtype=np.int64)[:, None]
    ang_c = 2.0 * np.pi * ((k_idx * np.arange(0, half + 1, dtype=np.int64)[None, :]) % seq) / seq
    ang_s = 2.0 * np.pi * ((k_idx * np.arange(1, half, dtype=np.int64)[None, :]) % seq) / seq
    dft = np.concatenate([np.cos(ang_c), -np.sin(ang_s)], axis=1)
    ri = np.arange(REV_TILE)
    rev = (np.arange(2 * REV_TILE)[None, :] == (REV_TILE - ri)[:, None]).astype(np.float64)
    head_id = np.arange(RWKV_W) // HEAD
    ones_blk = (head_id[:, None] == head_id[None, :]).astype(np.float64)
    ti = np.arange(TILE)
    same_chunk = (ti[:, None] // CHUNK) == (ti[None, :] // CHUNK)
    tri = np.stack([same_chunk & (ti[None, :] <= ti[:, None]),
                    same_chunk & (ti[None, :] >= ti[:, None])]).astype(np.float64)
    to_bf16 = lambda arr: np.asarray(arr, dtype=np.float32).astype(BF16)
    return (np.asarray(cs, dtype=np.float32), np.asarray(dft, dtype=np.float32),
            to_bf16(ones_blk), to_bf16(tri), to_bf16(rev))


def _block_diag_dirs(w):
    z = jnp.zeros_like(w[0])
    return jnp.concatenate([jnp.concatenate([w[0], z], axis=1),
                            jnp.concatenate([z, w[1]], axis=1)], axis=0)


def kernel(x, c, ctx, c_ctx, ada_w, ada_b, norm1_g, norm2_g, w_in, rwkv_conv_w, decay_w0, decay_w2, iclr_a0, iclr_a2, gate_g2, k_k, k_a, r_k, gn_g, gn_b, fourier_g, w_out, ffn_w_up, ffn_conv_w, ffn_conv_b, ffn_w_down, final_g):
    b, s, d = x.shape
    assert ada_w.shape[0] == 1, "single-layer configuration"
    assert (b, s, d) == (c.shape[0], s, D_MODEL) and ctx.shape == (b, CTX_LEN, d)
    assert s % TILE == 0 and CTX_LEN == TILE
    cs, dft, ones_blk, tri, rev = (jnp.asarray(t) for t in _constants(s))
    cs, dft = cs.astype(BF16), dft.astype(BF16)

    cc = jnp.concatenate([c, c_ctx[None, :], jnp.zeros((16 - b - 1, d), F32)], axis=0)
    mod = _mod_call(cc, ada_w[0], ada_b[0][None, :])
    modx = mod[:b].reshape(b, 6, d)
    modc = jnp.broadcast_to(mod[b].reshape(1, 6, d)[:, :2], (b, 2, d))
    modsel = jnp.stack([modc, modx[:, :2]], axis=1)

    row = lambda t: t.reshape(1, -1)
    at, bt, kt, rt, v, plast, bonus, gate, z = _prep_call(
        x, ctx, modsel, row(norm1_g[0]), w_in[0].astype(BF16), rwkv_conv_w[0],
        row(decay_w0[0]), _block_diag_dirs(decay_w2[0]).astype(BF16),
        row(iclr_a0[0]), _block_diag_dirs(iclr_a2[0]).astype(BF16),
        gate_g2[0].astype(BF16), row(k_k[0]), row(k_a[0]), row(r_k[0]), cs, ones_blk, tri)

    yf, yb = _scan_call(at, bt, kt, rt, v, plast, s)

    x1, h2 = _mix_call(dft, rev, z, yf, yb, bonus, gate, x, modx, row(gn_g[0]), row(gn_b[0]),
                       row(fourier_g[0]), w_out[0].astype(BF16), row(norm2_g[0]), ones_blk)

    act = _ffn_up_call(h2, ffn_w_up[0].astype(BF16), ffn_conv_w[0].reshape(9, 2 * D_FF),
                       row(ffn_conv_b[0]))
    return _ffn_down_call(act, ffn_w_down[0].astype(BF16), x1, modx, row(final_g))
```

```python
import functools
import math

import numpy as np
import jax
import jax.numpy as jnp
from jax import lax
from jax.experimental import pallas as pl
from jax.experimental.pallas import tpu as pltpu

F32 = jnp.float32
BF16 = jnp.bfloat16

D_MODEL = 1024
CTX_LEN = 256
GRID_W = 64
FOURIER_W = 512
FOURIER_GROUPS = 8
RWKV_W = 512
HEAD = 64
N_HEADS = 8
DECAY_RANK = 64
ICLR_RANK = 64
GATE_RANK = 128
D_FF = 2816
NORM_EPS = 1e-6
GN_EPS = 64e-5
KK_EPS = 1e-12

CHUNK = 64
TILE = 256
MIX_TILE = 512
REV_TILE = 256
CHUNKS_PER_TILE = TILE // CHUNK
PREP_BATCH = 2
PAIR = 2 * HEAD
N_PAIRS = RWKV_W // PAIR
QUAD_HEADS = 4
SCAN_BATCH = 8
MOD_TILE = 1536
FF_TILE = 1408
FFN_ROWS = 256
DOWN_TILE = 1024
VMEM_LIMIT = 56 * 1024 * 1024
HEAD_SUM_TERMS = 1
CUMSUM_TERMS = 2
MOD_WEIGHT_TERMS = 2

RKV_LO = FOURIER_W
RKV_HI = FOURIER_W + 3 * RWKV_W
WD_LO = RKV_HI
AD_LO = WD_LO + 2 * DECAY_RANK
GD_LO = AD_LO + 2 * ICLR_RANK
PROJ_W = GD_LO + GATE_RANK


def _bdot(a, b):
    return jnp.dot(a.astype(BF16), b.astype(BF16), preferred_element_type=F32)


def _bdot_nt(a, b):
    return lax.dot_general(a.astype(BF16), b.astype(BF16), (((1,), (1,)), ((), ())),
                           preferred_element_type=F32)


def _bdot_tn(a, b):
    return lax.dot_general(a.astype(BF16), b.astype(BF16), (((0,), (0,)), ((), ())),
                           preferred_element_type=F32)


def _split_terms(a, terms):
    out, rem = [], a
    for _ in range(terms):
        hi = rem.astype(BF16)
        out.append(hi)
        rem = rem - hi.astype(F32)
    return out


def _dot_exact_rhs(a, b_exact, terms=HEAD_SUM_TERMS):
    acc = None
    for piece in _split_terms(a, terms):
        t = jnp.dot(piece, b_exact, preferred_element_type=F32)
        acc = t if acc is None else acc + t
    return acc


def _dot_exact_lhs(a_exact, b, terms=CUMSUM_TERMS):
    acc = None
    for piece in _split_terms(b, terms):
        t = jnp.dot(a_exact, piece, preferred_element_type=F32)
        acc = t if acc is None else acc + t
    return acc


def _sigmoid(x):
    return 1.0 / (1.0 + jnp.exp(-x))


def _rms(x, g):
    return x * lax.rsqrt(jnp.mean(x * x, axis=-1, keepdims=True) + NORM_EPS) * g


def _mod_kernel(c_ref, w_ref, b_ref, o_ref):
    cc = c_ref[...]
    s = cc * _sigmoid(cc)
    rows = s.shape[0]
    s_parts = jnp.concatenate(_split_terms(s, 3), axis=0)
    acc = b_ref[...]
    for wp in _split_terms(w_ref[...], MOD_WEIGHT_TERMS):
        t = jnp.dot(s_parts, wp, preferred_element_type=F32)
        acc = acc + (t[:rows] + t[rows:2 * rows] + t[2 * rows:])
    o_ref[...] = acc


def _mod_call(cc, ada_w, ada_b):
    rows, d = cc.shape
    n = ada_w.shape[1]
    tn = MOD_TILE
    return pl.pallas_call(
        _mod_kernel,
        grid=(n // tn,),
        in_specs=[pl.BlockSpec((rows, d), lambda i: (0, 0)),
                  pl.BlockSpec((d, tn), lambda i: (0, i)),
                  pl.BlockSpec((1, tn), lambda i: (0, i))],
        out_specs=pl.BlockSpec((rows, tn), lambda i: (0, i)),
        out_shape=jax.ShapeDtypeStruct((rows, n), F32),
        compiler_params=pltpu.CompilerParams(dimension_semantics=("arbitrary",),
                                             vmem_limit_bytes=VMEM_LIMIT),
        name="mod",
    )(cc, ada_w, ada_b)


def _prep_kernel(x_ref, xp_ref, xn_ref, ctx_ref, ms_ref, g1_ref, win_ref, cw_ref,
                 w0_ref, w2_ref, a0_ref, a2_ref, g2_ref, kk_ref, ka_ref, rk_ref,
                 cs_ref, ones_ref, tri_ref,
                 at_ref, bt_ref, kt_ref, rt_ref, v_ref, pl_ref, bonus_ref, gate_ref, z_ref):
    j = pl.program_id(1)
    refs = (x_ref, xp_ref, xn_ref, ctx_ref, ms_ref, g1_ref, win_ref, cw_ref,
            w0_ref, w2_ref, a0_ref, a2_ref, kk_ref, ka_ref, ones_ref, tri_ref,
            at_ref, bt_ref, kt_ref, rt_ref, v_ref, pl_ref)
    tiles = _run_interleaved([_prep_tile(bi, *refs) for bi in range(x_ref.shape[0])])

    @pl.when(j >= 1)
    def _():
        ones_blk = ones_ref[...]
        for bi, (u_four, r, v, gd, kd_sum) in enumerate(tiles):
            bs = _dot_exact_rhs(r * rk_ref[...] * kd_sum, ones_blk)
            bonus_ref[bi] = bs * v
            gate_ref[bi] = _bdot(_sigmoid(gd), g2_ref[...])
            z_ref[bi] = _bdot(u_four, cs_ref[...]).astype(BF16)


def _prep_tile(bi, x_ref, xp_ref, xn_ref, ctx_ref, ms_ref, g1_ref, win_ref, cw_ref,
               w0_ref, w2_ref, a0_ref, a2_ref, kk_ref, ka_ref, ones_ref, tri_ref,
               at_ref, bt_ref, kt_ref, rt_ref, v_ref, pl_ref):
    j = pl.program_id(1)
    n_j = pl.num_programs(1)
    is_ctx = j == 0
    shift = ms_ref[bi, 0, 0:1, :]
    scale = ms_ref[bi, 0, 1:2, :]
    g = g1_ref[...]

    def norm_mod(xr):
        return _rms(xr, g) * (1.0 + scale) + shift

    xin = jnp.where(is_ctx, ctx_ref[bi], x_ref[bi])
    h = norm_mod(xin).astype(BF16)
    u = jnp.dot(h, win_ref[...], preferred_element_type=F32)

    halo = jnp.concatenate([xp_ref[bi], xn_ref[bi]], axis=0)
    uh = jnp.dot(norm_mod(halo).astype(BF16), win_ref[:, RKV_LO:RKV_HI],
                 preferred_element_type=F32)
    yield
    prev_row = jnp.where(j >= 2, uh[7:8], 0.0)
    next_row = jnp.where(jnp.logical_and(j >= 1, j < n_j - 1), uh[8:9], 0.0)

    rkv = u[:, RKV_LO:RKV_HI]
    row = lax.broadcasted_iota(jnp.int32, rkv.shape, 0)
    u_m1 = jnp.where(row == 0, prev_row, pltpu.roll(rkv, 1, 0))
    u_p1 = jnp.where(row == TILE - 1, next_row, pltpu.roll(rkv, TILE - 1, 0))
    rkv = cw_ref[0:1, :] * u_m1 + cw_ref[1:2, :] * rkv + cw_ref[2:3, :] * u_p1
    r = rkv[:, 0:RWKV_W]
    k = rkv[:, RWKV_W:2 * RWKV_W]
    v = rkv[:, 2 * RWKV_W:3 * RWKV_W]

    wd = u[:, WD_LO:AD_LO]
    ad = u[:, AD_LO:GD_LO]
    gd = u[:, GD_LO:PROJ_W]
    w_lora = w0_ref[...] + _bdot(jnp.tanh(wd), w2_ref[...])
    lw = -math.exp(-0.5) * _sigmoid(w_lora)
    a = _sigmoid(a0_ref[...] + _bdot(ad, a2_ref[...]))

    ones_blk = ones_ref[...]
    kraw = k * kk_ref[...]
    kk = kraw * lax.rsqrt(_dot_exact_rhs(kraw * kraw, ones_blk) + KK_EPS)
    ka = ka_ref[...]
    kd = [k * (1.0 + (a[:, d * RWKV_W:(d + 1) * RWKV_W] - 1.0) * ka) for d in range(2)]

    v_ref[bi] = v.astype(BF16)
    for d in range(2):
        lwd = lw[:, d * RWKV_W:(d + 1) * RWKV_W]
        ad_ = a[:, d * RWKV_W:(d + 1) * RWKV_W]
        c = _dot_exact_lhs(tri_ref[d], lwd)
        e_pos = jnp.exp(c)
        e_neg = jnp.exp(-c)
        e_prev = jnp.exp(c - lwd)
        at_ref[bi, d] = (kk * e_prev).astype(BF16)
        bt_ref[bi, d] = (kk * ad_ * e_neg).astype(BF16)
        kt_ref[bi, d] = (kd[d] * e_neg).astype(BF16)
        rt_ref[bi, d] = (r * e_pos).astype(BF16)
        for q in range(CHUNKS_PER_TILE):
            last = q * CHUNK + (CHUNK - 1 if d == 0 else 0)
            pl_ref[bi, d, q] = e_pos[last:last + 1, :]
    return u[:, 0:FOURIER_W], r, v, gd, kd[0] + kd[1]


def _prep_call(x, ctx, modsel, norm1_g, w_in, conv_w, w0, w2bd, a0, a2bd, g2, k_k, k_a, r_k,
               cs, ones_blk, tri):
    b, s, d = x.shape
    n_j = 1 + s // TILE
    n_chunks = (CTX_LEN + s) // CHUNK
    t_all = CTX_LEN + s
    c2 = 2 * RWKV_W

    def full(arr):
        nd = arr.ndim
        return pl.BlockSpec(arr.shape, lambda bi, j, _n=nd: (0,) * _n)

    rows8 = TILE // 8
    nb = PREP_BATCH
    in_specs = [
        pl.BlockSpec((nb, TILE, d), lambda bi, j: (bi, jnp.maximum(j - 1, 0), 0)),
        pl.BlockSpec((nb, 8, d), lambda bi, j: (bi, jnp.maximum((j - 1) * rows8 - 1, 0), 0)),
        pl.BlockSpec((nb, 8, d), lambda bi, j: (bi, jnp.minimum(j * rows8, s // 8 - 1), 0)),
        pl.BlockSpec((nb, CTX_LEN, d), lambda bi, j: (bi, 0, 0)),
        pl.BlockSpec((nb, 1, 2, d), lambda bi, j: (bi, jnp.minimum(j, 1), 0, 0)),
        full(norm1_g), full(w_in), full(conv_w), full(w0), full(w2bd), full(a0), full(a2bd),
        full(g2), full(k_k), full(k_a), full(r_k), full(cs), full(ones_blk), full(tri),
    ]
    lat = lambda bi, j: (bi, jnp.maximum(j - 1, 0), 0)
    out_specs = [
        pl.BlockSpec((nb, 2, TILE, RWKV_W), lambda bi, j: (bi, 0, j, 0)),
        pl.BlockSpec((nb, 2, TILE, RWKV_W), lambda bi, j: (bi, 0, j, 0)),
        pl.BlockSpec((nb, 2, TILE, RWKV_W), lambda bi, j: (bi, 0, j, 0)),
        pl.BlockSpec((nb, 2, TILE, RWKV_W), lambda bi, j: (bi, 0, j, 0)),
        pl.BlockSpec((nb, TILE, RWKV_W), lambda bi, j: (bi, j, 0)),
        pl.BlockSpec((nb, 2, CHUNKS_PER_TILE, 1, RWKV_W), lambda bi, j: (bi, 0, j, 0, 0)),
        pl.BlockSpec((nb, TILE, RWKV_W), lat),
        pl.BlockSpec((nb, TILE, RWKV_W), lat),
        pl.BlockSpec((nb, TILE, 2 * FOURIER_W), lat),
    ]
    out_shape = [
        jax.ShapeDtypeStruct((b, 2, t_all, RWKV_W), BF16),
        jax.ShapeDtypeStruct((b, 2, t_all, RWKV_W), BF16),
        jax.ShapeDtypeStruct((b, 2, t_all, RWKV_W), BF16),
        jax.ShapeDtypeStruct((b, 2, t_all, RWKV_W), BF16),
        jax.ShapeDtypeStruct((b, t_all, RWKV_W), BF16),
        jax.ShapeDtypeStruct((b, 2, n_chunks, 1, RWKV_W), F32),
        jax.ShapeDtypeStruct((b, s, RWKV_W), F32),
        jax.ShapeDtypeStruct((b, s, RWKV_W), F32),
        jax.ShapeDtypeStruct((b, s, 2 * FOURIER_W), BF16),
    ]
    return pl.pallas_call(
        _prep_kernel,
        grid=(b // nb, n_j),
        in_specs=in_specs,
        out_specs=out_specs,
        out_shape=out_shape,
        compiler_params=pltpu.CompilerParams(dimension_semantics=("arbitrary", "arbitrary"),
                                             vmem_limit_bytes=VMEM_LIMIT),
        name="prep",
    )(x, x, x, ctx, modsel, norm1_g, w_in, conv_w, w0, w2bd, a0, a2bd, g2, k_k, k_a, r_k,
      cs, ones_blk, tri)


def _quad_chunk(a, bm, km, r, v, plrow, g_states, strict, incl, eye, lane_lo, mask_bd, eye_row,
                level_masks):
    n = PAIR

    def dup(xv):
        z = jnp.zeros_like(xv)
        return jnp.concatenate([jnp.where(lane_lo, xv, z), jnp.where(lane_lo, z, xv)], axis=0)

    def to_row(x2):
        return x2[:CHUNK] + x2[CHUNK:]

    def bdiag(xr):
        return jnp.where(mask_bd, jnp.concatenate([xr] * QUAD_HEADS, axis=0), jnp.zeros((), xr.dtype))

    pairs = []
    for p in range(2):
        sl = slice(p * n, (p + 1) * n)
        a2, b2, k2, r2, v2 = (dup(t[:, sl]) for t in (a, bm, km, r, v))
        rhs = jnp.concatenate([b2, k2], axis=0)
        s = _bdot_nt(jnp.concatenate([a2, r2], axis=0), rhs)
        pairs.append((sl, r2, v2, rhs, s))
    yield

    a_ab = jnp.concatenate([to_row(jnp.where(strict, s[:n, :n], 0.0)) for *_, s in pairs], axis=1)
    a_ak = jnp.concatenate([to_row(jnp.where(strict, s[:n, n:], 0.0)) for *_, s in pairs], axis=1)
    m_rbk = [jnp.concatenate([jnp.where(incl, s[n:, :n], 0.0), jnp.where(incl, s[n:, n:], 0.0)],
                             axis=1).astype(BF16) for *_, s in pairs]

    av = _bdot(a_ak, bdiag(v))
    t_row = jnp.where(eye_row, 1.0, 0.0) - jnp.where(level_masks[0], a_ab, 0.0)
    for lvl in range(1, len(level_masks)):
        a_off = jnp.where(level_masks[lvl], a_ab, 0.0)
        x_row = _bdot(a_off, bdiag(t_row.astype(BF16)))
        yield
        t_row = t_row - _bdot(t_row, bdiag(x_row.astype(BF16)))
        yield
    tx = _bdot(t_row, jnp.concatenate([bdiag(a), bdiag(av.astype(BF16))], axis=1))
    yield
    a_w = tx[:, :2 * n].astype(BF16)
    u_v = (-tx[:, 2 * n:]).astype(BF16)
    stage5 = []
    for (sl, r2, v2, rhs, _), m2 in zip(pairs, m_rbk):
        zmat = jnp.concatenate(
            [jnp.concatenate([dup(a_w[:, sl]), dup(u_v[:, sl])], axis=1),
             jnp.concatenate([jnp.zeros_like(v2), v2], axis=1)], axis=0)
        rhs_end = rhs.astype(F32) * plrow[:, sl]
        lhs = jnp.concatenate([rhs_end.T.astype(BF16), m2], axis=0)
        stage5.append(jnp.dot(lhs, zmat, preferred_element_type=F32))
    yield
    ys, g_new = [], []
    for (sl, r2, v2, rhs, _), o5, g_state in zip(pairs, stage5, g_states):
        phi = jnp.where(eye, plrow[:, sl], 0.0) - o5[:n, :n]
        r_y = r2.astype(F32) - o5[n:, :n]
        o6 = _bdot(jnp.concatenate([r_y, phi], axis=0), g_state)
        ys.append(to_row(o6[:n] + o5[n:, n:]))
        g_new.append(o6[n:] + o5[:n, n:])
    return jnp.concatenate(ys, axis=1), g_new


def _run_interleaved(gens):
    results = [None] * len(gens)
    active = list(range(len(gens)))
    while active:
        for i in list(active):
            try:
                next(gens[i])
            except StopIteration as stop:
                results[i] = stop.value
                active.remove(i)
    return results


def _scan_kernel(atf, btf, ktf, rtf, vf, plf, atb, btb, ktb, rtb, vb, plb,
                 yf_ref, yb_ref, g_ref):
    s = pl.program_id(1)

    @pl.when(s == 0)
    def _():
        g_ref[...] = jnp.zeros_like(g_ref)

    ri = lax.broadcasted_iota(jnp.int32, (PAIR, PAIR), 0)
    ci = lax.broadcasted_iota(jnp.int32, (PAIR, PAIR), 1)
    same = (ri >= HEAD) == (ci >= HEAD)
    ti = ri & (HEAD - 1)
    tj = ci & (HEAD - 1)
    eye = ri == ci
    lane_lo = lax.broadcasted_iota(jnp.int32, (CHUNK, PAIR), 1) < HEAD
    masks = [(same & (tj < ti), same & (tj <= ti)),
             (same & (tj > ti), same & (tj >= ti))]
    quad_w = QUAD_HEADS * HEAD
    rq = lax.broadcasted_iota(jnp.int32, (quad_w, quad_w), 0)
    cq = lax.broadcasted_iota(jnp.int32, (quad_w, quad_w), 1)
    mask_bd = (rq // HEAD) == (cq // HEAD)
    t_i = lax.broadcasted_iota(jnp.int32, (CHUNK, quad_w), 0)
    s_i = lax.broadcasted_iota(jnp.int32, (CHUNK, quad_w), 1) & (HEAD - 1)
    eye_row = s_i == t_i
    level_masks = []
    for d in range(2):
        late, early = (t_i, s_i) if d == 0 else (s_i, t_i)
        level_masks.append([
            ((t_i >> (lvl + 1)) == (s_i >> (lvl + 1)))
            & (((late >> lvl) & 1) == 1) & (((early >> lvl) & 1) == 0)
            for lvl in range(int(math.log2(CHUNK)))])
    refs = [(atf, btf, ktf, rtf, vf, plf, yf_ref), (atb, btb, ktb, rtb, vb, plb, yb_ref)]
    gens, dests = [], []
    for bi in range(yf_ref.shape[0]):
        for d in range(2):
            at, bt, kt, rt, vv, plr, y_ref = refs[d]
            strict, incl = masks[d]
            for q in range(RWKV_W // quad_w):
                sl = slice(q * quad_w, (q + 1) * quad_w)
                gens.append(_quad_chunk(at[bi, 0, :, sl], bt[bi, 0, :, sl], kt[bi, 0, :, sl],
                                        rt[bi, 0, :, sl], vv[bi, :, sl], plr[bi, 0, 0, :, sl],
                                        [g_ref[bi, d, 2 * q], g_ref[bi, d, 2 * q + 1]],
                                        strict, incl, eye, lane_lo, mask_bd, eye_row,
                                        level_masks[d]))
                dests.append((y_ref, bi, d, q, sl))
    for (y, g_new), (y_ref, bi, d, q, sl) in zip(_run_interleaved(gens), dests):
        g_ref[bi, d, 2 * q] = g_new[0]
        g_ref[bi, d, 2 * q + 1] = g_new[1]
        y_ref[bi, :, sl] = y


def _scan_call(at, bt, kt, rt, v, plast, s_lat):
    b = at.shape[0]
    n_chunks = at.shape[2] // CHUNK
    n_ctx = CTX_LEN // CHUNK
    n_lat = s_lat // CHUNK

    def fwd_c(s):
        return s

    def bwd_c(s):
        return jnp.where(s < n_ctx, n_ctx - 1 - s, n_chunks + n_ctx - 1 - s)

    nb = SCAN_BATCH

    def dir_spec(d, cfun):
        return pl.BlockSpec((nb, 1, CHUNK, RWKV_W), lambda bi, s: (bi, d, cfun(s), 0))

    def v_spec(cfun):
        return pl.BlockSpec((nb, CHUNK, RWKV_W), lambda bi, s: (bi, cfun(s), 0))

    def pl_spec(d, cfun):
        return pl.BlockSpec((nb, 1, 1, 1, RWKV_W), lambda bi, s: (bi, d, cfun(s), 0, 0))

    in_specs = ([dir_spec(0, fwd_c)] * 4 + [v_spec(fwd_c), pl_spec(0, fwd_c)]
                + [dir_spec(1, bwd_c)] * 4 + [v_spec(bwd_c), pl_spec(1, bwd_c)])
    out_specs = [
        pl.BlockSpec((nb, CHUNK, RWKV_W), lambda bi, s: (bi, jnp.maximum(s - n_ctx, 0), 0)),
        pl.BlockSpec((nb, CHUNK, RWKV_W),
                     lambda bi, s: (bi, jnp.minimum(n_chunks - 1 - s, n_lat - 1), 0)),
    ]
    out_shape = [jax.ShapeDtypeStruct((b, s_lat, RWKV_W), F32)] * 2
    return pl.pallas_call(
        _scan_kernel,
        grid=(b // nb, n_chunks),
        in_specs=in_specs,
        out_specs=out_specs,
        out_shape=out_shape,
        scratch_shapes=[pltpu.VMEM((nb, 2, N_PAIRS, PAIR, PAIR), F32)],
        compiler_params=pltpu.CompilerParams(dimension_semantics=("arbitrary", "arbitrary"),
                                             vmem_limit_bytes=VMEM_LIMIT),
        name="scan",
    )(at, bt, kt, rt, v, plast, at, bt, kt, rt, v, plast)


def _mix_kernel(dft_ref, rev_ref, z_ref, yf_ref, yb_ref, bonus_ref, gate_ref, x_ref, mod_ref,
                gng_ref, gnb_ref, fg_ref, wout_ref, n2g_ref, ones_ref, x1_ref, h2_ref, zf_ref,
                *, fscale):
    seq = z_ref.shape[1]
    half = seq // 2
    rt = rev_ref.shape[0]

    @pl.when(pl.program_id(1) == 0)
    def _():
        mid = z_ref[0, half:half + 1, :FOURIER_W].astype(F32)
        row0 = lax.broadcasted_iota(jnp.int32, (rt, FOURIER_W), 0) == 0
        for part, sign in ((0, 1.0), (1, -1.0)):
            lanes = slice(part * FOURIER_W, (part + 1) * FOURIER_W)
            for r in range(half // rt):
                lo = seq - rt * (r + 1)
                nxt = (z_ref[0, lo + rt:lo + 2 * rt, lanes] if r > 0
                       else jnp.zeros((rt, FOURIER_W), BF16))
                rev = jnp.dot(rev_ref[...], jnp.concatenate([z_ref[0, lo:lo + rt, lanes], nxt], axis=0),
                              preferred_element_type=F32)
                folded = z_ref[0, r * rt:(r + 1) * rt, lanes].astype(F32) + sign * rev
                if part == 1 and r == 0:
                    folded = jnp.where(row0, mid, folded)
                zf_ref[part * half + r * rt:part * half + (r + 1) * rt, :] = folded.astype(BF16)

    yfour = jnp.dot(dft_ref[...], zf_ref[...], preferred_element_type=F32)
    four = _rms(yfour * fscale, fg_ref[...])

    ones_blk = ones_ref[...]
    y = yf_ref[0] + yb_ref[0]
    mu = _dot_exact_rhs(y, ones_blk) * (1.0 / HEAD)
    dy = y - mu
    var = _dot_exact_rhs(dy * dy, ones_blk) * (1.0 / HEAD)
    yn = dy * lax.rsqrt(var + GN_EPS) * gng_ref[...] + gnb_ref[...]
    rw = (yn + bonus_ref[0]) * gate_ref[0]

    out = _bdot(four, wout_ref[:FOURIER_W, :]) + _bdot(rw, wout_ref[FOURIER_W:, :])
    x1 = x_ref[0] + mod_ref[0, 2:3, :] * out
    x1_ref[0] = x1
    h2 = _rms(x1, n2g_ref[...]) * (1.0 + mod_ref[0, 4:5, :]) + mod_ref[0, 3:4, :]
    h2_ref[0] = h2.astype(BF16)


def _mix_call(dft, rev, z, yf, yb, bonus, gate, x, modx, gn_g, gn_b, f_g, w_out, norm2_g,
              ones_blk):
    b, s, d = x.shape
    tm = MIX_TILE

    def full(arr):
        nd = arr.ndim
        return pl.BlockSpec(arr.shape, lambda bi, m, _n=nd: (0,) * _n)

    tok = lambda w: pl.BlockSpec((1, tm, w), lambda bi, m: (bi, m, 0))
    in_specs = [
        pl.BlockSpec((tm, s), lambda bi, m: (m, 0)),
        full(rev),
        pl.BlockSpec((1, s, 2 * FOURIER_W), lambda bi, m: (bi, 0, 0)),
        tok(RWKV_W), tok(RWKV_W), tok(RWKV_W), tok(RWKV_W), tok(d),
        pl.BlockSpec((1, 6, d), lambda bi, m: (bi, 0, 0)),
        full(gn_g), full(gn_b), full(f_g), full(w_out), full(norm2_g), full(ones_blk),
    ]
    fscale = 1.0 / math.sqrt(s * (FOURIER_W // FOURIER_GROUPS))
    return pl.pallas_call(
        functools.partial(_mix_kernel, fscale=fscale),
        grid=(b, s // tm),
        in_specs=in_specs,
        out_specs=[tok(d), tok(d)],
        out_shape=[jax.ShapeDtypeStruct((b, s, d), F32), jax.ShapeDtypeStruct((b, s, d), BF16)],
        scratch_shapes=[pltpu.VMEM((s, FOURIER_W), BF16)],
        compiler_params=pltpu.CompilerParams(dimension_semantics=("arbitrary", "arbitrary"),
                                             vmem_limit_bytes=VMEM_LIMIT),
        name="mix",
    )(dft, rev, z, yf, yb, bonus, gate, x, modx, gn_g, gn_b, f_g, w_out, norm2_g, ones_blk)


def _ffn_up_kernel(h2_ref, wg_ref, wv_ref, cwg_ref, cwv_ref, cbg_ref, cbv_ref, act_ref):
    t = h2_ref.shape[1]
    rows = FFN_ROWS
    n_blk = t // rows
    w = jnp.concatenate([wg_ref[...], wv_ref[...]], axis=1)
    cw = jnp.concatenate([cwg_ref[...], cwv_ref[...]], axis=1).astype(BF16)
    cb = jnp.concatenate([cbg_ref[...], cbv_ref[...]], axis=1).astype(BF16)
    col = lax.broadcasted_iota(jnp.int32, (rows, 2 * FF_TILE), 0) & (GRID_W - 1)
    has_left = col > 0
    has_right = col < GRID_W - 1
    zeros = jnp.zeros((GRID_W, 2 * FF_TILE), BF16)

    def row_mixes(i):
        u = jnp.dot(h2_ref[0, i * rows:(i + 1) * rows, :], w, preferred_element_type=F32)
        u_l = jnp.where(has_left, pltpu.roll(u, 1, 0), 0.0).astype(BF16)
        u_r = jnp.where(has_right, pltpu.roll(u, rows - 1, 0), 0.0).astype(BF16)
        u_c = u.astype(BF16)
        return [cw[3 * kh:3 * kh + 1] * u_l + cw[3 * kh + 1:3 * kh + 2] * u_c
                + cw[3 * kh + 2:3 * kh + 3] * u_r for kh in range(3)]

    def finish(i, prev, cur, nxt):
        above = zeros if prev is None else prev[0][rows - GRID_W:]
        below = zeros if nxt is None else nxt[2][:GRID_W]
        c = (cur[1] + jnp.concatenate([above, cur[0][:rows - GRID_W]], axis=0)
             + jnp.concatenate([cur[2][GRID_W:], below], axis=0) + cb)
        cg = c[:, :FF_TILE]
        act_ref[0, i * rows:(i + 1) * rows, :] = cg * _sigmoid(cg) * c[:, FF_TILE:]

    mixes = [None] * (n_blk + 1)
    for i in range(n_blk):
        mixes[i] = row_mixes(i)
        if i >= 1:
            finish(i - 1, mixes[i - 2] if i >= 2 else None, mixes[i - 1], mixes[i])
    finish(n_blk - 1, mixes[n_blk - 2] if n_blk >= 2 else None, mixes[n_blk - 1], None)


def _ffn_up_call(h2, w_up, conv_w9, conv_b):
    b, s, d = h2.shape
    n_f = D_FF // FF_TILE
    in_specs = [
        pl.BlockSpec((1, s, d), lambda bi, f: (bi, 0, 0)),
        pl.BlockSpec((d, FF_TILE), lambda bi, f: (0, f)),
        pl.BlockSpec((d, FF_TILE), lambda bi, f: (0, n_f + f)),
        pl.BlockSpec((9, FF_TILE), lambda bi, f: (0, f)),
        pl.BlockSpec((9, FF_TILE), lambda bi, f: (0, n_f + f)),
        pl.BlockSpec((1, FF_TILE), lambda bi, f: (0, f)),
        pl.BlockSpec((1, FF_TILE), lambda bi, f: (0, n_f + f)),
    ]
    return pl.pallas_call(
        _ffn_up_kernel,
        grid=(b, n_f),
        in_specs=in_specs,
        out_specs=pl.BlockSpec((1, s, FF_TILE), lambda bi, f: (bi, 0, f)),
        out_shape=jax.ShapeDtypeStruct((b, s, D_FF), BF16),
        compiler_params=pltpu.CompilerParams(dimension_semantics=("arbitrary", "arbitrary"),
                                             vmem_limit_bytes=VMEM_LIMIT),
        name="ffn_up",
    )(h2, w_up, w_up, conv_w9, conv_w9, conv_b, conv_b)


def _ffn_down_kernel(act_ref, wd_ref, x1_ref, mod_ref, fg_ref, out_ref):
    y = jnp.dot(act_ref[0], wd_ref[...], preferred_element_type=F32)
    out_ref[0] = _rms(x1_ref[0] + mod_ref[0, 5:6, :] * y, fg_ref[...])


def _ffn_down_call(act, w_down, x1, modx, final_g):
    b, s, d = x1.shape
    tm = DOWN_TILE
    in_specs = [
        pl.BlockSpec((1, tm, D_FF), lambda bi, m: (bi, m, 0)),
        pl.BlockSpec((D_FF, d), lambda bi, m: (0, 0)),
        pl.BlockSpec((1, tm, d), lambda bi, m: (bi, m, 0)),
        pl.BlockSpec((1, 6, d), lambda bi, m: (bi, 0, 0)),
        pl.BlockSpec((1, d), lambda bi, m: (0, 0)),
    ]
    return pl.pallas_call(
        _ffn_down_kernel,
        grid=(b, s // tm),
        in_specs=in_specs,
        out_specs=pl.BlockSpec((1, tm, d), lambda bi, m: (bi, m, 0)),
        out_shape=jax.ShapeDtypeStruct((b, s, d), F32),
        compiler_params=pltpu.CompilerParams(dimension_semantics=("arbitrary", "arbitrary"),
                                             vmem_limit_bytes=VMEM_LIMIT),
        name="ffn_down",
    )(act, w_down, x1, modx, final_g)


@functools.lru_cache(maxsize=None)
def _constants(seq):
    gw = FOURIER_W // FOURIER_GROUPS
    nm = np.outer(np.arange(gw), np.arange(gw)) % gw
    ang = 2.0 * np.pi * nm / gw
    cs = np.zeros((FOURIER_W, 2 * FOURIER_W), np.float64)
    for gi in range(FOURIER_GROUPS):
        sl = slice(gi * gw, (gi + 1) * gw)
        cs[sl, sl] = np.cos(ang)
        cs[sl, FOURIER_W + gi * gw:FOURIER_W + (gi + 1) * gw] = np.sin(ang)
    half = seq // 2
    k_idx = np.arange(seq, dtype=np.int64)[:, None]
    ang_c = 2.0 * np.pi * ((k_idx * np.arange(0, half + 1, dtype=np.int64)[None, :]) % seq) / seq
    ang_s = 2.0 * np.pi * ((k_idx * np.arange(1, half, dtype=np.int64)[None, :]) % seq) / seq
    dft = np.concatenate([np.cos(ang_c), -np.sin(ang_s)], axis=1)
    ri = np.arange(REV_TILE)
    rev = (np.arange(2 * REV_TILE)[None, :] == (REV_TILE - ri)[:, None]).astype(np.float64)
    head_id = np.arange(RWKV_W) // HEAD
    ones_blk = (head_id[:, None] == head_id[None, :]).astype(np.float64)
    ti = np.arange(TILE)
    same_chunk = (ti[:, None] // CHUNK) == (ti[None, :] // CHUNK)
    tri = np.stack([same_chunk & (ti[None, :] <= ti[:, None]),
                    same_chunk & (ti[None, :] >= ti[:, None])]).astype(np.float64)
    to_bf16 = lambda arr: np.asarray(arr, dtype=np.float32).astype(BF16)
    return (np.asarray(cs, dtype=np.float32), np.asarray(dft, dtype=np.float32),
            to_bf16(ones_blk), to_bf16(tri), to_bf16(rev))


def _block_diag_dirs(w):
    z = jnp.zeros_like(w[0])
    return jnp.concatenate([jnp.concatenate([w[0], z], axis=1),
                            jnp.concatenate([z, w[1]], axis=1)], axis=0)


def kernel(x, c, ctx, c_ctx, ada_w, ada_b, norm1_g, norm2_g, w_in, rwkv_conv_w, decay_w0, decay_w2, iclr_a0, iclr_a2, gate_g2, k_k, k_a, r_k, gn_g, gn_b, fourier_g, w_out, ffn_w_up, ffn_conv_w, ffn_conv_b, ffn_w_down, final_g):
    b, s, d = x.shape
    assert ada_w.shape[0] == 1, "single-layer configuration"
    assert (b, s, d) == (c.shape[0], s, D_MODEL) and ctx.shape == (b, CTX_LEN, d)
    assert s % TILE == 0 and CTX_LEN == TILE
    cs, dft, ones_blk, tri, rev = (jnp.asarray(t) for t in _constants(s))
    cs, dft = cs.astype(BF16), dft.astype(BF16)

    cc = jnp.concatenate([c, c_ctx[None, :], jnp.zeros((16 - b - 1, d), F32)], axis=0)
    mod = _mod_call(cc, ada_w[0], ada_b[0][None, :])
    modx = mod[:b].reshape(b, 6, d)
    modc = jnp.broadcast_to(mod[b].reshape(1, 6, d)[:, :2], (b, 2, d))
    modsel = jnp.stack([modc, modx[:, :2]], axis=1)

    row = lambda t: t.reshape(1, -1)
    at, bt, kt, rt, v, plast, bonus, gate, z = _prep_call(
        x, ctx, modsel, row(norm1_g[0]), w_in[0].astype(BF16), rwkv_conv_w[0],
        row(decay_w0[0]), _block_diag_dirs(decay_w2[0]).astype(BF16),
        row(iclr_a0[0]), _block_diag_dirs(iclr_a2[0]).astype(BF16),
        gate_g2[0].astype(BF16), row(k_k[0]), row(k_a[0]), row(r_k[0]), cs, ones_blk, tri)

    yf, yb = _scan_call(at, bt, kt, rt, v, plast, s)

    x1, h2 = _mix_call(dft, rev, z, yf, yb, bonus, gate, x, modx, row(gn_g[0]), row(gn_b[0]),
                       row(fourier_g[0]), w_out[0].astype(BF16), row(norm2_g[0]), ones_blk)

    act = _ffn_up_call(h2, ffn_w_up[0].astype(BF16), ffn_conv_w[0].reshape(9, 2 * D_FF),
                       row(ffn_conv_b[0]))
    return _ffn_down_call(act, ffn_w_down[0].astype(BF16), x1, modx, row(final_g))
```

```python
import functools
import math

import numpy as np
import jax
import jax.numpy as jnp
from jax import lax
from jax.experimental import pallas as pl
from jax.experimental.pallas import tpu as pltpu

F32 = jnp.float32
BF16 = jnp.bfloat16

D_MODEL = 1024
CTX_LEN = 256
GRID_W = 64
FOURIER_W = 512
FOURIER_GROUPS = 8
RWKV_W = 512
HEAD = 64
N_HEADS = 8
DECAY_RANK = 64
ICLR_RANK = 64
GATE_RANK = 128
D_FF = 2816
NORM_EPS = 1e-6
GN_EPS = 64e-5
KK_EPS = 1e-12

CHUNK = 64
TILE = 256
MIX_TILE = 512
REV_TILE = 256
CHUNKS_PER_TILE = TILE // CHUNK
PREP_BATCH = 2
PAIR = 2 * HEAD
N_PAIRS = RWKV_W // PAIR
QUAD_HEADS = 4
SCAN_BATCH = 8
MOD_TILE = 1536
FF_TILE = 1408
FFN_ROWS = 256
DOWN_TILE = 1024
VMEM_LIMIT = 56 * 1024 * 1024
HEAD_SUM_TERMS = 1
CUMSUM_TERMS = 2
MOD_WEIGHT_TERMS = 2

RKV_LO = FOURIER_W
RKV_HI = FOURIER_W + 3 * RWKV_W
WD_LO = RKV_HI
AD_LO = WD_LO + 2 * DECAY_RANK
GD_LO = AD_LO + 2 * ICLR_RANK
PROJ_W = GD_LO + GATE_RANK


def _bdot(a, b):
    return jnp.dot(a.astype(BF16), b.astype(BF16), preferred_element_type=F32)


def _bdot_nt(a, b):
    return lax.dot_general(a.astype(BF16), b.astype(BF16), (((1,), (1,)), ((), ())),
                           preferred_element_type=F32)


def _bdot_tn(a, b):
    return lax.dot_general(a.astype(BF16), b.astype(BF16), (((0,), (0,)), ((), ())),
                           preferred_element_type=F32)


def _split_terms(a, terms):
    out, rem = [], a
    for _ in range(terms):
        hi = rem.astype(BF16)
        out.append(hi)
        rem = rem - hi.astype(F32)
    return out


def _dot_exact_rhs(a, b_exact, terms=HEAD_SUM_TERMS):
    acc = None
    for piece in _split_terms(a, terms):
        t = jnp.dot(piece, b_exact, preferred_element_type=F32)
        acc = t if acc is None else acc + t
    return acc


def _dot_exact_lhs(a_exact, b, terms=CUMSUM_TERMS):
    acc = None
    for piece in _split_terms(b, terms):
        t = jnp.dot(a_exact, piece, preferred_element_type=F32)
        acc = t if acc is None else acc + t
    return acc


def _sigmoid(x):
    return 1.0 / (1.0 + jnp.exp(-x))


def _rms(x, g):
    return x * lax.rsqrt(jnp.mean(x * x, axis=-1, keepdims=True) + NORM_EPS) * g


def _mod_kernel(c_ref, w_ref, b_ref, o_ref):
    cc = c_ref[...]
    s = cc * _sigmoid(cc)
    rows = s.shape[0]
    s_parts = jnp.concatenate(_split_terms(s, 3), axis=0)
    acc = b_ref[...]
    for wp in _split_terms(w_ref[...], MOD_WEIGHT_TERMS):
        t = jnp.dot(s_parts, wp, preferred_element_type=F32)
        acc = acc + (t[:rows] + t[rows:2 * rows] + t[2 * rows:])
    o_ref[...] = acc


def _mod_call(cc, ada_w, ada_b):
    rows, d = cc.shape
    n = ada_w.shape[1]
    tn = MOD_TILE
    return pl.pallas_call(
        _mod_kernel,
        grid=(n // tn,),
        in_specs=[pl.BlockSpec((rows, d), lambda i: (0, 0)),
                  pl.BlockSpec((d, tn), lambda i: (0, i)),
                  pl.BlockSpec((1, tn), lambda i: (0, i))],
        out_specs=pl.BlockSpec((rows, tn), lambda i: (0, i)),
        out_shape=jax.ShapeDtypeStruct((rows, n), F32),
        compiler_params=pltpu.CompilerParams(dimension_semantics=("arbitrary",),
                                             vmem_limit_bytes=VMEM_LIMIT),
        name="mod",
    )(cc, ada_w, ada_b)


def _prep_kernel(x_ref, xp_ref, xn_ref, ctx_ref, ms_ref, g1_ref, win_ref, cw_ref,
                 w0_ref, w2_ref, a0_ref, a2_ref, g2_ref, kk_ref, ka_ref, rk_ref,
                 cs_ref, ones_ref, tri_ref,
                 at_ref, bt_ref, kt_ref, rt_ref, v_ref, pl_ref, bonus_ref, gate_ref, z_ref):
    j = pl.program_id(1)
    refs = (x_ref, xp_ref, xn_ref, ctx_ref, ms_ref, g1_ref, win_ref, cw_ref,
            w0_ref, w2_ref, a0_ref, a2_ref, kk_ref, ka_ref, ones_ref, tri_ref,
            at_ref, bt_ref, kt_ref, rt_ref, v_ref, pl_ref)
    tiles = _run_interleaved([_prep_tile(bi, *refs) for bi in range(x_ref.shape[0])])

    @pl.when(j >= 1)
    def _():
        ones_blk = ones_ref[...]
        for bi, (u_four, r, v, gd, kd_sum) in enumerate(tiles):
            bs = _dot_exact_rhs(r * rk_ref[...] * kd_sum, ones_blk)
            bonus_ref[bi] = bs * v
            gate_ref[bi] = _bdot(_sigmoid(gd), g2_ref[...])
            z_ref[bi] = _bdot(u_four, cs_ref[...]).astype(BF16)


def _prep_tile(bi, x_ref, xp_ref, xn_ref, ctx_ref, ms_ref, g1_ref, win_ref, cw_ref,
               w0_ref, w2_ref, a0_ref, a2_ref, kk_ref, ka_ref, ones_ref, tri_ref,
               at_ref, bt_ref, kt_ref, rt_ref, v_ref, pl_ref):
    j = pl.program_id(1)
    n_j = pl.num_programs(1)
    is_ctx = j == 0
    shift = ms_ref[bi, 0, 0:1, :]
    scale = ms_ref[bi, 0, 1:2, :]
    g = g1_ref[...]

    def norm_mod(xr):
        return _rms(xr, g) * (1.0 + scale) + shift

    xin = jnp.where(is_ctx, ctx_ref[bi], x_ref[bi])
    h = norm_mod(xin).astype(BF16)
    u = jnp.dot(h, win_ref[...], preferred_element_type=F32)

    halo = jnp.concatenate([xp_ref[bi], xn_ref[bi]], axis=0)
    uh = jnp.dot(norm_mod(halo).astype(BF16), win_ref[:, RKV_LO:RKV_HI],
                 preferred_element_type=F32)
    yield
    prev_row = jnp.where(j >= 2, uh[7:8], 0.0)
    next_row = jnp.where(jnp.logical_and(j >= 1, j < n_j - 1), uh[8:9], 0.0)

    rkv = u[:, RKV_LO:RKV_HI]
    row = lax.broadcasted_iota(jnp.int32, rkv.shape, 0)
    u_m1 = jnp.where(row == 0, prev_row, pltpu.roll(rkv, 1, 0))
    u_p1 = jnp.where(row == TILE - 1, next_row, pltpu.roll(rkv, TILE - 1, 0))
    rkv = cw_ref[0:1, :] * u_m1 + cw_ref[1:2, :] * rkv + cw_ref[2:3, :] * u_p1
    r = rkv[:, 0:RWKV_W]
    k = rkv[:, RWKV_W:2 * RWKV_W]
    v = rkv[:, 2 * RWKV_W:3 * RWKV_W]

    wd = u[:, WD_LO:AD_LO]
    ad = u[:, AD_LO:GD_LO]
    gd = u[:, GD_LO:PROJ_W]
    w_lora = w0_ref[...] + _bdot(jnp.tanh(wd), w2_ref[...])
    lw = -math.exp(-0.5) * _sigmoid(w_lora)
    a = _sigmoid(a0_ref[...] + _bdot(ad, a2_ref[...]))

    ones_blk = ones_ref[...]
    kraw = k * kk_ref[...]
    kk = kraw * lax.rsqrt(_dot_exact_rhs(kraw * kraw, ones_blk) + KK_EPS)
    ka = ka_ref[...]
    kd = [k * (1.0 + (a[:, d * RWKV_W:(d + 1) * RWKV_W] - 1.0) * ka) for d in range(2)]

    v_ref[bi] = v.astype(BF16)
    for d in range(2):
        lwd = lw[:, d * RWKV_W:(d + 1) * RWKV_W]
        ad_ = a[:, d * RWKV_W:(d + 1) * RWKV_W]
        c = _dot_exact_lhs(tri_ref[d], lwd)
        e_pos = jnp.exp(c)
        e_neg = jnp.exp(-c)
        e_prev = jnp.exp(c - lwd)
        at_ref[bi, d] = (kk * e_prev).astype(BF16)
        bt_ref[bi, d] = (kk * ad_ * e_neg).astype(BF16)
        kt_ref[bi, d] = (kd[d] * e_neg).astype(BF16)
        rt_ref[bi, d] = (r * e_pos).astype(BF16)
        for q in range(CHUNKS_PER_TILE):
            last = q * CHUNK + (CHUNK - 1 if d == 0 else 0)
            pl_ref[bi, d, q] = e_pos[last:last + 1, :]
    return u[:, 0:FOURIER_W], r, v, gd, kd[0] + kd[1]


def _prep_call(x, ctx, modsel, norm1_g, w_in, conv_w, w0, w2bd, a0, a2bd, g2, k_k, k_a, r_k,
               cs, ones_blk, tri):
    b, s, d = x.shape
    n_j = 1 + s // TILE
    n_chunks = (CTX_LEN + s) // CHUNK
    t_all = CTX_LEN + s
    c2 = 2 * RWKV_W

    def full(arr):
        nd = arr.ndim
        return pl.BlockSpec(arr.shape, lambda bi, j, _n=nd: (0,) * _n)

    rows8 = TILE // 8
    nb = PREP_BATCH
    in_specs = [
        pl.BlockSpec((nb, TILE, d), lambda bi, j: (bi, jnp.maximum(j - 1, 0), 0)),
        pl.BlockSpec((nb, 8, d), lambda bi, j: (bi, jnp.maximum((j - 1) * rows8 - 1, 0), 0)),
        pl.BlockSpec((nb, 8, d), lambda bi, j: (bi, jnp.minimum(j * rows8, s // 8 - 1), 0)),
        pl.BlockSpec((nb, CTX_LEN, d), lambda bi, j: (bi, 0, 0)),
        pl.BlockSpec((nb, 1, 2, d), lambda bi, j: (bi, jnp.minimum(j, 1), 0, 0)),
        full(norm1_g), full(w_in), full(conv_w), full(w0), full(w2bd), full(a0), full(a2bd),
        full(g2), full(k_k), full(k_a), full(r_k), full(cs), full(ones_blk), full(tri),
    ]
    lat = lambda bi, j: (bi, jnp.maximum(j - 1, 0), 0)
    out_specs = [
        pl.BlockSpec((nb, 2, TILE, RWKV_W), lambda bi, j: (bi, 0, j, 0)),
        pl.BlockSpec((nb, 2, TILE, RWKV_W), lambda bi, j: (bi, 0, j, 0)),
        pl.BlockSpec((nb, 2, TILE, RWKV_W), lambda bi, j: (bi, 0, j, 0)),
        pl.BlockSpec((nb, 2, TILE, RWKV_W), lambda bi, j: (bi, 0, j, 0)),
        pl.BlockSpec((nb, TILE, RWKV_W), lambda bi, j: (bi, j, 0)),
        pl.BlockSpec((nb, 2, CHUNKS_PER_TILE, 1, RWKV_W), lambda bi, j: (bi, 0, j, 0, 0)),
        pl.BlockSpec((nb, TILE, RWKV_W), lat),
        pl.BlockSpec((nb, TILE, RWKV_W), lat),
        pl.BlockSpec((nb, TILE, 2 * FOURIER_W), lat),
    ]
    out_shape = [
        jax.ShapeDtypeStruct((b, 2, t_all, RWKV_W), BF16),
        jax.ShapeDtypeStruct((b, 2, t_all, RWKV_W), BF16),
        jax.ShapeDtypeStruct((b, 2, t_all, RWKV_W), BF16),
        jax.ShapeDtypeStruct((b, 2, t_all, RWKV_W), BF16),
        jax.ShapeDtypeStruct((b, t_all, RWKV_W), BF16),
        jax.ShapeDtypeStruct((b, 2, n_chunks, 1, RWKV_W), F32),
        jax.ShapeDtypeStruct((b, s, RWKV_W), F32),
        jax.ShapeDtypeStruct((b, s, RWKV_W), F32),
        jax.ShapeDtypeStruct((b, s, 2 * FOURIER_W), BF16),
    ]
    return pl.pallas_call(
        _prep_kernel,
        grid=(b // nb, n_j),
        in_specs=in_specs,
        out_specs=out_specs,
        out_shape=out_shape,
        compiler_params=pltpu.CompilerParams(dimension_semantics=("arbitrary", "arbitrary"),
                                             vmem_limit_bytes=VMEM_LIMIT),
        name="prep",
    )(x, x, x, ctx, modsel, norm1_g, w_in, conv_w, w0, w2bd, a0, a2bd, g2, k_k, k_a, r_k,
      cs, ones_blk, tri)


def _quad_chunk(a, bm, km, r, v, plrow, g_states, strict, incl, eye, lane_lo, mask_bd, eye_row,
                level_masks):
    n = PAIR

    def dup(xv):
        z = jnp.zeros_like(xv)
        return jnp.concatenate([jnp.where(lane_lo, xv, z), jnp.where(lane_lo, z, xv)], axis=0)

    def to_row(x2):
        return x2[:CHUNK] + x2[CHUNK:]

    def bdiag(xr):
        return jnp.where(mask_bd, jnp.concatenate([xr] * QUAD_HEADS, axis=0), jnp.zeros((), xr.dtype))

    pairs = []
    for p in range(2):
        sl = slice(p * n, (p + 1) * n)
        a2, b2, k2, r2, v2 = (dup(t[:, sl]) for t in (a, bm, km, r, v))
        rhs = jnp.concatenate([b2, k2], axis=0)
        s = _bdot_nt(jnp.concatenate([a2, r2], axis=0), rhs)
        pairs.append((sl, r2, v2, rhs, s))
    yield

    a_ab = jnp.concatenate([to_row(jnp.where(strict, s[:n, :n], 0.0)) for *_, s in pairs], axis=1)
    a_ak = jnp.concatenate([to_row(jnp.where(strict, s[:n, n:], 0.0)) for *_, s in pairs], axis=1)
    m_rbk = [jnp.concatenate([jnp.where(incl, s[n:, :n], 0.0), jnp.where(incl, s[n:, n:], 0.0)],
                             axis=1).astype(BF16) for *_, s in pairs]

    av = _bdot(a_ak, bdiag(v))
    a_lvl = [jnp.where(m, a_ab, 0.0).astype(BF16) for m in level_masks]
    t_row = jnp.where(eye_row, 1.0, 0.0) - jnp.where(level_masks[0], a_ab, 0.0)
    lvl = 1
    while lvl < len(a_lvl):
        paired = lvl + 1 < len(a_lvl)
        lhs = jnp.concatenate([a_lvl[lvl], a_lvl[lvl + 1]], axis=0) if paired else a_lvl[lvl]
        yc = jnp.dot(lhs, bdiag(t_row.astype(BF16)), preferred_element_type=F32)
        yield
        y_bd = bdiag(yc[:CHUNK].astype(BF16))
        if paired:
            both = jnp.dot(jnp.concatenate([t_row, yc[CHUNK:]], axis=0).astype(BF16), y_bd,
                           preferred_element_type=F32)
            yield
            t_row = t_row - both[:CHUNK]
            y_next = yc[CHUNK:] - both[CHUNK:]
            t_row = t_row - _bdot(t_row, bdiag(y_next.astype(BF16)))
            lvl += 2
        else:
            t_row = t_row - _bdot(t_row, y_bd)
            lvl += 1
        yield
    tx = _bdot(t_row, jnp.concatenate([bdiag(a), bdiag(av.astype(BF16))], axis=1))
    yield
    a_w = tx[:, :2 * n].astype(BF16)
    u_v = (-tx[:, 2 * n:]).astype(BF16)
    stage5 = []
    for (sl, r2, v2, rhs, _), m2 in zip(pairs, m_rbk):
        zmat = jnp.concatenate(
            [jnp.concatenate([dup(a_w[:, sl]), dup(u_v[:, sl])], axis=1),
             jnp.concatenate([jnp.zeros_like(v2), v2], axis=1)], axis=0)
        rhs_end = rhs.astype(F32) * plrow[:, sl]
        lhs = jnp.concatenate([rhs_end.T.astype(BF16), m2], axis=0)
        stage5.append(jnp.dot(lhs, zmat, preferred_element_type=F32))
    yield
    ys, g_new = [], []
    for (sl, r2, v2, rhs, _), o5, g_state in zip(pairs, stage5, g_states):
        phi = jnp.where(eye, plrow[:, sl], 0.0) - o5[:n, :n]
        r_y = r2.astype(F32) - o5[n:, :n]
        o6 = _bdot(jnp.concatenate([r_y, phi], axis=0), g_state)
        ys.append(to_row(o6[:n] + o5[n:, n:]))
        g_new.append(o6[n:] + o5[:n, n:])
    return jnp.concatenate(ys, axis=1), g_new


def _run_interleaved(gens):
    results = [None] * len(gens)
    active = list(range(len(gens)))
    while active:
        for i in list(active):
            try:
                next(gens[i])
            except StopIteration as stop:
                results[i] = stop.value
                active.remove(i)
    return results


def _scan_kernel(atf, btf, ktf, rtf, vf, plf, atb, btb, ktb, rtb, vb, plb,
                 yf_ref, yb_ref, g_ref):
    s = pl.program_id(1)

    @pl.when(s == 0)
    def _():
        g_ref[...] = jnp.zeros_like(g_ref)

    ri = lax.broadcasted_iota(jnp.int32, (PAIR, PAIR), 0)
    ci = lax.broadcasted_iota(jnp.int32, (PAIR, PAIR), 1)
    same = (ri >= HEAD) == (ci >= HEAD)
    ti = ri & (HEAD - 1)
    tj = ci & (HEAD - 1)
    eye = ri == ci
    lane_lo = lax.broadcasted_iota(jnp.int32, (CHUNK, PAIR), 1) < HEAD
    masks = [(same & (tj < ti), same & (tj <= ti)),
             (same & (tj > ti), same & (tj >= ti))]
    quad_w = QUAD_HEADS * HEAD
    rq = lax.broadcasted_iota(jnp.int32, (quad_w, quad_w), 0)
    cq = lax.broadcasted_iota(jnp.int32, (quad_w, quad_w), 1)
    mask_bd = (rq // HEAD) == (cq // HEAD)
    t_i = lax.broadcasted_iota(jnp.int32, (CHUNK, quad_w), 0)
    s_i = lax.broadcasted_iota(jnp.int32, (CHUNK, quad_w), 1) & (HEAD - 1)
    eye_row = s_i == t_i
    level_masks = []
    for d in range(2):
        late, early = (t_i, s_i) if d == 0 else (s_i, t_i)
        level_masks.append([
            ((t_i >> (lvl + 1)) == (s_i >> (lvl + 1)))
            & (((late >> lvl) & 1) == 1) & (((early >> lvl) & 1) == 0)
            for lvl in range(int(math.log2(CHUNK)))])
    refs = [(atf, btf, ktf, rtf, vf, plf, yf_ref), (atb, btb, ktb, rtb, vb, plb, yb_ref)]
    gens, dests = [], []
    for bi in range(yf_ref.shape[0]):
        for d in range(2):
            at, bt, kt, rt, vv, plr, y_ref = refs[d]
            strict, incl = masks[d]
            for q in range(RWKV_W // quad_w):
                sl = slice(q * quad_w, (q + 1) * quad_w)
                gens.append(_quad_chunk(at[bi, 0, :, sl], bt[bi, 0, :, sl], kt[bi, 0, :, sl],
                                        rt[bi, 0, :, sl], vv[bi, :, sl], plr[bi, 0, 0, :, sl],
                                        [g_ref[bi, d, 2 * q], g_ref[bi, d, 2 * q + 1]],
                                        strict, incl, eye, lane_lo, mask_bd, eye_row,
                                        level_masks[d]))
                dests.append((y_ref, bi, d, q, sl))
    for (y, g_new), (y_ref, bi, d, q, sl) in zip(_run_interleaved(gens), dests):
        g_ref[bi, d, 2 * q] = g_new[0]
        g_ref[bi, d, 2 * q + 1] = g_new[1]
        y_ref[bi, :, sl] = y


def _scan_call(at, bt, kt, rt, v, plast, s_lat):
    b = at.shape[0]
    n_chunks = at.shape[2] // CHUNK
    n_ctx = CTX_LEN // CHUNK
    n_lat = s_lat // CHUNK

    def fwd_c(s):
        return s

    def bwd_c(s):
        return jnp.where(s < n_ctx, n_ctx - 1 - s, n_chunks + n_ctx - 1 - s)

    nb = SCAN_BATCH

    def dir_spec(d, cfun):
        return pl.BlockSpec((nb, 1, CHUNK, RWKV_W), lambda bi, s: (bi, d, cfun(s), 0))

    def v_spec(cfun):
        return pl.BlockSpec((nb, CHUNK, RWKV_W), lambda bi, s: (bi, cfun(s), 0))

    def pl_spec(d, cfun):
        return pl.BlockSpec((nb, 1, 1, 1, RWKV_W), lambda bi, s: (bi, d, cfun(s), 0, 0))

    in_specs = ([dir_spec(0, fwd_c)] * 4 + [v_spec(fwd_c), pl_spec(0, fwd_c)]
                + [dir_spec(1, bwd_c)] * 4 + [v_spec(bwd_c), pl_spec(1, bwd_c)])
    out_specs = [
        pl.BlockSpec((nb, CHUNK, RWKV_W), lambda bi, s: (bi, jnp.maximum(s - n_ctx, 0), 0)),
        pl.BlockSpec((nb, CHUNK, RWKV_W),
                     lambda bi, s: (bi, jnp.minimum(n_chunks - 1 - s, n_lat - 1), 0)),
    ]
    out_shape = [jax.ShapeDtypeStruct((b, s_lat, RWKV_W), F32)] * 2
    return pl.pallas_call(
        _scan_kernel,
        grid=(b // nb, n_chunks),
        in_specs=in_specs,
        out_specs=out_specs,
        out_shape=out_shape,
        scratch_shapes=[pltpu.VMEM((nb, 2, N_PAIRS, PAIR, PAIR), F32)],
        compiler_params=pltpu.CompilerParams(dimension_semantics=("arbitrary", "arbitrary"),
                                             vmem_limit_bytes=VMEM_LIMIT),
        name="scan",
    )(at, bt, kt, rt, v, plast, at, bt, kt, rt, v, plast)


def _mix_kernel(dft_ref, rev_ref, z_ref, yf_ref, yb_ref, bonus_ref, gate_ref, x_ref, mod_ref,
                gng_ref, gnb_ref, fg_ref, wout_ref, n2g_ref, ones_ref, x1_ref, h2_ref, zf_ref,
                *, fscale):
    seq = z_ref.shape[1]
    half = seq // 2
    rt = rev_ref.shape[0]

    @pl.when(pl.program_id(1) == 0)
    def _():
        mid = z_ref[0, half:half + 1, :FOURIER_W].astype(F32)
        row0 = lax.broadcasted_iota(jnp.int32, (rt, FOURIER_W), 0) == 0
        for part, sign in ((0, 1.0), (1, -1.0)):
            lanes = slice(part * FOURIER_W, (part + 1) * FOURIER_W)
            for r in range(half // rt):
                lo = seq - rt * (r + 1)
                nxt = (z_ref[0, lo + rt:lo + 2 * rt, lanes] if r > 0
                       else jnp.zeros((rt, FOURIER_W), BF16))
                rev = jnp.dot(rev_ref[...], jnp.concatenate([z_ref[0, lo:lo + rt, lanes], nxt], axis=0),
                              preferred_element_type=F32)
                folded = z_ref[0, r * rt:(r + 1) * rt, lanes].astype(F32) + sign * rev
                if part == 1 and r == 0:
                    folded = jnp.where(row0, mid, folded)
                zf_ref[part * half + r * rt:part * half + (r + 1) * rt, :] = folded.astype(BF16)

    yfour = jnp.dot(dft_ref[...], zf_ref[...], preferred_element_type=F32)
    four = _rms(yfour * fscale, fg_ref[...])

    ones_blk = ones_ref[...]
    y = yf_ref[0] + yb_ref[0]
    mu = _dot_exact_rhs(y, ones_blk) * (1.0 / HEAD)
    dy = y - mu
    var = _dot_exact_rhs(dy * dy, ones_blk) * (1.0 / HEAD)
    yn = dy * lax.rsqrt(var + GN_EPS) * gng_ref[...] + gnb_ref[...]
    rw = (yn + bonus_ref[0]) * gate_ref[0]

    out = _bdot(four, wout_ref[:FOURIER_W, :]) + _bdot(rw, wout_ref[FOURIER_W:, :])
    x1 = x_ref[0] + mod_ref[0, 2:3, :] * out
    x1_ref[0] = x1
    h2 = _rms(x1, n2g_ref[...]) * (1.0 + mod_ref[0, 4:5, :]) + mod_ref[0, 3:4, :]
    h2_ref[0] = h2.astype(BF16)


def _mix_call(dft, rev, z, yf, yb, bonus, gate, x, modx, gn_g, gn_b, f_g, w_out, norm2_g,
              ones_blk):
    b, s, d = x.shape
    tm = MIX_TILE

    def full(arr):
        nd = arr.ndim
        return pl.BlockSpec(arr.shape, lambda bi, m, _n=nd: (0,) * _n)

    tok = lambda w: pl.BlockSpec((1, tm, w), lambda bi, m: (bi, m, 0))
    in_specs = [
        pl.BlockSpec((tm, s), lambda bi, m: (m, 0)),
        full(rev),
        pl.BlockSpec((1, s, 2 * FOURIER_W), lambda bi, m: (bi, 0, 0)),
        tok(RWKV_W), tok(RWKV_W), tok(RWKV_W), tok(RWKV_W), tok(d),
        pl.BlockSpec((1, 6, d), lambda bi, m: (bi, 0, 0)),
        full(gn_g), full(gn_b), full(f_g), full(w_out), full(norm2_g), full(ones_blk),
    ]
    fscale = 1.0 / math.sqrt(s * (FOURIER_W // FOURIER_GROUPS))
    return pl.pallas_call(
        functools.partial(_mix_kernel, fscale=fscale),
        grid=(b, s // tm),
        in_specs=in_specs,
        out_specs=[tok(d), tok(d)],
        out_shape=[jax.ShapeDtypeStruct((b, s, d), F32), jax.ShapeDtypeStruct((b, s, d), BF16)],
        scratch_shapes=[pltpu.VMEM((s, FOURIER_W), BF16)],
        compiler_params=pltpu.CompilerParams(dimension_semantics=("arbitrary", "arbitrary"),
                                             vmem_limit_bytes=VMEM_LIMIT),
        name="mix",
    )(dft, rev, z, yf, yb, bonus, gate, x, modx, gn_g, gn_b, f_g, w_out, norm2_g, ones_blk)


def _ffn_up_kernel(h2_ref, wg_ref, wv_ref, cwg_ref, cwv_ref, cbg_ref, cbv_ref, act_ref):
    t = h2_ref.shape[1]
    rows = FFN_ROWS
    n_blk = t // rows
    w = jnp.concatenate([wg_ref[...], wv_ref[...]], axis=1)
    cw = jnp.concatenate([cwg_ref[...], cwv_ref[...]], axis=1).astype(BF16)
    cb = jnp.concatenate([cbg_ref[...], cbv_ref[...]], axis=1).astype(BF16)
    col = lax.broadcasted_iota(jnp.int32, (rows, 2 * FF_TILE), 0) & (GRID_W - 1)
    has_left = col > 0
    has_right = col < GRID_W - 1
    zeros = jnp.zeros((GRID_W, 2 * FF_TILE), BF16)

    def row_mixes(i):
        u = jnp.dot(h2_ref[0, i * rows:(i + 1) * rows, :], w, preferred_element_type=F32)
        u_l = jnp.where(has_left, pltpu.roll(u, 1, 0), 0.0).astype(BF16)
        u_r = jnp.where(has_right, pltpu.roll(u, rows - 1, 0), 0.0).astype(BF16)
        u_c = u.astype(BF16)
        return [cw[3 * kh:3 * kh + 1] * u_l + cw[3 * kh + 1:3 * kh + 2] * u_c
                + cw[3 * kh + 2:3 * kh + 3] * u_r for kh in range(3)]

    def finish(i, prev, cur, nxt):
        above = zeros if prev is None else prev[0][rows - GRID_W:]
        below = zeros if nxt is None else nxt[2][:GRID_W]
        c = (cur[1] + jnp.concatenate([above, cur[0][:rows - GRID_W]], axis=0)
             + jnp.concatenate([cur[2][GRID_W:], below], axis=0) + cb)
        cg = c[:, :FF_TILE]
        act_ref[0, i * rows:(i + 1) * rows, :] = cg * _sigmoid(cg) * c[:, FF_TILE:]

    mixes = [None] * (n_blk + 1)
    for i in range(n_blk):
        mixes[i] = row_mixes(i)
        if i >= 1:
            finish(i - 1, mixes[i - 2] if i >= 2 else None, mixes[i - 1], mixes[i])
    finish(n_blk - 1, mixes[n_blk - 2] if n_blk >= 2 else None, mixes[n_blk - 1], None)


def _ffn_up_call(h2, w_up, conv_w9, conv_b):
    b, s, d = h2.shape
    n_f = D_FF // FF_TILE
    in_specs = [
        pl.BlockSpec((1, s, d), lambda bi, f: (bi, 0, 0)),
        pl.BlockSpec((d, FF_TILE), lambda bi, f: (0, f)),
        pl.BlockSpec((d, FF_TILE), lambda bi, f: (0, n_f + f)),
        pl.BlockSpec((9, FF_TILE), lambda bi, f: (0, f)),
        pl.BlockSpec((9, FF_TILE), lambda bi, f: (0, n_f + f)),
        pl.BlockSpec((1, FF_TILE), lambda bi, f: (0, f)),
        pl.BlockSpec((1, FF_TILE), lambda bi, f: (0, n_f + f)),
    ]
    return pl.pallas_call(
        _ffn_up_kernel,
        grid=(b, n_f),
        in_specs=in_specs,
        out_specs=pl.BlockSpec((1, s, FF_TILE), lambda bi, f: (bi, 0, f)),
        out_shape=jax.ShapeDtypeStruct((b, s, D_FF), BF16),
        compiler_params=pltpu.CompilerParams(dimension_semantics=("arbitrary", "arbitrary"),
                                             vmem_limit_bytes=VMEM_LIMIT),
        name="ffn_up",
    )(h2, w_up, w_up, conv_w9, conv_w9, conv_b, conv_b)


def _ffn_down_kernel(act_ref, wd_ref, x1_ref, mod_ref, fg_ref, out_ref):
    y = jnp.dot(act_ref[0], wd_ref[...], preferred_element_type=F32)
    out_ref[0] = _rms(x1_ref[0] + mod_ref[0, 5:6, :] * y, fg_ref[...])


def _ffn_down_call(act, w_down, x1, modx, final_g):
    b, s, d = x1.shape
    tm = DOWN_TILE
    in_specs = [
        pl.BlockSpec((1, tm, D_FF), lambda bi, m: (bi, m, 0)),
        pl.BlockSpec((D_FF, d), lambda bi, m: (0, 0)),
        pl.BlockSpec((1, tm, d), lambda bi, m: (bi, m, 0)),
        pl.BlockSpec((1, 6, d), lambda bi, m: (bi, 0, 0)),
        pl.BlockSpec((1, d), lambda bi, m: (0, 0)),
    ]
    return pl.pallas_call(
        _ffn_down_kernel,
        grid=(b, s // tm),
        in_specs=in_specs,
        out_specs=pl.BlockSpec((1, tm, d), lambda bi, m: (bi, m, 0)),
        out_shape=jax.ShapeDtypeStruct((b, s, d), F32),
        compiler_params=pltpu.CompilerParams(dimension_semantics=("arbitrary", "arbitrary"),
                                             vmem_limit_bytes=VMEM_LIMIT),
        name="ffn_down",
    )(act, w_down, x1, modx, final_g)


@functools.lru_cache(maxsize=None)
def _constants(seq):
    gw = FOURIER_W // FOURIER_GROUPS
    nm = np.outer(np.arange(gw), np.arange(gw)) % gw
    ang = 2.0 * np.pi * nm / gw
    cs = np.zeros((FOURIER_W, 2 * FOURIER_W), np.float64)
    for gi in range(FOURIER_GROUPS):
        sl = slice(gi * gw, (gi + 1) * gw)
        cs[sl, sl] = np.cos(ang)
        cs[sl, FOURIER_W + gi * gw:FOURIER_W + (gi + 1) * gw] = np.sin(ang)
    half = seq // 2
    k_idx = np.arange(seq, dtype=np.int64)[:, None]
    ang_c = 2.0 * np.pi * ((k_idx * np.arange(0, half + 1, dtype=np.int64)[None, :]) % seq) / seq
    ang_s = 2.0 * np.pi * ((k_idx * np.arange(1, half, dtype=np.int64)[None, :]) % seq) / seq
    dft = np.concatenate([np.cos(ang_c), -np.sin(ang_s)], axis=1)
    ri = np.arange(REV_TILE)
    rev = (np.arange(2 * REV_TILE)[None, :] == (REV_TILE - ri)[:, None]).astype(np.float64)
    head_id = np.arange(RWKV_W) // HEAD
    ones_blk = (head_id[:, None] == head_id[None, :]).astype(np.float64)
    ti = np.arange(TILE)
    same_chunk = (ti[:, None] // CHUNK) == (ti[None, :] // CHUNK)
    tri = np.stack([same_chunk & (ti[None, :] <= ti[:, None]),
                    same_chunk & (ti[None, :] >= ti[:, None])]).astype(np.float64)
    to_bf16 = lambda arr: np.asarray(arr, dtype=np.float32).astype(BF16)
    return (np.asarray(cs, dtype=np.float32), np.asarray(dft, dtype=np.float32),
            to_bf16(ones_blk), to_bf16(tri), to_bf16(rev))


def _block_diag_dirs(w):
    z = jnp.zeros_like(w[0])
    return jnp.concatenate([jnp.concatenate([w[0], z], axis=1),
                            jnp.concatenate([z, w[1]], axis=1)], axis=0)


def kernel(x, c, ctx, c_ctx, ada_w, ada_b, norm1_g, norm2_g, w_in, rwkv_conv_w, decay_w0, decay_w2, iclr_a0, iclr_a2, gate_g2, k_k, k_a, r_k, gn_g, gn_b, fourier_g, w_out, ffn_w_up, ffn_conv_w, ffn_conv_b, ffn_w_down, final_g):
    b, s, d = x.shape
    assert ada_w.shape[0] == 1, "single-layer configuration"
    assert (b, s, d) == (c.shape[0], s, D_MODEL) and ctx.shape == (b, CTX_LEN, d)
    assert s % TILE == 0 and CTX_LEN == TILE
    cs, dft, ones_blk, tri, rev = (jnp.asarray(t) for t in _constants(s))
    cs, dft = cs.astype(BF16), dft.astype(BF16)

    cc = jnp.concatenate([c, c_ctx[None, :], jnp.zeros((16 - b - 1, d), F32)], axis=0)
    mod = _mod_call(cc, ada_w[0], ada_b[0][None, :])
    modx = mod[:b].reshape(b, 6, d)
    modc = jnp.broadcast_to(mod[b].reshape(1, 6, d)[:, :2], (b, 2, d))
    modsel = jnp.stack([modc, modx[:, :2]], axis=1)

    row = lambda t: t.reshape(1, -1)
    at, bt, kt, rt, v, plast, bonus, gate, z = _prep_call(
        x, ctx, modsel, row(norm1_g[0]), w_in[0].astype(BF16), rwkv_conv_w[0],
        row(decay_w0[0]), _block_diag_dirs(decay_w2[0]).astype(BF16),
        row(iclr_a0[0]), _block_diag_dirs(iclr_a2[0]).astype(BF16),
        gate_g2[0].astype(BF16), row(k_k[0]), row(k_a[0]), row(r_k[0]), cs, ones_blk, tri)

    yf, yb = _scan_call(at, bt, kt, rt, v, plast, s)

    x1, h2 = _mix_call(dft, rev, z, yf, yb, bonus, gate, x, modx, row(gn_g[0]), row(gn_b[0]),
                       row(fourier_g[0]), w_out[0].astype(BF16), row(norm2_g[0]), ones_blk)

    act = _ffn_up_call(h2, ffn_w_up[0].astype(BF16), ffn_conv_w[0].reshape(9, 2 * D_FF),
                       row(ffn_conv_b[0]))
    return _ffn_down_call(act, ffn_w_down[0].astype(BF16), x1, modx, row(final_g))
```

```python
import functools
import math

import numpy as np
import jax
import jax.numpy as jnp
from jax import lax
from jax.experimental import pallas as pl
from jax.experimental.pallas import tpu as pltpu

F32 = jnp.float32
BF16 = jnp.bfloat16

D_MODEL = 1024
CTX_LEN = 256
GRID_W = 64
FOURIER_W = 512
FOURIER_GROUPS = 8
RWKV_W = 512
HEAD = 64
N_HEADS = 8
DECAY_RANK = 64
ICLR_RANK = 64
GATE_RANK = 128
D_FF = 2816
NORM_EPS = 1e-6
GN_EPS = 64e-5
KK_EPS = 1e-12

CHUNK = 64
TILE = 256
MIX_TILE = 512
REV_TILE = 256
CHUNKS_PER_TILE = TILE // CHUNK
PREP_BATCH = 2
PAIR = 2 * HEAD
N_PAIRS = RWKV_W // PAIR
QUAD_HEADS = 4
SCAN_BATCH = 8
MOD_TILE = 1536
FF_TILE = 1408
FFN_ROWS = 256
DOWN_TILE = 1024
VMEM_LIMIT = 56 * 1024 * 1024
HEAD_SUM_TERMS = 1
CUMSUM_TERMS = 2
MOD_WEIGHT_TERMS = 2

RKV_LO = FOURIER_W
RKV_HI = FOURIER_W + 3 * RWKV_W
WD_LO = RKV_HI
AD_LO = WD_LO + 2 * DECAY_RANK
GD_LO = AD_LO + 2 * ICLR_RANK
PROJ_W = GD_LO + GATE_RANK


def _bdot(a, b):
    return jnp.dot(a.astype(BF16), b.astype(BF16), preferred_element_type=F32)


def _bdot_nt(a, b):
    return lax.dot_general(a.astype(BF16), b.astype(BF16), (((1,), (1,)), ((), ())),
                           preferred_element_type=F32)


def _bdot_tn(a, b):
    return lax.dot_general(a.astype(BF16), b.astype(BF16), (((0,), (0,)), ((), ())),
                           preferred_element_type=F32)


def _split_terms(a, terms):
    out, rem = [], a
    for _ in range(terms):
        hi = rem.astype(BF16)
        out.append(hi)
        rem = rem - hi.astype(F32)
    return out


def _dot_exact_rhs(a, b_exact, terms=HEAD_SUM_TERMS):
    acc = None
    for piece in _split_terms(a, terms):
        t = jnp.dot(piece, b_exact, preferred_element_type=F32)
        acc = t if acc is None else acc + t
    return acc


def _dot_exact_lhs(a_exact, b, terms=CUMSUM_TERMS):
    acc = None
    for piece in _split_terms(b, terms):
        t = jnp.dot(a_exact, piece, preferred_element_type=F32)
        acc = t if acc is None else acc + t
    return acc


def _sigmoid(x):
    return 1.0 / (1.0 + jnp.exp(-x))


def _rms(x, g):
    return x * lax.rsqrt(jnp.mean(x * x, axis=-1, keepdims=True) + NORM_EPS) * g


def _mod_kernel(c_ref, w_ref, b_ref, o_ref):
    cc = c_ref[...]
    s = cc * _sigmoid(cc)
    rows = s.shape[0]
    s_parts = jnp.concatenate(_split_terms(s, 3), axis=0)
    acc = b_ref[...]
    for wp in _split_terms(w_ref[...], MOD_WEIGHT_TERMS):
        t = jnp.dot(s_parts, wp, preferred_element_type=F32)
        acc = acc + (t[:rows] + t[rows:2 * rows] + t[2 * rows:])
    o_ref[...] = acc


def _mod_call(cc, ada_w, ada_b):
    rows, d = cc.shape
    n = ada_w.shape[1]
    tn = MOD_TILE
    return pl.pallas_call(
        _mod_kernel,
        grid=(n // tn,),
        in_specs=[pl.BlockSpec((rows, d), lambda i: (0, 0)),
                  pl.BlockSpec((d, tn), lambda i: (0, i)),
                  pl.BlockSpec((1, tn), lambda i: (0, i))],
        out_specs=pl.BlockSpec((rows, tn), lambda i: (0, i)),
        out_shape=jax.ShapeDtypeStruct((rows, n), F32),
        compiler_params=pltpu.CompilerParams(dimension_semantics=("arbitrary",),
                                             vmem_limit_bytes=VMEM_LIMIT),
        name="mod",
    )(cc, ada_w, ada_b)


def _prep_kernel(x_ref, xp_ref, xn_ref, ctx_ref, ms_ref, g1_ref, win_ref, cw_ref,
                 w0_ref, w2_ref, a0_ref, a2_ref, g2_ref, kk_ref, ka_ref, rk_ref,
                 cs_ref, ones_ref, tri_ref,
                 at_ref, bt_ref, kt_ref, rt_ref, v_ref, pl_ref, bonus_ref, gate_ref, z_ref):
    j = pl.program_id(1)
    refs = (x_ref, xp_ref, xn_ref, ctx_ref, ms_ref, g1_ref, win_ref, cw_ref,
            w0_ref, w2_ref, a0_ref, a2_ref, kk_ref, ka_ref, ones_ref, tri_ref,
            at_ref, bt_ref, kt_ref, rt_ref, v_ref, pl_ref)
    tiles = _run_interleaved([_prep_tile(bi, *refs) for bi in range(x_ref.shape[0])])

    @pl.when(j >= 1)
    def _():
        ones_blk = ones_ref[...]
        for bi, (u_four, r, v, gd, kd_sum) in enumerate(tiles):
            bs = _dot_exact_rhs(r * rk_ref[...] * kd_sum, ones_blk)
            bonus_ref[bi] = bs * v
            gate_ref[bi] = _bdot(_sigmoid(gd), g2_ref[...])
            z_ref[bi] = _bdot(u_four, cs_ref[...]).astype(BF16)


def _prep_tile(bi, x_ref, xp_ref, xn_ref, ctx_ref, ms_ref, g1_ref, win_ref, cw_ref,
               w0_ref, w2_ref, a0_ref, a2_ref, kk_ref, ka_ref, ones_ref, tri_ref,
               at_ref, bt_ref, kt_ref, rt_ref, v_ref, pl_ref):
    j = pl.program_id(1)
    n_j = pl.num_programs(1)
    is_ctx = j == 0
    shift = ms_ref[bi, 0, 0:1, :]
    scale = ms_ref[bi, 0, 1:2, :]
    g = g1_ref[...]

    def norm_mod(xr):
        return _rms(xr, g) * (1.0 + scale) + shift

    xin = jnp.where(is_ctx, ctx_ref[bi], x_ref[bi])
    h = norm_mod(xin).astype(BF16)
    u = jnp.dot(h, win_ref[...], preferred_element_type=F32)

    halo = jnp.concatenate([xp_ref[bi], xn_ref[bi]], axis=0)
    uh = jnp.dot(norm_mod(halo).astype(BF16), win_ref[:, RKV_LO:RKV_HI],
                 preferred_element_type=F32)
    yield
    prev_row = jnp.where(j >= 2, uh[7:8], 0.0)
    next_row = jnp.where(jnp.logical_and(j >= 1, j < n_j - 1), uh[8:9], 0.0)

    rkv = u[:, RKV_LO:RKV_HI]
    row = lax.broadcasted_iota(jnp.int32, rkv.shape, 0)
    u_m1 = jnp.where(row == 0, prev_row, pltpu.roll(rkv, 1, 0))
    u_p1 = jnp.where(row == TILE - 1, next_row, pltpu.roll(rkv, TILE - 1, 0))
    rkv = cw_ref[0:1, :] * u_m1 + cw_ref[1:2, :] * rkv + cw_ref[2:3, :] * u_p1
    r = rkv[:, 0:RWKV_W]
    k = rkv[:, RWKV_W:2 * RWKV_W]
    v = rkv[:, 2 * RWKV_W:3 * RWKV_W]

    wd = u[:, WD_LO:AD_LO]
    ad = u[:, AD_LO:GD_LO]
    gd = u[:, GD_LO:PROJ_W]
    w_lora = w0_ref[...] + _bdot(jnp.tanh(wd), w2_ref[...])
    lw = -math.exp(-0.5) * _sigmoid(w_lora)
    a = _sigmoid(a0_ref[...] + _bdot(ad, a2_ref[...]))

    ones_blk = ones_ref[...]
    kraw = k * kk_ref[...]
    kk = kraw * lax.rsqrt(_dot_exact_rhs(kraw * kraw, ones_blk) + KK_EPS)
    ka = ka_ref[...]
    kd = [k * (1.0 + (a[:, d * RWKV_W:(d + 1) * RWKV_W] - 1.0) * ka) for d in range(2)]

    v_ref[bi] = v.astype(BF16)
    for d in range(2):
        lwd = lw[:, d * RWKV_W:(d + 1) * RWKV_W]
        ad_ = a[:, d * RWKV_W:(d + 1) * RWKV_W]
        c = _dot_exact_lhs(tri_ref[d], lwd)
        e_pos = jnp.exp(c)
        e_neg = jnp.exp(-c)
        e_prev = jnp.exp(c - lwd)
        at_ref[bi, d] = (kk * e_prev).astype(BF16)
        bt_ref[bi, d] = (kk * ad_ * e_neg).astype(BF16)
        kt_ref[bi, d] = (kd[d] * e_neg).astype(BF16)
        rt_ref[bi, d] = (r * e_pos).astype(BF16)
        for q in range(CHUNKS_PER_TILE):
            last = q * CHUNK + (CHUNK - 1 if d == 0 else 0)
            pl_ref[bi, d, q] = e_pos[last:last + 1, :]
    return u[:, 0:FOURIER_W], r, v, gd, kd[0] + kd[1]


def _prep_call(x, ctx, modsel, norm1_g, w_in, conv_w, w0, w2bd, a0, a2bd, g2, k_k, k_a, r_k,
               cs, ones_blk, tri):
    b, s, d = x.shape
    n_j = 1 + s // TILE
    n_chunks = (CTX_LEN + s) // CHUNK
    t_all = CTX_LEN + s
    c2 = 2 * RWKV_W

    def full(arr):
        nd = arr.ndim
        return pl.BlockSpec(arr.shape, lambda bi, j, _n=nd: (0,) * _n)

    rows8 = TILE // 8
    nb = PREP_BATCH
    in_specs = [
        pl.BlockSpec((nb, TILE, d), lambda bi, j: (bi, jnp.maximum(j - 1, 0), 0)),
        pl.BlockSpec((nb, 8, d), lambda bi, j: (bi, jnp.maximum((j - 1) * rows8 - 1, 0), 0)),
        pl.BlockSpec((nb, 8, d), lambda bi, j: (bi, jnp.minimum(j * rows8, s // 8 - 1), 0)),
        pl.BlockSpec((nb, CTX_LEN, d), lambda bi, j: (bi, 0, 0)),
        pl.BlockSpec((nb, 1, 2, d), lambda bi, j: (bi, jnp.minimum(j, 1), 0, 0)),
        full(norm1_g), full(w_in), full(conv_w), full(w0), full(w2bd), full(a0), full(a2bd),
        full(g2), full(k_k), full(k_a), full(r_k), full(cs), full(ones_blk), full(tri),
    ]
    lat = lambda bi, j: (bi, jnp.maximum(j - 1, 0), 0)
    out_specs = [
        pl.BlockSpec((nb, 2, TILE, RWKV_W), lambda bi, j: (bi, 0, j, 0)),
        pl.BlockSpec((nb, 2, TILE, RWKV_W), lambda bi, j: (bi, 0, j, 0)),
        pl.BlockSpec((nb, 2, TILE, RWKV_W), lambda bi, j: (bi, 0, j, 0)),
        pl.BlockSpec((nb, 2, TILE, RWKV_W), lambda bi, j: (bi, 0, j, 0)),
        pl.BlockSpec((nb, TILE, RWKV_W), lambda bi, j: (bi, j, 0)),
        pl.BlockSpec((nb, 2, CHUNKS_PER_TILE, 1, RWKV_W), lambda bi, j: (bi, 0, j, 0, 0)),
        pl.BlockSpec((nb, TILE, RWKV_W), lat),
        pl.BlockSpec((nb, TILE, RWKV_W), lat),
        pl.BlockSpec((nb, TILE, 2 * FOURIER_W), lat),
    ]
    out_shape = [
        jax.ShapeDtypeStruct((b, 2, t_all, RWKV_W), BF16),
        jax.ShapeDtypeStruct((b, 2, t_all, RWKV_W), BF16),
        jax.ShapeDtypeStruct((b, 2, t_all, RWKV_W), BF16),
        jax.ShapeDtypeStruct((b, 2, t_all, RWKV_W), BF16),
        jax.ShapeDtypeStruct((b, t_all, RWKV_W), BF16),
        jax.ShapeDtypeStruct((b, 2, n_chunks, 1, RWKV_W), F32),
        jax.ShapeDtypeStruct((b, s, RWKV_W), F32),
        jax.ShapeDtypeStruct((b, s, RWKV_W), F32),
        jax.ShapeDtypeStruct((b, s, 2 * FOURIER_W), BF16),
    ]
    return pl.pallas_call(
        _prep_kernel,
        grid=(b // nb, n_j),
        in_specs=in_specs,
        out_specs=out_specs,
        out_shape=out_shape,
        compiler_params=pltpu.CompilerParams(dimension_semantics=("arbitrary", "arbitrary"),
                                             vmem_limit_bytes=VMEM_LIMIT),
        name="prep",
    )(x, x, x, ctx, modsel, norm1_g, w_in, conv_w, w0, w2bd, a0, a2bd, g2, k_k, k_a, r_k,
      cs, ones_blk, tri)


def _quad_chunk(a, bm, km, r, v, plrow, g_states, strict, incl, eye, lane_lo, mask_bd, eye_row,
                level_masks, quad_pos, quad_shift):
    n = PAIR

    def dup(xv):
        z = jnp.zeros_like(xv)
        return jnp.concatenate([jnp.where(lane_lo, xv, z), jnp.where(lane_lo, z, xv)], axis=0)

    def to_row(x2):
        return x2[:CHUNK] + x2[CHUNK:]

    def bdiag(xr):
        return jnp.where(mask_bd, jnp.concatenate([xr] * QUAD_HEADS, axis=0), jnp.zeros((), xr.dtype))

    pairs = []
    for p in range(2):
        sl = slice(p * n, (p + 1) * n)
        a2, b2, k2, r2, v2 = (dup(t[:, sl]) for t in (a, bm, km, r, v))
        rhs = jnp.concatenate([b2, k2], axis=0)
        s = _bdot_nt(jnp.concatenate([a2, r2], axis=0), rhs)
        pairs.append((sl, r2, v2, rhs, s))
    yield

    a_ab = jnp.concatenate([to_row(jnp.where(strict, s[:n, :n], 0.0)) for *_, s in pairs], axis=1)
    a_ak = jnp.concatenate([to_row(jnp.where(strict, s[:n, n:], 0.0)) for *_, s in pairs], axis=1)
    m_rbk = [jnp.concatenate([jnp.where(incl, s[n:, :n], 0.0), jnp.where(incl, s[n:, n:], 0.0)],
                             axis=1).astype(BF16) for *_, s in pairs]

    av = _bdot(a_ak, bdiag(v))
    a_lvl = [jnp.where(m, a_ab, 0.0).astype(BF16) for m in level_masks]
    l0 = jnp.where(level_masks[0], a_ab, 0.0)
    m1 = jnp.where(level_masks[1], a_ab, 0.0)
    step = quad_shift
    rows_n, lanes_n = l0.shape

    def shift_rows(xv, k):
        return pltpu.roll(xv, (k * step) % rows_n, 0)

    def shift_lanes(xv, k):
        return pltpu.roll(xv, (-k * step) % lanes_n, 1)

    row_scale = (jnp.where(quad_pos[0], shift_lanes(l0, 2), 0.0)
                 + jnp.where(quad_pos[1], shift_lanes(l0, 1), 0.0))
    col_scale = (jnp.where(quad_pos[2], shift_rows(l0, 1), 0.0)
                 + jnp.where(quad_pos[3], shift_rows(l0, 2), 0.0))
    ml = jnp.where(quad_pos[0], shift_lanes(m1, 1), 0.0) * col_scale
    lm = jnp.where(quad_pos[3], shift_rows(m1, 1), 0.0) * row_scale
    lml = jnp.where(quad_pos[3], shift_rows(ml, 1), 0.0) * row_scale
    t_row = jnp.where(eye_row, 1.0, 0.0) - l0 - m1 + lm + ml - lml
    lvl = 2
    while lvl < len(a_lvl):
        paired = lvl + 1 < len(a_lvl)
        lhs = jnp.concatenate([a_lvl[lvl], a_lvl[lvl + 1]], axis=0) if paired else a_lvl[lvl]
        yc = jnp.dot(lhs, bdiag(t_row.astype(BF16)), preferred_element_type=F32)
        yield
        y_bd = bdiag(yc[:CHUNK].astype(BF16))
        if paired:
            both = jnp.dot(jnp.concatenate([t_row, yc[CHUNK:]], axis=0).astype(BF16), y_bd,
                           preferred_element_type=F32)
            yield
            t_row = t_row - both[:CHUNK]
            y_next = yc[CHUNK:] - both[CHUNK:]
            t_row = t_row - _bdot(t_row, bdiag(y_next.astype(BF16)))
            lvl += 2
        else:
            t_row = t_row - _bdot(t_row, y_bd)
            lvl += 1
        yield
    tx = _bdot(t_row, jnp.concatenate([bdiag(a), bdiag(av.astype(BF16))], axis=1))
    yield
    a_w = tx[:, :2 * n].astype(BF16)
    u_v = (-tx[:, 2 * n:]).astype(BF16)
    stage5 = []
    for (sl, r2, v2, rhs, _), m2 in zip(pairs, m_rbk):
        zmat = jnp.concatenate(
            [jnp.concatenate([dup(a_w[:, sl]), dup(u_v[:, sl])], axis=1),
             jnp.concatenate([jnp.zeros_like(v2), v2], axis=1)], axis=0)
        rhs_end = rhs.astype(F32) * plrow[:, sl]
        lhs = jnp.concatenate([rhs_end.T.astype(BF16), m2], axis=0)
        stage5.append(jnp.dot(lhs, zmat, preferred_element_type=F32))
    yield
    ys, g_new = [], []
    for (sl, r2, v2, rhs, _), o5, g_state in zip(pairs, stage5, g_states):
        phi = jnp.where(eye, plrow[:, sl], 0.0) - o5[:n, :n]
        r_y = r2.astype(F32) - o5[n:, :n]
        o6 = _bdot(jnp.concatenate([r_y, phi], axis=0), g_state)
        ys.append(to_row(o6[:n] + o5[n:, n:]))
        g_new.append(o6[n:] + o5[:n, n:])
    return jnp.concatenate(ys, axis=1), g_new


def _run_interleaved(gens):
    results = [None] * len(gens)
    active = list(range(len(gens)))
    while active:
        for i in list(active):
            try:
                next(gens[i])
            except StopIteration as stop:
                results[i] = stop.value
                active.remove(i)
    return results


def _scan_kernel(atf, btf, ktf, rtf, vf, plf, atb, btb, ktb, rtb, vb, plb,
                 yf_ref, yb_ref, g_ref):
    s = pl.program_id(1)

    @pl.when(s == 0)
    def _():
        g_ref[...] = jnp.zeros_like(g_ref)

    ri = lax.broadcasted_iota(jnp.int32, (PAIR, PAIR), 0)
    ci = lax.broadcasted_iota(jnp.int32, (PAIR, PAIR), 1)
    same = (ri >= HEAD) == (ci >= HEAD)
    ti = ri & (HEAD - 1)
    tj = ci & (HEAD - 1)
    eye = ri == ci
    lane_lo = lax.broadcasted_iota(jnp.int32, (CHUNK, PAIR), 1) < HEAD
    masks = [(same & (tj < ti), same & (tj <= ti)),
             (same & (tj > ti), same & (tj >= ti))]
    quad_w = QUAD_HEADS * HEAD
    rq = lax.broadcasted_iota(jnp.int32, (quad_w, quad_w), 0)
    cq = lax.broadcasted_iota(jnp.int32, (quad_w, quad_w), 1)
    mask_bd = (rq // HEAD) == (cq // HEAD)
    t_i = lax.broadcasted_iota(jnp.int32, (CHUNK, quad_w), 0)
    s_i = lax.broadcasted_iota(jnp.int32, (CHUNK, quad_w), 1) & (HEAD - 1)
    eye_row = s_i == t_i
    level_masks, quad_pos = [], []
    for d in range(2):
        late, early = (t_i, s_i) if d == 0 else (s_i, t_i)
        level_masks.append([
            ((t_i >> (lvl + 1)) == (s_i >> (lvl + 1)))
            & (((late >> lvl) & 1) == 1) & (((early >> lvl) & 1) == 0)
            for lvl in range(int(math.log2(CHUNK)))])
        first, last = (0, 3) if d == 0 else (3, 0)
        inner = 1 if d == 0 else -1
        quad_pos.append([(s_i & 3) == first, (s_i & 3) == first + inner,
                         (t_i & 3) == last - inner, (t_i & 3) == last])
    refs = [(atf, btf, ktf, rtf, vf, plf, yf_ref), (atb, btb, ktb, rtb, vb, plb, yb_ref)]
    gens, dests = [], []
    for bi in range(yf_ref.shape[0]):
        for d in range(2):
            at, bt, kt, rt, vv, plr, y_ref = refs[d]
            strict, incl = masks[d]
            for q in range(RWKV_W // quad_w):
                sl = slice(q * quad_w, (q + 1) * quad_w)
                gens.append(_quad_chunk(at[bi, 0, :, sl], bt[bi, 0, :, sl], kt[bi, 0, :, sl],
                                        rt[bi, 0, :, sl], vv[bi, :, sl], plr[bi, 0, 0, :, sl],
                                        [g_ref[bi, d, 2 * q], g_ref[bi, d, 2 * q + 1]],
                                        strict, incl, eye, lane_lo, mask_bd, eye_row,
                                        level_masks[d], quad_pos[d], 1 if d == 0 else -1))
                dests.append((y_ref, bi, d, q, sl))
    for (y, g_new), (y_ref, bi, d, q, sl) in zip(_run_interleaved(gens), dests):
        g_ref[bi, d, 2 * q] = g_new[0]
        g_ref[bi, d, 2 * q + 1] = g_new[1]
        y_ref[bi, :, sl] = y


def _scan_call(at, bt, kt, rt, v, plast, s_lat):
    b = at.shape[0]
    n_chunks = at.shape[2] // CHUNK
    n_ctx = CTX_LEN // CHUNK
    n_lat = s_lat // CHUNK

    def fwd_c(s):
        return s

    def bwd_c(s):
        return jnp.where(s < n_ctx, n_ctx - 1 - s, n_chunks + n_ctx - 1 - s)

    nb = SCAN_BATCH

    def dir_spec(d, cfun):
        return pl.BlockSpec((nb, 1, CHUNK, RWKV_W), lambda bi, s: (bi, d, cfun(s), 0))

    def v_spec(cfun):
        return pl.BlockSpec((nb, CHUNK, RWKV_W), lambda bi, s: (bi, cfun(s), 0))

    def pl_spec(d, cfun):
        return pl.BlockSpec((nb, 1, 1, 1, RWKV_W), lambda bi, s: (bi, d, cfun(s), 0, 0))

    in_specs = ([dir_spec(0, fwd_c)] * 4 + [v_spec(fwd_c), pl_spec(0, fwd_c)]
                + [dir_spec(1, bwd_c)] * 4 + [v_spec(bwd_c), pl_spec(1, bwd_c)])
    out_specs = [
        pl.BlockSpec((nb, CHUNK, RWKV_W), lambda bi, s: (bi, jnp.maximum(s - n_ctx, 0), 0)),
        pl.BlockSpec((nb, CHUNK, RWKV_W),
                     lambda bi, s: (bi, jnp.minimum(n_chunks - 1 - s, n_lat - 1), 0)),
    ]
    out_shape = [jax.ShapeDtypeStruct((b, s_lat, RWKV_W), F32)] * 2
    return pl.pallas_call(
        _scan_kernel,
        grid=(b // nb, n_chunks),
        in_specs=in_specs,
        out_specs=out_specs,
        out_shape=out_shape,
        scratch_shapes=[pltpu.VMEM((nb, 2, N_PAIRS, PAIR, PAIR), F32)],
        compiler_params=pltpu.CompilerParams(dimension_semantics=("arbitrary", "arbitrary"),
                                             vmem_limit_bytes=VMEM_LIMIT),
        name="scan",
    )(at, bt, kt, rt, v, plast, at, bt, kt, rt, v, plast)


def _mix_kernel(dft_ref, rev_ref, z_ref, yf_ref, yb_ref, bonus_ref, gate_ref, x_ref, mod_ref,
                gng_ref, gnb_ref, fg_ref, wout_ref, n2g_ref, ones_ref, x1_ref, h2_ref, zf_ref,
                *, fscale):
    seq = z_ref.shape[1]
    half = seq // 2
    rt = rev_ref.shape[0]

    @pl.when(pl.program_id(1) == 0)
    def _():
        mid = z_ref[0, half:half + 1, :FOURIER_W].astype(F32)
        row0 = lax.broadcasted_iota(jnp.int32, (rt, FOURIER_W), 0) == 0
        for part, sign in ((0, 1.0), (1, -1.0)):
            lanes = slice(part * FOURIER_W, (part + 1) * FOURIER_W)
            for r in range(half // rt):
                lo = seq - rt * (r + 1)
                nxt = (z_ref[0, lo + rt:lo + 2 * rt, lanes] if r > 0
                       else jnp.zeros((rt, FOURIER_W), BF16))
                rev = jnp.dot(rev_ref[...], jnp.concatenate([z_ref[0, lo:lo + rt, lanes], nxt], axis=0),
                              preferred_element_type=F32)
                folded = z_ref[0, r * rt:(r + 1) * rt, lanes].astype(F32) + sign * rev
                if part == 1 and r == 0:
                    folded = jnp.where(row0, mid, folded)
                zf_ref[part * half + r * rt:part * half + (r + 1) * rt, :] = folded.astype(BF16)

    yfour = jnp.dot(dft_ref[...], zf_ref[...], preferred_element_type=F32)
    four = _rms(yfour * fscale, fg_ref[...])

    ones_blk = ones_ref[...]
    y = yf_ref[0] + yb_ref[0]
    mu = _dot_exact_rhs(y, ones_blk) * (1.0 / HEAD)
    dy = y - mu
    var = _dot_exact_rhs(dy * dy, ones_blk) * (1.0 / HEAD)
    yn = dy * lax.rsqrt(var + GN_EPS) * gng_ref[...] + gnb_ref[...]
    rw = (yn + bonus_ref[0]) * gate_ref[0]

    out = _bdot(four, wout_ref[:FOURIER_W, :]) + _bdot(rw, wout_ref[FOURIER_W:, :])
    x1 = x_ref[0] + mod_ref[0, 2:3, :] * out
    x1_ref[0] = x1
    h2 = _rms(x1, n2g_ref[...]) * (1.0 + mod_ref[0, 4:5, :]) + mod_ref[0, 3:4, :]
    h2_ref[0] = h2.astype(BF16)


def _mix_call(dft, rev, z, yf, yb, bonus, gate, x, modx, gn_g, gn_b, f_g, w_out, norm2_g,
              ones_blk):
    b, s, d = x.shape
    tm = MIX_TILE

    def full(arr):
        nd = arr.ndim
        return pl.BlockSpec(arr.shape, lambda bi, m, _n=nd: (0,) * _n)

    tok = lambda w: pl.BlockSpec((1, tm, w), lambda bi, m: (bi, m, 0))
    in_specs = [
        pl.BlockSpec((tm, s), lambda bi, m: (m, 0)),
        full(rev),
        pl.BlockSpec((1, s, 2 * FOURIER_W), lambda bi, m: (bi, 0, 0)),
        tok(RWKV_W), tok(RWKV_W), tok(RWKV_W), tok(RWKV_W), tok(d),
        pl.BlockSpec((1, 6, d), lambda bi, m: (bi, 0, 0)),
        full(gn_g), full(gn_b), full(f_g), full(w_out), full(norm2_g), full(ones_blk),
    ]
    fscale = 1.0 / math.sqrt(s * (FOURIER_W // FOURIER_GROUPS))
    return pl.pallas_call(
        functools.partial(_mix_kernel, fscale=fscale),
        grid=(b, s // tm),
        in_specs=in_specs,
        out_specs=[tok(d), tok(d)],
        out_shape=[jax.ShapeDtypeStruct((b, s, d), F32), jax.ShapeDtypeStruct((b, s, d), BF16)],
        scratch_shapes=[pltpu.VMEM((s, FOURIER_W), BF16)],
        compiler_params=pltpu.CompilerParams(dimension_semantics=("arbitrary", "arbitrary"),
                                             vmem_limit_bytes=VMEM_LIMIT),
        name="mix",
    )(dft, rev, z, yf, yb, bonus, gate, x, modx, gn_g, gn_b, f_g, w_out, norm2_g, ones_blk)


def _ffn_up_kernel(h2_ref, wg_ref, wv_ref, cwg_ref, cwv_ref, cbg_ref, cbv_ref, act_ref):
    t = h2_ref.shape[1]
    rows = FFN_ROWS
    n_blk = t // rows
    w = jnp.concatenate([wg_ref[...], wv_ref[...]], axis=1)
    cw = jnp.concatenate([cwg_ref[...], cwv_ref[...]], axis=1).astype(BF16)
    cb = jnp.concatenate([cbg_ref[...], cbv_ref[...]], axis=1).astype(BF16)
    col = lax.broadcasted_iota(jnp.int32, (rows, 2 * FF_TILE), 0) & (GRID_W - 1)
    has_left = col > 0
    has_right = col < GRID_W - 1
    zeros = jnp.zeros((GRID_W, 2 * FF_TILE), BF16)

    def row_mixes(i):
        u = jnp.dot(h2_ref[0, i * rows:(i + 1) * rows, :], w, preferred_element_type=F32)
        u_l = jnp.where(has_left, pltpu.roll(u, 1, 0), 0.0).astype(BF16)
        u_r = jnp.where(has_right, pltpu.roll(u, rows - 1, 0), 0.0).astype(BF16)
        u_c = u.astype(BF16)
        return [cw[3 * kh:3 * kh + 1] * u_l + cw[3 * kh + 1:3 * kh + 2] * u_c
                + cw[3 * kh + 2:3 * kh + 3] * u_r for kh in range(3)]

    def finish(i, prev, cur, nxt):
        above = zeros if prev is None else prev[0][rows - GRID_W:]
        below = zeros if nxt is None else nxt[2][:GRID_W]
        c = (cur[1] + jnp.concatenate([above, cur[0][:rows - GRID_W]], axis=0)
             + jnp.concatenate([cur[2][GRID_W:], below], axis=0) + cb)
        cg = c[:, :FF_TILE]
        act_ref[0, i * rows:(i + 1) * rows, :] = cg * _sigmoid(cg) * c[:, FF_TILE:]

    mixes = [None] * (n_blk + 1)
    for i in range(n_blk):
        mixes[i] = row_mixes(i)
        if i >= 1:
            finish(i - 1, mixes[i - 2] if i >= 2 else None, mixes[i - 1], mixes[i])
    finish(n_blk - 1, mixes[n_blk - 2] if n_blk >= 2 else None, mixes[n_blk - 1], None)


def _ffn_up_call(h2, w_up, conv_w9, conv_b):
    b, s, d = h2.shape
    n_f = D_FF // FF_TILE
    in_specs = [
        pl.BlockSpec((1, s, d), lambda bi, f: (bi, 0, 0)),
        pl.BlockSpec((d, FF_TILE), lambda bi, f: (0, f)),
        pl.BlockSpec((d, FF_TILE), lambda bi, f: (0, n_f + f)),
        pl.BlockSpec((9, FF_TILE), lambda bi, f: (0, f)),
        pl.BlockSpec((9, FF_TILE), lambda bi, f: (0, n_f + f)),
        pl.BlockSpec((1, FF_TILE), lambda bi, f: (0, f)),
        pl.BlockSpec((1, FF_TILE), lambda bi, f: (0, n_f + f)),
    ]
    return pl.pallas_call(
        _ffn_up_kernel,
        grid=(b, n_f),
        in_specs=in_specs,
        out_specs=pl.BlockSpec((1, s, FF_TILE), lambda bi, f: (bi, 0, f)),
        out_shape=jax.ShapeDtypeStruct((b, s, D_FF), BF16),
        compiler_params=pltpu.CompilerParams(dimension_semantics=("arbitrary", "arbitrary"),
                                             vmem_limit_bytes=VMEM_LIMIT),
        name="ffn_up",
    )(h2, w_up, w_up, conv_w9, conv_w9, conv_b, conv_b)


def _ffn_down_kernel(act_ref, wd_ref, x1_ref, mod_ref, fg_ref, out_ref):
    y = jnp.dot(act_ref[0], wd_ref[...], preferred_element_type=F32)
    out_ref[0] = _rms(x1_ref[0] + mod_ref[0, 5:6, :] * y, fg_ref[...])


def _ffn_down_call(act, w_down, x1, modx, final_g):
    b, s, d = x1.shape
    tm = DOWN_TILE
    in_specs = [
        pl.BlockSpec((1, tm, D_FF), lambda bi, m: (bi, m, 0)),
        pl.BlockSpec((D_FF, d), lambda bi, m: (0, 0)),
        pl.BlockSpec((1, tm, d), lambda bi, m: (bi, m, 0)),
        pl.BlockSpec((1, 6, d), lambda bi, m: (bi, 0, 0)),
        pl.BlockSpec((1, d), lambda bi, m: (0, 0)),
    ]
    return pl.pallas_call(
        _ffn_down_kernel,
        grid=(b, s // tm),
        in_specs=in_specs,
        out_specs=pl.BlockSpec((1, tm, d), lambda bi, m: (bi, m, 0)),
        out_shape=jax.ShapeDtypeStruct((b, s, d), F32),
        compiler_params=pltpu.CompilerParams(dimension_semantics=("arbitrary", "arbitrary"),
                                             vmem_limit_bytes=VMEM_LIMIT),
        name="ffn_down",
    )(act, w_down, x1, modx, final_g)


@functools.lru_cache(maxsize=None)
def _constants(seq):
    gw = FOURIER_W // FOURIER_GROUPS
    nm = np.outer(np.arange(gw), np.arange(gw)) % gw
    ang = 2.0 * np.pi * nm / gw
    cs = np.zeros((FOURIER_W, 2 * FOURIER_W), np.float64)
    for gi in range(FOURIER_GROUPS):
        sl = slice(gi * gw, (gi + 1) * gw)
        cs[sl, sl] = np.cos(ang)
        cs[sl, FOURIER_W + gi * gw:FOURIER_W + (gi + 1) * gw] = np.sin(ang)
    half = seq // 2
    k_idx = np.arange(seq, dtype=np.int64)[:, None]
    ang_c = 2.0 * np.pi * ((k_idx * np.arange(0, half + 1, dtype=np.int64)[None, :]) % seq) / seq
    ang_s = 2.0 * np.pi * ((k_idx * np.arange(1, half, dtype=np.int64)[None, :]) % seq) / seq
    dft = np.concatenate([np.cos(ang_c), -np.sin(ang_s)], axis=1)
    ri = np.arange(REV_TILE)
    rev = (np.arange(2 * REV_TILE)[None, :] == (REV_TILE - ri)[:, None]).astype(np.float64)
    head_id = np.arange(RWKV_W) // HEAD
    ones_blk = (head_id[:, None] == head_id[None, :]).astype(np.float64)
    ti = np.arange(TILE)
    same_chunk = (ti[:, None] // CHUNK) == (ti[None, :] // CHUNK)
    tri = np.stack([same_chunk & (ti[None, :] <= ti[:, None]),
                    same_chunk & (ti[None, :] >= ti[:, None])]).astype(np.float64)
    to_bf16 = lambda arr: np.asarray(arr, dtype=np.float32).astype(BF16)
    return (np.asarray(cs, dtype=np.float32), np.asarray(dft, dtype=np.float32),
            to_bf16(ones_blk), to_bf16(tri), to_bf16(rev))


def _block_diag_dirs(w):
    z = jnp.zeros_like(w[0])
    return jnp.concatenate([jnp.concatenate([w[0], z], axis=1),
                            jnp.concatenate([z, w[1]], axis=1)], axis=0)


def kernel(x, c, ctx, c_ctx, ada_w, ada_b, norm1_g, norm2_g, w_in, rwkv_conv_w, decay_w0, decay_w2, iclr_a0, iclr_a2, gate_g2, k_k, k_a, r_k, gn_g, gn_b, fourier_g, w_out, ffn_w_up, ffn_conv_w, ffn_conv_b, ffn_w_down, final_g):
    b, s, d = x.shape
    assert ada_w.shape[0] == 1, "single-layer configuration"
    assert (b, s, d) == (c.shape[0], s, D_MODEL) and ctx.shape == (b, CTX_LEN, d)
    assert s % TILE == 0 and CTX_LEN == TILE
    cs, dft, ones_blk, tri, rev = (jnp.asarray(t) for t in _constants(s))
    cs, dft = cs.astype(BF16), dft.astype(BF16)

    cc = jnp.concatenate([c, c_ctx[None, :], jnp.zeros((16 - b - 1, d), F32)], axis=0)
    mod = _mod_call(cc, ada_w[0], ada_b[0][None, :])
    modx = mod[:b].reshape(b, 6, d)
    modc = jnp.broadcast_to(mod[b].reshape(1, 6, d)[:, :2], (b, 2, d))
    modsel = jnp.stack([modc, modx[:, :2]], axis=1)

    row = lambda t: t.reshape(1, -1)
    at, bt, kt, rt, v, plast, bonus, gate, z = _prep_call(
        x, ctx, modsel, row(norm1_g[0]), w_in[0].astype(BF16), rwkv_conv_w[0],
        row(decay_w0[0]), _block_diag_dirs(decay_w2[0]).astype(BF16),
        row(iclr_a0[0]), _block_diag_dirs(iclr_a2[0]).astype(BF16),
        gate_g2[0].astype(BF16), row(k_k[0]), row(k_a[0]), row(r_k[0]), cs, ones_blk, tri)

    yf, yb = _scan_call(at, bt, kt, rt, v, plast, s)

    x1, h2 = _mix_call(dft, rev, z, yf, yb, bonus, gate, x, modx, row(gn_g[0]), row(gn_b[0]),
                       row(fourier_g[0]), w_out[0].astype(BF16), row(norm2_g[0]), ones_blk)

    act = _ffn_up_call(h2, ffn_w_up[0].astype(BF16), ffn_conv_w[0].reshape(9, 2 * D_FF),
                       row(ffn_conv_b[0]))
    return _ffn_down_call(act, ffn_w_down[0].astype(BF16), x1, modx, row(final_g))
```

```python
import functools
import math

import numpy as np
import jax
import jax.numpy as jnp
from jax import lax
from jax.experimental import pallas as pl
from jax.experimental.pallas import tpu as pltpu

F32 = jnp.float32
BF16 = jnp.bfloat16

D_MODEL = 1024
CTX_LEN = 256
GRID_W = 64
FOURIER_W = 512
FOURIER_GROUPS = 8
RWKV_W = 512
HEAD = 64
N_HEADS = 8
DECAY_RANK = 64
ICLR_RANK = 64
GATE_RANK = 128
D_FF = 2816
NORM_EPS = 1e-6
GN_EPS = 64e-5
KK_EPS = 1e-12

CHUNK = 64
TILE = 256
MIX_TILE = 512
REV_TILE = 256
CHUNKS_PER_TILE = TILE // CHUNK
PREP_BATCH = 2
PAIR = 2 * HEAD
N_PAIRS = RWKV_W // PAIR
QUAD_HEADS = 4
SCAN_BATCH = 8
SCAN_GROUPS = 4
SCAN_LAG = 3
MOD_TILE = 1536
FF_TILE = 1408
FFN_ROWS = 256
DOWN_TILE = 1024
VMEM_LIMIT = 56 * 1024 * 1024
HEAD_SUM_TERMS = 1
CUMSUM_TERMS = 2
MOD_WEIGHT_TERMS = 2

RKV_LO = FOURIER_W
RKV_HI = FOURIER_W + 3 * RWKV_W
WD_LO = RKV_HI
AD_LO = WD_LO + 2 * DECAY_RANK
GD_LO = AD_LO + 2 * ICLR_RANK
PROJ_W = GD_LO + GATE_RANK


def _bdot(a, b):
    return jnp.dot(a.astype(BF16), b.astype(BF16), preferred_element_type=F32)


def _bdot_nt(a, b):
    return lax.dot_general(a.astype(BF16), b.astype(BF16), (((1,), (1,)), ((), ())),
                           preferred_element_type=F32)


def _bdot_tn(a, b):
    return lax.dot_general(a.astype(BF16), b.astype(BF16), (((0,), (0,)), ((), ())),
                           preferred_element_type=F32)


def _split_terms(a, terms):
    out, rem = [], a
    for _ in range(terms):
        hi = rem.astype(BF16)
        out.append(hi)
        rem = rem - hi.astype(F32)
    return out


def _dot_exact_rhs(a, b_exact, terms=HEAD_SUM_TERMS):
    acc = None
    for piece in _split_terms(a, terms):
        t = jnp.dot(piece, b_exact, preferred_element_type=F32)
        acc = t if acc is None else acc + t
    return acc


def _dot_exact_lhs(a_exact, b, terms=CUMSUM_TERMS):
    acc = None
    for piece in _split_terms(b, terms):
        t = jnp.dot(a_exact, piece, preferred_element_type=F32)
        acc = t if acc is None else acc + t
    return acc


def _sigmoid(x):
    return 1.0 / (1.0 + jnp.exp(-x))


def _rms(x, g):
    return x * lax.rsqrt(jnp.mean(x * x, axis=-1, keepdims=True) + NORM_EPS) * g


def _mod_kernel(c_ref, w_ref, b_ref, o_ref):
    cc = c_ref[...]
    s = cc * _sigmoid(cc)
    rows = s.shape[0]
    s_parts = jnp.concatenate(_split_terms(s, 3), axis=0)
    acc = b_ref[...]
    for wp in _split_terms(w_ref[...], MOD_WEIGHT_TERMS):
        t = jnp.dot(s_parts, wp, preferred_element_type=F32)
        acc = acc + (t[:rows] + t[rows:2 * rows] + t[2 * rows:])
    o_ref[...] = acc


def _mod_call(cc, ada_w, ada_b):
    rows, d = cc.shape
    n = ada_w.shape[1]
    tn = MOD_TILE
    return pl.pallas_call(
        _mod_kernel,
        grid=(n // tn,),
        in_specs=[pl.BlockSpec((rows, d), lambda i: (0, 0)),
                  pl.BlockSpec((d, tn), lambda i: (0, i)),
                  pl.BlockSpec((1, tn), lambda i: (0, i))],
        out_specs=pl.BlockSpec((rows, tn), lambda i: (0, i)),
        out_shape=jax.ShapeDtypeStruct((rows, n), F32),
        compiler_params=pltpu.CompilerParams(dimension_semantics=("arbitrary",),
                                             vmem_limit_bytes=VMEM_LIMIT),
        name="mod",
    )(cc, ada_w, ada_b)


def _prep_kernel(x_ref, xp_ref, xn_ref, ctx_ref, ms_ref, g1_ref, win_ref, cw_ref,
                 w0_ref, w2_ref, a0_ref, a2_ref, g2_ref, kk_ref, ka_ref, rk_ref,
                 cs_ref, ones_ref, tri_ref,
                 at_ref, bt_ref, kt_ref, rt_ref, v_ref, pl_ref, bonus_ref, gate_ref, z_ref):
    j = pl.program_id(1)
    refs = (x_ref, xp_ref, xn_ref, ctx_ref, ms_ref, g1_ref, win_ref, cw_ref,
            w0_ref, w2_ref, a0_ref, a2_ref, kk_ref, ka_ref, ones_ref, tri_ref,
            at_ref, bt_ref, kt_ref, rt_ref, v_ref, pl_ref)
    tiles = _run_interleaved([_prep_tile(bi, *refs) for bi in range(x_ref.shape[0])])

    @pl.when(j >= 1)
    def _():
        ones_blk = ones_ref[...]
        for bi, (u_four, r, v, gd, kd_sum) in enumerate(tiles):
            bs = _dot_exact_rhs(r * rk_ref[...] * kd_sum, ones_blk)
            bonus_ref[bi] = bs * v
            gate_ref[bi] = _bdot(_sigmoid(gd), g2_ref[...])
            z_ref[bi] = _bdot(u_four, cs_ref[...]).astype(BF16)


def _prep_tile(bi, x_ref, xp_ref, xn_ref, ctx_ref, ms_ref, g1_ref, win_ref, cw_ref,
               w0_ref, w2_ref, a0_ref, a2_ref, kk_ref, ka_ref, ones_ref, tri_ref,
               at_ref, bt_ref, kt_ref, rt_ref, v_ref, pl_ref):
    j = pl.program_id(1)
    n_j = pl.num_programs(1)
    is_ctx = j == 0
    shift = ms_ref[bi, 0, 0:1, :]
    scale = ms_ref[bi, 0, 1:2, :]
    g = g1_ref[...]

    def norm_mod(xr):
        return _rms(xr, g) * (1.0 + scale) + shift

    xin = jnp.where(is_ctx, ctx_ref[bi], x_ref[bi])
    h = norm_mod(xin).astype(BF16)
    u = jnp.dot(h, win_ref[...], preferred_element_type=F32)

    halo = jnp.concatenate([xp_ref[bi], xn_ref[bi]], axis=0)
    uh = jnp.dot(norm_mod(halo).astype(BF16), win_ref[:, RKV_LO:RKV_HI],
                 preferred_element_type=F32)
    yield
    prev_row = jnp.where(j >= 2, uh[7:8], 0.0)
    next_row = jnp.where(jnp.logical_and(j >= 1, j < n_j - 1), uh[8:9], 0.0)

    rkv = u[:, RKV_LO:RKV_HI]
    row = lax.broadcasted_iota(jnp.int32, rkv.shape, 0)
    u_m1 = jnp.where(row == 0, prev_row, pltpu.roll(rkv, 1, 0))
    u_p1 = jnp.where(row == TILE - 1, next_row, pltpu.roll(rkv, TILE - 1, 0))
    rkv = cw_ref[0:1, :] * u_m1 + cw_ref[1:2, :] * rkv + cw_ref[2:3, :] * u_p1
    r = rkv[:, 0:RWKV_W]
    k = rkv[:, RWKV_W:2 * RWKV_W]
    v = rkv[:, 2 * RWKV_W:3 * RWKV_W]

    wd = u[:, WD_LO:AD_LO]
    ad = u[:, AD_LO:GD_LO]
    gd = u[:, GD_LO:PROJ_W]
    w_lora = w0_ref[...] + _bdot(jnp.tanh(wd), w2_ref[...])
    lw = -math.exp(-0.5) * _sigmoid(w_lora)
    a = _sigmoid(a0_ref[...] + _bdot(ad, a2_ref[...]))

    ones_blk = ones_ref[...]
    kraw = k * kk_ref[...]
    kk = kraw * lax.rsqrt(_dot_exact_rhs(kraw * kraw, ones_blk) + KK_EPS)
    ka = ka_ref[...]
    kd = [k * (1.0 + (a[:, d * RWKV_W:(d + 1) * RWKV_W] - 1.0) * ka) for d in range(2)]

    v_ref[bi] = v.astype(BF16)
    for d in range(2):
        lwd = lw[:, d * RWKV_W:(d + 1) * RWKV_W]
        ad_ = a[:, d * RWKV_W:(d + 1) * RWKV_W]
        c = _dot_exact_lhs(tri_ref[d], lwd)
        e_pos = jnp.exp(c)
        e_neg = jnp.exp(-c)
        e_prev = jnp.exp(c - lwd)
        at_ref[bi, d] = (kk * e_prev).astype(BF16)
        bt_ref[bi, d] = (kk * ad_ * e_neg).astype(BF16)
        kt_ref[bi, d] = (kd[d] * e_neg).astype(BF16)
        rt_ref[bi, d] = (r * e_pos).astype(BF16)
        for q in range(CHUNKS_PER_TILE):
            last = q * CHUNK + (CHUNK - 1 if d == 0 else 0)
            pl_ref[bi, d, q] = e_pos[last:last + 1, :]
    return u[:, 0:FOURIER_W], r, v, gd, kd[0] + kd[1]


def _prep_call(x, ctx, modsel, norm1_g, w_in, conv_w, w0, w2bd, a0, a2bd, g2, k_k, k_a, r_k,
               cs, ones_blk, tri):
    b, s, d = x.shape
    n_j = 1 + s // TILE
    n_chunks = (CTX_LEN + s) // CHUNK
    t_all = CTX_LEN + s
    c2 = 2 * RWKV_W

    def full(arr):
        nd = arr.ndim
        return pl.BlockSpec(arr.shape, lambda bi, j, _n=nd: (0,) * _n)

    rows8 = TILE // 8
    nb = PREP_BATCH
    in_specs = [
        pl.BlockSpec((nb, TILE, d), lambda bi, j: (bi, jnp.maximum(j - 1, 0), 0)),
        pl.BlockSpec((nb, 8, d), lambda bi, j: (bi, jnp.maximum((j - 1) * rows8 - 1, 0), 0)),
        pl.BlockSpec((nb, 8, d), lambda bi, j: (bi, jnp.minimum(j * rows8, s // 8 - 1), 0)),
        pl.BlockSpec((nb, CTX_LEN, d), lambda bi, j: (bi, 0, 0)),
        pl.BlockSpec((nb, 1, 2, d), lambda bi, j: (bi, jnp.minimum(j, 1), 0, 0)),
        full(norm1_g), full(w_in), full(conv_w), full(w0), full(w2bd), full(a0), full(a2bd),
        full(g2), full(k_k), full(k_a), full(r_k), full(cs), full(ones_blk), full(tri),
    ]
    lat = lambda bi, j: (bi, jnp.maximum(j - 1, 0), 0)
    out_specs = [
        pl.BlockSpec((nb, 2, TILE, RWKV_W), lambda bi, j: (bi, 0, j, 0)),
        pl.BlockSpec((nb, 2, TILE, RWKV_W), lambda bi, j: (bi, 0, j, 0)),
        pl.BlockSpec((nb, 2, TILE, RWKV_W), lambda bi, j: (bi, 0, j, 0)),
        pl.BlockSpec((nb, 2, TILE, RWKV_W), lambda bi, j: (bi, 0, j, 0)),
        pl.BlockSpec((nb, TILE, RWKV_W), lambda bi, j: (bi, j, 0)),
        pl.BlockSpec((nb, 2, CHUNKS_PER_TILE, 1, RWKV_W), lambda bi, j: (bi, 0, j, 0, 0)),
        pl.BlockSpec((nb, TILE, RWKV_W), lat),
        pl.BlockSpec((nb, TILE, RWKV_W), lat),
        pl.BlockSpec((nb, TILE, 2 * FOURIER_W), lat),
    ]
    out_shape = [
        jax.ShapeDtypeStruct((b, 2, t_all, RWKV_W), BF16),
        jax.ShapeDtypeStruct((b, 2, t_all, RWKV_W), BF16),
        jax.ShapeDtypeStruct((b, 2, t_all, RWKV_W), BF16),
        jax.ShapeDtypeStruct((b, 2, t_all, RWKV_W), BF16),
        jax.ShapeDtypeStruct((b, t_all, RWKV_W), BF16),
        jax.ShapeDtypeStruct((b, 2, n_chunks, 1, RWKV_W), F32),
        jax.ShapeDtypeStruct((b, s, RWKV_W), F32),
        jax.ShapeDtypeStruct((b, s, RWKV_W), F32),
        jax.ShapeDtypeStruct((b, s, 2 * FOURIER_W), BF16),
    ]
    return pl.pallas_call(
        _prep_kernel,
        grid=(b // nb, n_j),
        in_specs=in_specs,
        out_specs=out_specs,
        out_shape=out_shape,
        compiler_params=pltpu.CompilerParams(dimension_semantics=("arbitrary", "arbitrary"),
                                             vmem_limit_bytes=VMEM_LIMIT),
        name="prep",
    )(x, x, x, ctx, modsel, norm1_g, w_in, conv_w, w0, w2bd, a0, a2bd, g2, k_k, k_a, r_k,
      cs, ones_blk, tri)


def _quad_chunk(a, bm, km, r, v, plrow, g_states, strict, incl, eye, lane_lo, mask_bd, eye_row,
                level_masks):
    n = PAIR

    def dup(xv):
        z = jnp.zeros_like(xv)
        return jnp.concatenate([jnp.where(lane_lo, xv, z), jnp.where(lane_lo, z, xv)], axis=0)

    def to_row(x2):
        return x2[:CHUNK] + x2[CHUNK:]

    def bdiag(xr):
        return jnp.where(mask_bd, jnp.concatenate([xr] * QUAD_HEADS, axis=0), jnp.zeros((), xr.dtype))

    pairs = []
    for p in range(2):
        sl = slice(p * n, (p + 1) * n)
        a2, b2, k2, r2, v2 = (dup(t[:, sl]) for t in (a, bm, km, r, v))
        rhs = jnp.concatenate([b2, k2], axis=0)
        s = _bdot_nt(jnp.concatenate([a2, r2], axis=0), rhs)
        pairs.append((sl, r2, v2, rhs, s))
    yield

    a_ab = jnp.concatenate([to_row(jnp.where(strict, s[:n, :n], 0.0)) for *_, s in pairs], axis=1)
    a_ak = jnp.concatenate([to_row(jnp.where(strict, s[:n, n:], 0.0)) for *_, s in pairs], axis=1)
    m_rbk = [jnp.concatenate([jnp.where(incl, s[n:, :n], 0.0), jnp.where(incl, s[n:, n:], 0.0)],
                             axis=1).astype(BF16) for *_, s in pairs]

    av = _bdot(a_ak, bdiag(v))
    a_lvl = [jnp.where(m, a_ab, 0.0).astype(BF16) for m in level_masks]
    t_row = jnp.where(eye_row, 1.0, 0.0) - jnp.where(level_masks[0], a_ab, 0.0)
    lvl = 1
    while lvl < len(a_lvl):
        paired = lvl + 1 < len(a_lvl)
        lhs = jnp.concatenate([a_lvl[lvl], a_lvl[lvl + 1]], axis=0) if paired else a_lvl[lvl]
        yc = jnp.dot(lhs, bdiag(t_row.astype(BF16)), preferred_element_type=F32)
        yield
        y_bd = bdiag(yc[:CHUNK].astype(BF16))
        if paired:
            both = jnp.dot(jnp.concatenate([t_row, yc[CHUNK:]], axis=0).astype(BF16), y_bd,
                           preferred_element_type=F32)
            yield
            t_row = t_row - both[:CHUNK]
            y_next = yc[CHUNK:] - both[CHUNK:]
            t_row = t_row - _bdot(t_row, bdiag(y_next.astype(BF16)))
            lvl += 2
        else:
            t_row = t_row - _bdot(t_row, y_bd)
            lvl += 1
        yield
    tx = _bdot(t_row, jnp.concatenate([bdiag(a), bdiag(av.astype(BF16))], axis=1))
    yield
    a_w = tx[:, :2 * n].astype(BF16)
    u_v = (-tx[:, 2 * n:]).astype(BF16)
    stage5 = []
    for (sl, r2, v2, rhs, _), m2 in zip(pairs, m_rbk):
        zmat = jnp.concatenate(
            [jnp.concatenate([dup(a_w[:, sl]), dup(u_v[:, sl])], axis=1),
             jnp.concatenate([jnp.zeros_like(v2), v2], axis=1)], axis=0)
        rhs_end = rhs.astype(F32) * plrow[:, sl]
        lhs = jnp.concatenate([rhs_end.T.astype(BF16), m2], axis=0)
        stage5.append(jnp.dot(lhs, zmat, preferred_element_type=F32))
    yield
    ys, g_new = [], []
    for (sl, r2, v2, rhs, _), o5, g_state in zip(pairs, stage5, g_states):
        phi = jnp.where(eye, plrow[:, sl], 0.0) - o5[:n, :n]
        r_y = r2.astype(F32) - o5[n:, :n]
        o6 = _bdot(jnp.concatenate([r_y, phi], axis=0), g_state)
        ys.append(to_row(o6[:n] + o5[n:, n:]))
        g_new.append(o6[n:] + o5[:n, n:])
    return jnp.concatenate(ys, axis=1), g_new


def _run_interleaved(gens, groups=1, lag=0):
    results = [None] * len(gens)
    active = list(range(len(gens)))
    rnd = 0
    while active:
        for i in list(active):
            if rnd < (i % groups) * lag:
                continue
            try:
                next(gens[i])
            except StopIteration as stop:
                results[i] = stop.value
                active.remove(i)
        rnd += 1
    return results


def _scan_kernel(atf, btf, ktf, rtf, vf, plf, atb, btb, ktb, rtb, vb, plb,
                 yf_ref, yb_ref, g_ref):
    s = pl.program_id(1)

    @pl.when(s == 0)
    def _():
        g_ref[...] = jnp.zeros_like(g_ref)

    ri = lax.broadcasted_iota(jnp.int32, (PAIR, PAIR), 0)
    ci = lax.broadcasted_iota(jnp.int32, (PAIR, PAIR), 1)
    same = (ri >= HEAD) == (ci >= HEAD)
    ti = ri & (HEAD - 1)
    tj = ci & (HEAD - 1)
    eye = ri == ci
    lane_lo = lax.broadcasted_iota(jnp.int32, (CHUNK, PAIR), 1) < HEAD
    masks = [(same & (tj < ti), same & (tj <= ti)),
             (same & (tj > ti), same & (tj >= ti))]
    quad_w = QUAD_HEADS * HEAD
    rq = lax.broadcasted_iota(jnp.int32, (quad_w, quad_w), 0)
    cq = lax.broadcasted_iota(jnp.int32, (quad_w, quad_w), 1)
    mask_bd = (rq // HEAD) == (cq // HEAD)
    t_i = lax.broadcasted_iota(jnp.int32, (CHUNK, quad_w), 0)
    s_i = lax.broadcasted_iota(jnp.int32, (CHUNK, quad_w), 1) & (HEAD - 1)
    eye_row = s_i == t_i
    level_masks = []
    for d in range(2):
        late, early = (t_i, s_i) if d == 0 else (s_i, t_i)
        level_masks.append([
            ((t_i >> (lvl + 1)) == (s_i >> (lvl + 1)))
            & (((late >> lvl) & 1) == 1) & (((early >> lvl) & 1) == 0)
            for lvl in range(int(math.log2(CHUNK)))])
    refs = [(atf, btf, ktf, rtf, vf, plf, yf_ref), (atb, btb, ktb, rtb, vb, plb, yb_ref)]
    gens, dests = [], []
    for bi in range(yf_ref.shape[0]):
        for d in range(2):
            at, bt, kt, rt, vv, plr, y_ref = refs[d]
            strict, incl = masks[d]
            for q in range(RWKV_W // quad_w):
                sl = slice(q * quad_w, (q + 1) * quad_w)
                gens.append(_quad_chunk(at[bi, 0, :, sl], bt[bi, 0, :, sl], kt[bi, 0, :, sl],
                                        rt[bi, 0, :, sl], vv[bi, :, sl], plr[bi, 0, 0, :, sl],
                                        [g_ref[bi, d, 2 * q], g_ref[bi, d, 2 * q + 1]],
                                        strict, incl, eye, lane_lo, mask_bd, eye_row,
                                        level_masks[d]))
                dests.append((y_ref, bi, d, q, sl))
    done = _run_interleaved(gens, groups=SCAN_GROUPS, lag=SCAN_LAG)
    for (y, g_new), (y_ref, bi, d, q, sl) in zip(done, dests):
        g_ref[bi, d, 2 * q] = g_new[0]
        g_ref[bi, d, 2 * q + 1] = g_new[1]
        y_ref[bi, :, sl] = y


def _scan_call(at, bt, kt, rt, v, plast, s_lat):
    b = at.shape[0]
    n_chunks = at.shape[2] // CHUNK
    n_ctx = CTX_LEN // CHUNK
    n_lat = s_lat // CHUNK

    def fwd_c(s):
        return s

    def bwd_c(s):
        return jnp.where(s < n_ctx, n_ctx - 1 - s, n_chunks + n_ctx - 1 - s)

    nb = SCAN_BATCH

    def dir_spec(d, cfun):
        return pl.BlockSpec((nb, 1, CHUNK, RWKV_W), lambda bi, s: (bi, d, cfun(s), 0))

    def v_spec(cfun):
        return pl.BlockSpec((nb, CHUNK, RWKV_W), lambda bi, s: (bi, cfun(s), 0))

    def pl_spec(d, cfun):
        return pl.BlockSpec((nb, 1, 1, 1, RWKV_W), lambda bi, s: (bi, d, cfun(s), 0, 0))

    in_specs = ([dir_spec(0, fwd_c)] * 4 + [v_spec(fwd_c), pl_spec(0, fwd_c)]
                + [dir_spec(1, bwd_c)] * 4 + [v_spec(bwd_c), pl_spec(1, bwd_c)])
    out_specs = [
        pl.BlockSpec((nb, CHUNK, RWKV_W), lambda bi, s: (bi, jnp.maximum(s - n_ctx, 0), 0)),
        pl.BlockSpec((nb, CHUNK, RWKV_W),
                     lambda bi, s: (bi, jnp.minimum(n_chunks - 1 - s, n_lat - 1), 0)),
    ]
    out_shape = [jax.ShapeDtypeStruct((b, s_lat, RWKV_W), F32)] * 2
    return pl.pallas_call(
        _scan_kernel,
        grid=(b // nb, n_chunks),
        in_specs=in_specs,
        out_specs=out_specs,
        out_shape=out_shape,
        scratch_shapes=[pltpu.VMEM((nb, 2, N_PAIRS, PAIR, PAIR), F32)],
        compiler_params=pltpu.CompilerParams(dimension_semantics=("arbitrary", "arbitrary"),
                                             vmem_limit_bytes=VMEM_LIMIT),
        name="scan",
    )(at, bt, kt, rt, v, plast, at, bt, kt, rt, v, plast)


def _mix_kernel(dft_ref, rev_ref, z_ref, yf_ref, yb_ref, bonus_ref, gate_ref, x_ref, mod_ref,
                gng_ref, gnb_ref, fg_ref, wout_ref, n2g_ref, ones_ref, x1_ref, h2_ref, zf_ref,
                *, fscale):
    seq = z_ref.shape[1]
    half = seq // 2
    rt = rev_ref.shape[0]

    @pl.when(pl.program_id(1) == 0)
    def _():
        mid = z_ref[0, half:half + 1, :FOURIER_W].astype(F32)
        row0 = lax.broadcasted_iota(jnp.int32, (rt, FOURIER_W), 0) == 0
        for part, sign in ((0, 1.0), (1, -1.0)):
            lanes = slice(part * FOURIER_W, (part + 1) * FOURIER_W)
            for r in range(half // rt):
                lo = seq - rt * (r + 1)
                nxt = (z_ref[0, lo + rt:lo + 2 * rt, lanes] if r > 0
                       else jnp.zeros((rt, FOURIER_W), BF16))
                rev = jnp.dot(rev_ref[...], jnp.concatenate([z_ref[0, lo:lo + rt, lanes], nxt], axis=0),
                              preferred_element_type=F32)
                folded = z_ref[0, r * rt:(r + 1) * rt, lanes].astype(F32) + sign * rev
                if part == 1 and r == 0:
                    folded = jnp.where(row0, mid, folded)
                zf_ref[part * half + r * rt:part * half + (r + 1) * rt, :] = folded.astype(BF16)

    yfour = jnp.dot(dft_ref[...], zf_ref[...], preferred_element_type=F32)
    four = _rms(yfour * fscale, fg_ref[...])

    ones_blk = ones_ref[...]
    y = yf_ref[0] + yb_ref[0]
    mu = _dot_exact_rhs(y, ones_blk) * (1.0 / HEAD)
    dy = y - mu
    var = _dot_exact_rhs(dy * dy, ones_blk) * (1.0 / HEAD)
    yn = dy * lax.rsqrt(var + GN_EPS) * gng_ref[...] + gnb_ref[...]
    rw = (yn + bonus_ref[0]) * gate_ref[0]

    out = _bdot(four, wout_ref[:FOURIER_W, :]) + _bdot(rw, wout_ref[FOURIER_W:, :])
    x1 = x_ref[0] + mod_ref[0, 2:3, :] * out
    x1_ref[0] = x1
    h2 = _rms(x1, n2g_ref[...]) * (1.0 + mod_ref[0, 4:5, :]) + mod_ref[0, 3:4, :]
    h2_ref[0] = h2.astype(BF16)


def _mix_call(dft, rev, z, yf, yb, bonus, gate, x, modx, gn_g, gn_b, f_g, w_out, norm2_g,
              ones_blk):
    b, s, d = x.shape
    tm = MIX_TILE

    def full(arr):
        nd = arr.ndim
        return pl.BlockSpec(arr.shape, lambda bi, m, _n=nd: (0,) * _n)

    tok = lambda w: pl.BlockSpec((1, tm, w), lambda bi, m: (bi, m, 0))
    in_specs = [
        pl.BlockSpec((tm, s), lambda bi, m: (m, 0)),
        full(rev),
        pl.BlockSpec((1, s, 2 * FOURIER_W), lambda bi, m: (bi, 0, 0)),
        tok(RWKV_W), tok(RWKV_W), tok(RWKV_W), tok(RWKV_W), tok(d),
        pl.BlockSpec((1, 6, d), lambda bi, m: (bi, 0, 0)),
        full(gn_g), full(gn_b), full(f_g), full(w_out), full(norm2_g), full(ones_blk),
    ]
    fscale = 1.0 / math.sqrt(s * (FOURIER_W // FOURIER_GROUPS))
    return pl.pallas_call(
        functools.partial(_mix_kernel, fscale=fscale),
        grid=(b, s // tm),
        in_specs=in_specs,
        out_specs=[tok(d), tok(d)],
        out_shape=[jax.ShapeDtypeStruct((b, s, d), F32), jax.ShapeDtypeStruct((b, s, d), BF16)],
        scratch_shapes=[pltpu.VMEM((s, FOURIER_W), BF16)],
        compiler_params=pltpu.CompilerParams(dimension_semantics=("arbitrary", "arbitrary"),
                                             vmem_limit_bytes=VMEM_LIMIT),
        name="mix",
    )(dft, rev, z, yf, yb, bonus, gate, x, modx, gn_g, gn_b, f_g, w_out, norm2_g, ones_blk)


def _ffn_up_kernel(h2_ref, wg_ref, wv_ref, cwg_ref, cwv_ref, cbg_ref, cbv_ref, act_ref):
    t = h2_ref.shape[1]
    rows = FFN_ROWS
    n_blk = t // rows
    w = jnp.concatenate([wg_ref[...], wv_ref[...]], axis=1)
    cw = jnp.concatenate([cwg_ref[...], cwv_ref[...]], axis=1).astype(BF16)
    cb = jnp.concatenate([cbg_ref[...], cbv_ref[...]], axis=1).astype(BF16)
    col = lax.broadcasted_iota(jnp.int32, (rows, 2 * FF_TILE), 0) & (GRID_W - 1)
    has_left = col > 0
    has_right = col < GRID_W - 1
    zeros = jnp.zeros((GRID_W, 2 * FF_TILE), BF16)

    def row_mixes(i):
        u = jnp.dot(h2_ref[0, i * rows:(i + 1) * rows, :], w, preferred_element_type=F32)
        u_l = jnp.where(has_left, pltpu.roll(u, 1, 0), 0.0).astype(BF16)
        u_r = jnp.where(has_right, pltpu.roll(u, rows - 1, 0), 0.0).astype(BF16)
        u_c = u.astype(BF16)
        return [cw[3 * kh:3 * kh + 1] * u_l + cw[3 * kh + 1:3 * kh + 2] * u_c
                + cw[3 * kh + 2:3 * kh + 3] * u_r for kh in range(3)]

    def finish(i, prev, cur, nxt):
        above = zeros if prev is None else prev[0][rows - GRID_W:]
        below = zeros if nxt is None else nxt[2][:GRID_W]
        c = (cur[1] + jnp.concatenate([above, cur[0][:rows - GRID_W]], axis=0)
             + jnp.concatenate([cur[2][GRID_W:], below], axis=0) + cb)
        cg = c[:, :FF_TILE]
        act_ref[0, i * rows:(i + 1) * rows, :] = cg * _sigmoid(cg) * c[:, FF_TILE:]

    mixes = [None] * (n_blk + 1)
    for i in range(n_blk):
        mixes[i] = row_mixes(i)
        if i >= 1:
            finish(i - 1, mixes[i - 2] if i >= 2 else None, mixes[i - 1], mixes[i])
    finish(n_blk - 1, mixes[n_blk - 2] if n_blk >= 2 else None, mixes[n_blk - 1], None)


def _ffn_up_call(h2, w_up, conv_w9, conv_b):
    b, s, d = h2.shape
    n_f = D_FF // FF_TILE
    in_specs = [
        pl.BlockSpec((1, s, d), lambda bi, f: (bi, 0, 0)),
        pl.BlockSpec((d, FF_TILE), lambda bi, f: (0, f)),
        pl.BlockSpec((d, FF_TILE), lambda bi, f: (0, n_f + f)),
        pl.BlockSpec((9, FF_TILE), lambda bi, f: (0, f)),
        pl.BlockSpec((9, FF_TILE), lambda bi, f: (0, n_f + f)),
        pl.BlockSpec((1, FF_TILE), lambda bi, f: (0, f)),
        pl.BlockSpec((1, FF_TILE), lambda bi, f: (0, n_f + f)),
    ]
    return pl.pallas_call(
        _ffn_up_kernel,
        grid=(b, n_f),
        in_specs=in_specs,
        out_specs=pl.BlockSpec((1, s, FF_TILE), lambda bi, f: (bi, 0, f)),
        out_shape=jax.ShapeDtypeStruct((b, s, D_FF), BF16),
        compiler_params=pltpu.CompilerParams(dimension_semantics=("arbitrary", "arbitrary"),
                                             vmem_limit_bytes=VMEM_LIMIT),
        name="ffn_up",
    )(h2, w_up, w_up, conv_w9, conv_w9, conv_b, conv_b)


def _ffn_down_kernel(act_ref, wd_ref, x1_ref, mod_ref, fg_ref, out_ref):
    y = jnp.dot(act_ref[0], wd_ref[...], preferred_element_type=F32)
    out_ref[0] = _rms(x1_ref[0] + mod_ref[0, 5:6, :] * y, fg_ref[...])


def _ffn_down_call(act, w_down, x1, modx, final_g):
    b, s, d = x1.shape
    tm = DOWN_TILE
    in_specs = [
        pl.BlockSpec((1, tm, D_FF), lambda bi, m: (bi, m, 0)),
        pl.BlockSpec((D_FF, d), lambda bi, m: (0, 0)),
        pl.BlockSpec((1, tm, d), lambda bi, m: (bi, m, 0)),
        pl.BlockSpec((1, 6, d), lambda bi, m: (bi, 0, 0)),
        pl.BlockSpec((1, d), lambda bi, m: (0, 0)),
    ]
    return pl.pallas_call(
        _ffn_down_kernel,
        grid=(b, s // tm),
        in_specs=in_specs,
        out_specs=pl.BlockSpec((1, tm, d), lambda bi, m: (bi, m, 0)),
        out_shape=jax.ShapeDtypeStruct((b, s, d), F32),
        compiler_params=pltpu.CompilerParams(dimension_semantics=("arbitrary", "arbitrary"),
                                             vmem_limit_bytes=VMEM_LIMIT),
        name="ffn_down",
    )(act, w_down, x1, modx, final_g)


@functools.lru_cache(maxsize=None)
def _constants(seq):
    gw = FOURIER_W // FOURIER_GROUPS
    nm = np.outer(np.arange(gw), np.arange(gw)) % gw
    ang = 2.0 * np.pi * nm / gw
    cs = np.zeros((FOURIER_W, 2 * FOURIER_W), np.float64)
    for gi in range(FOURIER_GROUPS):
        sl = slice(gi * gw, (gi + 1) * gw)
        cs[sl, sl] = np.cos(ang)
        cs[sl, FOURIER_W + gi * gw:FOURIER_W + (gi + 1) * gw] = np.sin(ang)
    half = seq // 2
    k_idx = np.arange(seq, dtype=np.int64)[:, None]
    ang_c = 2.0 * np.pi * ((k_idx * np.arange(0, half + 1, dtype=np.int64)[None, :]) % seq) / seq
    ang_s = 2.0 * np.pi * ((k_idx * np.arange(1, half, dtype=np.int64)[None, :]) % seq) / seq
    dft = np.concatenate([np.cos(ang_c), -np.sin(ang_s)], axis=1)
    ri = np.arange(REV_TILE)
    rev = (np.arange(2 * REV_TILE)[None, :] == (REV_TILE - ri)[:, None]).astype(np.float64)
    head_id = np.arange(RWKV_W) // HEAD
    ones_blk = (head_id[:, None] == head_id[None, :]).astype(np.float64)
    ti = np.arange(TILE)
    same_chunk = (ti[:, None] // CHUNK) == (ti[None, :] // CHUNK)
    tri = np.stack([same_chunk & (ti[None, :] <= ti[:, None]),
                    same_chunk & (ti[None, :] >= ti[:, None])]).astype(np.float64)
    to_bf16 = lambda arr: np.asarray(arr, dtype=np.float32).astype(BF16)
    return (np.asarray(cs, dtype=np.float32), np.asarray(dft, dtype=np.float32),
            to_bf16(ones_blk), to_bf16(tri), to_bf16(rev))


def _block_diag_dirs(w):
    z = jnp.zeros_like(w[0])
    return jnp.concatenate([jnp.concatenate([w[0], z], axis=1),
                            jnp.concatenate([z, w[1]], axis=1)], axis=0)


def kernel(x, c, ctx, c_ctx, ada_w, ada_b, norm1_g, norm2_g, w_in, rwkv_conv_w, decay_w0, decay_w2, iclr_a0, iclr_a2, gate_g2, k_k, k_a, r_k, gn_g, gn_b, fourier_g, w_out, ffn_w_up, ffn_conv_w, ffn_conv_b, ffn_w_down, final_g):
    b, s, d = x.shape
    assert ada_w.shape[0] == 1, "single-layer configuration"
    assert (b, s, d) == (c.shape[0], s, D_MODEL) and ctx.shape == (b, CTX_LEN, d)
    assert s % TILE == 0 and CTX_LEN == TILE
    cs, dft, ones_blk, tri, rev = (jnp.asarray(t) for t in _constants(s))
    cs, dft = cs.astype(BF16), dft.astype(BF16)

    cc = jnp.concatenate([c, c_ctx[None, :], jnp.zeros((16 - b - 1, d), F32)], axis=0)
    mod = _mod_call(cc, ada_w[0], ada_b[0][None, :])
    modx = mod[:b].reshape(b, 6, d)
    modc = jnp.broadcast_to(mod[b].reshape(1, 6, d)[:, :2], (b, 2, d))
    modsel = jnp.stack([modc, modx[:, :2]], axis=1)

    row = lambda t: t.reshape(1, -1)
    at, bt, kt, rt, v, plast, bonus, gate, z = _prep_call(
        x, ctx, modsel, row(norm1_g[0]), w_in[0].astype(BF16), rwkv_conv_w[0],
        row(decay_w0[0]), _block_diag_dirs(decay_w2[0]).astype(BF16),
        row(iclr_a0[0]), _block_diag_dirs(iclr_a2[0]).astype(BF16),
        gate_g2[0].astype(BF16), row(k_k[0]), row(k_a[0]), row(r_k[0]), cs, ones_blk, tri)

    yf, yb = _scan_call(at, bt, kt, rt, v, plast, s)

    x1, h2 = _mix_call(dft, rev, z, yf, yb, bonus, gate, x, modx, row(gn_g[0]), row(gn_b[0]),
                       row(fourier_g[0]), w_out[0].astype(BF16), row(norm2_g[0]), ones_blk)

    act = _ffn_up_call(h2, ffn_w_up[0].astype(BF16), ffn_conv_w[0].reshape(9, 2 * D_FF),
                       row(ffn_conv_b[0]))
    return _ffn_down_call(act, ffn_w_down[0].astype(BF16), x1, modx, row(final_g))
```

```python
import functools
import math

import numpy as np
import jax
import jax.numpy as jnp
from jax import lax
from jax.experimental import pallas as pl
from jax.experimental.pallas import tpu as pltpu

F32 = jnp.float32
BF16 = jnp.bfloat16

D_MODEL = 1024
CTX_LEN = 256
GRID_W = 64
FOURIER_W = 512
FOURIER_GROUPS = 8
RWKV_W = 512
HEAD = 64
N_HEADS = 8
DECAY_RANK = 64
ICLR_RANK = 64
GATE_RANK = 128
D_FF = 2816
NORM_EPS = 1e-6
GN_EPS = 64e-5
KK_EPS = 1e-12

CHUNK = 64
TILE = 256
MIX_TILE = 512
REV_TILE = 256
CHUNKS_PER_TILE = TILE // CHUNK
PREP_BATCH = 2
PAIR = 2 * HEAD
N_PAIRS = RWKV_W // PAIR
QUAD_HEADS = 4
SCAN_BATCH = 8
SCAN_GROUPS = 8
SCAN_LAG = 2
MOD_TILE = 1536
FF_TILE = 1408
FFN_ROWS = 256
DOWN_TILE = 1024
VMEM_LIMIT = 56 * 1024 * 1024
HEAD_SUM_TERMS = 1
CUMSUM_TERMS = 2
MOD_WEIGHT_TERMS = 2

RKV_LO = FOURIER_W
RKV_HI = FOURIER_W + 3 * RWKV_W
WD_LO = RKV_HI
AD_LO = WD_LO + 2 * DECAY_RANK
GD_LO = AD_LO + 2 * ICLR_RANK
PROJ_W = GD_LO + GATE_RANK


def _bdot(a, b):
    return jnp.dot(a.astype(BF16), b.astype(BF16), preferred_element_type=F32)


def _bdot_nt(a, b):
    return lax.dot_general(a.astype(BF16), b.astype(BF16), (((1,), (1,)), ((), ())),
                           preferred_element_type=F32)


def _bdot_tn(a, b):
    return lax.dot_general(a.astype(BF16), b.astype(BF16), (((0,), (0,)), ((), ())),
                           preferred_element_type=F32)


def _split_terms(a, terms):
    out, rem = [], a
    for _ in range(terms):
        hi = rem.astype(BF16)
        out.append(hi)
        rem = rem - hi.astype(F32)
    return out


def _dot_exact_rhs(a, b_exact, terms=HEAD_SUM_TERMS):
    acc = None
    for piece in _split_terms(a, terms):
        t = jnp.dot(piece, b_exact, preferred_element_type=F32)
        acc = t if acc is None else acc + t
    return acc


def _dot_exact_lhs(a_exact, b, terms=CUMSUM_TERMS):
    acc = None
    for piece in _split_terms(b, terms):
        t = jnp.dot(a_exact, piece, preferred_element_type=F32)
        acc = t if acc is None else acc + t
    return acc


def _sigmoid(x):
    return 1.0 / (1.0 + jnp.exp(-x))


def _rms(x, g):
    return x * lax.rsqrt(jnp.mean(x * x, axis=-1, keepdims=True) + NORM_EPS) * g


def _mod_kernel(c_ref, w_ref, b_ref, o_ref):
    cc = c_ref[...]
    s = cc * _sigmoid(cc)
    rows = s.shape[0]
    s_parts = jnp.concatenate(_split_terms(s, 3), axis=0)
    acc = b_ref[...]
    for wp in _split_terms(w_ref[...], MOD_WEIGHT_TERMS):
        t = jnp.dot(s_parts, wp, preferred_element_type=F32)
        acc = acc + (t[:rows] + t[rows:2 * rows] + t[2 * rows:])
    o_ref[...] = acc


def _mod_call(cc, ada_w, ada_b):
    rows, d = cc.shape
    n = ada_w.shape[1]
    tn = MOD_TILE
    return pl.pallas_call(
        _mod_kernel,
        grid=(n // tn,),
        in_specs=[pl.BlockSpec((rows, d), lambda i: (0, 0)),
                  pl.BlockSpec((d, tn), lambda i: (0, i)),
                  pl.BlockSpec((1, tn), lambda i: (0, i))],
        out_specs=pl.BlockSpec((rows, tn), lambda i: (0, i)),
        out_shape=jax.ShapeDtypeStruct((rows, n), F32),
        compiler_params=pltpu.CompilerParams(dimension_semantics=("arbitrary",),
                                             vmem_limit_bytes=VMEM_LIMIT),
        name="mod",
    )(cc, ada_w, ada_b)


def _prep_kernel(x_ref, xp_ref, xn_ref, ctx_ref, ms_ref, g1_ref, win_ref, cw_ref,
                 w0_ref, w2_ref, a0_ref, a2_ref, g2_ref, kk_ref, ka_ref, rk_ref,
                 cs_ref, ones_ref, tri_ref,
                 at_ref, bt_ref, kt_ref, rt_ref, v_ref, pl_ref, bonus_ref, gate_ref, z_ref):
    j = pl.program_id(1)
    refs = (x_ref, xp_ref, xn_ref, ctx_ref, ms_ref, g1_ref, win_ref, cw_ref,
            w0_ref, w2_ref, a0_ref, a2_ref, kk_ref, ka_ref, ones_ref, tri_ref,
            at_ref, bt_ref, kt_ref, rt_ref, v_ref, pl_ref)
    tiles = _run_interleaved([_prep_tile(bi, *refs) for bi in range(x_ref.shape[0])])

    @pl.when(j >= 1)
    def _():
        ones_blk = ones_ref[...]
        for bi, (u_four, r, v, gd, kd_sum) in enumerate(tiles):
            bs = _dot_exact_rhs(r * rk_ref[...] * kd_sum, ones_blk)
            bonus_ref[bi] = bs * v
            gate_ref[bi] = _bdot(_sigmoid(gd), g2_ref[...])
            z_ref[bi] = _bdot(u_four, cs_ref[...]).astype(BF16)


def _prep_tile(bi, x_ref, xp_ref, xn_ref, ctx_ref, ms_ref, g1_ref, win_ref, cw_ref,
               w0_ref, w2_ref, a0_ref, a2_ref, kk_ref, ka_ref, ones_ref, tri_ref,
               at_ref, bt_ref, kt_ref, rt_ref, v_ref, pl_ref):
    j = pl.program_id(1)
    n_j = pl.num_programs(1)
    is_ctx = j == 0
    shift = ms_ref[bi, 0, 0:1, :]
    scale = ms_ref[bi, 0, 1:2, :]
    g = g1_ref[...]

    def norm_mod(xr):
        return _rms(xr, g) * (1.0 + scale) + shift

    xin = jnp.where(is_ctx, ctx_ref[bi], x_ref[bi])
    h = norm_mod(xin).astype(BF16)
    u = jnp.dot(h, win_ref[...], preferred_element_type=F32)

    halo = jnp.concatenate([xp_ref[bi], xn_ref[bi]], axis=0)
    uh = jnp.dot(norm_mod(halo).astype(BF16), win_ref[:, RKV_LO:RKV_HI],
                 preferred_element_type=F32)
    yield
    prev_row = jnp.where(j >= 2, uh[7:8], 0.0)
    next_row = jnp.where(jnp.logical_and(j >= 1, j < n_j - 1), uh[8:9], 0.0)

    rkv = u[:, RKV_LO:RKV_HI]
    row = lax.broadcasted_iota(jnp.int32, rkv.shape, 0)
    u_m1 = jnp.where(row == 0, prev_row, pltpu.roll(rkv, 1, 0))
    u_p1 = jnp.where(row == TILE - 1, next_row, pltpu.roll(rkv, TILE - 1, 0))
    rkv = cw_ref[0:1, :] * u_m1 + cw_ref[1:2, :] * rkv + cw_ref[2:3, :] * u_p1
    r = rkv[:, 0:RWKV_W]
    k = rkv[:, RWKV_W:2 * RWKV_W]
    v = rkv[:, 2 * RWKV_W:3 * RWKV_W]

    wd = u[:, WD_LO:AD_LO]
    ad = u[:, AD_LO:GD_LO]
    gd = u[:, GD_LO:PROJ_W]
    w_lora = w0_ref[...] + _bdot(jnp.tanh(wd), w2_ref[...])
    lw = -math.exp(-0.5) * _sigmoid(w_lora)
    a = _sigmoid(a0_ref[...] + _bdot(ad, a2_ref[...]))

    ones_blk = ones_ref[...]
    kraw = k * kk_ref[...]
    kk = kraw * lax.rsqrt(_dot_exact_rhs(kraw * kraw, ones_blk) + KK_EPS)
    ka = ka_ref[...]
    kd = [k * (1.0 + (a[:, d * RWKV_W:(d + 1) * RWKV_W] - 1.0) * ka) for d in range(2)]

    v_ref[bi] = v.astype(BF16)
    for d in range(2):
        lwd = lw[:, d * RWKV_W:(d + 1) * RWKV_W]
        ad_ = a[:, d * RWKV_W:(d + 1) * RWKV_W]
        c = _dot_exact_lhs(tri_ref[d], lwd)
        e_pos = jnp.exp(c)
        e_neg = jnp.exp(-c)
        e_prev = jnp.exp(c - lwd)
        at_ref[bi, d] = (kk * e_prev).astype(BF16)
        bt_ref[bi, d] = (kk * ad_ * e_neg).astype(BF16)
        kt_ref[bi, d] = (kd[d] * e_neg).astype(BF16)
        rt_ref[bi, d] = (r * e_pos).astype(BF16)
        for q in range(CHUNKS_PER_TILE):
            last = q * CHUNK + (CHUNK - 1 if d == 0 else 0)
            pl_ref[bi, d, q] = e_pos[last:last + 1, :]
    return u[:, 0:FOURIER_W], r, v, gd, kd[0] + kd[1]


def _prep_call(x, ctx, modsel, norm1_g, w_in, conv_w, w0, w2bd, a0, a2bd, g2, k_k, k_a, r_k,
               cs, ones_blk, tri):
    b, s, d = x.shape
    n_j = 1 + s // TILE
    n_chunks = (CTX_LEN + s) // CHUNK
    t_all = CTX_LEN + s
    c2 = 2 * RWKV_W

    def full(arr):
        nd = arr.ndim
        return pl.BlockSpec(arr.shape, lambda bi, j, _n=nd: (0,) * _n)

    rows8 = TILE // 8
    nb = PREP_BATCH
    in_specs = [
        pl.BlockSpec((nb, TILE, d), lambda bi, j: (bi, jnp.maximum(j - 1, 0), 0)),
        pl.BlockSpec((nb, 8, d), lambda bi, j: (bi, jnp.maximum((j - 1) * rows8 - 1, 0), 0)),
        pl.BlockSpec((nb, 8, d), lambda bi, j: (bi, jnp.minimum(j * rows8, s // 8 - 1), 0)),
        pl.BlockSpec((nb, CTX_LEN, d), lambda bi, j: (bi, 0, 0)),
        pl.BlockSpec((nb, 1, 2, d), lambda bi, j: (bi, jnp.minimum(j, 1), 0, 0)),
        full(norm1_g), full(w_in), full(conv_w), full(w0), full(w2bd), full(a0), full(a2bd),
        full(g2), full(k_k), full(k_a), full(r_k), full(cs), full(ones_blk), full(tri),
    ]
    lat = lambda bi, j: (bi, jnp.maximum(j - 1, 0), 0)
    out_specs = [
        pl.BlockSpec((nb, 2, TILE, RWKV_W), lambda bi, j: (bi, 0, j, 0)),
        pl.BlockSpec((nb, 2, TILE, RWKV_W), lambda bi, j: (bi, 0, j, 0)),
        pl.BlockSpec((nb, 2, TILE, RWKV_W), lambda bi, j: (bi, 0, j, 0)),
        pl.BlockSpec((nb, 2, TILE, RWKV_W), lambda bi, j: (bi, 0, j, 0)),
        pl.BlockSpec((nb, TILE, RWKV_W), lambda bi, j: (bi, j, 0)),
        pl.BlockSpec((nb, 2, CHUNKS_PER_TILE, 1, RWKV_W), lambda bi, j: (bi, 0, j, 0, 0)),
        pl.BlockSpec((nb, TILE, RWKV_W), lat),
        pl.BlockSpec((nb, TILE, RWKV_W), lat),
        pl.BlockSpec((nb, TILE, 2 * FOURIER_W), lat),
    ]
    out_shape = [
        jax.ShapeDtypeStruct((b, 2, t_all, RWKV_W), BF16),
        jax.ShapeDtypeStruct((b, 2, t_all, RWKV_W), BF16),
        jax.ShapeDtypeStruct((b, 2, t_all, RWKV_W), BF16),
        jax.ShapeDtypeStruct((b, 2, t_all, RWKV_W), BF16),
        jax.ShapeDtypeStruct((b, t_all, RWKV_W), BF16),
        jax.ShapeDtypeStruct((b, 2, n_chunks, 1, RWKV_W), F32),
        jax.ShapeDtypeStruct((b, s, RWKV_W), F32),
        jax.ShapeDtypeStruct((b, s, RWKV_W), F32),
        jax.ShapeDtypeStruct((b, s, 2 * FOURIER_W), BF16),
    ]
    return pl.pallas_call(
        _prep_kernel,
        grid=(b // nb, n_j),
        in_specs=in_specs,
        out_specs=out_specs,
        out_shape=out_shape,
        compiler_params=pltpu.CompilerParams(dimension_semantics=("arbitrary", "arbitrary"),
                                             vmem_limit_bytes=VMEM_LIMIT),
        name="prep",
    )(x, x, x, ctx, modsel, norm1_g, w_in, conv_w, w0, w2bd, a0, a2bd, g2, k_k, k_a, r_k,
      cs, ones_blk, tri)


def _quad_chunk(a, bm, km, r, v, plrow, g_states, strict, incl, eye, lane_lo, mask_bd, eye_row,
                level_masks):
    n = PAIR

    def dup(xv):
        z = jnp.zeros_like(xv)
        return jnp.concatenate([jnp.where(lane_lo, xv, z), jnp.where(lane_lo, z, xv)], axis=0)

    def to_row(x2):
        return x2[:CHUNK] + x2[CHUNK:]

    def bdiag(xr):
        return jnp.where(mask_bd, jnp.concatenate([xr] * QUAD_HEADS, axis=0), jnp.zeros((), xr.dtype))

    pairs = []
    for p in range(2):
        sl = slice(p * n, (p + 1) * n)
        a2, b2, k2, r2, v2 = (dup(t[:, sl]) for t in (a, bm, km, r, v))
        rhs = jnp.concatenate([b2, k2], axis=0)
        s = _bdot_nt(jnp.concatenate([a2, r2], axis=0), rhs)
        pairs.append((sl, r2, v2, rhs, s))
    yield

    a_ab = jnp.concatenate([to_row(jnp.where(strict, s[:n, :n], 0.0)) for *_, s in pairs], axis=1)
    a_ak = jnp.concatenate([to_row(jnp.where(strict, s[:n, n:], 0.0)) for *_, s in pairs], axis=1)
    m_rbk = [jnp.concatenate([jnp.where(incl, s[n:, :n], 0.0), jnp.where(incl, s[n:, n:], 0.0)],
                             axis=1).astype(BF16) for *_, s in pairs]

    av = _bdot(a_ak, bdiag(v))
    a_lvl = [jnp.where(m, a_ab, 0.0).astype(BF16) for m in level_masks]
    t_row = jnp.where(eye_row, 1.0, 0.0) - jnp.where(level_masks[0], a_ab, 0.0)
    lvl = 1
    while lvl < len(a_lvl):
        paired = lvl + 1 < len(a_lvl)
        lhs = jnp.concatenate([a_lvl[lvl], a_lvl[lvl + 1]], axis=0) if paired else a_lvl[lvl]
        yc = jnp.dot(lhs, bdiag(t_row.astype(BF16)), preferred_element_type=F32)
        yield
        y_bd = bdiag(yc[:CHUNK].astype(BF16))
        if paired:
            both = jnp.dot(jnp.concatenate([t_row, yc[CHUNK:]], axis=0).astype(BF16), y_bd,
                           preferred_element_type=F32)
            yield
            t_row = t_row - both[:CHUNK]
            y_next = yc[CHUNK:] - both[CHUNK:]
            t_row = t_row - _bdot(t_row, bdiag(y_next.astype(BF16)))
            lvl += 2
        else:
            t_row = t_row - _bdot(t_row, y_bd)
            lvl += 1
        yield
    tx = _bdot(t_row, jnp.concatenate([bdiag(a), bdiag(av.astype(BF16))], axis=1))
    yield
    a_w = tx[:, :2 * n].astype(BF16)
    u_v = (-tx[:, 2 * n:]).astype(BF16)
    stage5 = []
    for (sl, r2, v2, rhs, _), m2 in zip(pairs, m_rbk):
        zmat = jnp.concatenate(
            [jnp.concatenate([dup(a_w[:, sl]), dup(u_v[:, sl])], axis=1),
             jnp.concatenate([jnp.zeros_like(v2), v2], axis=1)], axis=0)
        rhs_end = rhs.astype(F32) * plrow[:, sl]
        lhs = jnp.concatenate([rhs_end.T.astype(BF16), m2], axis=0)
        stage5.append(jnp.dot(lhs, zmat, preferred_element_type=F32))
    yield
    ys, g_new = [], []
    for (sl, r2, v2, rhs, _), o5, g_state in zip(pairs, stage5, g_states):
        phi = jnp.where(eye, plrow[:, sl], 0.0) - o5[:n, :n]
        r_y = r2.astype(F32) - o5[n:, :n]
        o6 = _bdot(jnp.concatenate([r_y, phi], axis=0), g_state)
        ys.append(to_row(o6[:n] + o5[n:, n:]))
        g_new.append(o6[n:] + o5[:n, n:])
    return jnp.concatenate(ys, axis=1), g_new


def _run_interleaved(gens, groups=1, lag=0):
    results = [None] * len(gens)
    active = list(range(len(gens)))
    rnd = 0
    while active:
        for i in list(active):
            if rnd < (i % groups) * lag:
                continue
            try:
                next(gens[i])
            except StopIteration as stop:
                results[i] = stop.value
                active.remove(i)
        rnd += 1
    return results


def _scan_kernel(atf, btf, ktf, rtf, vf, plf, atb, btb, ktb, rtb, vb, plb,
                 yf_ref, yb_ref, g_ref):
    s = pl.program_id(1)

    @pl.when(s == 0)
    def _():
        g_ref[...] = jnp.zeros_like(g_ref)

    ri = lax.broadcasted_iota(jnp.int32, (PAIR, PAIR), 0)
    ci = lax.broadcasted_iota(jnp.int32, (PAIR, PAIR), 1)
    same = (ri >= HEAD) == (ci >= HEAD)
    ti = ri & (HEAD - 1)
    tj = ci & (HEAD - 1)
    eye = ri == ci
    lane_lo = lax.broadcasted_iota(jnp.int32, (CHUNK, PAIR), 1) < HEAD
    masks = [(same & (tj < ti), same & (tj <= ti)),
             (same & (tj > ti), same & (tj >= ti))]
    quad_w = QUAD_HEADS * HEAD
    rq = lax.broadcasted_iota(jnp.int32, (quad_w, quad_w), 0)
    cq = lax.broadcasted_iota(jnp.int32, (quad_w, quad_w), 1)
    mask_bd = (rq // HEAD) == (cq // HEAD)
    t_i = lax.broadcasted_iota(jnp.int32, (CHUNK, quad_w), 0)
    s_i = lax.broadcasted_iota(jnp.int32, (CHUNK, quad_w), 1) & (HEAD - 1)
    eye_row = s_i == t_i
    level_masks = []
    for d in range(2):
        late, early = (t_i, s_i) if d == 0 else (s_i, t_i)
        level_masks.append([
            ((t_i >> (lvl + 1)) == (s_i >> (lvl + 1)))
            & (((late >> lvl) & 1) == 1) & (((early >> lvl) & 1) == 0)
            for lvl in range(int(math.log2(CHUNK)))])
    refs = [(atf, btf, ktf, rtf, vf, plf, yf_ref), (atb, btb, ktb, rtb, vb, plb, yb_ref)]
    gens, dests = [], []
    for bi in range(yf_ref.shape[0]):
        for d in range(2):
            at, bt, kt, rt, vv, plr, y_ref = refs[d]
            strict, incl = masks[d]
            for q in range(RWKV_W // quad_w):
                sl = slice(q * quad_w, (q + 1) * quad_w)
                gens.append(_quad_chunk(at[bi, 0, :, sl], bt[bi, 0, :, sl], kt[bi, 0, :, sl],
                                        rt[bi, 0, :, sl], vv[bi, :, sl], plr[bi, 0, 0, :, sl],
                                        [g_ref[bi, d, 2 * q], g_ref[bi, d, 2 * q + 1]],
                                        strict, incl, eye, lane_lo, mask_bd, eye_row,
                                        level_masks[d]))
                dests.append((y_ref, bi, d, q, sl))
    done = _run_interleaved(gens, groups=SCAN_GROUPS, lag=SCAN_LAG)
    for (y, g_new), (y_ref, bi, d, q, sl) in zip(done, dests):
        g_ref[bi, d, 2 * q] = g_new[0]
        g_ref[bi, d, 2 * q + 1] = g_new[1]
        y_ref[bi, :, sl] = y


def _scan_call(at, bt, kt, rt, v, plast, s_lat):
    b = at.shape[0]
    n_chunks = at.shape[2] // CHUNK
    n_ctx = CTX_LEN // CHUNK
    n_lat = s_lat // CHUNK

    def fwd_c(s):
        return s

    def bwd_c(s):
        return jnp.where(s < n_ctx, n_ctx - 1 - s, n_chunks + n_ctx - 1 - s)

    nb = SCAN_BATCH

    def dir_spec(d, cfun):
        return pl.BlockSpec((nb, 1, CHUNK, RWKV_W), lambda bi, s: (bi, d, cfun(s), 0))

    def v_spec(cfun):
        return pl.BlockSpec((nb, CHUNK, RWKV_W), lambda bi, s: (bi, cfun(s), 0))

    def pl_spec(d, cfun):
        return pl.BlockSpec((nb, 1, 1, 1, RWKV_W), lambda bi, s: (bi, d, cfun(s), 0, 0))

    in_specs = ([dir_spec(0, fwd_c)] * 4 + [v_spec(fwd_c), pl_spec(0, fwd_c)]
                + [dir_spec(1, bwd_c)] * 4 + [v_spec(bwd_c), pl_spec(1, bwd_c)])
    out_specs = [
        pl.BlockSpec((nb, CHUNK, RWKV_W), lambda bi, s: (bi, jnp.maximum(s - n_ctx, 0), 0)),
        pl.BlockSpec((nb, CHUNK, RWKV_W),
                     lambda bi, s: (bi, jnp.minimum(n_chunks - 1 - s, n_lat - 1), 0)),
    ]
    out_shape = [jax.ShapeDtypeStruct((b, s_lat, RWKV_W), F32)] * 2
    return pl.pallas_call(
        _scan_kernel,
        grid=(b // nb, n_chunks),
        in_specs=in_specs,
        out_specs=out_specs,
        out_shape=out_shape,
        scratch_shapes=[pltpu.VMEM((nb, 2, N_PAIRS, PAIR, PAIR), F32)],
        compiler_params=pltpu.CompilerParams(dimension_semantics=("arbitrary", "arbitrary"),
                                             vmem_limit_bytes=VMEM_LIMIT),
        name="scan",
    )(at, bt, kt, rt, v, plast, at, bt, kt, rt, v, plast)


def _mix_kernel(dft_ref, rev_ref, z_ref, yf_ref, yb_ref, bonus_ref, gate_ref, x_ref, mod_ref,
                gng_ref, gnb_ref, fg_ref, wout_ref, n2g_ref, ones_ref, x1_ref, h2_ref, zf_ref,
                *, fscale):
    seq = z_ref.shape[1]
    half = seq // 2
    rt = rev_ref.shape[0]

    @pl.when(pl.program_id(1) == 0)
    def _():
        mid = z_ref[0, half:half + 1, :FOURIER_W].astype(F32)
        row0 = lax.broadcasted_iota(jnp.int32, (rt, FOURIER_W), 0) == 0
        for part, sign in ((0, 1.0), (1, -1.0)):
            lanes = slice(part * FOURIER_W, (part + 1) * FOURIER_W)
            for r in range(half // rt):
                lo = seq - rt * (r + 1)
                nxt = (z_ref[0, lo + rt:lo + 2 * rt, lanes] if r > 0
                       else jnp.zeros((rt, FOURIER_W), BF16))
                rev = jnp.dot(rev_ref[...], jnp.concatenate([z_ref[0, lo:lo + rt, lanes], nxt], axis=0),
                              preferred_element_type=F32)
                folded = z_ref[0, r * rt:(r + 1) * rt, lanes].astype(F32) + sign * rev
                if part == 1 and r == 0:
                    folded = jnp.where(row0, mid, folded)
                zf_ref[part * half + r * rt:part * half + (r + 1) * rt, :] = folded.astype(BF16)

    yfour = jnp.dot(dft_ref[...], zf_ref[...], preferred_element_type=F32)
    four = _rms(yfour * fscale, fg_ref[...])

    ones_blk = ones_ref[...]
    y = yf_ref[0] + yb_ref[0]
    mu = _dot_exact_rhs(y, ones_blk) * (1.0 / HEAD)
    dy = y - mu
    var = _dot_exact_rhs(dy * dy, ones_blk) * (1.0 / HEAD)
    yn = dy * lax.rsqrt(var + GN_EPS) * gng_ref[...] + gnb_ref[...]
    rw = (yn + bonus_ref[0]) * gate_ref[0]

    out = _bdot(four, wout_ref[:FOURIER_W, :]) + _bdot(rw, wout_ref[FOURIER_W:, :])
    x1 = x_ref[0] + mod_ref[0, 2:3, :] * out
    x1_ref[0] = x1
    h2 = _rms(x1, n2g_ref[...]) * (1.0 + mod_ref[0, 4:5, :]) + mod_ref[0, 3:4, :]
    h2_ref[0] = h2.astype(BF16)


def _mix_call(dft, rev, z, yf, yb, bonus, gate, x, modx, gn_g, gn_b, f_g, w_out, norm2_g,
              ones_blk):
    b, s, d = x.shape
    tm = MIX_TILE

    def full(arr):
        nd = arr.ndim
        return pl.BlockSpec(arr.shape, lambda bi, m, _n=nd: (0,) * _n)

    tok = lambda w: pl.BlockSpec((1, tm, w), lambda bi, m: (bi, m, 0))
    in_specs = [
        pl.BlockSpec((tm, s), lambda bi, m: (m, 0)),
        full(rev),
        pl.BlockSpec((1, s, 2 * FOURIER_W), lambda bi, m: (bi, 0, 0)),
        tok(RWKV_W), tok(RWKV_W), tok(RWKV_W), tok(RWKV_W), tok(d),
        pl.BlockSpec((1, 6, d), lambda bi, m: (bi, 0, 0)),
        full(gn_g), full(gn_b), full(f_g), full(w_out), full(norm2_g), full(ones_blk),
    ]
    fscale = 1.0 / math.sqrt(s * (FOURIER_W // FOURIER_GROUPS))
    return pl.pallas_call(
        functools.partial(_mix_kernel, fscale=fscale),
        grid=(b, s // tm),
        in_specs=in_specs,
        out_specs=[tok(d), tok(d)],
        out_shape=[jax.ShapeDtypeStruct((b, s, d), F32), jax.ShapeDtypeStruct((b, s, d), BF16)],
        scratch_shapes=[pltpu.VMEM((s, FOURIER_W), BF16)],
        compiler_params=pltpu.CompilerParams(dimension_semantics=("arbitrary", "arbitrary"),
                                             vmem_limit_bytes=VMEM_LIMIT),
        name="mix",
    )(dft, rev, z, yf, yb, bonus, gate, x, modx, gn_g, gn_b, f_g, w_out, norm2_g, ones_blk)


def _ffn_up_kernel(h2_ref, wg_ref, wv_ref, cwg_ref, cwv_ref, cbg_ref, cbv_ref, act_ref):
    t = h2_ref.shape[1]
    rows = FFN_ROWS
    n_blk = t // rows
    w = jnp.concatenate([wg_ref[...], wv_ref[...]], axis=1)
    cw = jnp.concatenate([cwg_ref[...], cwv_ref[...]], axis=1).astype(BF16)
    cb = jnp.concatenate([cbg_ref[...], cbv_ref[...]], axis=1).astype(BF16)
    col = lax.broadcasted_iota(jnp.int32, (rows, 2 * FF_TILE), 0) & (GRID_W - 1)
    has_left = col > 0
    has_right = col < GRID_W - 1
    zeros = jnp.zeros((GRID_W, 2 * FF_TILE), BF16)

    def row_mixes(i):
        u = jnp.dot(h2_ref[0, i * rows:(i + 1) * rows, :], w, preferred_element_type=F32)
        u_l = jnp.where(has_left, pltpu.roll(u, 1, 0), 0.0).astype(BF16)
        u_r = jnp.where(has_right, pltpu.roll(u, rows - 1, 0), 0.0).astype(BF16)
        u_c = u.astype(BF16)
        return [cw[3 * kh:3 * kh + 1] * u_l + cw[3 * kh + 1:3 * kh + 2] * u_c
                + cw[3 * kh + 2:3 * kh + 3] * u_r for kh in range(3)]

    def finish(i, prev, cur, nxt):
        above = zeros if prev is None else prev[0][rows - GRID_W:]
        below = zeros if nxt is None else nxt[2][:GRID_W]
        c = (cur[1] + jnp.concatenate([above, cur[0][:rows - GRID_W]], axis=0)
             + jnp.concatenate([cur[2][GRID_W:], below], axis=0) + cb)
        cg = c[:, :FF_TILE]
        act_ref[0, i * rows:(i + 1) * rows, :] = cg * _sigmoid(cg) * c[:, FF_TILE:]

    mixes = [None] * (n_blk + 1)
    for i in range(n_blk):
        mixes[i] = row_mixes(i)
        if i >= 1:
            finish(i - 1, mixes[i - 2] if i >= 2 else None, mixes[i - 1], mixes[i])
    finish(n_blk - 1, mixes[n_blk - 2] if n_blk >= 2 else None, mixes[n_blk - 1], None)


def _ffn_up_call(h2, w_up, conv_w9, conv_b):
    b, s, d = h2.shape
    n_f = D_FF // FF_TILE
    in_specs = [
        pl.BlockSpec((1, s, d), lambda bi, f: (bi, 0, 0)),
        pl.BlockSpec((d, FF_TILE), lambda bi, f: (0, f)),
        pl.BlockSpec((d, FF_TILE), lambda bi, f: (0, n_f + f)),
        pl.BlockSpec((9, FF_TILE), lambda bi, f: (0, f)),
        pl.BlockSpec((9, FF_TILE), lambda bi, f: (0, n_f + f)),
        pl.BlockSpec((1, FF_TILE), lambda bi, f: (0, f)),
        pl.BlockSpec((1, FF_TILE), lambda bi, f: (0, n_f + f)),
    ]
    return pl.pallas_call(
        _ffn_up_kernel,
        grid=(b, n_f),
        in_specs=in_specs,
        out_specs=pl.BlockSpec((1, s, FF_TILE), lambda bi, f: (bi, 0, f)),
        out_shape=jax.ShapeDtypeStruct((b, s, D_FF), BF16),
        compiler_params=pltpu.CompilerParams(dimension_semantics=("arbitrary", "arbitrary"),
                                             vmem_limit_bytes=VMEM_LIMIT),
        name="ffn_up",
    )(h2, w_up, w_up, conv_w9, conv_w9, conv_b, conv_b)


def _ffn_down_kernel(act_ref, wd_ref, x1_ref, mod_ref, fg_ref, out_ref):
    y = jnp.dot(act_ref[0], wd_ref[...], preferred_element_type=F32)
    out_ref[0] = _rms(x1_ref[0] + mod_ref[0, 5:6, :] * y, fg_ref[...])


def _ffn_down_call(act, w_down, x1, modx, final_g):
    b, s, d = x1.shape
    tm = DOWN_TILE
    in_specs = [
        pl.BlockSpec((1, tm, D_FF), lambda bi, m: (bi, m, 0)),
        pl.BlockSpec((D_FF, d), lambda bi, m: (0, 0)),
        pl.BlockSpec((1, tm, d), lambda bi, m: (bi, m, 0)),
        pl.BlockSpec((1, 6, d), lambda bi, m: (bi, 0, 0)),
        pl.BlockSpec((1, d), lambda bi, m: (0, 0)),
    ]
    return pl.pallas_call(
        _ffn_down_kernel,
        grid=(b, s // tm),
        in_specs=in_specs,
        out_specs=pl.BlockSpec((1, tm, d), lambda bi, m: (bi, m, 0)),
        out_shape=jax.ShapeDtypeStruct((b, s, d), F32),
        compiler_params=pltpu.CompilerParams(dimension_semantics=("arbitrary", "arbitrary"),
                                             vmem_limit_bytes=VMEM_LIMIT),
        name="ffn_down",
    )(act, w_down, x1, modx, final_g)


@functools.lru_cache(maxsize=None)
def _constants(seq):
    gw = FOURIER_W // FOURIER_GROUPS
    nm = np.outer(np.arange(gw), np.arange(gw)) % gw
    ang = 2.0 * np.pi * nm / gw
    cs = np.zeros((FOURIER_W, 2 * FOURIER_W), np.float64)
    for gi in range(FOURIER_GROUPS):
        sl = slice(gi * gw, (gi + 1) * gw)
        cs[sl, sl] = np.cos(ang)
        cs[sl, FOURIER_W + gi * gw:FOURIER_W + (gi + 1) * gw] = np.sin(ang)
    half = seq // 2
    k_idx = np.arange(seq, dtype=np.int64)[:, None]
    ang_c = 2.0 * np.pi * ((k_idx * np.arange(0, half + 1, dtype=np.int64)[None, :]) % seq) / seq
    ang_s = 2.0 * np.pi * ((k_idx * np.arange(1, half, dtype=np.int64)[None, :]) % seq) / seq
    dft = np.concatenate([np.cos(ang_c), -np.sin(ang_s)], axis=1)
    ri = np.arange(REV_TILE)
    rev = (np.arange(2 * REV_TILE)[None, :] == (REV_TILE - ri)[:, None]).astype(np.float64)
    head_id = np.arange(RWKV_W) // HEAD
    ones_blk = (head_id[:, None] == head_id[None, :]).astype(np.float64)
    ti = np.arange(TILE)
    same_chunk = (ti[:, None] // CHUNK) == (ti[None, :] // CHUNK)
    tri = np.stack([same_chunk & (ti[None, :] <= ti[:, None]),
                    same_chunk & (ti[None, :] >= ti[:, None])]).astype(np.float64)
    to_bf16 = lambda arr: np.asarray(arr, dtype=np.float32).astype(BF16)
    return (np.asarray(cs, dtype=np.float32), np.asarray(dft, dtype=np.float32),
            to_bf16(ones_blk), to_bf16(tri), to_bf16(rev))


def _block_diag_dirs(w):
    z = jnp.zeros_like(w[0])
    return jnp.concatenate([jnp.concatenate([w[0], z], axis=1),
                            jnp.concatenate([z, w[1]], axis=1)], axis=0)


def kernel(x, c, ctx, c_ctx, ada_w, ada_b, norm1_g, norm2_g, w_in, rwkv_conv_w, decay_w0, decay_w2, iclr_a0, iclr_a2, gate_g2, k_k, k_a, r_k, gn_g, gn_b, fourier_g, w_out, ffn_w_up, ffn_conv_w, ffn_conv_b, ffn_w_down, final_g):
    b, s, d = x.shape
    assert ada_w.shape[0] == 1, "single-layer configuration"
    assert (b, s, d) == (c.shape[0], s, D_MODEL) and ctx.shape == (b, CTX_LEN, d)
    assert s % TILE == 0 and CTX_LEN == TILE
    cs, dft, ones_blk, tri, rev = (jnp.asarray(t) for t in _constants(s))
    cs, dft = cs.astype(BF16), dft.astype(BF16)

    cc = jnp.concatenate([c, c_ctx[None, :], jnp.zeros((16 - b - 1, d), F32)], axis=0)
    mod = _mod_call(cc, ada_w[0], ada_b[0][None, :])
    modx = mod[:b].reshape(b, 6, d)
    modc = jnp.broadcast_to(mod[b].reshape(1, 6, d)[:, :2], (b, 2, d))
    modsel = jnp.stack([modc, modx[:, :2]], axis=1)

    row = lambda t: t.reshape(1, -1)
    at, bt, kt, rt, v, plast, bonus, gate, z = _prep_call(
        x, ctx, modsel, row(norm1_g[0]), w_in[0].astype(BF16), rwkv_conv_w[0],
        row(decay_w0[0]), _block_diag_dirs(decay_w2[0]).astype(BF16),
        row(iclr_a0[0]), _block_diag_dirs(iclr_a2[0]).astype(BF16),
        gate_g2[0].astype(BF16), row(k_k[0]), row(k_a[0]), row(r_k[0]), cs, ones_blk, tri)

    yf, yb = _scan_call(at, bt, kt, rt, v, plast, s)

    x1, h2 = _mix_call(dft, rev, z, yf, yb, bonus, gate, x, modx, row(gn_g[0]), row(gn_b[0]),
                       row(fourier_g[0]), w_out[0].astype(BF16), row(norm2_g[0]), ones_blk)

    act = _ffn_up_call(h2, ffn_w_up[0].astype(BF16), ffn_conv_w[0].reshape(9, 2 * D_FF),
                       row(ffn_conv_b[0]))
    return _ffn_down_call(act, ffn_w_down[0].astype(BF16), x1, modx, row(final_g))
```

```python
import functools
import math

import numpy as np
import jax
import jax.numpy as jnp
from jax import lax
from jax.experimental import pallas as pl
from jax.experimental.pallas import tpu as pltpu

F32 = jnp.float32
BF16 = jnp.bfloat16

D_MODEL = 1024
CTX_LEN = 256
GRID_W = 64
FOURIER_W = 512
FOURIER_GROUPS = 8
RWKV_W = 512
HEAD = 64
N_HEADS = 8
DECAY_RANK = 64
ICLR_RANK = 64
GATE_RANK = 128
D_FF = 2816
NORM_EPS = 1e-6
GN_EPS = 64e-5
KK_EPS = 1e-12

CHUNK = 64
TILE = 256
MIX_TILE = 512
REV_TILE = 256
CHUNKS_PER_TILE = TILE // CHUNK
PREP_BATCH = 2
PAIR = 2 * HEAD
N_PAIRS = RWKV_W // PAIR
QUAD_HEADS = 4
SCAN_BATCH = 8
SCAN_GROUPS = 12
SCAN_LAG = 1
MOD_TILE = 1536
FF_TILE = 1408
FFN_ROWS = 256
DOWN_TILE = 1024
VMEM_LIMIT = 56 * 1024 * 1024
HEAD_SUM_TERMS = 1
CUMSUM_TERMS = 2
MOD_WEIGHT_TERMS = 2

RKV_LO = FOURIER_W
RKV_HI = FOURIER_W + 3 * RWKV_W
WD_LO = RKV_HI
AD_LO = WD_LO + 2 * DECAY_RANK
GD_LO = AD_LO + 2 * ICLR_RANK
PROJ_W = GD_LO + GATE_RANK


def _bdot(a, b):
    return jnp.dot(a.astype(BF16), b.astype(BF16), preferred_element_type=F32)


def _bdot_nt(a, b):
    return lax.dot_general(a.astype(BF16), b.astype(BF16), (((1,), (1,)), ((), ())),
                           preferred_element_type=F32)


def _bdot_tn(a, b):
    return lax.dot_general(a.astype(BF16), b.astype(BF16), (((0,), (0,)), ((), ())),
                           preferred_element_type=F32)


def _split_terms(a, terms):
    out, rem = [], a
    for _ in range(terms):
        hi = rem.astype(BF16)
        out.append(hi)
        rem = rem - hi.astype(F32)
    return out


def _dot_exact_rhs(a, b_exact, terms=HEAD_SUM_TERMS):
    acc = None
    for piece in _split_terms(a, terms):
        t = jnp.dot(piece, b_exact, preferred_element_type=F32)
        acc = t if acc is None else acc + t
    return acc


def _dot_exact_lhs(a_exact, b, terms=CUMSUM_TERMS):
    acc = None
    for piece in _split_terms(b, terms):
        t = jnp.dot(a_exact, piece, preferred_element_type=F32)
        acc = t if acc is None else acc + t
    return acc


def _sigmoid(x):
    return 1.0 / (1.0 + jnp.exp(-x))


def _rms(x, g):
    return x * lax.rsqrt(jnp.mean(x * x, axis=-1, keepdims=True) + NORM_EPS) * g


def _mod_kernel(c_ref, w_ref, b_ref, o_ref):
    cc = c_ref[...]
    s = cc * _sigmoid(cc)
    rows = s.shape[0]
    s_parts = jnp.concatenate(_split_terms(s, 3), axis=0)
    acc = b_ref[...]
    for wp in _split_terms(w_ref[...], MOD_WEIGHT_TERMS):
        t = jnp.dot(s_parts, wp, preferred_element_type=F32)
        acc = acc + (t[:rows] + t[rows:2 * rows] + t[2 * rows:])
    o_ref[...] = acc


def _mod_call(cc, ada_w, ada_b):
    rows, d = cc.shape
    n = ada_w.shape[1]
    tn = MOD_TILE
    return pl.pallas_call(
        _mod_kernel,
        grid=(n // tn,),
        in_specs=[pl.BlockSpec((rows, d), lambda i: (0, 0)),
                  pl.BlockSpec((d, tn), lambda i: (0, i)),
                  pl.BlockSpec((1, tn), lambda i: (0, i))],
        out_specs=pl.BlockSpec((rows, tn), lambda i: (0, i)),
        out_shape=jax.ShapeDtypeStruct((rows, n), F32),
        compiler_params=pltpu.CompilerParams(dimension_semantics=("arbitrary",),
                                             vmem_limit_bytes=VMEM_LIMIT),
        name="mod",
    )(cc, ada_w, ada_b)


def _prep_kernel(x_ref, xp_ref, xn_ref, ctx_ref, ms_ref, g1_ref, win_ref, cw_ref,
                 w0_ref, w2_ref, a0_ref, a2_ref, g2_ref, kk_ref, ka_ref, rk_ref,
                 cs_ref, ones_ref, tri_ref,
                 at_ref, bt_ref, kt_ref, rt_ref, v_ref, pl_ref, bonus_ref, gate_ref, z_ref):
    j = pl.program_id(1)
    refs = (x_ref, xp_ref, xn_ref, ctx_ref, ms_ref, g1_ref, win_ref, cw_ref,
            w0_ref, w2_ref, a0_ref, a2_ref, kk_ref, ka_ref, ones_ref, tri_ref,
            at_ref, bt_ref, kt_ref, rt_ref, v_ref, pl_ref)
    tiles = _run_interleaved([_prep_tile(bi, *refs) for bi in range(x_ref.shape[0])])

    @pl.when(j >= 1)
    def _():
        ones_blk = ones_ref[...]
        for bi, (u_four, r, v, gd, kd_sum) in enumerate(tiles):
            bs = _dot_exact_rhs(r * rk_ref[...] * kd_sum, ones_blk)
            bonus_ref[bi] = bs * v
            gate_ref[bi] = _bdot(_sigmoid(gd), g2_ref[...])
            z_ref[bi] = _bdot(u_four, cs_ref[...]).astype(BF16)


def _prep_tile(bi, x_ref, xp_ref, xn_ref, ctx_ref, ms_ref, g1_ref, win_ref, cw_ref,
               w0_ref, w2_ref, a0_ref, a2_ref, kk_ref, ka_ref, ones_ref, tri_ref,
               at_ref, bt_ref, kt_ref, rt_ref, v_ref, pl_ref):
    j = pl.program_id(1)
    n_j = pl.num_programs(1)
    is_ctx = j == 0
    shift = ms_ref[bi, 0, 0:1, :]
    scale = ms_ref[bi, 0, 1:2, :]
    g = g1_ref[...]

    def norm_mod(xr):
        return _rms(xr, g) * (1.0 + scale) + shift

    xin = jnp.where(is_ctx, ctx_ref[bi], x_ref[bi])
    h = norm_mod(xin).astype(BF16)
    u = jnp.dot(h, win_ref[...], preferred_element_type=F32)

    halo = jnp.concatenate([xp_ref[bi], xn_ref[bi]], axis=0)
    uh = jnp.dot(norm_mod(halo).astype(BF16), win_ref[:, RKV_LO:RKV_HI],
                 preferred_element_type=F32)
    yield
    prev_row = jnp.where(j >= 2, uh[7:8], 0.0)
    next_row = jnp.where(jnp.logical_and(j >= 1, j < n_j - 1), uh[8:9], 0.0)

    rkv = u[:, RKV_LO:RKV_HI]
    row = lax.broadcasted_iota(jnp.int32, rkv.shape, 0)
    u_m1 = jnp.where(row == 0, prev_row, pltpu.roll(rkv, 1, 0))
    u_p1 = jnp.where(row == TILE - 1, next_row, pltpu.roll(rkv, TILE - 1, 0))
    rkv = cw_ref[0:1, :] * u_m1 + cw_ref[1:2, :] * rkv + cw_ref[2:3, :] * u_p1
    r = rkv[:, 0:RWKV_W]
    k = rkv[:, RWKV_W:2 * RWKV_W]
    v = rkv[:, 2 * RWKV_W:3 * RWKV_W]

    wd = u[:, WD_LO:AD_LO]
    ad = u[:, AD_LO:GD_LO]
    gd = u[:, GD_LO:PROJ_W]
    w_lora = w0_ref[...] + _bdot(jnp.tanh(wd), w2_ref[...])
    lw = -math.exp(-0.5) * _sigmoid(w_lora)
    a = _sigmoid(a0_ref[...] + _bdot(ad, a2_ref[...]))

    ones_blk = ones_ref[...]
    kraw = k * kk_ref[...]
    kk = kraw * lax.rsqrt(_dot_exact_rhs(kraw * kraw, ones_blk) + KK_EPS)
    ka = ka_ref[...]
    kd = [k * (1.0 + (a[:, d * RWKV_W:(d + 1) * RWKV_W] - 1.0) * ka) for d in range(2)]

    v_ref[bi] = v.astype(BF16)
    for d in range(2):
        lwd = lw[:, d * RWKV_W:(d + 1) * RWKV_W]
        ad_ = a[:, d * RWKV_W:(d + 1) * RWKV_W]
        c = _dot_exact_lhs(tri_ref[d], lwd)
        e_pos = jnp.exp(c)
        e_neg = jnp.exp(-c)
        e_prev = jnp.exp(c - lwd)
        at_ref[bi, d] = (kk * e_prev).astype(BF16)
        bt_ref[bi, d] = (kk * ad_ * e_neg).astype(BF16)
        kt_ref[bi, d] = (kd[d] * e_neg).astype(BF16)
        rt_ref[bi, d] = (r * e_pos).astype(BF16)
        for q in range(CHUNKS_PER_TILE):
            last = q * CHUNK + (CHUNK - 1 if d == 0 else 0)
            pl_ref[bi, d, q] = e_pos[last:last + 1, :]
    return u[:, 0:FOURIER_W], r, v, gd, kd[0] + kd[1]


def _prep_call(x, ctx, modsel, norm1_g, w_in, conv_w, w0, w2bd, a0, a2bd, g2, k_k, k_a, r_k,
               cs, ones_blk, tri):
    b, s, d = x.shape
    n_j = 1 + s // TILE
    n_chunks = (CTX_LEN + s) // CHUNK
    t_all = CTX_LEN + s
    c2 = 2 * RWKV_W

    def full(arr):
        nd = arr.ndim
        return pl.BlockSpec(arr.shape, lambda bi, j, _n=nd: (0,) * _n)

    rows8 = TILE // 8
    nb = PREP_BATCH
    in_specs = [
        pl.BlockSpec((nb, TILE, d), lambda bi, j: (bi, jnp.maximum(j - 1, 0), 0)),
        pl.BlockSpec((nb, 8, d), lambda bi, j: (bi, jnp.maximum((j - 1) * rows8 - 1, 0), 0)),
        pl.BlockSpec((nb, 8, d), lambda bi, j: (bi, jnp.minimum(j * rows8, s // 8 - 1), 0)),
        pl.BlockSpec((nb, CTX_LEN, d), lambda bi, j: (bi, 0, 0)),
        pl.BlockSpec((nb, 1, 2, d), lambda bi, j: (bi, jnp.minimum(j, 1), 0, 0)),
        full(norm1_g), full(w_in), full(conv_w), full(w0), full(w2bd), full(a0), full(a2bd),
        full(g2), full(k_k), full(k_a), full(r_k), full(cs), full(ones_blk), full(tri),
    ]
    lat = lambda bi, j: (bi, jnp.maximum(j - 1, 0), 0)
    out_specs = [
        pl.BlockSpec((nb, 2, TILE, RWKV_W), lambda bi, j: (bi, 0, j, 0)),
        pl.BlockSpec((nb, 2, TILE, RWKV_W), lambda bi, j: (bi, 0, j, 0)),
        pl.BlockSpec((nb, 2, TILE, RWKV_W), lambda bi, j: (bi, 0, j, 0)),
        pl.BlockSpec((nb, 2, TILE, RWKV_W), lambda bi, j: (bi, 0, j, 0)),
        pl.BlockSpec((nb, TILE, RWKV_W), lambda bi, j: (bi, j, 0)),
        pl.BlockSpec((nb, 2, CHUNKS_PER_TILE, 1, RWKV_W), lambda bi, j: (bi, 0, j, 0, 0)),
        pl.BlockSpec((nb, TILE, RWKV_W), lat),
        pl.BlockSpec((nb, TILE, RWKV_W), lat),
        pl.BlockSpec((nb, TILE, 2 * FOURIER_W), lat),
    ]
    out_shape = [
        jax.ShapeDtypeStruct((b, 2, t_all, RWKV_W), BF16),
        jax.ShapeDtypeStruct((b, 2, t_all, RWKV_W), BF16),
        jax.ShapeDtypeStruct((b, 2, t_all, RWKV_W), BF16),
        jax.ShapeDtypeStruct((b, 2, t_all, RWKV_W), BF16),
        jax.ShapeDtypeStruct((b, t_all, RWKV_W), BF16),
        jax.ShapeDtypeStruct((b, 2, n_chunks, 1, RWKV_W), F32),
        jax.ShapeDtypeStruct((b, s, RWKV_W), F32),
        jax.ShapeDtypeStruct((b, s, RWKV_W), F32),
        jax.ShapeDtypeStruct((b, s, 2 * FOURIER_W), BF16),
    ]
    return pl.pallas_call(
        _prep_kernel,
        grid=(b // nb, n_j),
        in_specs=in_specs,
        out_specs=out_specs,
        out_shape=out_shape,
        compiler_params=pltpu.CompilerParams(dimension_semantics=("arbitrary", "arbitrary"),
                                             vmem_limit_bytes=VMEM_LIMIT),
        name="prep",
    )(x, x, x, ctx, modsel, norm1_g, w_in, conv_w, w0, w2bd, a0, a2bd, g2, k_k, k_a, r_k,
      cs, ones_blk, tri)


def _quad_chunk(a, bm, km, r, v, plrow, g_states, strict, incl, eye, lane_lo, mask_bd, eye_row,
                level_masks):
    n = PAIR

    def dup(xv):
        z = jnp.zeros_like(xv)
        return jnp.concatenate([jnp.where(lane_lo, xv, z), jnp.where(lane_lo, z, xv)], axis=0)

    def to_row(x2):
        return x2[:CHUNK] + x2[CHUNK:]

    def bdiag(xr):
        return jnp.where(mask_bd, jnp.concatenate([xr] * QUAD_HEADS, axis=0), jnp.zeros((), xr.dtype))

    pairs = []
    for p in range(2):
        sl = slice(p * n, (p + 1) * n)
        a2, b2, k2, r2, v2 = (dup(t[:, sl]) for t in (a, bm, km, r, v))
        rhs = jnp.concatenate([b2, k2], axis=0)
        s = _bdot_nt(jnp.concatenate([a2, r2], axis=0), rhs)
        pairs.append((sl, r2, v2, rhs, s))
    yield

    a_ab = jnp.concatenate([to_row(jnp.where(strict, s[:n, :n], 0.0)) for *_, s in pairs], axis=1)
    a_ak = jnp.concatenate([to_row(jnp.where(strict, s[:n, n:], 0.0)) for *_, s in pairs], axis=1)
    m_rbk = [jnp.concatenate([jnp.where(incl, s[n:, :n], 0.0), jnp.where(incl, s[n:, n:], 0.0)],
                             axis=1).astype(BF16) for *_, s in pairs]

    av = _bdot(a_ak, bdiag(v))
    a_lvl = [jnp.where(m, a_ab, 0.0).astype(BF16) for m in level_masks]
    t_row = jnp.where(eye_row, 1.0, 0.0) - jnp.where(level_masks[0], a_ab, 0.0)
    lvl = 1
    while lvl < len(a_lvl):
        paired = lvl + 1 < len(a_lvl)
        lhs = jnp.concatenate([a_lvl[lvl], a_lvl[lvl + 1]], axis=0) if paired else a_lvl[lvl]
        yc = jnp.dot(lhs, bdiag(t_row.astype(BF16)), preferred_element_type=F32)
        yield
        y_bd = bdiag(yc[:CHUNK].astype(BF16))
        if paired:
            both = jnp.dot(jnp.concatenate([t_row, yc[CHUNK:]], axis=0).astype(BF16), y_bd,
                           preferred_element_type=F32)
            yield
            t_row = t_row - both[:CHUNK]
            y_next = yc[CHUNK:] - both[CHUNK:]
            t_row = t_row - _bdot(t_row, bdiag(y_next.astype(BF16)))
            lvl += 2
        else:
            t_row = t_row - _bdot(t_row, y_bd)
            lvl += 1
        yield
    tx = _bdot(t_row, jnp.concatenate([bdiag(a), bdiag(av.astype(BF16))], axis=1))
    yield
    a_w = tx[:, :2 * n].astype(BF16)
    u_v = (-tx[:, 2 * n:]).astype(BF16)
    stage5 = []
    for (sl, r2, v2, rhs, _), m2 in zip(pairs, m_rbk):
        zmat = jnp.concatenate(
            [jnp.concatenate([dup(a_w[:, sl]), dup(u_v[:, sl])], axis=1),
             jnp.concatenate([jnp.zeros_like(v2), v2], axis=1)], axis=0)
        rhs_end = rhs.astype(F32) * plrow[:, sl]
        lhs = jnp.concatenate([rhs_end.T.astype(BF16), m2], axis=0)
        stage5.append(jnp.dot(lhs, zmat, preferred_element_type=F32))
    yield
    ys, g_new = [], []
    for (sl, r2, v2, rhs, _), o5, g_state in zip(pairs, stage5, g_states):
        phi = jnp.where(eye, plrow[:, sl], 0.0) - o5[:n, :n]
        r_y = r2.astype(F32) - o5[n:, :n]
        o6 = _bdot(jnp.concatenate([r_y, phi], axis=0), g_state)
        ys.append(to_row(o6[:n] + o5[n:, n:]))
        g_new.append(o6[n:] + o5[:n, n:])
    return jnp.concatenate(ys, axis=1), g_new


def _run_interleaved(gens, groups=1, lag=0):
    results = [None] * len(gens)
    active = list(range(len(gens)))
    rnd = 0
    while active:
        for i in list(active):
            if rnd < (i % groups) * lag:
                continue
            try:
                next(gens[i])
            except StopIteration as stop:
                results[i] = stop.value
                active.remove(i)
        rnd += 1
    return results


def _scan_kernel(atf, btf, ktf, rtf, vf, plf, atb, btb, ktb, rtb, vb, plb,
                 yf_ref, yb_ref, g_ref):
    s = pl.program_id(1)

    @pl.when(s == 0)
    def _():
        g_ref[...] = jnp.zeros_like(g_ref)

    ri = lax.broadcasted_iota(jnp.int32, (PAIR, PAIR), 0)
    ci = lax.broadcasted_iota(jnp.int32, (PAIR, PAIR), 1)
    same = (ri >= HEAD) == (ci >= HEAD)
    ti = ri & (HEAD - 1)
    tj = ci & (HEAD - 1)
    eye = ri == ci
    lane_lo = lax.broadcasted_iota(jnp.int32, (CHUNK, PAIR), 1) < HEAD
    masks = [(same & (tj < ti), same & (tj <= ti)),
             (same & (tj > ti), same & (tj >= ti))]
    quad_w = QUAD_HEADS * HEAD
    rq = lax.broadcasted_iota(jnp.int32, (quad_w, quad_w), 0)
    cq = lax.broadcasted_iota(jnp.int32, (quad_w, quad_w), 1)
    mask_bd = (rq // HEAD) == (cq // HEAD)
    t_i = lax.broadcasted_iota(jnp.int32, (CHUNK, quad_w), 0)
    s_i = lax.broadcasted_iota(jnp.int32, (CHUNK, quad_w), 1) & (HEAD - 1)
    eye_row = s_i == t_i
    level_masks = []
    for d in range(2):
        late, early = (t_i, s_i) if d == 0 else (s_i, t_i)
        level_masks.append([
            ((t_i >> (lvl + 1)) == (s_i >> (lvl + 1)))
            & (((late >> lvl) & 1) == 1) & (((early >> lvl) & 1) == 0)
            for lvl in range(int(math.log2(CHUNK)))])
    refs = [(atf, btf, ktf, rtf, vf, plf, yf_ref), (atb, btb, ktb, rtb, vb, plb, yb_ref)]
    gens, dests = [], []
    for bi in range(yf_ref.shape[0]):
        for d in range(2):
            at, bt, kt, rt, vv, plr, y_ref = refs[d]
            strict, incl = masks[d]
            for q in range(RWKV_W // quad_w):
                sl = slice(q * quad_w, (q + 1) * quad_w)
                gens.append(_quad_chunk(at[bi, 0, :, sl], bt[bi, 0, :, sl], kt[bi, 0, :, sl],
                                        rt[bi, 0, :, sl], vv[bi, :, sl], plr[bi, 0, 0, :, sl],
                                        [g_ref[bi, d, 2 * q], g_ref[bi, d, 2 * q + 1]],
                                        strict, incl, eye, lane_lo, mask_bd, eye_row,
                                        level_masks[d]))
                dests.append((y_ref, bi, d, q, sl))
    done = _run_interleaved(gens, groups=SCAN_GROUPS, lag=SCAN_LAG)
    for (y, g_new), (y_ref, bi, d, q, sl) in zip(done, dests):
        g_ref[bi, d, 2 * q] = g_new[0]
        g_ref[bi, d, 2 * q + 1] = g_new[1]
        y_ref[bi, :, sl] = y


def _scan_call(at, bt, kt, rt, v, plast, s_lat):
    b = at.shape[0]
    n_chunks = at.shape[2] // CHUNK
    n_ctx = CTX_LEN // CHUNK
    n_lat = s_lat // CHUNK

    def fwd_c(s):
        return s

    def bwd_c(s):
        return jnp.where(s < n_ctx, n_ctx - 1 - s, n_chunks + n_ctx - 1 - s)

    nb = SCAN_BATCH

    def dir_spec(d, cfun):
        return pl.BlockSpec((nb, 1, CHUNK, RWKV_W), lambda bi, s: (bi, d, cfun(s), 0))

    def v_spec(cfun):
        return pl.BlockSpec((nb, CHUNK, RWKV_W), lambda bi, s: (bi, cfun(s), 0))

    def pl_spec(d, cfun):
        return pl.BlockSpec((nb, 1, 1, 1, RWKV_W), lambda bi, s: (bi, d, cfun(s), 0, 0))

    in_specs = ([dir_spec(0, fwd_c)] * 4 + [v_spec(fwd_c), pl_spec(0, fwd_c)]
                + [dir_spec(1, bwd_c)] * 4 + [v_spec(bwd_c), pl_spec(1, bwd_c)])
    out_specs = [
        pl.BlockSpec((nb, CHUNK, RWKV_W), lambda bi, s: (bi, jnp.maximum(s - n_ctx, 0), 0)),
        pl.BlockSpec((nb, CHUNK, RWKV_W),
                     lambda bi, s: (bi, jnp.minimum(n_chunks - 1 - s, n_lat - 1), 0)),
    ]
    out_shape = [jax.ShapeDtypeStruct((b, s_lat, RWKV_W), F32)] * 2
    return pl.pallas_call(
        _scan_kernel,
        grid=(b // nb, n_chunks),
        in_specs=in_specs,
        out_specs=out_specs,
        out_shape=out_shape,
        scratch_shapes=[pltpu.VMEM((nb, 2, N_PAIRS, PAIR, PAIR), F32)],
        compiler_params=pltpu.CompilerParams(dimension_semantics=("arbitrary", "arbitrary"),
                                             vmem_limit_bytes=VMEM_LIMIT),
        name="scan",
    )(at, bt, kt, rt, v, plast, at, bt, kt, rt, v, plast)


def _mix_kernel(dft_ref, rev_ref, z_ref, yf_ref, yb_ref, bonus_ref, gate_ref, x_ref, mod_ref,
                gng_ref, gnb_ref, fg_ref, wout_ref, n2g_ref, ones_ref, x1_ref, h2_ref, zf_ref,
                *, fscale):
    seq = z_ref.shape[1]
    half = seq // 2
    rt = rev_ref.shape[0]

    @pl.when(pl.program_id(1) == 0)
    def _():
        mid = z_ref[0, half:half + 1, :FOURIER_W].astype(F32)
        row0 = lax.broadcasted_iota(jnp.int32, (rt, FOURIER_W), 0) == 0
        for part, sign in ((0, 1.0), (1, -1.0)):
            lanes = slice(part * FOURIER_W, (part + 1) * FOURIER_W)
            for r in range(half // rt):
                lo = seq - rt * (r + 1)
                nxt = (z_ref[0, lo + rt:lo + 2 * rt, lanes] if r > 0
                       else jnp.zeros((rt, FOURIER_W), BF16))
                rev = jnp.dot(rev_ref[...], jnp.concatenate([z_ref[0, lo:lo + rt, lanes], nxt], axis=0),
                              preferred_element_type=F32)
                folded = z_ref[0, r * rt:(r + 1) * rt, lanes].astype(F32) + sign * rev
                if part == 1 and r == 0:
                    folded = jnp.where(row0, mid, folded)
                zf_ref[part * half + r * rt:part * half + (r + 1) * rt, :] = folded.astype(BF16)

    yfour = jnp.dot(dft_ref[...], zf_ref[...], preferred_element_type=F32)
    four = _rms(yfour * fscale, fg_ref[...])

    ones_blk = ones_ref[...]
    y = yf_ref[0] + yb_ref[0]
    mu = _dot_exact_rhs(y, ones_blk) * (1.0 / HEAD)
    dy = y - mu
    var = _dot_exact_rhs(dy * dy, ones_blk) * (1.0 / HEAD)
    yn = dy * lax.rsqrt(var + GN_EPS) * gng_ref[...] + gnb_ref[...]
    rw = (yn + bonus_ref[0]) * gate_ref[0]

    out = _bdot(four, wout_ref[:FOURIER_W, :]) + _bdot(rw, wout_ref[FOURIER_W:, :])
    x1 = x_ref[0] + mod_ref[0, 2:3, :] * out
    x1_ref[0] = x1
    h2 = _rms(x1, n2g_ref[...]) * (1.0 + mod_ref[0, 4:5, :]) + mod_ref[0, 3:4, :]
    h2_ref[0] = h2.astype(BF16)


def _mix_call(dft, rev, z, yf, yb, bonus, gate, x, modx, gn_g, gn_b, f_g, w_out, norm2_g,
              ones_blk):
    b, s, d = x.shape
    tm = MIX_TILE

    def full(arr):
        nd = arr.ndim
        return pl.BlockSpec(arr.shape, lambda bi, m, _n=nd: (0,) * _n)

    tok = lambda w: pl.BlockSpec((1, tm, w), lambda bi, m: (bi, m, 0))
    in_specs = [
        pl.BlockSpec((tm, s), lambda bi, m: (m, 0)),
        full(rev),
        pl.BlockSpec((1, s, 2 * FOURIER_W), lambda bi, m: (bi, 0, 0)),
        tok(RWKV_W), tok(RWKV_W), tok(RWKV_W), tok(RWKV_W), tok(d),
        pl.BlockSpec((1, 6, d), lambda bi, m: (bi, 0, 0)),
        full(gn_g), full(gn_b), full(f_g), full(w_out), full(norm2_g), full(ones_blk),
    ]
    fscale = 1.0 / math.sqrt(s * (FOURIER_W // FOURIER_GROUPS))
    return pl.pallas_call(
        functools.partial(_mix_kernel, fscale=fscale),
        grid=(b, s // tm),
        in_specs=in_specs,
        out_specs=[tok(d), tok(d)],
        out_shape=[jax.ShapeDtypeStruct((b, s, d), F32), jax.ShapeDtypeStruct((b, s, d), BF16)],
        scratch_shapes=[pltpu.VMEM((s, FOURIER_W), BF16)],
        compiler_params=pltpu.CompilerParams(dimension_semantics=("arbitrary", "arbitrary"),
                                             vmem_limit_bytes=VMEM_LIMIT),
        name="mix",
    )(dft, rev, z, yf, yb, bonus, gate, x, modx, gn_g, gn_b, f_g, w_out, norm2_g, ones_blk)


def _ffn_up_kernel(h2_ref, wg_ref, wv_ref, cwg_ref, cwv_ref, cbg_ref, cbv_ref, act_ref):
    t = h2_ref.shape[1]
    rows = FFN_ROWS
    n_blk = t // rows
    w = jnp.concatenate([wg_ref[...], wv_ref[...]], axis=1)
    cw = jnp.concatenate([cwg_ref[...], cwv_ref[...]], axis=1).astype(BF16)
    cb = jnp.concatenate([cbg_ref[...], cbv_ref[...]], axis=1).astype(BF16)
    col = lax.broadcasted_iota(jnp.int32, (rows, 2 * FF_TILE), 0) & (GRID_W - 1)
    has_left = col > 0
    has_right = col < GRID_W - 1
    zeros = jnp.zeros((GRID_W, 2 * FF_TILE), BF16)

    def row_mixes(i):
        u = jnp.dot(h2_ref[0, i * rows:(i + 1) * rows, :], w, preferred_element_type=F32)
        u_l = jnp.where(has_left, pltpu.roll(u, 1, 0), 0.0).astype(BF16)
        u_r = jnp.where(has_right, pltpu.roll(u, rows - 1, 0), 0.0).astype(BF16)
        u_c = u.astype(BF16)
        return [cw[3 * kh:3 * kh + 1] * u_l + cw[3 * kh + 1:3 * kh + 2] * u_c
                + cw[3 * kh + 2:3 * kh + 3] * u_r for kh in range(3)]

    def finish(i, prev, cur, nxt):
        above = zeros if prev is None else prev[0][rows - GRID_W:]
        below = zeros if nxt is None else nxt[2][:GRID_W]
        c = (cur[1] + jnp.concatenate([above, cur[0][:rows - GRID_W]], axis=0)
             + jnp.concatenate([cur[2][GRID_W:], below], axis=0) + cb)
        cg = c[:, :FF_TILE]
        act_ref[0, i * rows:(i + 1) * rows, :] = cg * _sigmoid(cg) * c[:, FF_TILE:]

    mixes = [None] * (n_blk + 1)
    for i in range(n_blk):
        mixes[i] = row_mixes(i)
        if i >= 1:
            finish(i - 1, mixes[i - 2] if i >= 2 else None, mixes[i - 1], mixes[i])
    finish(n_blk - 1, mixes[n_blk - 2] if n_blk >= 2 else None, mixes[n_blk - 1], None)


def _ffn_up_call(h2, w_up, conv_w9, conv_b):
    b, s, d = h2.shape
    n_f = D_FF // FF_TILE
    in_specs = [
        pl.BlockSpec((1, s, d), lambda bi, f: (bi, 0, 0)),
        pl.BlockSpec((d, FF_TILE), lambda bi, f: (0, f)),
        pl.BlockSpec((d, FF_TILE), lambda bi, f: (0, n_f + f)),
        pl.BlockSpec((9, FF_TILE), lambda bi, f: (0, f)),
        pl.BlockSpec((9, FF_TILE), lambda bi, f: (0, n_f + f)),
        pl.BlockSpec((1, FF_TILE), lambda bi, f: (0, f)),
        pl.BlockSpec((1, FF_TILE), lambda bi, f: (0, n_f + f)),
    ]
    return pl.pallas_call(
        _ffn_up_kernel,
        grid=(b, n_f),
        in_specs=in_specs,
        out_specs=pl.BlockSpec((1, s, FF_TILE), lambda bi, f: (bi, 0, f)),
        out_shape=jax.ShapeDtypeStruct((b, s, D_FF), BF16),
        compiler_params=pltpu.CompilerParams(dimension_semantics=("arbitrary", "arbitrary"),
                                             vmem_limit_bytes=VMEM_LIMIT),
        name="ffn_up",
    )(h2, w_up, w_up, conv_w9, conv_w9, conv_b, conv_b)


def _ffn_down_kernel(act_ref, wd_ref, x1_ref, mod_ref, fg_ref, out_ref):
    y = jnp.dot(act_ref[0], wd_ref[...], preferred_element_type=F32)
    out_ref[0] = _rms(x1_ref[0] + mod_ref[0, 5:6, :] * y, fg_ref[...])


def _ffn_down_call(act, w_down, x1, modx, final_g):
    b, s, d = x1.shape
    tm = DOWN_TILE
    in_specs = [
        pl.BlockSpec((1, tm, D_FF), lambda bi, m: (bi, m, 0)),
        pl.BlockSpec((D_FF, d), lambda bi, m: (0, 0)),
        pl.BlockSpec((1, tm, d), lambda bi, m: (bi, m, 0)),
        pl.BlockSpec((1, 6, d), lambda bi, m: (bi, 0, 0)),
        pl.BlockSpec((1, d), lambda bi, m: (0, 0)),
    ]
    return pl.pallas_call(
        _ffn_down_kernel,
        grid=(b, s // tm),
        in_specs=in_specs,
        out_specs=pl.BlockSpec((1, tm, d), lambda bi, m: (bi, m, 0)),
        out_shape=jax.ShapeDtypeStruct((b, s, d), F32),
        compiler_params=pltpu.CompilerParams(dimension_semantics=("arbitrary", "arbitrary"),
                                             vmem_limit_bytes=VMEM_LIMIT),
        name="ffn_down",
    )(act, w_down, x1, modx, final_g)


@functools.lru_cache(maxsize=None)
def _constants(seq):
    gw = FOURIER_W // FOURIER_GROUPS
    nm = np.outer(np.arange(gw), np.arange(gw)) % gw
    ang = 2.0 * np.pi * nm / gw
    cs = np.zeros((FOURIER_W, 2 * FOURIER_W), np.float64)
    for gi in range(FOURIER_GROUPS):
        sl = slice(gi * gw, (gi + 1) * gw)
        cs[sl, sl] = np.cos(ang)
        cs[sl, FOURIER_W + gi * gw:FOURIER_W + (gi + 1) * gw] = np.sin(ang)
    half = seq // 2
    k_idx = np.arange(seq, dtype=np.int64)[:, None]
    ang_c = 2.0 * np.pi * ((k_idx * np.arange(0, half + 1, dtype=np.int64)[None, :]) % seq) / seq
    ang_s = 2.0 * np.pi * ((k_idx * np.arange(1, half, dtype=np.int64)[None, :]) % seq) / seq
    dft = np.concatenate([np.cos(ang_c), -np.sin(ang_s)], axis=1)
    ri = np.arange(REV_TILE)
    rev = (np.arange(2 * REV_TILE)[None, :] == (REV_TILE - ri)[:, None]).astype(np.float64)
    head_id = np.arange(RWKV_W) // HEAD
    ones_blk = (head_id[:, None] == head_id[None, :]).astype(np.float64)
    ti = np.arange(TILE)
    same_chunk = (ti[:, None] // CHUNK) == (ti[None, :] // CHUNK)
    tri = np.stack([same_chunk & (ti[None, :] <= ti[:, None]),
                    same_chunk & (ti[None, :] >= ti[:, None])]).astype(np.float64)
    to_bf16 = lambda arr: np.asarray(arr, dtype=np.float32).astype(BF16)
    return (np.asarray(cs, dtype=np.float32), np.asarray(dft, dtype=np.float32),
            to_bf16(ones_blk), to_bf16(tri), to_bf16(rev))


def _block_diag_dirs(w):
    z = jnp.zeros_like(w[0])
    return jnp.concatenate([jnp.concatenate([w[0], z], axis=1),
                            jnp.concatenate([z, w[1]], axis=1)], axis=0)


def kernel(x, c, ctx, c_ctx, ada_w, ada_b, norm1_g, norm2_g, w_in, rwkv_conv_w, decay_w0, decay_w2, iclr_a0, iclr_a2, gate_g2, k_k, k_a, r_k, gn_g, gn_b, fourier_g, w_out, ffn_w_up, ffn_conv_w, ffn_conv_b, ffn_w_down, final_g):
    b, s, d = x.shape
    assert ada_w.shape[0] == 1, "single-layer configuration"
    assert (b, s, d) == (c.shape[0], s, D_MODEL) and ctx.shape == (b, CTX_LEN, d)
    assert s % TILE == 0 and CTX_LEN == TILE
    cs, dft, ones_blk, tri, rev = (jnp.asarray(t) for t in _constants(s))
    cs, dft = cs.astype(BF16), dft.astype(BF16)

    cc = jnp.concatenate([c, c_ctx[None, :], jnp.zeros((16 - b - 1, d), F32)], axis=0)
    mod = _mod_call(cc, ada_w[0], ada_b[0][None, :])
    modx = mod[:b].reshape(b, 6, d)
    modc = jnp.broadcast_to(mod[b].reshape(1, 6, d)[:, :2], (b, 2, d))
    modsel = jnp.stack([modc, modx[:, :2]], axis=1)

    row = lambda t: t.reshape(1, -1)
    at, bt, kt, rt, v, plast, bonus, gate, z = _prep_call(
        x, ctx, modsel, row(norm1_g[0]), w_in[0].astype(BF16), rwkv_conv_w[0],
        row(decay_w0[0]), _block_diag_dirs(decay_w2[0]).astype(BF16),
        row(iclr_a0[0]), _block_diag_dirs(iclr_a2[0]).astype(BF16),
        gate_g2[0].astype(BF16), row(k_k[0]), row(k_a[0]), row(r_k[0]), cs, ones_blk, tri)

    yf, yb = _scan_call(at, bt, kt, rt, v, plast, s)

    x1, h2 = _mix_call(dft, rev, z, yf, yb, bonus, gate, x, modx, row(gn_g[0]), row(gn_b[0]),
                       row(fourier_g[0]), w_out[0].astype(BF16), row(norm2_g[0]), ones_blk)

    act = _ffn_up_call(h2, ffn_w_up[0].astype(BF16), ffn_conv_w[0].reshape(9, 2 * D_FF),
                       row(ffn_conv_b[0]))
    return _ffn_down_call(act, ffn_w_down[0].astype(BF16), x1, modx, row(final_g))
```

```python
import functools
import math

import numpy as np
import jax
import jax.numpy as jnp
from jax import lax
from jax.experimental import pallas as pl
from jax.experimental.pallas import tpu as pltpu

F32 = jnp.float32
BF16 = jnp.bfloat16

D_MODEL = 1024
CTX_LEN = 256
GRID_W = 64
FOURIER_W = 512
FOURIER_GROUPS = 8
RWKV_W = 512
HEAD = 64
N_HEADS = 8
DECAY_RANK = 64
ICLR_RANK = 64
GATE_RANK = 128
D_FF = 2816
NORM_EPS = 1e-6
GN_EPS = 64e-5
KK_EPS = 1e-12

CHUNK = 64
TILE = 256
MIX_TILE = 512
REV_TILE = 256
CHUNKS_PER_TILE = TILE // CHUNK
PREP_BATCH = 2
PAIR = 2 * HEAD
N_PAIRS = RWKV_W // PAIR
QUAD_HEADS = 4
SCAN_BATCH = 8
SCAN_GROUPS = 4
SCAN_LAG = 3
MOD_TILE = 1536
FF_TILE = 1408
FFN_ROWS = 256
DOWN_TILE = 1024
VMEM_LIMIT = 56 * 1024 * 1024
HEAD_SUM_TERMS = 1
CUMSUM_TERMS = 2
MOD_WEIGHT_TERMS = 2

RKV_LO = FOURIER_W
RKV_HI = FOURIER_W + 3 * RWKV_W
WD_LO = RKV_HI
AD_LO = WD_LO + 2 * DECAY_RANK
GD_LO = AD_LO + 2 * ICLR_RANK
PROJ_W = GD_LO + GATE_RANK


def _bdot(a, b):
    return jnp.dot(a.astype(BF16), b.astype(BF16), preferred_element_type=F32)


def _bdot_nt(a, b):
    return lax.dot_general(a.astype(BF16), b.astype(BF16), (((1,), (1,)), ((), ())),
                           preferred_element_type=F32)


def _bdot_tn(a, b):
    return lax.dot_general(a.astype(BF16), b.astype(BF16), (((0,), (0,)), ((), ())),
                           preferred_element_type=F32)


def _split_terms(a, terms):
    out, rem = [], a
    for _ in range(terms):
        hi = rem.astype(BF16)
        out.append(hi)
        rem = rem - hi.astype(F32)
    return out


def _dot_exact_rhs(a, b_exact, terms=HEAD_SUM_TERMS):
    acc = None
    for piece in _split_terms(a, terms):
        t = jnp.dot(piece, b_exact, preferred_element_type=F32)
        acc = t if acc is None else acc + t
    return acc


def _dot_exact_lhs(a_exact, b, terms=CUMSUM_TERMS):
    acc = None
    for piece in _split_terms(b, terms):
        t = jnp.dot(a_exact, piece, preferred_element_type=F32)
        acc = t if acc is None else acc + t
    return acc


def _sigmoid(x):
    return 1.0 / (1.0 + jnp.exp(-x))


def _rms(x, g):
    return x * lax.rsqrt(jnp.mean(x * x, axis=-1, keepdims=True) + NORM_EPS) * g


def _mod_kernel(c_ref, w_ref, b_ref, o_ref):
    cc = c_ref[...]
    s = cc * _sigmoid(cc)
    rows = s.shape[0]
    s_parts = jnp.concatenate(_split_terms(s, 3), axis=0)
    acc = b_ref[...]
    for wp in _split_terms(w_ref[...], MOD_WEIGHT_TERMS):
        t = jnp.dot(s_parts, wp, preferred_element_type=F32)
        acc = acc + (t[:rows] + t[rows:2 * rows] + t[2 * rows:])
    o_ref[...] = acc


def _mod_call(cc, ada_w, ada_b):
    rows, d = cc.shape
    n = ada_w.shape[1]
    tn = MOD_TILE
    return pl.pallas_call(
        _mod_kernel,
        grid=(n // tn,),
        in_specs=[pl.BlockSpec((rows, d), lambda i: (0, 0)),
                  pl.BlockSpec((d, tn), lambda i: (0, i)),
                  pl.BlockSpec((1, tn), lambda i: (0, i))],
        out_specs=pl.BlockSpec((rows, tn), lambda i: (0, i)),
        out_shape=jax.ShapeDtypeStruct((rows, n), F32),
        compiler_params=pltpu.CompilerParams(dimension_semantics=("arbitrary",),
                                             vmem_limit_bytes=VMEM_LIMIT),
        name="mod",
    )(cc, ada_w, ada_b)


def _prep_kernel(x_ref, xp_ref, xn_ref, ctx_ref, ms_ref, g1_ref, win_ref, cw_ref,
                 w0_ref, w2_ref, a0_ref, a2_ref, g2_ref, kk_ref, ka_ref, rk_ref,
                 cs_ref, ones_ref, tri_ref,
                 at_ref, bt_ref, kt_ref, rt_ref, v_ref, pl_ref, bonus_ref, gate_ref, z_ref):
    j = pl.program_id(1)
    refs = (x_ref, xp_ref, xn_ref, ctx_ref, ms_ref, g1_ref, win_ref, cw_ref,
            w0_ref, w2_ref, a0_ref, a2_ref, kk_ref, ka_ref, ones_ref, tri_ref,
            at_ref, bt_ref, kt_ref, rt_ref, v_ref, pl_ref)
    tiles = _run_interleaved([_prep_tile(bi, *refs) for bi in range(x_ref.shape[0])],
                             groups=PREP_BATCH, lag=1)

    @pl.when(j >= 1)
    def _():
        ones_blk = ones_ref[...]
        for bi, (u_four, r, v, gd, kd_sum) in enumerate(tiles):
            bs = _dot_exact_rhs(r * rk_ref[...] * kd_sum, ones_blk)
            bonus_ref[bi] = bs * v
            gate_ref[bi] = _bdot(_sigmoid(gd), g2_ref[...])
            z_ref[bi] = _bdot(u_four, cs_ref[...]).astype(BF16)


def _prep_tile(bi, x_ref, xp_ref, xn_ref, ctx_ref, ms_ref, g1_ref, win_ref, cw_ref,
               w0_ref, w2_ref, a0_ref, a2_ref, kk_ref, ka_ref, ones_ref, tri_ref,
               at_ref, bt_ref, kt_ref, rt_ref, v_ref, pl_ref):
    j = pl.program_id(1)
    n_j = pl.num_programs(1)
    is_ctx = j == 0
    shift = ms_ref[bi, 0, 0:1, :]
    scale = ms_ref[bi, 0, 1:2, :]
    g = g1_ref[...]

    def norm_mod(xr):
        return _rms(xr, g) * (1.0 + scale) + shift

    xin = jnp.where(is_ctx, ctx_ref[bi], x_ref[bi])
    h = norm_mod(xin).astype(BF16)
    u = jnp.dot(h, win_ref[...], preferred_element_type=F32)

    halo = jnp.concatenate([xp_ref[bi], xn_ref[bi]], axis=0)
    uh = jnp.dot(norm_mod(halo).astype(BF16), win_ref[:, RKV_LO:RKV_HI],
                 preferred_element_type=F32)
    yield
    prev_row = jnp.where(j >= 2, uh[7:8], 0.0)
    next_row = jnp.where(jnp.logical_and(j >= 1, j < n_j - 1), uh[8:9], 0.0)

    rkv = u[:, RKV_LO:RKV_HI]
    row = lax.broadcasted_iota(jnp.int32, rkv.shape, 0)
    u_m1 = jnp.where(row == 0, prev_row, pltpu.roll(rkv, 1, 0))
    u_p1 = jnp.where(row == TILE - 1, next_row, pltpu.roll(rkv, TILE - 1, 0))
    rkv = cw_ref[0:1, :] * u_m1 + cw_ref[1:2, :] * rkv + cw_ref[2:3, :] * u_p1
    r = rkv[:, 0:RWKV_W]
    k = rkv[:, RWKV_W:2 * RWKV_W]
    v = rkv[:, 2 * RWKV_W:3 * RWKV_W]

    wd = u[:, WD_LO:AD_LO]
    ad = u[:, AD_LO:GD_LO]
    gd = u[:, GD_LO:PROJ_W]
    w_lora = w0_ref[...] + _bdot(jnp.tanh(wd), w2_ref[...])
    lw = -math.exp(-0.5) * _sigmoid(w_lora)
    a = _sigmoid(a0_ref[...] + _bdot(ad, a2_ref[...]))

    ones_blk = ones_ref[...]
    kraw = k * kk_ref[...]
    kk = kraw * lax.rsqrt(_dot_exact_rhs(kraw * kraw, ones_blk) + KK_EPS)
    ka = ka_ref[...]
    kd = [k * (1.0 + (a[:, d * RWKV_W:(d + 1) * RWKV_W] - 1.0) * ka) for d in range(2)]
    yield

    v_ref[bi] = v.astype(BF16)
    for d in range(2):
        lwd = lw[:, d * RWKV_W:(d + 1) * RWKV_W]
        ad_ = a[:, d * RWKV_W:(d + 1) * RWKV_W]
        c = _dot_exact_lhs(tri_ref[d], lwd)
        e_pos = jnp.exp(c)
        e_neg = jnp.exp(-c)
        e_prev = jnp.exp(c - lwd)
        at_ref[bi, d] = (kk * e_prev).astype(BF16)
        bt_ref[bi, d] = (kk * ad_ * e_neg).astype(BF16)
        kt_ref[bi, d] = (kd[d] * e_neg).astype(BF16)
        rt_ref[bi, d] = (r * e_pos).astype(BF16)
        for q in range(CHUNKS_PER_TILE):
            last = q * CHUNK + (CHUNK - 1 if d == 0 else 0)
            pl_ref[bi, d, q] = e_pos[last:last + 1, :]
    return u[:, 0:FOURIER_W], r, v, gd, kd[0] + kd[1]


def _prep_call(x, ctx, modsel, norm1_g, w_in, conv_w, w0, w2bd, a0, a2bd, g2, k_k, k_a, r_k,
               cs, ones_blk, tri):
    b, s, d = x.shape
    n_j = 1 + s // TILE
    n_chunks = (CTX_LEN + s) // CHUNK
    t_all = CTX_LEN + s
    c2 = 2 * RWKV_W

    def full(arr):
        nd = arr.ndim
        return pl.BlockSpec(arr.shape, lambda bi, j, _n=nd: (0,) * _n)

    rows8 = TILE // 8
    nb = PREP_BATCH
    in_specs = [
        pl.BlockSpec((nb, TILE, d), lambda bi, j: (bi, jnp.maximum(j - 1, 0), 0)),
        pl.BlockSpec((nb, 8, d), lambda bi, j: (bi, jnp.maximum((j - 1) * rows8 - 1, 0), 0)),
        pl.BlockSpec((nb, 8, d), lambda bi, j: (bi, jnp.minimum(j * rows8, s // 8 - 1), 0)),
        pl.BlockSpec((nb, CTX_LEN, d), lambda bi, j: (bi, 0, 0)),
        pl.BlockSpec((nb, 1, 2, d), lambda bi, j: (bi, jnp.minimum(j, 1), 0, 0)),
        full(norm1_g), full(w_in), full(conv_w), full(w0), full(w2bd), full(a0), full(a2bd),
        full(g2), full(k_k), full(k_a), full(r_k), full(cs), full(ones_blk), full(tri),
    ]
    lat = lambda bi, j: (bi, jnp.maximum(j - 1, 0), 0)
    out_specs = [
        pl.BlockSpec((nb, 2, TILE, RWKV_W), lambda bi, j: (bi, 0, j, 0)),
        pl.BlockSpec((nb, 2, TILE, RWKV_W), lambda bi, j: (bi, 0, j, 0)),
        pl.BlockSpec((nb, 2, TILE, RWKV_W), lambda bi, j: (bi, 0, j, 0)),
        pl.BlockSpec((nb, 2, TILE, RWKV_W), lambda bi, j: (bi, 0, j, 0)),
        pl.BlockSpec((nb, TILE, RWKV_W), lambda bi, j: (bi, j, 0)),
        pl.BlockSpec((nb, 2, CHUNKS_PER_TILE, 1, RWKV_W), lambda bi, j: (bi, 0, j, 0, 0)),
        pl.BlockSpec((nb, TILE, RWKV_W), lat),
        pl.BlockSpec((nb, TILE, RWKV_W), lat),
        pl.BlockSpec((nb, TILE, 2 * FOURIER_W), lat),
    ]
    out_shape = [
        jax.ShapeDtypeStruct((b, 2, t_all, RWKV_W), BF16),
        jax.ShapeDtypeStruct((b, 2, t_all, RWKV_W), BF16),
        jax.ShapeDtypeStruct((b, 2, t_all, RWKV_W), BF16),
        jax.ShapeDtypeStruct((b, 2, t_all, RWKV_W), BF16),
        jax.ShapeDtypeStruct((b, t_all, RWKV_W), BF16),
        jax.ShapeDtypeStruct((b, 2, n_chunks, 1, RWKV_W), F32),
        jax.ShapeDtypeStruct((b, s, RWKV_W), F32),
        jax.ShapeDtypeStruct((b, s, RWKV_W), F32),
        jax.ShapeDtypeStruct((b, s, 2 * FOURIER_W), BF16),
    ]
    return pl.pallas_call(
        _prep_kernel,
        grid=(b // nb, n_j),
        in_specs=in_specs,
        out_specs=out_specs,
        out_shape=out_shape,
        compiler_params=pltpu.CompilerParams(dimension_semantics=("arbitrary", "arbitrary"),
                                             vmem_limit_bytes=VMEM_LIMIT),
        name="prep",
    )(x, x, x, ctx, modsel, norm1_g, w_in, conv_w, w0, w2bd, a0, a2bd, g2, k_k, k_a, r_k,
      cs, ones_blk, tri)


def _quad_chunk(a, bm, km, r, v, plrow, g_states, strict, incl, eye, lane_lo, mask_bd, eye_row,
                level_masks):
    n = PAIR

    def dup(xv):
        z = jnp.zeros_like(xv)
        return jnp.concatenate([jnp.where(lane_lo, xv, z), jnp.where(lane_lo, z, xv)], axis=0)

    def to_row(x2):
        return x2[:CHUNK] + x2[CHUNK:]

    def bdiag(xr):
        return jnp.where(mask_bd, jnp.concatenate([xr] * QUAD_HEADS, axis=0), jnp.zeros((), xr.dtype))

    pairs = []
    for p in range(2):
        sl = slice(p * n, (p + 1) * n)
        a2, b2, k2, r2, v2 = (dup(t[:, sl]) for t in (a, bm, km, r, v))
        rhs = jnp.concatenate([b2, k2], axis=0)
        s = _bdot_nt(jnp.concatenate([a2, r2], axis=0), rhs)
        pairs.append((sl, r2, v2, rhs, s))
    yield

    a_ab = jnp.concatenate([to_row(jnp.where(strict, s[:n, :n], 0.0)) for *_, s in pairs], axis=1)
    a_ak = jnp.concatenate([to_row(jnp.where(strict, s[:n, n:], 0.0)) for *_, s in pairs], axis=1)
    m_rbk = [jnp.concatenate([jnp.where(incl, s[n:, :n], 0.0), jnp.where(incl, s[n:, n:], 0.0)],
                             axis=1).astype(BF16) for *_, s in pairs]

    av = _bdot(a_ak, bdiag(v))
    a_lvl = [jnp.where(m, a_ab, 0.0).astype(BF16) for m in level_masks]
    t_row = jnp.where(eye_row, 1.0, 0.0) - jnp.where(level_masks[0], a_ab, 0.0)
    lvl = 1
    while lvl < len(a_lvl):
        paired = lvl + 1 < len(a_lvl)
        lhs = jnp.concatenate([a_lvl[lvl], a_lvl[lvl + 1]], axis=0) if paired else a_lvl[lvl]
        yc = jnp.dot(lhs, bdiag(t_row.astype(BF16)), preferred_element_type=F32)
        yield
        y_bd = bdiag(yc[:CHUNK].astype(BF16))
        if paired:
            both = jnp.dot(jnp.concatenate([t_row, yc[CHUNK:]], axis=0).astype(BF16), y_bd,
                           preferred_element_type=F32)
            yield
            t_row = t_row - both[:CHUNK]
            y_next = yc[CHUNK:] - both[CHUNK:]
            t_row = t_row - _bdot(t_row, bdiag(y_next.astype(BF16)))
            lvl += 2
        else:
            t_row = t_row - _bdot(t_row, y_bd)
            lvl += 1
        yield
    tx = _bdot(t_row, jnp.concatenate([bdiag(a), bdiag(av.astype(BF16))], axis=1))
    yield
    a_w = tx[:, :2 * n].astype(BF16)
    u_v = (-tx[:, 2 * n:]).astype(BF16)
    stage5 = []
    for (sl, r2, v2, rhs, _), m2 in zip(pairs, m_rbk):
        zmat = jnp.concatenate(
            [jnp.concatenate([dup(a_w[:, sl]), dup(u_v[:, sl])], axis=1),
             jnp.concatenate([jnp.zeros_like(v2), v2], axis=1)], axis=0)
        rhs_end = rhs.astype(F32) * plrow[:, sl]
        lhs = jnp.concatenate([rhs_end.T.astype(BF16), m2], axis=0)
        stage5.append(jnp.dot(lhs, zmat, preferred_element_type=F32))
    yield
    ys, g_new = [], []
    for (sl, r2, v2, rhs, _), o5, g_state in zip(pairs, stage5, g_states):
        phi = jnp.where(eye, plrow[:, sl], 0.0) - o5[:n, :n]
        r_y = r2.astype(F32) - o5[n:, :n]
        o6 = _bdot(jnp.concatenate([r_y, phi], axis=0), g_state)
        ys.append(to_row(o6[:n] + o5[n:, n:]))
        g_new.append(o6[n:] + o5[:n, n:])
    return jnp.concatenate(ys, axis=1), g_new


def _run_interleaved(gens, groups=1, lag=0):
    results = [None] * len(gens)
    active = list(range(len(gens)))
    rnd = 0
    while active:
        for i in list(active):
            if rnd < (i % groups) * lag:
                continue
            try:
                next(gens[i])
            except StopIteration as stop:
                results[i] = stop.value
                active.remove(i)
        rnd += 1
    return results


def _scan_kernel(atf, btf, ktf, rtf, vf, plf, atb, btb, ktb, rtb, vb, plb,
                 yf_ref, yb_ref, g_ref):
    s = pl.program_id(1)

    @pl.when(s == 0)
    def _():
        g_ref[...] = jnp.zeros_like(g_ref)

    ri = lax.broadcasted_iota(jnp.int32, (PAIR, PAIR), 0)
    ci = lax.broadcasted_iota(jnp.int32, (PAIR, PAIR), 1)
    same = (ri >= HEAD) == (ci >= HEAD)
    ti = ri & (HEAD - 1)
    tj = ci & (HEAD - 1)
    eye = ri == ci
    lane_lo = lax.broadcasted_iota(jnp.int32, (CHUNK, PAIR), 1) < HEAD
    masks = [(same & (tj < ti), same & (tj <= ti)),
             (same & (tj > ti), same & (tj >= ti))]
    quad_w = QUAD_HEADS * HEAD
    rq = lax.broadcasted_iota(jnp.int32, (quad_w, quad_w), 0)
    cq = lax.broadcasted_iota(jnp.int32, (quad_w, quad_w), 1)
    mask_bd = (rq // HEAD) == (cq // HEAD)
    t_i = lax.broadcasted_iota(jnp.int32, (CHUNK, quad_w), 0)
    s_i = lax.broadcasted_iota(jnp.int32, (CHUNK, quad_w), 1) & (HEAD - 1)
    eye_row = s_i == t_i
    level_masks = []
    for d in range(2):
        late, early = (t_i, s_i) if d == 0 else (s_i, t_i)
        level_masks.append([
            ((t_i >> (lvl + 1)) == (s_i >> (lvl + 1)))
            & (((late >> lvl) & 1) == 1) & (((early >> lvl) & 1) == 0)
            for lvl in range(int(math.log2(CHUNK)))])
    refs = [(atf, btf, ktf, rtf, vf, plf, yf_ref), (atb, btb, ktb, rtb, vb, plb, yb_ref)]
    gens, dests = [], []
    for bi in range(yf_ref.shape[0]):
        for d in range(2):
            at, bt, kt, rt, vv, plr, y_ref = refs[d]
            strict, incl = masks[d]
            for q in range(RWKV_W // quad_w):
                sl = slice(q * quad_w, (q + 1) * quad_w)
                gens.append(_quad_chunk(at[bi, 0, :, sl], bt[bi, 0, :, sl], kt[bi, 0, :, sl],
                                        rt[bi, 0, :, sl], vv[bi, :, sl], plr[bi, 0, 0, :, sl],
                                        [g_ref[bi, d, 2 * q], g_ref[bi, d, 2 * q + 1]],
                                        strict, incl, eye, lane_lo, mask_bd, eye_row,
                                        level_masks[d]))
                dests.append((y_ref, bi, d, q, sl))
    done = _run_interleaved(gens, groups=SCAN_GROUPS, lag=SCAN_LAG)
    for (y, g_new), (y_ref, bi, d, q, sl) in zip(done, dests):
        g_ref[bi, d, 2 * q] = g_new[0]
        g_ref[bi, d, 2 * q + 1] = g_new[1]
        y_ref[bi, :, sl] = y


def _scan_call(at, bt, kt, rt, v, plast, s_lat):
    b = at.shape[0]
    n_chunks = at.shape[2] // CHUNK
    n_ctx = CTX_LEN // CHUNK
    n_lat = s_lat // CHUNK

    def fwd_c(s):
        return s

    def bwd_c(s):
        return jnp.where(s < n_ctx, n_ctx - 1 - s, n_chunks + n_ctx - 1 - s)

    nb = SCAN_BATCH

    def dir_spec(d, cfun):
        return pl.BlockSpec((nb, 1, CHUNK, RWKV_W), lambda bi, s: (bi, d, cfun(s), 0))

    def v_spec(cfun):
        return pl.BlockSpec((nb, CHUNK, RWKV_W), lambda bi, s: (bi, cfun(s), 0))

    def pl_spec(d, cfun):
        return pl.BlockSpec((nb, 1, 1, 1, RWKV_W), lambda bi, s: (bi, d, cfun(s), 0, 0))

    in_specs = ([dir_spec(0, fwd_c)] * 4 + [v_spec(fwd_c), pl_spec(0, fwd_c)]
                + [dir_spec(1, bwd_c)] * 4 + [v_spec(bwd_c), pl_spec(1, bwd_c)])
    out_specs = [
        pl.BlockSpec((nb, CHUNK, RWKV_W), lambda bi, s: (bi, jnp.maximum(s - n_ctx, 0), 0)),
        pl.BlockSpec((nb, CHUNK, RWKV_W),
                     lambda bi, s: (bi, jnp.minimum(n_chunks - 1 - s, n_lat - 1), 0)),
    ]
    out_shape = [jax.ShapeDtypeStruct((b, s_lat, RWKV_W), F32)] * 2
    return pl.pallas_call(
        _scan_kernel,
        grid=(b // nb, n_chunks),
        in_specs=in_specs,
        out_specs=out_specs,
        out_shape=out_shape,
        scratch_shapes=[pltpu.VMEM((nb, 2, N_PAIRS, PAIR, PAIR), F32)],
        compiler_params=pltpu.CompilerParams(dimension_semantics=("arbitrary", "arbitrary"),
                                             vmem_limit_bytes=VMEM_LIMIT),
        name="scan",
    )(at, bt, kt, rt, v, plast, at, bt, kt, rt, v, plast)


def _mix_kernel(dft_ref, rev_ref, z_ref, yf_ref, yb_ref, bonus_ref, gate_ref, x_ref, mod_ref,
                gng_ref, gnb_ref, fg_ref, wout_ref, n2g_ref, ones_ref, x1_ref, h2_ref, zf_ref,
                *, fscale):
    seq = z_ref.shape[1]
    half = seq // 2
    rt = rev_ref.shape[0]

    @pl.when(pl.program_id(1) == 0)
    def _():
        mid = z_ref[0, half:half + 1, :FOURIER_W].astype(F32)
        row0 = lax.broadcasted_iota(jnp.int32, (rt, FOURIER_W), 0) == 0
        for part, sign in ((0, 1.0), (1, -1.0)):
            lanes = slice(part * FOURIER_W, (part + 1) * FOURIER_W)
            for r in range(half // rt):
                lo = seq - rt * (r + 1)
                nxt = (z_ref[0, lo + rt:lo + 2 * rt, lanes] if r > 0
                       else jnp.zeros((rt, FOURIER_W), BF16))
                rev = jnp.dot(rev_ref[...], jnp.concatenate([z_ref[0, lo:lo + rt, lanes], nxt], axis=0),
                              preferred_element_type=F32)
                folded = z_ref[0, r * rt:(r + 1) * rt, lanes].astype(F32) + sign * rev
                if part == 1 and r == 0:
                    folded = jnp.where(row0, mid, folded)
                zf_ref[part * half + r * rt:part * half + (r + 1) * rt, :] = folded.astype(BF16)

    yfour = jnp.dot(dft_ref[...], zf_ref[...], preferred_element_type=F32)
    four = _rms(yfour * fscale, fg_ref[...])

    ones_blk = ones_ref[...]
    y = yf_ref[0] + yb_ref[0]
    mu = _dot_exact_rhs(y, ones_blk) * (1.0 / HEAD)
    dy = y - mu
    var = _dot_exact_rhs(dy * dy, ones_blk) * (1.0 / HEAD)
    yn = dy * lax.rsqrt(var + GN_EPS) * gng_ref[...] + gnb_ref[...]
    rw = (yn + bonus_ref[0]) * gate_ref[0]

    out = _bdot(four, wout_ref[:FOURIER_W, :]) + _bdot(rw, wout_ref[FOURIER_W:, :])
    x1 = x_ref[0] + mod_ref[0, 2:3, :] * out
    x1_ref[0] = x1
    h2 = _rms(x1, n2g_ref[...]) * (1.0 + mod_ref[0, 4:5, :]) + mod_ref[0, 3:4, :]
    h2_ref[0] = h2.astype(BF16)


def _mix_call(dft, rev, z, yf, yb, bonus, gate, x, modx, gn_g, gn_b, f_g, w_out, norm2_g,
              ones_blk):
    b, s, d = x.shape
    tm = MIX_TILE

    def full(arr):
        nd = arr.ndim
        return pl.BlockSpec(arr.shape, lambda bi, m, _n=nd: (0,) * _n)

    tok = lambda w: pl.BlockSpec((1, tm, w), lambda bi, m: (bi, m, 0))
    in_specs = [
        pl.BlockSpec((tm, s), lambda bi, m: (m, 0)),
        full(rev),
        pl.BlockSpec((1, s, 2 * FOURIER_W), lambda bi, m: (bi, 0, 0)),
        tok(RWKV_W), tok(RWKV_W), tok(RWKV_W), tok(RWKV_W), tok(d),
        pl.BlockSpec((1, 6, d), lambda bi, m: (bi, 0, 0)),
        full(gn_g), full(gn_b), full(f_g), full(w_out), full(norm2_g), full(ones_blk),
    ]
    fscale = 1.0 / math.sqrt(s * (FOURIER_W // FOURIER_GROUPS))
    return pl.pallas_call(
        functools.partial(_mix_kernel, fscale=fscale),
        grid=(b, s // tm),
        in_specs=in_specs,
        out_specs=[tok(d), tok(d)],
        out_shape=[jax.ShapeDtypeStruct((b, s, d), F32), jax.ShapeDtypeStruct((b, s, d), BF16)],
        scratch_shapes=[pltpu.VMEM((s, FOURIER_W), BF16)],
        compiler_params=pltpu.CompilerParams(dimension_semantics=("arbitrary", "arbitrary"),
                                             vmem_limit_bytes=VMEM_LIMIT),
        name="mix",
    )(dft, rev, z, yf, yb, bonus, gate, x, modx, gn_g, gn_b, f_g, w_out, norm2_g, ones_blk)


def _ffn_up_kernel(h2_ref, wg_ref, wv_ref, cwg_ref, cwv_ref, cbg_ref, cbv_ref, act_ref):
    t = h2_ref.shape[1]
    rows = FFN_ROWS
    n_blk = t // rows
    w = jnp.concatenate([wg_ref[...], wv_ref[...]], axis=1)
    cw = jnp.concatenate([cwg_ref[...], cwv_ref[...]], axis=1).astype(BF16)
    cb = jnp.concatenate([cbg_ref[...], cbv_ref[...]], axis=1).astype(BF16)
    col = lax.broadcasted_iota(jnp.int32, (rows, 2 * FF_TILE), 0) & (GRID_W - 1)
    has_left = col > 0
    has_right = col < GRID_W - 1
    zeros = jnp.zeros((GRID_W, 2 * FF_TILE), BF16)

    def row_mixes(i):
        u = jnp.dot(h2_ref[0, i * rows:(i + 1) * rows, :], w, preferred_element_type=F32)
        u_l = jnp.where(has_left, pltpu.roll(u, 1, 0), 0.0).astype(BF16)
        u_r = jnp.where(has_right, pltpu.roll(u, rows - 1, 0), 0.0).astype(BF16)
        u_c = u.astype(BF16)
        return [cw[3 * kh:3 * kh + 1] * u_l + cw[3 * kh + 1:3 * kh + 2] * u_c
                + cw[3 * kh + 2:3 * kh + 3] * u_r for kh in range(3)]

    def finish(i, prev, cur, nxt):
        above = zeros if prev is None else prev[0][rows - GRID_W:]
        below = zeros if nxt is None else nxt[2][:GRID_W]
        c = (cur[1] + jnp.concatenate([above, cur[0][:rows - GRID_W]], axis=0)
             + jnp.concatenate([cur[2][GRID_W:], below], axis=0) + cb)
        cg = c[:, :FF_TILE]
        act_ref[0, i * rows:(i + 1) * rows, :] = cg * _sigmoid(cg) * c[:, FF_TILE:]

    mixes = [None] * (n_blk + 1)
    for i in range(n_blk):
        mixes[i] = row_mixes(i)
        if i >= 1:
            finish(i - 1, mixes[i - 2] if i >= 2 else None, mixes[i - 1], mixes[i])
    finish(n_blk - 1, mixes[n_blk - 2] if n_blk >= 2 else None, mixes[n_blk - 1], None)


def _ffn_up_call(h2, w_up, conv_w9, conv_b):
    b, s, d = h2.shape
    n_f = D_FF // FF_TILE
    in_specs = [
        pl.BlockSpec((1, s, d), lambda bi, f: (bi, 0, 0)),
        pl.BlockSpec((d, FF_TILE), lambda bi, f: (0, f)),
        pl.BlockSpec((d, FF_TILE), lambda bi, f: (0, n_f + f)),
        pl.BlockSpec((9, FF_TILE), lambda bi, f: (0, f)),
        pl.BlockSpec((9, FF_TILE), lambda bi, f: (0, n_f + f)),
        pl.BlockSpec((1, FF_TILE), lambda bi, f: (0, f)),
        pl.BlockSpec((1, FF_TILE), lambda bi, f: (0, n_f + f)),
    ]
    return pl.pallas_call(
        _ffn_up_kernel,
        grid=(b, n_f),
        in_specs=in_specs,
        out_specs=pl.BlockSpec((1, s, FF_TILE), lambda bi, f: (bi, 0, f)),
        out_shape=jax.ShapeDtypeStruct((b, s, D_FF), BF16),
        compiler_params=pltpu.CompilerParams(dimension_semantics=("arbitrary", "arbitrary"),
                                             vmem_limit_bytes=VMEM_LIMIT),
        name="ffn_up",
    )(h2, w_up, w_up, conv_w9, conv_w9, conv_b, conv_b)


def _ffn_down_kernel(act_ref, wd_ref, x1_ref, mod_ref, fg_ref, out_ref):
    y = jnp.dot(act_ref[0], wd_ref[...], preferred_element_type=F32)
    out_ref[0] = _rms(x1_ref[0] + mod_ref[0, 5:6, :] * y, fg_ref[...])


def _ffn_down_call(act, w_down, x1, modx, final_g):
    b, s, d = x1.shape
    tm = DOWN_TILE
    in_specs = [
        pl.BlockSpec((1, tm, D_FF), lambda bi, m: (bi, m, 0)),
        pl.BlockSpec((D_FF, d), lambda bi, m: (0, 0)),
        pl.BlockSpec((1, tm, d), lambda bi, m: (bi, m, 0)),
        pl.BlockSpec((1, 6, d), lambda bi, m: (bi, 0, 0)),
        pl.BlockSpec((1, d), lambda bi, m: (0, 0)),
    ]
    return pl.pallas_call(
        _ffn_down_kernel,
        grid=(b, s // tm),
        in_specs=in_specs,
        out_specs=pl.BlockSpec((1, tm, d), lambda bi, m: (bi, m, 0)),
        out_shape=jax.ShapeDtypeStruct((b, s, d), F32),
        compiler_params=pltpu.CompilerParams(dimension_semantics=("arbitrary", "arbitrary"),
                                             vmem_limit_bytes=VMEM_LIMIT),
        name="ffn_down",
    )(act, w_down, x1, modx, final_g)


@functools.lru_cache(maxsize=None)
def _constants(seq):
    gw = FOURIER_W // FOURIER_GROUPS
    nm = np.outer(np.arange(gw), np.arange(gw)) % gw
    ang = 2.0 * np.pi * nm / gw
    cs = np.zeros((FOURIER_W, 2 * FOURIER_W), np.float64)
    for gi in range(FOURIER_GROUPS):
        sl = slice(gi * gw, (gi + 1) * gw)
        cs[sl, sl] = np.cos(ang)
        cs[sl, FOURIER_W + gi * gw:FOURIER_W + (gi + 1) * gw] = np.sin(ang)
    half = seq // 2
    k_idx = np.arange(seq, dtype=np.int64)[:, None]
    ang_c = 2.0 * np.pi * ((k_idx * np.arange(0, half + 1, dtype=np.int64)[None, :]) % seq) / seq
    ang_s = 2.0 * np.pi * ((k_idx * np.arange(1, half, dtype=np.int64)[None, :]) % seq) / seq
    dft = np.concatenate([np.cos(ang_c), -np.sin(ang_s)], axis=1)
    ri = np.arange(REV_TILE)
    rev = (np.arange(2 * REV_TILE)[None, :] == (REV_TILE - ri)[:, None]).astype(np.float64)
    head_id = np.arange(RWKV_W) // HEAD
    ones_blk = (head_id[:, None] == head_id[None, :]).astype(np.float64)
    ti = np.arange(TILE)
    same_chunk = (ti[:, None] // CHUNK) == (ti[None, :] // CHUNK)
    tri = np.stack([same_chunk & (ti[None, :] <= ti[:, None]),
                    same_chunk & (ti[None, :] >= ti[:, None])]).astype(np.float64)
    to_bf16 = lambda arr: np.asarray(arr, dtype=np.float32).astype(BF16)
    return (np.asarray(cs, dtype=np.float32), np.asarray(dft, dtype=np.float32),
            to_bf16(ones_blk), to_bf16(tri), to_bf16(rev))


def _block_diag_dirs(w):
    z = jnp.zeros_like(w[0])
    return jnp.concatenate([jnp.concatenate([w[0], z], axis=1),
                            jnp.concatenate([z, w[1]], axis=1)], axis=0)


def kernel(x, c, ctx, c_ctx, ada_w, ada_b, norm1_g, norm2_g, w_in, rwkv_conv_w, decay_w0, decay_w2, iclr_a0, iclr_a2, gate_g2, k_k, k_a, r_k, gn_g, gn_b, fourier_g, w_out, ffn_w_up, ffn_conv_w, ffn_conv_b, ffn_w_down, final_g):
    b, s, d = x.shape
    assert ada_w.shape[0] == 1, "single-layer configuration"
    assert (b, s, d) == (c.shape[0], s, D_MODEL) and ctx.shape == (b, CTX_LEN, d)
    assert s % TILE == 0 and CTX_LEN == TILE
    cs, dft, ones_blk, tri, rev = (jnp.asarray(t) for t in _constants(s))
    cs, dft = cs.astype(BF16), dft.astype(BF16)

    cc = jnp.concatenate([c, c_ctx[None, :], jnp.zeros((16 - b - 1, d), F32)], axis=0)
    mod = _mod_call(cc, ada_w[0], ada_b[0][None, :])
    modx = mod[:b].reshape(b, 6, d)
    modc = jnp.broadcast_to(mod[b].reshape(1, 6, d)[:, :2], (b, 2, d))
    modsel = jnp.stack([modc, modx[:, :2]], axis=1)

    row = lambda t: t.reshape(1, -1)
    at, bt, kt, rt, v, plast, bonus, gate, z = _prep_call(
        x, ctx, modsel, row(norm1_g[0]), w_in[0].astype(BF16), rwkv_conv_w[0],
        row(decay_w0[0]), _block_diag_dirs(decay_w2[0]).astype(BF16),
        row(iclr_a0[0]), _block_diag_dirs(iclr_a2[0]).astype(BF16),
        gate_g2[0].astype(BF16), row(k_k[0]), row(k_a[0]), row(r_k[0]), cs, ones_blk, tri)

    yf, yb = _scan_call(at, bt, kt, rt, v, plast, s)

    x1, h2 = _mix_call(dft, rev, z, yf, yb, bonus, gate, x, modx, row(gn_g[0]), row(gn_b[0]),
                       row(fourier_g[0]), w_out[0].astype(BF16), row(norm2_g[0]), ones_blk)

    act = _ffn_up_call(h2, ffn_w_up[0].astype(BF16), ffn_conv_w[0].reshape(9, 2 * D_FF),
                       row(ffn_conv_b[0]))
    return _ffn_down_call(act, ffn_w_down[0].astype(BF16), x1, modx, row(final_g))
```

```python
import functools
import math

import numpy as np
import jax
import jax.numpy as jnp
from jax import lax
from jax.experimental import pallas as pl
from jax.experimental.pallas import tpu as pltpu

F32 = jnp.float32
BF16 = jnp.bfloat16

D_MODEL = 1024
CTX_LEN = 256
GRID_W = 64
FOURIER_W = 512
FOURIER_GROUPS = 8
RWKV_W = 512
HEAD = 64
N_HEADS = 8
DECAY_RANK = 64
ICLR_RANK = 64
GATE_RANK = 128
D_FF = 2816
NORM_EPS = 1e-6
GN_EPS = 64e-5
KK_EPS = 1e-12

CHUNK = 64
TILE = 256
MIX_TILE = 512
REV_TILE = 256
CHUNKS_PER_TILE = TILE // CHUNK
PREP_BATCH = 2
PAIR = 2 * HEAD
N_PAIRS = RWKV_W // PAIR
QUAD_HEADS = 4
SCAN_BATCH = 8
SCAN_GROUPS = 4
SCAN_LAG = 3
MOD_TILE = 1536
FF_TILE = 1408
FFN_ROWS = 256
DOWN_TILE = 1024
VMEM_LIMIT = 56 * 1024 * 1024
HEAD_SUM_TERMS = 1
CUMSUM_TERMS = 2
MOD_WEIGHT_TERMS = 2

RKV_LO = FOURIER_W
RKV_HI = FOURIER_W + 3 * RWKV_W
WD_LO = RKV_HI
AD_LO = WD_LO + 2 * DECAY_RANK
GD_LO = AD_LO + 2 * ICLR_RANK
PROJ_W = GD_LO + GATE_RANK


def _bdot(a, b):
    return jnp.dot(a.astype(BF16), b.astype(BF16), preferred_element_type=F32)


def _bdot_nt(a, b):
    return lax.dot_general(a.astype(BF16), b.astype(BF16), (((1,), (1,)), ((), ())),
                           preferred_element_type=F32)


def _bdot_tn(a, b):
    return lax.dot_general(a.astype(BF16), b.astype(BF16), (((0,), (0,)), ((), ())),
                           preferred_element_type=F32)


def _split_terms(a, terms):
    out, rem = [], a
    for _ in range(terms):
        hi = rem.astype(BF16)
        out.append(hi)
        rem = rem - hi.astype(F32)
    return out


def _dot_exact_rhs(a, b_exact, terms=HEAD_SUM_TERMS):
    acc = None
    for piece in _split_terms(a, terms):
        t = jnp.dot(piece, b_exact, preferred_element_type=F32)
        acc = t if acc is None else acc + t
    return acc


def _dot_exact_lhs(a_exact, b, terms=CUMSUM_TERMS):
    acc = None
    for piece in _split_terms(b, terms):
        t = jnp.dot(a_exact, piece, preferred_element_type=F32)
        acc = t if acc is None else acc + t
    return acc


def _sigmoid(x):
    return 1.0 / (1.0 + jnp.exp(-x))


def _rms(x, g):
    return x * lax.rsqrt(jnp.mean(x * x, axis=-1, keepdims=True) + NORM_EPS) * g


def _mod_kernel(c_ref, w_ref, b_ref, o_ref):
    cc = c_ref[...]
    s = cc * _sigmoid(cc)
    rows = s.shape[0]
    s_parts = jnp.concatenate(_split_terms(s, 3), axis=0)
    acc = b_ref[...]
    for wp in _split_terms(w_ref[...], MOD_WEIGHT_TERMS):
        t = jnp.dot(s_parts, wp, preferred_element_type=F32)
        acc = acc + (t[:rows] + t[rows:2 * rows] + t[2 * rows:])
    o_ref[...] = acc


def _mod_call(cc, ada_w, ada_b):
    rows, d = cc.shape
    n = ada_w.shape[1]
    tn = MOD_TILE
    return pl.pallas_call(
        _mod_kernel,
        grid=(n // tn,),
        in_specs=[pl.BlockSpec((rows, d), lambda i: (0, 0)),
                  pl.BlockSpec((d, tn), lambda i: (0, i)),
                  pl.BlockSpec((1, tn), lambda i: (0, i))],
        out_specs=pl.BlockSpec((rows, tn), lambda i: (0, i)),
        out_shape=jax.ShapeDtypeStruct((rows, n), F32),
        compiler_params=pltpu.CompilerParams(dimension_semantics=("arbitrary",),
                                             vmem_limit_bytes=VMEM_LIMIT),
        name="mod",
    )(cc, ada_w, ada_b)


def _prep_kernel(x_ref, xp_ref, xn_ref, ctx_ref, ms_ref, g1_ref, win_ref, cw_ref,
                 w0_ref, w2_ref, a0_ref, a2_ref, g2_ref, kk_ref, ka_ref, rk_ref,
                 cs_ref, ones_ref, tri_ref,
                 at_ref, bt_ref, kt_ref, rt_ref, v_ref, pl_ref, bonus_ref, gate_ref, z_ref):
    j = pl.program_id(1)
    refs = (x_ref, xp_ref, xn_ref, ctx_ref, ms_ref, g1_ref, win_ref, cw_ref,
            w0_ref, w2_ref, a0_ref, a2_ref, kk_ref, ka_ref, ones_ref, tri_ref,
            at_ref, bt_ref, kt_ref, rt_ref, v_ref, pl_ref)
    tiles = _run_interleaved([_prep_tile(bi, *refs) for bi in range(x_ref.shape[0])],
                             groups=PREP_BATCH, lag=1)

    @pl.when(j >= 1)
    def _():
        ones_blk = ones_ref[...]
        for bi, (u_four, r, v, gd, kd_sum) in enumerate(tiles):
            bs = _dot_exact_rhs(r * rk_ref[...] * kd_sum, ones_blk)
            bonus_ref[bi] = bs * v
            gate_ref[bi] = _bdot(_sigmoid(gd), g2_ref[...])
            z_ref[bi] = _bdot(u_four, cs_ref[...]).astype(BF16)


def _prep_tile(bi, x_ref, xp_ref, xn_ref, ctx_ref, ms_ref, g1_ref, win_ref, cw_ref,
               w0_ref, w2_ref, a0_ref, a2_ref, kk_ref, ka_ref, ones_ref, tri_ref,
               at_ref, bt_ref, kt_ref, rt_ref, v_ref, pl_ref):
    j = pl.program_id(1)
    n_j = pl.num_programs(1)
    is_ctx = j == 0
    shift = ms_ref[bi, 0, 0:1, :]
    scale = ms_ref[bi, 0, 1:2, :]
    g = g1_ref[...]

    def norm_mod(xr):
        return _rms(xr, g) * (1.0 + scale) + shift

    xin = jnp.where(is_ctx, ctx_ref[bi], x_ref[bi])
    h = norm_mod(xin).astype(BF16)
    u = jnp.dot(h, win_ref[...], preferred_element_type=F32)

    halo = jnp.concatenate([xp_ref[bi], xn_ref[bi]], axis=0)
    uh = jnp.dot(norm_mod(halo).astype(BF16), win_ref[:, RKV_LO:RKV_HI],
                 preferred_element_type=F32)
    yield
    prev_row = jnp.where(j >= 2, uh[7:8], 0.0)
    next_row = jnp.where(jnp.logical_and(j >= 1, j < n_j - 1), uh[8:9], 0.0)

    rkv = u[:, RKV_LO:RKV_HI]
    row = lax.broadcasted_iota(jnp.int32, rkv.shape, 0)
    u_m1 = jnp.where(row == 0, prev_row, pltpu.roll(rkv, 1, 0))
    u_p1 = jnp.where(row == TILE - 1, next_row, pltpu.roll(rkv, TILE - 1, 0))
    rkv = cw_ref[0:1, :] * u_m1 + cw_ref[1:2, :] * rkv + cw_ref[2:3, :] * u_p1
    r = rkv[:, 0:RWKV_W]
    k = rkv[:, RWKV_W:2 * RWKV_W]
    v = rkv[:, 2 * RWKV_W:3 * RWKV_W]
    yield

    wd = u[:, WD_LO:AD_LO]
    ad = u[:, AD_LO:GD_LO]
    gd = u[:, GD_LO:PROJ_W]
    w_lora = w0_ref[...] + _bdot(jnp.tanh(wd), w2_ref[...])
    lw = -math.exp(-0.5) * _sigmoid(w_lora)
    a = _sigmoid(a0_ref[...] + _bdot(ad, a2_ref[...]))

    ones_blk = ones_ref[...]
    kraw = k * kk_ref[...]
    kk = kraw * lax.rsqrt(_dot_exact_rhs(kraw * kraw, ones_blk) + KK_EPS)
    ka = ka_ref[...]
    kd = [k * (1.0 + (a[:, d * RWKV_W:(d + 1) * RWKV_W] - 1.0) * ka) for d in range(2)]
    yield

    v_ref[bi] = v.astype(BF16)
    for d in range(2):
        lwd = lw[:, d * RWKV_W:(d + 1) * RWKV_W]
        ad_ = a[:, d * RWKV_W:(d + 1) * RWKV_W]
        c = _dot_exact_lhs(tri_ref[d], lwd)
        e_pos = jnp.exp(c)
        e_neg = jnp.exp(-c)
        e_prev = jnp.exp(c - lwd)
        at_ref[bi, d] = (kk * e_prev).astype(BF16)
        bt_ref[bi, d] = (kk * ad_ * e_neg).astype(BF16)
        kt_ref[bi, d] = (kd[d] * e_neg).astype(BF16)
        rt_ref[bi, d] = (r * e_pos).astype(BF16)
        for q in range(CHUNKS_PER_TILE):
            last = q * CHUNK + (CHUNK - 1 if d == 0 else 0)
            pl_ref[bi, d, q] = e_pos[last:last + 1, :]
    return u[:, 0:FOURIER_W], r, v, gd, kd[0] + kd[1]


def _prep_call(x, ctx, modsel, norm1_g, w_in, conv_w, w0, w2bd, a0, a2bd, g2, k_k, k_a, r_k,
               cs, ones_blk, tri):
    b, s, d = x.shape
    n_j = 1 + s // TILE
    n_chunks = (CTX_LEN + s) // CHUNK
    t_all = CTX_LEN + s
    c2 = 2 * RWKV_W

    def full(arr):
        nd = arr.ndim
        return pl.BlockSpec(arr.shape, lambda bi, j, _n=nd: (0,) * _n)

    rows8 = TILE // 8
    nb = PREP_BATCH
    in_specs = [
        pl.BlockSpec((nb, TILE, d), lambda bi, j: (bi, jnp.maximum(j - 1, 0), 0)),
        pl.BlockSpec((nb, 8, d), lambda bi, j: (bi, jnp.maximum((j - 1) * rows8 - 1, 0), 0)),
        pl.BlockSpec((nb, 8, d), lambda bi, j: (bi, jnp.minimum(j * rows8, s // 8 - 1), 0)),
        pl.BlockSpec((nb, CTX_LEN, d), lambda bi, j: (bi, 0, 0)),
        pl.BlockSpec((nb, 1, 2, d), lambda bi, j: (bi, jnp.minimum(j, 1), 0, 0)),
        full(norm1_g), full(w_in), full(conv_w), full(w0), full(w2bd), full(a0), full(a2bd),
        full(g2), full(k_k), full(k_a), full(r_k), full(cs), full(ones_blk), full(tri),
    ]
    lat = lambda bi, j: (bi, jnp.maximum(j - 1, 0), 0)
    out_specs = [
        pl.BlockSpec((nb, 2, TILE, RWKV_W), lambda bi, j: (bi, 0, j, 0)),
        pl.BlockSpec((nb, 2, TILE, RWKV_W), lambda bi, j: (bi, 0, j, 0)),
        pl.BlockSpec((nb, 2, TILE, RWKV_W), lambda bi, j: (bi, 0, j, 0)),
        pl.BlockSpec((nb, 2, TILE, RWKV_W), lambda bi, j: (bi, 0, j, 0)),
        pl.BlockSpec((nb, TILE, RWKV_W), lambda bi, j: (bi, j, 0)),
        pl.BlockSpec((nb, 2, CHUNKS_PER_TILE, 1, RWKV_W), lambda bi, j: (bi, 0, j, 0, 0)),
        pl.BlockSpec((nb, TILE, RWKV_W), lat),
        pl.BlockSpec((nb, TILE, RWKV_W), lat),
        pl.BlockSpec((nb, TILE, 2 * FOURIER_W), lat),
    ]
    out_shape = [
        jax.ShapeDtypeStruct((b, 2, t_all, RWKV_W), BF16),
        jax.ShapeDtypeStruct((b, 2, t_all, RWKV_W), BF16),
        jax.ShapeDtypeStruct((b, 2, t_all, RWKV_W), BF16),
        jax.ShapeDtypeStruct((b, 2, t_all, RWKV_W), BF16),
        jax.ShapeDtypeStruct((b, t_all, RWKV_W), BF16),
        jax.ShapeDtypeStruct((b, 2, n_chunks, 1, RWKV_W), F32),
        jax.ShapeDtypeStruct((b, s, RWKV_W), F32),
        jax.ShapeDtypeStruct((b, s, RWKV_W), F32),
        jax.ShapeDtypeStruct((b, s, 2 * FOURIER_W), BF16),
    ]
    return pl.pallas_call(
        _prep_kernel,
        grid=(b // nb, n_j),
        in_specs=in_specs,
        out_specs=out_specs,
        out_shape=out_shape,
        compiler_params=pltpu.CompilerParams(dimension_semantics=("arbitrary", "arbitrary"),
                                             vmem_limit_bytes=VMEM_LIMIT),
        name="prep",
    )(x, x, x, ctx, modsel, norm1_g, w_in, conv_w, w0, w2bd, a0, a2bd, g2, k_k, k_a, r_k,
      cs, ones_blk, tri)


def _quad_chunk(a, bm, km, r, v, plrow, g_states, strict, incl, eye, lane_lo, mask_bd, eye_row,
                level_masks):
    n = PAIR

    def dup(xv):
        z = jnp.zeros_like(xv)
        return jnp.concatenate([jnp.where(lane_lo, xv, z), jnp.where(lane_lo, z, xv)], axis=0)

    def to_row(x2):
        return x2[:CHUNK] + x2[CHUNK:]

    def bdiag(xr):
        return jnp.where(mask_bd, jnp.concatenate([xr] * QUAD_HEADS, axis=0), jnp.zeros((), xr.dtype))

    pairs = []
    for p in range(2):
        sl = slice(p * n, (p + 1) * n)
        a2, b2, k2, r2, v2 = (dup(t[:, sl]) for t in (a, bm, km, r, v))
        rhs = jnp.concatenate([b2, k2], axis=0)
        s = _bdot_nt(jnp.concatenate([a2, r2], axis=0), rhs)
        pairs.append((sl, r2, v2, rhs, s))
    yield

    a_ab = jnp.concatenate([to_row(jnp.where(strict, s[:n, :n], 0.0)) for *_, s in pairs], axis=1)
    a_ak = jnp.concatenate([to_row(jnp.where(strict, s[:n, n:], 0.0)) for *_, s in pairs], axis=1)
    m_rbk = [jnp.concatenate([jnp.where(incl, s[n:, :n], 0.0), jnp.where(incl, s[n:, n:], 0.0)],
                             axis=1).astype(BF16) for *_, s in pairs]

    av = _bdot(a_ak, bdiag(v))
    a_lvl = [jnp.where(m, a_ab, 0.0).astype(BF16) for m in level_masks]
    t_row = jnp.where(eye_row, 1.0, 0.0) - jnp.where(level_masks[0], a_ab, 0.0)
    lvl = 1
    while lvl < len(a_lvl):
        paired = lvl + 1 < len(a_lvl)
        lhs = jnp.concatenate([a_lvl[lvl], a_lvl[lvl + 1]], axis=0) if paired else a_lvl[lvl]
        yc = jnp.dot(lhs, bdiag(t_row.astype(BF16)), preferred_element_type=F32)
        yield
        y_bd = bdiag(yc[:CHUNK].astype(BF16))
        if paired:
            both = jnp.dot(jnp.concatenate([t_row, yc[CHUNK:]], axis=0).astype(BF16), y_bd,
                           preferred_element_type=F32)
            yield
            t_row = t_row - both[:CHUNK]
            y_next = yc[CHUNK:] - both[CHUNK:]
            t_row = t_row - _bdot(t_row, bdiag(y_next.astype(BF16)))
            lvl += 2
        else:
            t_row = t_row - _bdot(t_row, y_bd)
            lvl += 1
        yield
    tx = _bdot(t_row, jnp.concatenate([bdiag(a), bdiag(av.astype(BF16))], axis=1))
    yield
    a_w = tx[:, :2 * n].astype(BF16)
    u_v = (-tx[:, 2 * n:]).astype(BF16)
    stage5 = []
    for (sl, r2, v2, rhs, _), m2 in zip(pairs, m_rbk):
        zmat = jnp.concatenate(
            [jnp.concatenate([dup(a_w[:, sl]), dup(u_v[:, sl])], axis=1),
             jnp.concatenate([jnp.zeros_like(v2), v2], axis=1)], axis=0)
        rhs_end = rhs.astype(F32) * plrow[:, sl]
        lhs = jnp.concatenate([rhs_end.T.astype(BF16), m2], axis=0)
        stage5.append(jnp.dot(lhs, zmat, preferred_element_type=F32))
    yield
    ys, g_new = [], []
    for (sl, r2, v2, rhs, _), o5, g_state in zip(pairs, stage5, g_states):
        phi = jnp.where(eye, plrow[:, sl], 0.0) - o5[:n, :n]
        r_y = r2.astype(F32) - o5[n:, :n]
        o6 = _bdot(jnp.concatenate([r_y, phi], axis=0), g_state)
        ys.append(to_row(o6[:n] + o5[n:, n:]))
        g_new.append(o6[n:] + o5[:n, n:])
    return jnp.concatenate(ys, axis=1), g_new


def _run_interleaved(gens, groups=1, lag=0):
    results = [None] * len(gens)
    active = list(range(len(gens)))
    rnd = 0
    while active:
        for i in list(active):
            if rnd < (i % groups) * lag:
                continue
            try:
                next(gens[i])
            except StopIteration as stop:
                results[i] = stop.value
                active.remove(i)
        rnd += 1
    return results


def _scan_kernel(atf, btf, ktf, rtf, vf, plf, atb, btb, ktb, rtb, vb, plb,
                 yf_ref, yb_ref, g_ref):
    s = pl.program_id(1)

    @pl.when(s == 0)
    def _():
        g_ref[...] = jnp.zeros_like(g_ref)

    ri = lax.broadcasted_iota(jnp.int32, (PAIR, PAIR), 0)
    ci = lax.broadcasted_iota(jnp.int32, (PAIR, PAIR), 1)
    same = (ri >= HEAD) == (ci >= HEAD)
    ti = ri & (HEAD - 1)
    tj = ci & (HEAD - 1)
    eye = ri == ci
    lane_lo = lax.broadcasted_iota(jnp.int32, (CHUNK, PAIR), 1) < HEAD
    masks = [(same & (tj < ti), same & (tj <= ti)),
             (same & (tj > ti), same & (tj >= ti))]
    quad_w = QUAD_HEADS * HEAD
    rq = lax.broadcasted_iota(jnp.int32, (quad_w, quad_w), 0)
    cq = lax.broadcasted_iota(jnp.int32, (quad_w, quad_w), 1)
    mask_bd = (rq // HEAD) == (cq // HEAD)
    t_i = lax.broadcasted_iota(jnp.int32, (CHUNK, quad_w), 0)
    s_i = lax.broadcasted_iota(jnp.int32, (CHUNK, quad_w), 1) & (HEAD - 1)
    eye_row = s_i == t_i
    level_masks = []
    for d in range(2):
        late, early = (t_i, s_i) if d == 0 else (s_i, t_i)
        level_masks.append([
            ((t_i >> (lvl + 1)) == (s_i >> (lvl + 1)))
            & (((late >> lvl) & 1) == 1) & (((early >> lvl) & 1) == 0)
            for lvl in range(int(math.log2(CHUNK)))])
    refs = [(atf, btf, ktf, rtf, vf, plf, yf_ref), (atb, btb, ktb, rtb, vb, plb, yb_ref)]
    gens, dests = [], []
    for bi in range(yf_ref.shape[0]):
        for d in range(2):
            at, bt, kt, rt, vv, plr, y_ref = refs[d]
            strict, incl = masks[d]
            for q in range(RWKV_W // quad_w):
                sl = slice(q * quad_w, (q + 1) * quad_w)
                gens.append(_quad_chunk(at[bi, 0, :, sl], bt[bi, 0, :, sl], kt[bi, 0, :, sl],
                                        rt[bi, 0, :, sl], vv[bi, :, sl], plr[bi, 0, 0, :, sl],
                                        [g_ref[bi, d, 2 * q], g_ref[bi, d, 2 * q + 1]],
                                        strict, incl, eye, lane_lo, mask_bd, eye_row,
                                        level_masks[d]))
                dests.append((y_ref, bi, d, q, sl))
    done = _run_interleaved(gens, groups=SCAN_GROUPS, lag=SCAN_LAG)
    for (y, g_new), (y_ref, bi, d, q, sl) in zip(done, dests):
        g_ref[bi, d, 2 * q] = g_new[0]
        g_ref[bi, d, 2 * q + 1] = g_new[1]
        y_ref[bi, :, sl] = y


def _scan_call(at, bt, kt, rt, v, plast, s_lat):
    b = at.shape[0]
    n_chunks = at.shape[2] // CHUNK
    n_ctx = CTX_LEN // CHUNK
    n_lat = s_lat // CHUNK

    def fwd_c(s):
        return s

    def bwd_c(s):
        return jnp.where(s < n_ctx, n_ctx - 1 - s, n_chunks + n_ctx - 1 - s)

    nb = SCAN_BATCH

    def dir_spec(d, cfun):
        return pl.BlockSpec((nb, 1, CHUNK, RWKV_W), lambda bi, s: (bi, d, cfun(s), 0))

    def v_spec(cfun):
        return pl.BlockSpec((nb, CHUNK, RWKV_W), lambda bi, s: (bi, cfun(s), 0))

    def pl_spec(d, cfun):
        return pl.BlockSpec((nb, 1, 1, 1, RWKV_W), lambda bi, s: (bi, d, cfun(s), 0, 0))

    in_specs = ([dir_spec(0, fwd_c)] * 4 + [v_spec(fwd_c), pl_spec(0, fwd_c)]
                + [dir_spec(1, bwd_c)] * 4 + [v_spec(bwd_c), pl_spec(1, bwd_c)])
    out_specs = [
        pl.BlockSpec((nb, CHUNK, RWKV_W), lambda bi, s: (bi, jnp.maximum(s - n_ctx, 0), 0)),
        pl.BlockSpec((nb, CHUNK, RWKV_W),
                     lambda bi, s: (bi, jnp.minimum(n_chunks - 1 - s, n_lat - 1), 0)),
    ]
    out_shape = [jax.ShapeDtypeStruct((b, s_lat, RWKV_W), F32)] * 2
    return pl.pallas_call(
        _scan_kernel,
        grid=(b // nb, n_chunks),
        in_specs=in_specs,
        out_specs=out_specs,
        out_shape=out_shape,
        scratch_shapes=[pltpu.VMEM((nb, 2, N_PAIRS, PAIR, PAIR), F32)],
        compiler_params=pltpu.CompilerParams(dimension_semantics=("arbitrary", "arbitrary"),
                                             vmem_limit_bytes=VMEM_LIMIT),
        name="scan",
    )(at, bt, kt, rt, v, plast, at, bt, kt, rt, v, plast)


def _mix_kernel(dft_ref, rev_ref, z_ref, yf_ref, yb_ref, bonus_ref, gate_ref, x_ref, mod_ref,
                gng_ref, gnb_ref, fg_ref, wout_ref, n2g_ref, ones_ref, x1_ref, h2_ref, zf_ref,
                *, fscale):
    seq = z_ref.shape[1]
    half = seq // 2
    rt = rev_ref.shape[0]

    @pl.when(pl.program_id(1) == 0)
    def _():
        mid = z_ref[0, half:half + 1, :FOURIER_W].astype(F32)
        row0 = lax.broadcasted_iota(jnp.int32, (rt, FOURIER_W), 0) == 0
        for part, sign in ((0, 1.0), (1, -1.0)):
            lanes = slice(part * FOURIER_W, (part + 1) * FOURIER_W)
            for r in range(half // rt):
                lo = seq - rt * (r + 1)
                nxt = (z_ref[0, lo + rt:lo + 2 * rt, lanes] if r > 0
                       else jnp.zeros((rt, FOURIER_W), BF16))
                rev = jnp.dot(rev_ref[...], jnp.concatenate([z_ref[0, lo:lo + rt, lanes], nxt], axis=0),
                              preferred_element_type=F32)
                folded = z_ref[0, r * rt:(r + 1) * rt, lanes].astype(F32) + sign * rev
                if part == 1 and r == 0:
                    folded = jnp.where(row0, mid, folded)
                zf_ref[part * half + r * rt:part * half + (r + 1) * rt, :] = folded.astype(BF16)

    yfour = jnp.dot(dft_ref[...], zf_ref[...], preferred_element_type=F32)
    four = _rms(yfour * fscale, fg_ref[...])

    ones_blk = ones_ref[...]
    y = yf_ref[0] + yb_ref[0]
    mu = _dot_exact_rhs(y, ones_blk) * (1.0 / HEAD)
    dy = y - mu
    var = _dot_exact_rhs(dy * dy, ones_blk) * (1.0 / HEAD)
    yn = dy * lax.rsqrt(var + GN_EPS) * gng_ref[...] + gnb_ref[...]
    rw = (yn + bonus_ref[0]) * gate_ref[0]

    out = _bdot(four, wout_ref[:FOURIER_W, :]) + _bdot(rw, wout_ref[FOURIER_W:, :])
    x1 = x_ref[0] + mod_ref[0, 2:3, :] * out
    x1_ref[0] = x1
    h2 = _rms(x1, n2g_ref[...]) * (1.0 + mod_ref[0, 4:5, :]) + mod_ref[0, 3:4, :]
    h2_ref[0] = h2.astype(BF16)


def _mix_call(dft, rev, z, yf, yb, bonus, gate, x, modx, gn_g, gn_b, f_g, w_out, norm2_g,
              ones_blk):
    b, s, d = x.shape
    tm = MIX_TILE

    def full(arr):
        nd = arr.ndim
        return pl.BlockSpec(arr.shape, lambda bi, m, _n=nd: (0,) * _n)

    tok = lambda w: pl.BlockSpec((1, tm, w), lambda bi, m: (bi, m, 0))
    in_specs = [
        pl.BlockSpec((tm, s), lambda bi, m: (m, 0)),
        full(rev),
        pl.BlockSpec((1, s, 2 * FOURIER_W), lambda bi, m: (bi, 0, 0)),
        tok(RWKV_W), tok(RWKV_W), tok(RWKV_W), tok(RWKV_W), tok(d),
        pl.BlockSpec((1, 6, d), lambda bi, m: (bi, 0, 0)),
        full(gn_g), full(gn_b), full(f_g), full(w_out), full(norm2_g), full(ones_blk),
    ]
    fscale = 1.0 / math.sqrt(s * (FOURIER_W // FOURIER_GROUPS))
    return pl.pallas_call(
        functools.partial(_mix_kernel, fscale=fscale),
        grid=(b, s // tm),
        in_specs=in_specs,
        out_specs=[tok(d), tok(d)],
        out_shape=[jax.ShapeDtypeStruct((b, s, d), F32), jax.ShapeDtypeStruct((b, s, d), BF16)],
        scratch_shapes=[pltpu.VMEM((s, FOURIER_W), BF16)],
        compiler_params=pltpu.CompilerParams(dimension_semantics=("arbitrary", "arbitrary"),
                                             vmem_limit_bytes=VMEM_LIMIT),
        name="mix",
    )(dft, rev, z, yf, yb, bonus, gate, x, modx, gn_g, gn_b, f_g, w_out, norm2_g, ones_blk)


def _ffn_up_kernel(h2_ref, wg_ref, wv_ref, cwg_ref, cwv_ref, cbg_ref, cbv_ref, act_ref):
    t = h2_ref.shape[1]
    rows = FFN_ROWS
    n_blk = t // rows
    w = jnp.concatenate([wg_ref[...], wv_ref[...]], axis=1)
    cw = jnp.concatenate([cwg_ref[...], cwv_ref[...]], axis=1).astype(BF16)
    cb = jnp.concatenate([cbg_ref[...], cbv_ref[...]], axis=1).astype(BF16)
    col = lax.broadcasted_iota(jnp.int32, (rows, 2 * FF_TILE), 0) & (GRID_W - 1)
    has_left = col > 0
    has_right = col < GRID_W - 1
    zeros = jnp.zeros((GRID_W, 2 * FF_TILE), BF16)

    def row_mixes(i):
        u = jnp.dot(h2_ref[0, i * rows:(i + 1) * rows, :], w, preferred_element_type=F32)
        u_l = jnp.where(has_left, pltpu.roll(u, 1, 0), 0.0).astype(BF16)
        u_r = jnp.where(has_right, pltpu.roll(u, rows - 1, 0), 0.0).astype(BF16)
        u_c = u.astype(BF16)
        return [cw[3 * kh:3 * kh + 1] * u_l + cw[3 * kh + 1:3 * kh + 2] * u_c
                + cw[3 * kh + 2:3 * kh + 3] * u_r for kh in range(3)]

    def finish(i, prev, cur, nxt):
        above = zeros if prev is None else prev[0][rows - GRID_W:]
        below = zeros if nxt is None else nxt[2][:GRID_W]
        c = (cur[1] + jnp.concatenate([above, cur[0][:rows - GRID_W]], axis=0)
             + jnp.concatenate([cur[2][GRID_W:], below], axis=0) + cb)
        cg = c[:, :FF_TILE]
        act_ref[0, i * rows:(i + 1) * rows, :] = cg * _sigmoid(cg) * c[:, FF_TILE:]

    mixes = [None] * (n_blk + 1)
    for i in range(n_blk):
        mixes[i] = row_mixes(i)
        if i >= 1:
            finish(i - 1, mixes[i - 2] if i >= 2 else None, mixes[i - 1], mixes[i])
    finish(n_blk - 1, mixes[n_blk - 2] if n_blk >= 2 else None, mixes[n_blk - 1], None)


def _ffn_up_call(h2, w_up, conv_w9, conv_b):
    b, s, d = h2.shape
    n_f = D_FF // FF_TILE
    in_specs = [
        pl.BlockSpec((1, s, d), lambda bi, f: (bi, 0, 0)),
        pl.BlockSpec((d, FF_TILE), lambda bi, f: (0, f)),
        pl.BlockSpec((d, FF_TILE), lambda bi, f: (0, n_f + f)),
        pl.BlockSpec((9, FF_TILE), lambda bi, f: (0, f)),
        pl.BlockSpec((9, FF_TILE), lambda bi, f: (0, n_f + f)),
        pl.BlockSpec((1, FF_TILE), lambda bi, f: (0, f)),
        pl.BlockSpec((1, FF_TILE), lambda bi, f: (0, n_f + f)),
    ]
    return pl.pallas_call(
        _ffn_up_kernel,
        grid=(b, n_f),
        in_specs=in_specs,
        out_specs=pl.BlockSpec((1, s, FF_TILE), lambda bi, f: (bi, 0, f)),
        out_shape=jax.ShapeDtypeStruct((b, s, D_FF), BF16),
        compiler_params=pltpu.CompilerParams(dimension_semantics=("arbitrary", "arbitrary"),
                                             vmem_limit_bytes=VMEM_LIMIT),
        name="ffn_up",
    )(h2, w_up, w_up, conv_w9, conv_w9, conv_b, conv_b)


def _ffn_down_kernel(act_ref, wd_ref, x1_ref, mod_ref, fg_ref, out_ref):
    y = jnp.dot(act_ref[0], wd_ref[...], preferred_element_type=F32)
    out_ref[0] = _rms(x1_ref[0] + mod_ref[0, 5:6, :] * y, fg_ref[...])


def _ffn_down_call(act, w_down, x1, modx, final_g):
    b, s, d = x1.shape
    tm = DOWN_TILE
    in_specs = [
        pl.BlockSpec((1, tm, D_FF), lambda bi, m: (bi, m, 0)),
        pl.BlockSpec((D_FF, d), lambda bi, m: (0, 0)),
        pl.BlockSpec((1, tm, d), lambda bi, m: (bi, m, 0)),
        pl.BlockSpec((1, 6, d), lambda bi, m: (bi, 0, 0)),
        pl.BlockSpec((1, d), lambda bi, m: (0, 0)),
    ]
    return pl.pallas_call(
        _ffn_down_kernel,
        grid=(b, s // tm),
        in_specs=in_specs,
        out_specs=pl.BlockSpec((1, tm, d), lambda bi, m: (bi, m, 0)),
        out_shape=jax.ShapeDtypeStruct((b, s, d), F32),
        compiler_params=pltpu.CompilerParams(dimension_semantics=("arbitrary", "arbitrary"),
                                             vmem_limit_bytes=VMEM_LIMIT),
        name="ffn_down",
    )(act, w_down, x1, modx, final_g)


@functools.lru_cache(maxsize=None)
def _constants(seq):
    gw = FOURIER_W // FOURIER_GROUPS
    nm = np.outer(np.arange(gw), np.arange(gw)) % gw
    ang = 2.0 * np.pi * nm / gw
    cs = np.zeros((FOURIER_W, 2 * FOURIER_W), np.float64)
    for gi in range(FOURIER_GROUPS):
        sl = slice(gi * gw, (gi + 1) * gw)
        cs[sl, sl] = np.cos(ang)
        cs[sl, FOURIER_W + gi * gw:FOURIER_W + (gi + 1) * gw] = np.sin(ang)
    half = seq // 2
    k_idx = np.arange(seq, dtype=np.int64)[:, None]
    ang_c = 2.0 * np.pi * ((k_idx * np.arange(0, half + 1, dtype=np.int64)[None, :]) % seq) / seq
    ang_s = 2.0 * np.pi * ((k_idx * np.arange(1, half, dtype=np.int64)[None, :]) % seq) / seq
    dft = np.concatenate([np.cos(ang_c), -np.sin(ang_s)], axis=1)
    ri = np.arange(REV_TILE)
    rev = (np.arange(2 * REV_TILE)[None, :] == (REV_TILE - ri)[:, None]).astype(np.float64)
    head_id = np.arange(RWKV_W) // HEAD
    ones_blk = (head_id[:, None] == head_id[None, :]).astype(np.float64)
    ti = np.arange(TILE)
    same_chunk = (ti[:, None] // CHUNK) == (ti[None, :] // CHUNK)
    tri = np.stack([same_chunk & (ti[None, :] <= ti[:, None]),
                    same_chunk & (ti[None, :] >= ti[:, None])]).astype(np.float64)
    to_bf16 = lambda arr: np.asarray(arr, dtype=np.float32).astype(BF16)
    return (np.asarray(cs, dtype=np.float32), np.asarray(dft, dtype=np.float32),
            to_bf16(ones_blk), to_bf16(tri), to_bf16(rev))


def _block_diag_dirs(w):
    z = jnp.zeros_like(w[0])
    return jnp.concatenate([jnp.concatenate([w[0], z], axis=1),
                            jnp.concatenate([z, w[1]], axis=1)], axis=0)


def kernel(x, c, ctx, c_ctx, ada_w, ada_b, norm1_g, norm2_g, w_in, rwkv_conv_w, decay_w0, decay_w2, iclr_a0, iclr_a2, gate_g2, k_k, k_a, r_k, gn_g, gn_b, fourier_g, w_out, ffn_w_up, ffn_conv_w, ffn_conv_b, ffn_w_down, final_g):
    b, s, d = x.shape
    assert ada_w.shape[0] == 1, "single-layer configuration"
    assert (b, s, d) == (c.shape[0], s, D_MODEL) and ctx.shape == (b, CTX_LEN, d)
    assert s % TILE == 0 and CTX_LEN == TILE
    cs, dft, ones_blk, tri, rev = (jnp.asarray(t) for t in _constants(s))
    cs, dft = cs.astype(BF16), dft.astype(BF16)

    cc = jnp.concatenate([c, c_ctx[None, :], jnp.zeros((16 - b - 1, d), F32)], axis=0)
    mod = _mod_call(cc, ada_w[0], ada_b[0][None, :])
    modx = mod[:b].reshape(b, 6, d)
    modc = jnp.broadcast_to(mod[b].reshape(1, 6, d)[:, :2], (b, 2, d))
    modsel = jnp.stack([modc, modx[:, :2]], axis=1)

    row = lambda t: t.reshape(1, -1)
    at, bt, kt, rt, v, plast, bonus, gate, z = _prep_call(
        x, ctx, modsel, row(norm1_g[0]), w_in[0].astype(BF16), rwkv_conv_w[0],
        row(decay_w0[0]), _block_diag_dirs(decay_w2[0]).astype(BF16),
        row(iclr_a0[0]), _block_diag_dirs(iclr_a2[0]).astype(BF16),
        gate_g2[0].astype(BF16), row(k_k[0]), row(k_a[0]), row(r_k[0]), cs, ones_blk, tri)

    yf, yb = _scan_call(at, bt, kt, rt, v, plast, s)

    x1, h2 = _mix_call(dft, rev, z, yf, yb, bonus, gate, x, modx, row(gn_g[0]), row(gn_b[0]),
                       row(fourier_g[0]), w_out[0].astype(BF16), row(norm2_g[0]), ones_blk)

    act = _ffn_up_call(h2, ffn_w_up[0].astype(BF16), ffn_conv_w[0].reshape(9, 2 * D_FF),
                       row(ffn_conv_b[0]))
    return _ffn_down_call(act, ffn_w_down[0].astype(BF16), x1, modx, row(final_g))
```

```python
import functools
import math

import numpy as np
import jax
import jax.numpy as jnp
from jax import lax
from jax.experimental import pallas as pl
from jax.experimental.pallas import tpu as pltpu

F32 = jnp.float32
BF16 = jnp.bfloat16

D_MODEL = 1024
CTX_LEN = 256
GRID_W = 64
FOURIER_W = 512
FOURIER_GROUPS = 8
RWKV_W = 512
HEAD = 64
N_HEADS = 8
DECAY_RANK = 64
ICLR_RANK = 64
GATE_RANK = 128
D_FF = 2816
NORM_EPS = 1e-6
GN_EPS = 64e-5
KK_EPS = 1e-12

CHUNK = 64
TILE = 256
MIX_TILE = 512
REV_TILE = 256
CHUNKS_PER_TILE = TILE // CHUNK
PREP_BATCH = 2
PAIR = 2 * HEAD
N_PAIRS = RWKV_W // PAIR
QUAD_HEADS = 4
SCAN_BATCH = 8
SCAN_GROUPS = 4
SCAN_LAG = 3
MOD_TILE = 1536
FF_TILE = 1408
FFN_ROWS = 512
DOWN_TILE = 1024
VMEM_LIMIT = 56 * 1024 * 1024
HEAD_SUM_TERMS = 1
CUMSUM_TERMS = 2
MOD_WEIGHT_TERMS = 2

RKV_LO = FOURIER_W
RKV_HI = FOURIER_W + 3 * RWKV_W
WD_LO = RKV_HI
AD_LO = WD_LO + 2 * DECAY_RANK
GD_LO = AD_LO + 2 * ICLR_RANK
PROJ_W = GD_LO + GATE_RANK


def _bdot(a, b):
    return jnp.dot(a.astype(BF16), b.astype(BF16), preferred_element_type=F32)


def _bdot_nt(a, b):
    return lax.dot_general(a.astype(BF16), b.astype(BF16), (((1,), (1,)), ((), ())),
                           preferred_element_type=F32)


def _bdot_tn(a, b):
    return lax.dot_general(a.astype(BF16), b.astype(BF16), (((0,), (0,)), ((), ())),
                           preferred_element_type=F32)


def _split_terms(a, terms):
    out, rem = [], a
    for _ in range(terms):
        hi = rem.astype(BF16)
        out.append(hi)
        rem = rem - hi.astype(F32)
    return out


def _dot_exact_rhs(a, b_exact, terms=HEAD_SUM_TERMS):
    acc = None
    for piece in _split_terms(a, terms):
        t = jnp.dot(piece, b_exact, preferred_element_type=F32)
        acc = t if acc is None else acc + t
    return acc


def _dot_exact_lhs(a_exact, b, terms=CUMSUM_TERMS):
    acc = None
    for piece in _split_terms(b, terms):
        t = jnp.dot(a_exact, piece, preferred_element_type=F32)
        acc = t if acc is None else acc + t
    return acc


def _sigmoid(x):
    return 1.0 / (1.0 + jnp.exp(-x))


def _rms(x, g):
    return x * lax.rsqrt(jnp.mean(x * x, axis=-1, keepdims=True) + NORM_EPS) * g


def _mod_kernel(c_ref, w_ref, b_ref, o_ref):
    cc = c_ref[...]
    s = cc * _sigmoid(cc)
    rows = s.shape[0]
    s_parts = jnp.concatenate(_split_terms(s, 3), axis=0)
    acc = b_ref[...]
    for wp in _split_terms(w_ref[...], MOD_WEIGHT_TERMS):
        t = jnp.dot(s_parts, wp, preferred_element_type=F32)
        acc = acc + (t[:rows] + t[rows:2 * rows] + t[2 * rows:])
    o_ref[...] = acc


def _mod_call(cc, ada_w, ada_b):
    rows, d = cc.shape
    n = ada_w.shape[1]
    tn = MOD_TILE
    return pl.pallas_call(
        _mod_kernel,
        grid=(n // tn,),
        in_specs=[pl.BlockSpec((rows, d), lambda i: (0, 0)),
                  pl.BlockSpec((d, tn), lambda i: (0, i)),
                  pl.BlockSpec((1, tn), lambda i: (0, i))],
        out_specs=pl.BlockSpec((rows, tn), lambda i: (0, i)),
        out_shape=jax.ShapeDtypeStruct((rows, n), F32),
        compiler_params=pltpu.CompilerParams(dimension_semantics=("arbitrary",),
                                             vmem_limit_bytes=VMEM_LIMIT),
        name="mod",
    )(cc, ada_w, ada_b)


def _prep_kernel(x_ref, xp_ref, xn_ref, ctx_ref, ms_ref, g1_ref, win_ref, cw_ref,
                 w0_ref, w2_ref, a0_ref, a2_ref, g2_ref, kk_ref, ka_ref, rk_ref,
                 cs_ref, ones_ref, tri_ref,
                 at_ref, bt_ref, kt_ref, rt_ref, v_ref, pl_ref, bonus_ref, gate_ref, z_ref):
    j = pl.program_id(1)
    refs = (x_ref, xp_ref, xn_ref, ctx_ref, ms_ref, g1_ref, win_ref, cw_ref,
            w0_ref, w2_ref, a0_ref, a2_ref, kk_ref, ka_ref, ones_ref, tri_ref,
            at_ref, bt_ref, kt_ref, rt_ref, v_ref, pl_ref)
    tiles = _run_interleaved([_prep_tile(bi, *refs) for bi in range(x_ref.shape[0])],
                             groups=PREP_BATCH, lag=1)

    @pl.when(j >= 1)
    def _():
        ones_blk = ones_ref[...]
        for bi, (u_four, r, v, gd, kd_sum) in enumerate(tiles):
            bs = _dot_exact_rhs(r * rk_ref[...] * kd_sum, ones_blk)
            bonus_ref[bi] = bs * v
            gate_ref[bi] = _bdot(_sigmoid(gd), g2_ref[...])
            z_ref[bi] = _bdot(u_four, cs_ref[...]).astype(BF16)


def _prep_tile(bi, x_ref, xp_ref, xn_ref, ctx_ref, ms_ref, g1_ref, win_ref, cw_ref,
               w0_ref, w2_ref, a0_ref, a2_ref, kk_ref, ka_ref, ones_ref, tri_ref,
               at_ref, bt_ref, kt_ref, rt_ref, v_ref, pl_ref):
    j = pl.program_id(1)
    n_j = pl.num_programs(1)
    is_ctx = j == 0
    shift = ms_ref[bi, 0, 0:1, :]
    scale = ms_ref[bi, 0, 1:2, :]
    g = g1_ref[...]

    def norm_mod(xr):
        return _rms(xr, g) * (1.0 + scale) + shift

    xin = jnp.where(is_ctx, ctx_ref[bi], x_ref[bi])
    h = norm_mod(xin).astype(BF16)
    u = jnp.dot(h, win_ref[...], preferred_element_type=F32)

    halo = jnp.concatenate([xp_ref[bi], xn_ref[bi]], axis=0)
    uh = jnp.dot(norm_mod(halo).astype(BF16), win_ref[:, RKV_LO:RKV_HI],
                 preferred_element_type=F32)
    yield
    prev_row = jnp.where(j >= 2, uh[7:8], 0.0)
    next_row = jnp.where(jnp.logical_and(j >= 1, j < n_j - 1), uh[8:9], 0.0)

    rkv = u[:, RKV_LO:RKV_HI]
    row = lax.broadcasted_iota(jnp.int32, rkv.shape, 0)
    u_m1 = jnp.where(row == 0, prev_row, pltpu.roll(rkv, 1, 0))
    u_p1 = jnp.where(row == TILE - 1, next_row, pltpu.roll(rkv, TILE - 1, 0))
    rkv = cw_ref[0:1, :] * u_m1 + cw_ref[1:2, :] * rkv + cw_ref[2:3, :] * u_p1
    r = rkv[:, 0:RWKV_W]
    k = rkv[:, RWKV_W:2 * RWKV_W]
    v = rkv[:, 2 * RWKV_W:3 * RWKV_W]

    wd = u[:, WD_LO:AD_LO]
    ad = u[:, AD_LO:GD_LO]
    gd = u[:, GD_LO:PROJ_W]
    w_lora = w0_ref[...] + _bdot(jnp.tanh(wd), w2_ref[...])
    lw = -math.exp(-0.5) * _sigmoid(w_lora)
    a = _sigmoid(a0_ref[...] + _bdot(ad, a2_ref[...]))

    ones_blk = ones_ref[...]
    kraw = k * kk_ref[...]
    kk = kraw * lax.rsqrt(_dot_exact_rhs(kraw * kraw, ones_blk) + KK_EPS)
    ka = ka_ref[...]
    kd = [k * (1.0 + (a[:, d * RWKV_W:(d + 1) * RWKV_W] - 1.0) * ka) for d in range(2)]
    yield

    v_ref[bi] = v.astype(BF16)
    for d in range(2):
        lwd = lw[:, d * RWKV_W:(d + 1) * RWKV_W]
        ad_ = a[:, d * RWKV_W:(d + 1) * RWKV_W]
        c = _dot_exact_lhs(tri_ref[d], lwd)
        e_pos = jnp.exp(c)
        e_neg = jnp.exp(-c)
        e_prev = jnp.exp(c - lwd)
        at_ref[bi, d] = (kk * e_prev).astype(BF16)
        bt_ref[bi, d] = (kk * ad_ * e_neg).astype(BF16)
        kt_ref[bi, d] = (kd[d] * e_neg).astype(BF16)
        rt_ref[bi, d] = (r * e_pos).astype(BF16)
        for q in range(CHUNKS_PER_TILE):
            last = q * CHUNK + (CHUNK - 1 if d == 0 else 0)
            pl_ref[bi, d, q] = e_pos[last:last + 1, :]
    return u[:, 0:FOURIER_W], r, v, gd, kd[0] + kd[1]


def _prep_call(x, ctx, modsel, norm1_g, w_in, conv_w, w0, w2bd, a0, a2bd, g2, k_k, k_a, r_k,
               cs, ones_blk, tri):
    b, s, d = x.shape
    n_j = 1 + s // TILE
    n_chunks = (CTX_LEN + s) // CHUNK
    t_all = CTX_LEN + s
    c2 = 2 * RWKV_W

    def full(arr):
        nd = arr.ndim
        return pl.BlockSpec(arr.shape, lambda bi, j, _n=nd: (0,) * _n)

    rows8 = TILE // 8
    nb = PREP_BATCH
    in_specs = [
        pl.BlockSpec((nb, TILE, d), lambda bi, j: (bi, jnp.maximum(j - 1, 0), 0)),
        pl.BlockSpec((nb, 8, d), lambda bi, j: (bi, jnp.maximum((j - 1) * rows8 - 1, 0), 0)),
        pl.BlockSpec((nb, 8, d), lambda bi, j: (bi, jnp.minimum(j * rows8, s // 8 - 1), 0)),
        pl.BlockSpec((nb, CTX_LEN, d), lambda bi, j: (bi, 0, 0)),
        pl.BlockSpec((nb, 1, 2, d), lambda bi, j: (bi, jnp.minimum(j, 1), 0, 0)),
        full(norm1_g), full(w_in), full(conv_w), full(w0), full(w2bd), full(a0), full(a2bd),
        full(g2), full(k_k), full(k_a), full(r_k), full(cs), full(ones_blk), full(tri),
    ]
    lat = lambda bi, j: (bi, jnp.maximum(j - 1, 0), 0)
    out_specs = [
        pl.BlockSpec((nb, 2, TILE, RWKV_W), lambda bi, j: (bi, 0, j, 0)),
        pl.BlockSpec((nb, 2, TILE, RWKV_W), lambda bi, j: (bi, 0, j, 0)),
        pl.BlockSpec((nb, 2, TILE, RWKV_W), lambda bi, j: (bi, 0, j, 0)),
        pl.BlockSpec((nb, 2, TILE, RWKV_W), lambda bi, j: (bi, 0, j, 0)),
        pl.BlockSpec((nb, TILE, RWKV_W), lambda bi, j: (bi, j, 0)),
        pl.BlockSpec((nb, 2, CHUNKS_PER_TILE, 1, RWKV_W), lambda bi, j: (bi, 0, j, 0, 0)),
        pl.BlockSpec((nb, TILE, RWKV_W), lat),
        pl.BlockSpec((nb, TILE, RWKV_W), lat),
        pl.BlockSpec((nb, TILE, 2 * FOURIER_W), lat),
    ]
    out_shape = [
        jax.ShapeDtypeStruct((b, 2, t_all, RWKV_W), BF16),
        jax.ShapeDtypeStruct((b, 2, t_all, RWKV_W), BF16),
        jax.ShapeDtypeStruct((b, 2, t_all, RWKV_W), BF16),
        jax.ShapeDtypeStruct((b, 2, t_all, RWKV_W), BF16),
        jax.ShapeDtypeStruct((b, t_all, RWKV_W), BF16),
        jax.ShapeDtypeStruct((b, 2, n_chunks, 1, RWKV_W), F32),
        jax.ShapeDtypeStruct((b, s, RWKV_W), F32),
        jax.ShapeDtypeStruct((b, s, RWKV_W), F32),
        jax.ShapeDtypeStruct((b, s, 2 * FOURIER_W), BF16),
    ]
    return pl.pallas_call(
        _prep_kernel,
        grid=(b // nb, n_j),
        in_specs=in_specs,
        out_specs=out_specs,
        out_shape=out_shape,
        compiler_params=pltpu.CompilerParams(dimension_semantics=("arbitrary", "arbitrary"),
                                             vmem_limit_bytes=VMEM_LIMIT),
        name="prep",
    )(x, x, x, ctx, modsel, norm1_g, w_in, conv_w, w0, w2bd, a0, a2bd, g2, k_k, k_a, r_k,
      cs, ones_blk, tri)


def _quad_chunk(a, bm, km, r, v, plrow, g_states, strict, incl, eye, lane_lo, mask_bd, eye_row,
                level_masks):
    n = PAIR

    def dup(xv):
        z = jnp.zeros_like(xv)
        return jnp.concatenate([jnp.where(lane_lo, xv, z), jnp.where(lane_lo, z, xv)], axis=0)

    def to_row(x2):
        return x2[:CHUNK] + x2[CHUNK:]

    def bdiag(xr):
        return jnp.where(mask_bd, jnp.concatenate([xr] * QUAD_HEADS, axis=0), jnp.zeros((), xr.dtype))

    pairs = []
    for p in range(2):
        sl = slice(p * n, (p + 1) * n)
        a2, b2, k2, r2, v2 = (dup(t[:, sl]) for t in (a, bm, km, r, v))
        rhs = jnp.concatenate([b2, k2], axis=0)
        s = _bdot_nt(jnp.concatenate([a2, r2], axis=0), rhs)
        pairs.append((sl, r2, v2, rhs, s))
    yield

    a_ab = jnp.concatenate([to_row(jnp.where(strict, s[:n, :n], 0.0)) for *_, s in pairs], axis=1)
    a_ak = jnp.concatenate([to_row(jnp.where(strict, s[:n, n:], 0.0)) for *_, s in pairs], axis=1)
    m_rbk = [jnp.concatenate([jnp.where(incl, s[n:, :n], 0.0), jnp.where(incl, s[n:, n:], 0.0)],
                             axis=1).astype(BF16) for *_, s in pairs]

    av = _bdot(a_ak, bdiag(v))
    a_lvl = [jnp.where(m, a_ab, 0.0).astype(BF16) for m in level_masks]
    t_row = jnp.where(eye_row, 1.0, 0.0) - jnp.where(level_masks[0], a_ab, 0.0)
    lvl = 1
    while lvl < len(a_lvl):
        paired = lvl + 1 < len(a_lvl)
        lhs = jnp.concatenate([a_lvl[lvl], a_lvl[lvl + 1]], axis=0) if paired else a_lvl[lvl]
        yc = jnp.dot(lhs, bdiag(t_row.astype(BF16)), preferred_element_type=F32)
        yield
        y_bd = bdiag(yc[:CHUNK].astype(BF16))
        if paired:
            both = jnp.dot(jnp.concatenate([t_row, yc[CHUNK:]], axis=0).astype(BF16), y_bd,
                           preferred_element_type=F32)
            yield
            t_row = t_row - both[:CHUNK]
            y_next = yc[CHUNK:] - both[CHUNK:]
            t_row = t_row - _bdot(t_row, bdiag(y_next.astype(BF16)))
            lvl += 2
        else:
            t_row = t_row - _bdot(t_row, y_bd)
            lvl += 1
        yield
    tx = _bdot(t_row, jnp.concatenate([bdiag(a), bdiag(av.astype(BF16))], axis=1))
    yield
    a_w = tx[:, :2 * n].astype(BF16)
    u_v = (-tx[:, 2 * n:]).astype(BF16)
    stage5 = []
    for (sl, r2, v2, rhs, _), m2 in zip(pairs, m_rbk):
        zmat = jnp.concatenate(
            [jnp.concatenate([dup(a_w[:, sl]), dup(u_v[:, sl])], axis=1),
             jnp.concatenate([jnp.zeros_like(v2), v2], axis=1)], axis=0)
        rhs_end = rhs.astype(F32) * plrow[:, sl]
        lhs = jnp.concatenate([rhs_end.T.astype(BF16), m2], axis=0)
        stage5.append(jnp.dot(lhs, zmat, preferred_element_type=F32))
    yield
    ys, g_new = [], []
    for (sl, r2, v2, rhs, _), o5, g_state in zip(pairs, stage5, g_states):
        phi = jnp.where(eye, plrow[:, sl], 0.0) - o5[:n, :n]
        r_y = r2.astype(F32) - o5[n:, :n]
        o6 = _bdot(jnp.concatenate([r_y, phi], axis=0), g_state)
        ys.append(to_row(o6[:n] + o5[n:, n:]))
        g_new.append(o6[n:] + o5[:n, n:])
    return jnp.concatenate(ys, axis=1), g_new


def _run_interleaved(gens, groups=1, lag=0):
    results = [None] * len(gens)
    active = list(range(len(gens)))
    rnd = 0
    while active:
        for i in list(active):
            if rnd < (i % groups) * lag:
                continue
            try:
                next(gens[i])
            except StopIteration as stop:
                results[i] = stop.value
                active.remove(i)
        rnd += 1
    return results


def _scan_kernel(atf, btf, ktf, rtf, vf, plf, atb, btb, ktb, rtb, vb, plb,
                 yf_ref, yb_ref, g_ref):
    s = pl.program_id(1)

    @pl.when(s == 0)
    def _():
        g_ref[...] = jnp.zeros_like(g_ref)

    ri = lax.broadcasted_iota(jnp.int32, (PAIR, PAIR), 0)
    ci = lax.broadcasted_iota(jnp.int32, (PAIR, PAIR), 1)
    same = (ri >= HEAD) == (ci >= HEAD)
    ti = ri & (HEAD - 1)
    tj = ci & (HEAD - 1)
    eye = ri == ci
    lane_lo = lax.broadcasted_iota(jnp.int32, (CHUNK, PAIR), 1) < HEAD
    masks = [(same & (tj < ti), same & (tj <= ti)),
             (same & (tj > ti), same & (tj >= ti))]
    quad_w = QUAD_HEADS * HEAD
    rq = lax.broadcasted_iota(jnp.int32, (quad_w, quad_w), 0)
    cq = lax.broadcasted_iota(jnp.int32, (quad_w, quad_w), 1)
    mask_bd = (rq // HEAD) == (cq // HEAD)
    t_i = lax.broadcasted_iota(jnp.int32, (CHUNK, quad_w), 0)
    s_i = lax.broadcasted_iota(jnp.int32, (CHUNK, quad_w), 1) & (HEAD - 1)
    eye_row = s_i == t_i
    level_masks = []
    for d in range(2):
        late, early = (t_i, s_i) if d == 0 else (s_i, t_i)
        level_masks.append([
            ((t_i >> (lvl + 1)) == (s_i >> (lvl + 1)))
            & (((late >> lvl) & 1) == 1) & (((early >> lvl) & 1) == 0)
            for lvl in range(int(math.log2(CHUNK)))])
    refs = [(atf, btf, ktf, rtf, vf, plf, yf_ref), (atb, btb, ktb, rtb, vb, plb, yb_ref)]
    gens, dests = [], []
    for bi in range(yf_ref.shape[0]):
        for d in range(2):
            at, bt, kt, rt, vv, plr, y_ref = refs[d]
            strict, incl = masks[d]
            for q in range(RWKV_W // quad_w):
                sl = slice(q * quad_w, (q + 1) * quad_w)
                gens.append(_quad_chunk(at[bi, 0, :, sl], bt[bi, 0, :, sl], kt[bi, 0, :, sl],
                                        rt[bi, 0, :, sl], vv[bi, :, sl], plr[bi, 0, 0, :, sl],
                                        [g_ref[bi, d, 2 * q], g_ref[bi, d, 2 * q + 1]],
                                        strict, incl, eye, lane_lo, mask_bd, eye_row,
                                        level_masks[d]))
                dests.append((y_ref, bi, d, q, sl))
    done = _run_interleaved(gens, groups=SCAN_GROUPS, lag=SCAN_LAG)
    for (y, g_new), (y_ref, bi, d, q, sl) in zip(done, dests):
        g_ref[bi, d, 2 * q] = g_new[0]
        g_ref[bi, d, 2 * q + 1] = g_new[1]
        y_ref[bi, :, sl] = y


def _scan_call(at, bt, kt, rt, v, plast, s_lat):
    b = at.shape[0]
    n_chunks = at.shape[2] // CHUNK
    n_ctx = CTX_LEN // CHUNK
    n_lat = s_lat // CHUNK

    def fwd_c(s):
        return s

    def bwd_c(s):
        return jnp.where(s < n_ctx, n_ctx - 1 - s, n_chunks + n_ctx - 1 - s)

    nb = SCAN_BATCH

    def dir_spec(d, cfun):
        return pl.BlockSpec((nb, 1, CHUNK, RWKV_W), lambda bi, s: (bi, d, cfun(s), 0))

    def v_spec(cfun):
        return pl.BlockSpec((nb, CHUNK, RWKV_W), lambda bi, s: (bi, cfun(s), 0))

    def pl_spec(d, cfun):
        return pl.BlockSpec((nb, 1, 1, 1, RWKV_W), lambda bi, s: (bi, d, cfun(s), 0, 0))

    in_specs = ([dir_spec(0, fwd_c)] * 4 + [v_spec(fwd_c), pl_spec(0, fwd_c)]
                + [dir_spec(1, bwd_c)] * 4 + [v_spec(bwd_c), pl_spec(1, bwd_c)])
    out_specs = [
        pl.BlockSpec((nb, CHUNK, RWKV_W), lambda bi, s: (bi, jnp.maximum(s - n_ctx, 0), 0)),
        pl.BlockSpec((nb, CHUNK, RWKV_W),
                     lambda bi, s: (bi, jnp.minimum(n_chunks - 1 - s, n_lat - 1), 0)),
    ]
    out_shape = [jax.ShapeDtypeStruct((b, s_lat, RWKV_W), F32)] * 2
    return pl.pallas_call(
        _scan_kernel,
        grid=(b // nb, n_chunks),
        in_specs=in_specs,
        out_specs=out_specs,
        out_shape=out_shape,
        scratch_shapes=[pltpu.VMEM((nb, 2, N_PAIRS, PAIR, PAIR), F32)],
        compiler_params=pltpu.CompilerParams(dimension_semantics=("arbitrary", "arbitrary"),
                                             vmem_limit_bytes=VMEM_LIMIT),
        name="scan",
    )(at, bt, kt, rt, v, plast, at, bt, kt, rt, v, plast)


def _mix_kernel(dft_ref, rev_ref, z_ref, yf_ref, yb_ref, bonus_ref, gate_ref, x_ref, mod_ref,
                gng_ref, gnb_ref, fg_ref, wout_ref, n2g_ref, ones_ref, x1_ref, h2_ref, zf_ref,
                *, fscale):
    seq = z_ref.shape[1]
    half = seq // 2
    rt = rev_ref.shape[0]

    @pl.when(pl.program_id(1) == 0)
    def _():
        mid = z_ref[0, half:half + 1, :FOURIER_W].astype(F32)
        row0 = lax.broadcasted_iota(jnp.int32, (rt, FOURIER_W), 0) == 0
        for part, sign in ((0, 1.0), (1, -1.0)):
            lanes = slice(part * FOURIER_W, (part + 1) * FOURIER_W)
            for r in range(half // rt):
                lo = seq - rt * (r + 1)
                nxt = (z_ref[0, lo + rt:lo + 2 * rt, lanes] if r > 0
                       else jnp.zeros((rt, FOURIER_W), BF16))
                rev = jnp.dot(rev_ref[...], jnp.concatenate([z_ref[0, lo:lo + rt, lanes], nxt], axis=0),
                              preferred_element_type=F32)
                folded = z_ref[0, r * rt:(r + 1) * rt, lanes].astype(F32) + sign * rev
                if part == 1 and r == 0:
                    folded = jnp.where(row0, mid, folded)
                zf_ref[part * half + r * rt:part * half + (r + 1) * rt, :] = folded.astype(BF16)

    yfour = jnp.dot(dft_ref[...], zf_ref[...], preferred_element_type=F32)
    four = _rms(yfour * fscale, fg_ref[...])

    ones_blk = ones_ref[...]
    y = yf_ref[0] + yb_ref[0]
    mu = _dot_exact_rhs(y, ones_blk) * (1.0 / HEAD)
    dy = y - mu
    var = _dot_exact_rhs(dy * dy, ones_blk) * (1.0 / HEAD)
    yn = dy * lax.rsqrt(var + GN_EPS) * gng_ref[...] + gnb_ref[...]
    rw = (yn + bonus_ref[0]) * gate_ref[0]

    out = _bdot(four, wout_ref[:FOURIER_W, :]) + _bdot(rw, wout_ref[FOURIER_W:, :])
    x1 = x_ref[0] + mod_ref[0, 2:3, :] * out
    x1_ref[0] = x1
    h2 = _rms(x1, n2g_ref[...]) * (1.0 + mod_ref[0, 4:5, :]) + mod_ref[0, 3:4, :]
    h2_ref[0] = h2.astype(BF16)


def _mix_call(dft, rev, z, yf, yb, bonus, gate, x, modx, gn_g, gn_b, f_g, w_out, norm2_g,
              ones_blk):
    b, s, d = x.shape
    tm = MIX_TILE

    def full(arr):
        nd = arr.ndim
        return pl.BlockSpec(arr.shape, lambda bi, m, _n=nd: (0,) * _n)

    tok = lambda w: pl.BlockSpec((1, tm, w), lambda bi, m: (bi, m, 0))
    in_specs = [
        pl.BlockSpec((tm, s), lambda bi, m: (m, 0)),
        full(rev),
        pl.BlockSpec((1, s, 2 * FOURIER_W), lambda bi, m: (bi, 0, 0)),
        tok(RWKV_W), tok(RWKV_W), tok(RWKV_W), tok(RWKV_W), tok(d),
        pl.BlockSpec((1, 6, d), lambda bi, m: (bi, 0, 0)),
        full(gn_g), full(gn_b), full(f_g), full(w_out), full(norm2_g), full(ones_blk),
    ]
    fscale = 1.0 / math.sqrt(s * (FOURIER_W // FOURIER_GROUPS))
    return pl.pallas_call(
        functools.partial(_mix_kernel, fscale=fscale),
        grid=(b, s // tm),
        in_specs=in_specs,
        out_specs=[tok(d), tok(d)],
        out_shape=[jax.ShapeDtypeStruct((b, s, d), F32), jax.ShapeDtypeStruct((b, s, d), BF16)],
        scratch_shapes=[pltpu.VMEM((s, FOURIER_W), BF16)],
        compiler_params=pltpu.CompilerParams(dimension_semantics=("arbitrary", "arbitrary"),
                                             vmem_limit_bytes=VMEM_LIMIT),
        name="mix",
    )(dft, rev, z, yf, yb, bonus, gate, x, modx, gn_g, gn_b, f_g, w_out, norm2_g, ones_blk)


def _ffn_up_kernel(h2_ref, wg_ref, wv_ref, cwg_ref, cwv_ref, cbg_ref, cbv_ref, act_ref):
    t = h2_ref.shape[1]
    rows = FFN_ROWS
    n_blk = t // rows
    w = jnp.concatenate([wg_ref[...], wv_ref[...]], axis=1)
    cw = jnp.concatenate([cwg_ref[...], cwv_ref[...]], axis=1).astype(BF16)
    cb = jnp.concatenate([cbg_ref[...], cbv_ref[...]], axis=1).astype(BF16)
    col = lax.broadcasted_iota(jnp.int32, (rows, 2 * FF_TILE), 0) & (GRID_W - 1)
    has_left = col > 0
    has_right = col < GRID_W - 1
    zeros = jnp.zeros((GRID_W, 2 * FF_TILE), BF16)

    def row_mixes(i):
        u = jnp.dot(h2_ref[0, i * rows:(i + 1) * rows, :], w, preferred_element_type=F32)
        u_l = jnp.where(has_left, pltpu.roll(u, 1, 0), 0.0).astype(BF16)
        u_r = jnp.where(has_right, pltpu.roll(u, rows - 1, 0), 0.0).astype(BF16)
        u_c = u.astype(BF16)
        return [cw[3 * kh:3 * kh + 1] * u_l + cw[3 * kh + 1:3 * kh + 2] * u_c
                + cw[3 * kh + 2:3 * kh + 3] * u_r for kh in range(3)]

    def finish(i, prev, cur, nxt):
        above = zeros if prev is None else prev[0][rows - GRID_W:]
        below = zeros if nxt is None else nxt[2][:GRID_W]
        c = (cur[1] + jnp.concatenate([above, cur[0][:rows - GRID_W]], axis=0)
             + jnp.concatenate([cur[2][GRID_W:], below], axis=0) + cb)
        cg = c[:, :FF_TILE]
        act_ref[0, i * rows:(i + 1) * rows, :] = cg * _sigmoid(cg) * c[:, FF_TILE:]

    mixes = [None] * (n_blk + 1)
    for i in range(n_blk):
        mixes[i] = row_mixes(i)
        if i >= 1:
            finish(i - 1, mixes[i - 2] if i >= 2 else None, mixes[i - 1], mixes[i])
    finish(n_blk - 1, mixes[n_blk - 2] if n_blk >= 2 else None, mixes[n_blk - 1], None)


def _ffn_up_call(h2, w_up, conv_w9, conv_b):
    b, s, d = h2.shape
    n_f = D_FF // FF_TILE
    in_specs = [
        pl.BlockSpec((1, s, d), lambda bi, f: (bi, 0, 0)),
        pl.BlockSpec((d, FF_TILE), lambda bi, f: (0, f)),
        pl.BlockSpec((d, FF_TILE), lambda bi, f: (0, n_f + f)),
        pl.BlockSpec((9, FF_TILE), lambda bi, f: (0, f)),
        pl.BlockSpec((9, FF_TILE), lambda bi, f: (0, n_f + f)),
        pl.BlockSpec((1, FF_TILE), lambda bi, f: (0, f)),
        pl.BlockSpec((1, FF_TILE), lambda bi, f: (0, n_f + f)),
    ]
    return pl.pallas_call(
        _ffn_up_kernel,
        grid=(b, n_f),
        in_specs=in_specs,
        out_specs=pl.BlockSpec((1, s, FF_TILE), lambda bi, f: (bi, 0, f)),
        out_shape=jax.ShapeDtypeStruct((b, s, D_FF), BF16),
        compiler_params=pltpu.CompilerParams(dimension_semantics=("arbitrary", "arbitrary"),
                                             vmem_limit_bytes=VMEM_LIMIT),
        name="ffn_up",
    )(h2, w_up, w_up, conv_w9, conv_w9, conv_b, conv_b)


def _ffn_down_kernel(act_ref, wd_ref, x1_ref, mod_ref, fg_ref, out_ref):
    y = jnp.dot(act_ref[0], wd_ref[...], preferred_element_type=F32)
    out_ref[0] = _rms(x1_ref[0] + mod_ref[0, 5:6, :] * y, fg_ref[...])


def _ffn_down_call(act, w_down, x1, modx, final_g):
    b, s, d = x1.shape
    tm = DOWN_TILE
    in_specs = [
        pl.BlockSpec((1, tm, D_FF), lambda bi, m: (bi, m, 0)),
        pl.BlockSpec((D_FF, d), lambda bi, m: (0, 0)),
        pl.BlockSpec((1, tm, d), lambda bi, m: (bi, m, 0)),
        pl.BlockSpec((1, 6, d), lambda bi, m: (bi, 0, 0)),
        pl.BlockSpec((1, d), lambda bi, m: (0, 0)),
    ]
    return pl.pallas_call(
        _ffn_down_kernel,
        grid=(b, s // tm),
        in_specs=in_specs,
        out_specs=pl.BlockSpec((1, tm, d), lambda bi, m: (bi, m, 0)),
        out_shape=jax.ShapeDtypeStruct((b, s, d), F32),
        compiler_params=pltpu.CompilerParams(dimension_semantics=("arbitrary", "arbitrary"),
                                             vmem_limit_bytes=VMEM_LIMIT),
        name="ffn_down",
    )(act, w_down, x1, modx, final_g)


@functools.lru_cache(maxsize=None)
def _constants(seq):
    gw = FOURIER_W // FOURIER_GROUPS
    nm = np.outer(np.arange(gw), np.arange(gw)) % gw
    ang = 2.0 * np.pi * nm / gw
    cs = np.zeros((FOURIER_W, 2 * FOURIER_W), np.float64)
    for gi in range(FOURIER_GROUPS):
        sl = slice(gi * gw, (gi + 1) * gw)
        cs[sl, sl] = np.cos(ang)
        cs[sl, FOURIER_W + gi * gw:FOURIER_W + (gi + 1) * gw] = np.sin(ang)
    half = seq // 2
    k_idx = np.arange(seq, dtype=np.int64)[:, None]
    ang_c = 2.0 * np.pi * ((k_idx * np.arange(0, half + 1, dtype=np.int64)[None, :]) % seq) / seq
    ang_s = 2.0 * np.pi * ((k_idx * np.arange(1, half, dtype=np.int64)[None, :]) % seq) / seq
    dft = np.concatenate([np.cos(ang_c), -np.sin(ang_s)], axis=1)
    ri = np.arange(REV_TILE)
    rev = (np.arange(2 * REV_TILE)[None, :] == (REV_TILE - ri)[:, None]).astype(np.float64)
    head_id = np.arange(RWKV_W) // HEAD
    ones_blk = (head_id[:, None] == head_id[None, :]).astype(np.float64)
    ti = np.arange(TILE)
    same_chunk = (ti[:, None] // CHUNK) == (ti[None, :] // CHUNK)
    tri = np.stack([same_chunk & (ti[None, :] <= ti[:, None]),
                    same_chunk & (ti[None, :] >= ti[:, None])]).astype(np.float64)
    to_bf16 = lambda arr: np.asarray(arr, dtype=np.float32).astype(BF16)
    return (np.asarray(cs, dtype=np.float32), np.asarray(dft, dtype=np.float32),
            to_bf16(ones_blk), to_bf16(tri), to_bf16(rev))


def _block_diag_dirs(w):
    z = jnp.zeros_like(w[0])
    return jnp.concatenate([jnp.concatenate([w[0], z], axis=1),
                            jnp.concatenate([z, w[1]], axis=1)], axis=0)


def kernel(x, c, ctx, c_ctx, ada_w, ada_b, norm1_g, norm2_g, w_in, rwkv_conv_w, decay_w0, decay_w2, iclr_a0, iclr_a2, gate_g2, k_k, k_a, r_k, gn_g, gn_b, fourier_g, w_out, ffn_w_up, ffn_conv_w, ffn_conv_b, ffn_w_down, final_g):
    b, s, d = x.shape
    assert ada_w.shape[0] == 1, "single-layer configuration"
    assert (b, s, d) == (c.shape[0], s, D_MODEL) and ctx.shape == (b, CTX_LEN, d)
    assert s % TILE == 0 and CTX_LEN == TILE
    cs, dft, ones_blk, tri, rev = (jnp.asarray(t) for t in _constants(s))
    cs, dft = cs.astype(BF16), dft.astype(BF16)

    cc = jnp.concatenate([c, c_ctx[None, :], jnp.zeros((16 - b - 1, d), F32)], axis=0)
    mod = _mod_call(cc, ada_w[0], ada_b[0][None, :])
    modx = mod[:b].reshape(b, 6, d)
    modc = jnp.broadcast_to(mod[b].reshape(1, 6, d)[:, :2], (b, 2, d))
    modsel = jnp.stack([modc, modx[:, :2]], axis=1)

    row = lambda t: t.reshape(1, -1)
    at, bt, kt, rt, v, plast, bonus, gate, z = _prep_call(
        x, ctx, modsel, row(norm1_g[0]), w_in[0].astype(BF16), rwkv_conv_w[0],
        row(decay_w0[0]), _block_diag_dirs(decay_w2[0]).astype(BF16),
        row(iclr_a0[0]), _block_diag_dirs(iclr_a2[0]).astype(BF16),
        gate_g2[0].astype(BF16), row(k_k[0]), row(k_a[0]), row(r_k[0]), cs, ones_blk, tri)

    yf, yb = _scan_call(at, bt, kt, rt, v, plast, s)

    x1, h2 = _mix_call(dft, rev, z, yf, yb, bonus, gate, x, modx, row(gn_g[0]), row(gn_b[0]),
                       row(fourier_g[0]), w_out[0].astype(BF16), row(norm2_g[0]), ones_blk)

    act = _ffn_up_call(h2, ffn_w_up[0].astype(BF16), ffn_conv_w[0].reshape(9, 2 * D_FF),
                       row(ffn_conv_b[0]))
    return _ffn_down_call(act, ffn_w_down[0].astype(BF16), x1, modx, row(final_g))
```

```python
import functools
import math

import numpy as np
import jax
import jax.numpy as jnp
from jax import lax
from jax.experimental import pallas as pl
from jax.experimental.pallas import tpu as pltpu

F32 = jnp.float32
BF16 = jnp.bfloat16

D_MODEL = 1024
CTX_LEN = 256
GRID_W = 64
FOURIER_W = 512
FOURIER_GROUPS = 8
RWKV_W = 512
HEAD = 64
N_HEADS = 8
DECAY_RANK = 64
ICLR_RANK = 64
GATE_RANK = 128
D_FF = 2816
NORM_EPS = 1e-6
GN_EPS = 64e-5
KK_EPS = 1e-12

CHUNK = 64
TILE = 256
MIX_TILE = 512
REV_TILE = 256
CHUNKS_PER_TILE = TILE // CHUNK
PREP_BATCH = 2
PAIR = 2 * HEAD
N_PAIRS = RWKV_W // PAIR
QUAD_HEADS = 4
SCAN_BATCH = 8
SCAN_GROUPS = 4
SCAN_LAG = 4
MOD_TILE = 1536
FF_TILE = 1408
FFN_ROWS = 256
DOWN_TILE = 1024
VMEM_LIMIT = 56 * 1024 * 1024
HEAD_SUM_TERMS = 1
CUMSUM_TERMS = 2
MOD_WEIGHT_TERMS = 2

RKV_LO = FOURIER_W
RKV_HI = FOURIER_W + 3 * RWKV_W
WD_LO = RKV_HI
AD_LO = WD_LO + 2 * DECAY_RANK
GD_LO = AD_LO + 2 * ICLR_RANK
PROJ_W = GD_LO + GATE_RANK


def _bdot(a, b):
    return jnp.dot(a.astype(BF16), b.astype(BF16), preferred_element_type=F32)


def _bdot_nt(a, b):
    return lax.dot_general(a.astype(BF16), b.astype(BF16), (((1,), (1,)), ((), ())),
                           preferred_element_type=F32)


def _bdot_tn(a, b):
    return lax.dot_general(a.astype(BF16), b.astype(BF16), (((0,), (0,)), ((), ())),
                           preferred_element_type=F32)


def _split_terms(a, terms):
    out, rem = [], a
    for _ in range(terms):
        hi = rem.astype(BF16)
        out.append(hi)
        rem = rem - hi.astype(F32)
    return out


def _dot_exact_rhs(a, b_exact, terms=HEAD_SUM_TERMS):
    acc = None
    for piece in _split_terms(a, terms):
        t = jnp.dot(piece, b_exact, preferred_element_type=F32)
        acc = t if acc is None else acc + t
    return acc


def _dot_exact_lhs(a_exact, b, terms=CUMSUM_TERMS):
    acc = None
    for piece in _split_terms(b, terms):
        t = jnp.dot(a_exact, piece, preferred_element_type=F32)
        acc = t if acc is None else acc + t
    return acc


def _sigmoid(x):
    return 1.0 / (1.0 + jnp.exp(-x))


def _rms(x, g):
    return x * lax.rsqrt(jnp.mean(x * x, axis=-1, keepdims=True) + NORM_EPS) * g


def _mod_kernel(c_ref, w_ref, b_ref, o_ref):
    cc = c_ref[...]
    s = cc * _sigmoid(cc)
    rows = s.shape[0]
    s_parts = jnp.concatenate(_split_terms(s, 3), axis=0)
    acc = b_ref[...]
    for wp in _split_terms(w_ref[...], MOD_WEIGHT_TERMS):
        t = jnp.dot(s_parts, wp, preferred_element_type=F32)
        acc = acc + (t[:rows] + t[rows:2 * rows] + t[2 * rows:])
    o_ref[...] = acc


def _mod_call(cc, ada_w, ada_b):
    rows, d = cc.shape
    n = ada_w.shape[1]
    tn = MOD_TILE
    return pl.pallas_call(
        _mod_kernel,
        grid=(n // tn,),
        in_specs=[pl.BlockSpec((rows, d), lambda i: (0, 0)),
                  pl.BlockSpec((d, tn), lambda i: (0, i)),
                  pl.BlockSpec((1, tn), lambda i: (0, i))],
        out_specs=pl.BlockSpec((rows, tn), lambda i: (0, i)),
        out_shape=jax.ShapeDtypeStruct((rows, n), F32),
        compiler_params=pltpu.CompilerParams(dimension_semantics=("arbitrary",),
                                             vmem_limit_bytes=VMEM_LIMIT),
        name="mod",
    )(cc, ada_w, ada_b)


def _prep_kernel(x_ref, xp_ref, xn_ref, ctx_ref, ms_ref, g1_ref, win_ref, cw_ref,
                 w0_ref, w2_ref, a0_ref, a2_ref, g2_ref, kk_ref, ka_ref, rk_ref,
                 cs_ref, ones_ref, tri_ref,
                 at_ref, bt_ref, kt_ref, rt_ref, v_ref, pl_ref, bonus_ref, gate_ref, z_ref):
    j = pl.program_id(1)
    refs = (x_ref, xp_ref, xn_ref, ctx_ref, ms_ref, g1_ref, win_ref, cw_ref,
            w0_ref, w2_ref, a0_ref, a2_ref, kk_ref, ka_ref, ones_ref, tri_ref,
            at_ref, bt_ref, kt_ref, rt_ref, v_ref, pl_ref)
    tiles = _run_interleaved([_prep_tile(bi, *refs) for bi in range(x_ref.shape[0])],
                             groups=PREP_BATCH, lag=1)

    @pl.when(j >= 1)
    def _():
        ones_blk = ones_ref[...]
        for bi, (u_four, r, v, gd, kd_sum) in enumerate(tiles):
            bs = _dot_exact_rhs(r * rk_ref[...] * kd_sum, ones_blk)
            bonus_ref[bi] = bs * v
            gate_ref[bi] = _bdot(_sigmoid(gd), g2_ref[...])
            z_ref[bi] = _bdot(u_four, cs_ref[...]).astype(BF16)


def _prep_tile(bi, x_ref, xp_ref, xn_ref, ctx_ref, ms_ref, g1_ref, win_ref, cw_ref,
               w0_ref, w2_ref, a0_ref, a2_ref, kk_ref, ka_ref, ones_ref, tri_ref,
               at_ref, bt_ref, kt_ref, rt_ref, v_ref, pl_ref):
    j = pl.program_id(1)
    n_j = pl.num_programs(1)
    is_ctx = j == 0
    shift = ms_ref[bi, 0, 0:1, :]
    scale = ms_ref[bi, 0, 1:2, :]
    g = g1_ref[...]

    def norm_mod(xr):
        return _rms(xr, g) * (1.0 + scale) + shift

    xin = jnp.where(is_ctx, ctx_ref[bi], x_ref[bi])
    h = norm_mod(xin).astype(BF16)
    u = jnp.dot(h, win_ref[...], preferred_element_type=F32)

    halo = jnp.concatenate([xp_ref[bi], xn_ref[bi]], axis=0)
    uh = jnp.dot(norm_mod(halo).astype(BF16), win_ref[:, RKV_LO:RKV_HI],
                 preferred_element_type=F32)
    yield
    prev_row = jnp.where(j >= 2, uh[7:8], 0.0)
    next_row = jnp.where(jnp.logical_and(j >= 1, j < n_j - 1), uh[8:9], 0.0)

    rkv = u[:, RKV_LO:RKV_HI]
    row = lax.broadcasted_iota(jnp.int32, rkv.shape, 0)
    u_m1 = jnp.where(row == 0, prev_row, pltpu.roll(rkv, 1, 0))
    u_p1 = jnp.where(row == TILE - 1, next_row, pltpu.roll(rkv, TILE - 1, 0))
    rkv = cw_ref[0:1, :] * u_m1 + cw_ref[1:2, :] * rkv + cw_ref[2:3, :] * u_p1
    r = rkv[:, 0:RWKV_W]
    k = rkv[:, RWKV_W:2 * RWKV_W]
    v = rkv[:, 2 * RWKV_W:3 * RWKV_W]

    wd = u[:, WD_LO:AD_LO]
    ad = u[:, AD_LO:GD_LO]
    gd = u[:, GD_LO:PROJ_W]
    w_lora = w0_ref[...] + _bdot(jnp.tanh(wd), w2_ref[...])
    lw = -math.exp(-0.5) * _sigmoid(w_lora)
    a = _sigmoid(a0_ref[...] + _bdot(ad, a2_ref[...]))

    ones_blk = ones_ref[...]
    kraw = k * kk_ref[...]
    kk = kraw * lax.rsqrt(_dot_exact_rhs(kraw * kraw, ones_blk) + KK_EPS)
    ka = ka_ref[...]
    kd = [k * (1.0 + (a[:, d * RWKV_W:(d + 1) * RWKV_W] - 1.0) * ka) for d in range(2)]
    yield

    v_ref[bi] = v.astype(BF16)
    for d in range(2):
        lwd = lw[:, d * RWKV_W:(d + 1) * RWKV_W]
        ad_ = a[:, d * RWKV_W:(d + 1) * RWKV_W]
        c = _dot_exact_lhs(tri_ref[d], lwd)
        e_pos = jnp.exp(c)
        e_neg = jnp.exp(-c)
        e_prev = jnp.exp(c - lwd)
        at_ref[bi, d] = (kk * e_prev).astype(BF16)
        bt_ref[bi, d] = (kk * ad_ * e_neg).astype(BF16)
        kt_ref[bi, d] = (kd[d] * e_neg).astype(BF16)
        rt_ref[bi, d] = (r * e_pos).astype(BF16)
        for q in range(CHUNKS_PER_TILE):
            last = q * CHUNK + (CHUNK - 1 if d == 0 else 0)
            pl_ref[bi, d, q] = e_pos[last:last + 1, :]
    return u[:, 0:FOURIER_W], r, v, gd, kd[0] + kd[1]


def _prep_call(x, ctx, modsel, norm1_g, w_in, conv_w, w0, w2bd, a0, a2bd, g2, k_k, k_a, r_k,
               cs, ones_blk, tri):
    b, s, d = x.shape
    n_j = 1 + s // TILE
    n_chunks = (CTX_LEN + s) // CHUNK
    t_all = CTX_LEN + s
    c2 = 2 * RWKV_W

    def full(arr):
        nd = arr.ndim
        return pl.BlockSpec(arr.shape, lambda bi, j, _n=nd: (0,) * _n)

    rows8 = TILE // 8
    nb = PREP_BATCH
    in_specs = [
        pl.BlockSpec((nb, TILE, d), lambda bi, j: (bi, jnp.maximum(j - 1, 0), 0)),
        pl.BlockSpec((nb, 8, d), lambda bi, j: (bi, jnp.maximum((j - 1) * rows8 - 1, 0), 0)),
        pl.BlockSpec((nb, 8, d), lambda bi, j: (bi, jnp.minimum(j * rows8, s // 8 - 1), 0)),
        pl.BlockSpec((nb, CTX_LEN, d), lambda bi, j: (bi, 0, 0)),
        pl.BlockSpec((nb, 1, 2, d), lambda bi, j: (bi, jnp.minimum(j, 1), 0, 0)),
        full(norm1_g), full(w_in), full(conv_w), full(w0), full(w2bd), full(a0), full(a2bd),
        full(g2), full(k_k), full(k_a), full(r_k), full(cs), full(ones_blk), full(tri),
    ]
    lat = lambda bi, j: (bi, jnp.maximum(j - 1, 0), 0)
    out_specs = [
        pl.BlockSpec((nb, 2, TILE, RWKV_W), lambda bi, j: (bi, 0, j, 0)),
        pl.BlockSpec((nb, 2, TILE, RWKV_W), lambda bi, j: (bi, 0, j, 0)),
        pl.BlockSpec((nb, 2, TILE, RWKV_W), lambda bi, j: (bi, 0, j, 0)),
        pl.BlockSpec((nb, 2, TILE, RWKV_W), lambda bi, j: (bi, 0, j, 0)),
        pl.BlockSpec((nb, TILE, RWKV_W), lambda bi, j: (bi, j, 0)),
        pl.BlockSpec((nb, 2, CHUNKS_PER_TILE, 1, RWKV_W), lambda bi, j: (bi, 0, j, 0, 0)),
        pl.BlockSpec((nb, TILE, RWKV_W), lat),
        pl.BlockSpec((nb, TILE, RWKV_W), lat),
        pl.BlockSpec((nb, TILE, 2 * FOURIER_W), lat),
    ]
    out_shape = [
        jax.ShapeDtypeStruct((b, 2, t_all, RWKV_W), BF16),
        jax.ShapeDtypeStruct((b, 2, t_all, RWKV_W), BF16),
        jax.ShapeDtypeStruct((b, 2, t_all, RWKV_W), BF16),
        jax.ShapeDtypeStruct((b, 2, t_all, RWKV_W), BF16),
        jax.ShapeDtypeStruct((b, t_all, RWKV_W), BF16),
        jax.ShapeDtypeStruct((b, 2, n_chunks, 1, RWKV_W), F32),
        jax.ShapeDtypeStruct((b, s, RWKV_W), F32),
        jax.ShapeDtypeStruct((b, s, RWKV_W), F32),
        jax.ShapeDtypeStruct((b, s, 2 * FOURIER_W), BF16),
    ]
    return pl.pallas_call(
        _prep_kernel,
        grid=(b // nb, n_j),
        in_specs=in_specs,
        out_specs=out_specs,
        out_shape=out_shape,
        compiler_params=pltpu.CompilerParams(dimension_semantics=("arbitrary", "arbitrary"),
                                             vmem_limit_bytes=VMEM_LIMIT),
        name="prep",
    )(x, x, x, ctx, modsel, norm1_g, w_in, conv_w, w0, w2bd, a0, a2bd, g2, k_k, k_a, r_k,
      cs, ones_blk, tri)


def _quad_chunk(a, bm, km, r, v, plrow, g_states, strict, incl, eye, lane_lo, mask_bd, eye_row,
                level_masks):
    n = PAIR

    def dup(xv):
        z = jnp.zeros_like(xv)
        return jnp.concatenate([jnp.where(lane_lo, xv, z), jnp.where(lane_lo, z, xv)], axis=0)

    def to_row(x2):
        return x2[:CHUNK] + x2[CHUNK:]

    def bdiag(xr):
        return jnp.where(mask_bd, jnp.concatenate([xr] * QUAD_HEADS, axis=0), jnp.zeros((), xr.dtype))

    pairs = []
    for p in range(2):
        sl = slice(p * n, (p + 1) * n)
        a2, b2, k2, r2, v2 = (dup(t[:, sl]) for t in (a, bm, km, r, v))
        rhs = jnp.concatenate([b2, k2], axis=0)
        s = _bdot_nt(jnp.concatenate([a2, r2], axis=0), rhs)
        pairs.append((sl, r2, v2, rhs, s))
    yield

    a_ab = jnp.concatenate([to_row(jnp.where(strict, s[:n, :n], 0.0)) for *_, s in pairs], axis=1)
    a_ak = jnp.concatenate([to_row(jnp.where(strict, s[:n, n:], 0.0)) for *_, s in pairs], axis=1)
    m_rbk = [jnp.concatenate([jnp.where(incl, s[n:, :n], 0.0), jnp.where(incl, s[n:, n:], 0.0)],
                             axis=1).astype(BF16) for *_, s in pairs]

    av = _bdot(a_ak, bdiag(v))
    a_lvl = [jnp.where(m, a_ab, 0.0).astype(BF16) for m in level_masks]
    t_row = jnp.where(eye_row, 1.0, 0.0) - jnp.where(level_masks[0], a_ab, 0.0)
    lvl = 1
    while lvl < len(a_lvl):
        paired = lvl + 1 < len(a_lvl)
        lhs = jnp.concatenate([a_lvl[lvl], a_lvl[lvl + 1]], axis=0) if paired else a_lvl[lvl]
        yc = jnp.dot(lhs, bdiag(t_row.astype(BF16)), preferred_element_type=F32)
        yield
        y_bd = bdiag(yc[:CHUNK].astype(BF16))
        if paired:
            both = jnp.dot(jnp.concatenate([t_row, yc[CHUNK:]], axis=0).astype(BF16), y_bd,
                           preferred_element_type=F32)
            yield
            t_row = t_row - both[:CHUNK]
            y_next = yc[CHUNK:] - both[CHUNK:]
            t_row = t_row - _bdot(t_row, bdiag(y_next.astype(BF16)))
            lvl += 2
        else:
            t_row = t_row - _bdot(t_row, y_bd)
            lvl += 1
        yield
    tx = _bdot(t_row, jnp.concatenate([bdiag(a), bdiag(av.astype(BF16))], axis=1))
    yield
    a_w = tx[:, :2 * n].astype(BF16)
    u_v = (-tx[:, 2 * n:]).astype(BF16)
    stage5 = []
    for (sl, r2, v2, rhs, _), m2 in zip(pairs, m_rbk):
        zmat = jnp.concatenate(
            [jnp.concatenate([dup(a_w[:, sl]), dup(u_v[:, sl])], axis=1),
             jnp.concatenate([jnp.zeros_like(v2), v2], axis=1)], axis=0)
        rhs_end = rhs.astype(F32) * plrow[:, sl]
        lhs = jnp.concatenate([rhs_end.T.astype(BF16), m2], axis=0)
        stage5.append(jnp.dot(lhs, zmat, preferred_element_type=F32))
    yield
    ys, g_new = [], []
    for (sl, r2, v2, rhs, _), o5, g_state in zip(pairs, stage5, g_states):
        phi = jnp.where(eye, plrow[:, sl], 0.0) - o5[:n, :n]
        r_y = r2.astype(F32) - o5[n:, :n]
        o6 = _bdot(jnp.concatenate([r_y, phi], axis=0), g_state)
        ys.append(to_row(o6[:n] + o5[n:, n:]))
        g_new.append(o6[n:] + o5[:n, n:])
    return jnp.concatenate(ys, axis=1), g_new


def _run_interleaved(gens, groups=1, lag=0):
    results = [None] * len(gens)
    active = list(range(len(gens)))
    rnd = 0
    while active:
        for i in list(active):
            if rnd < (i % groups) * lag:
                continue
            try:
                next(gens[i])
            except StopIteration as stop:
                results[i] = stop.value
                active.remove(i)
        rnd += 1
    return results


def _scan_kernel(atf, btf, ktf, rtf, vf, plf, atb, btb, ktb, rtb, vb, plb,
                 yf_ref, yb_ref, g_ref):
    s = pl.program_id(1)

    @pl.when(s == 0)
    def _():
        g_ref[...] = jnp.zeros_like(g_ref)

    ri = lax.broadcasted_iota(jnp.int32, (PAIR, PAIR), 0)
    ci = lax.broadcasted_iota(jnp.int32, (PAIR, PAIR), 1)
    same = (ri >= HEAD) == (ci >= HEAD)
    ti = ri & (HEAD - 1)
    tj = ci & (HEAD - 1)
    eye = ri == ci
    lane_lo = lax.broadcasted_iota(jnp.int32, (CHUNK, PAIR), 1) < HEAD
    masks = [(same & (tj < ti), same & (tj <= ti)),
             (same & (tj > ti), same & (tj >= ti))]
    quad_w = QUAD_HEADS * HEAD
    rq = lax.broadcasted_iota(jnp.int32, (quad_w, quad_w), 0)
    cq = lax.broadcasted_iota(jnp.int32, (quad_w, quad_w), 1)
    mask_bd = (rq // HEAD) == (cq // HEAD)
    t_i = lax.broadcasted_iota(jnp.int32, (CHUNK, quad_w), 0)
    s_i = lax.broadcasted_iota(jnp.int32, (CHUNK, quad_w), 1) & (HEAD - 1)
    eye_row = s_i == t_i
    level_masks = []
    for d in range(2):
        late, early = (t_i, s_i) if d == 0 else (s_i, t_i)
        level_masks.append([
            ((t_i >> (lvl + 1)) == (s_i >> (lvl + 1)))
            & (((late >> lvl) & 1) == 1) & (((early >> lvl) & 1) == 0)
            for lvl in range(int(math.log2(CHUNK)))])
    refs = [(atf, btf, ktf, rtf, vf, plf, yf_ref), (atb, btb, ktb, rtb, vb, plb, yb_ref)]
    gens, dests = [], []
    for bi in range(yf_ref.shape[0]):
        for d in range(2):
            at, bt, kt, rt, vv, plr, y_ref = refs[d]
            strict, incl = masks[d]
            for q in range(RWKV_W // quad_w):
                sl = slice(q * quad_w, (q + 1) * quad_w)
                gens.append(_quad_chunk(at[bi, 0, :, sl], bt[bi, 0, :, sl], kt[bi, 0, :, sl],
                                        rt[bi, 0, :, sl], vv[bi, :, sl], plr[bi, 0, 0, :, sl],
                                        [g_ref[bi, d, 2 * q], g_ref[bi, d, 2 * q + 1]],
                                        strict, incl, eye, lane_lo, mask_bd, eye_row,
                                        level_masks[d]))
                dests.append((y_ref, bi, d, q, sl))
    done = _run_interleaved(gens, groups=SCAN_GROUPS, lag=SCAN_LAG)
    for (y, g_new), (y_ref, bi, d, q, sl) in zip(done, dests):
        g_ref[bi, d, 2 * q] = g_new[0]
        g_ref[bi, d, 2 * q + 1] = g_new[1]
        y_ref[bi, :, sl] = y


def _scan_call(at, bt, kt, rt, v, plast, s_lat):
    b = at.shape[0]
    n_chunks = at.shape[2] // CHUNK
    n_ctx = CTX_LEN // CHUNK
    n_lat = s_lat // CHUNK

    def fwd_c(s):
        return s

    def bwd_c(s):
        return jnp.where(s < n_ctx, n_ctx - 1 - s, n_chunks + n_ctx - 1 - s)

    nb = SCAN_BATCH

    def dir_spec(d, cfun):
        return pl.BlockSpec((nb, 1, CHUNK, RWKV_W), lambda bi, s: (bi, d, cfun(s), 0))

    def v_spec(cfun):
        return pl.BlockSpec((nb, CHUNK, RWKV_W), lambda bi, s: (bi, cfun(s), 0))

    def pl_spec(d, cfun):
        return pl.BlockSpec((nb, 1, 1, 1, RWKV_W), lambda bi, s: (bi, d, cfun(s), 0, 0))

    in_specs = ([dir_spec(0, fwd_c)] * 4 + [v_spec(fwd_c), pl_spec(0, fwd_c)]
                + [dir_spec(1, bwd_c)] * 4 + [v_spec(bwd_c), pl_spec(1, bwd_c)])
    out_specs = [
        pl.BlockSpec((nb, CHUNK, RWKV_W), lambda bi, s: (bi, jnp.maximum(s - n_ctx, 0), 0)),
        pl.BlockSpec((nb, CHUNK, RWKV_W),
                     lambda bi, s: (bi, jnp.minimum(n_chunks - 1 - s, n_lat - 1), 0)),
    ]
    out_shape = [jax.ShapeDtypeStruct((b, s_lat, RWKV_W), F32)] * 2
    return pl.pallas_call(
        _scan_kernel,
        grid=(b // nb, n_chunks),
        in_specs=in_specs,
        out_specs=out_specs,
        out_shape=out_shape,
        scratch_shapes=[pltpu.VMEM((nb, 2, N_PAIRS, PAIR, PAIR), F32)],
        compiler_params=pltpu.CompilerParams(dimension_semantics=("arbitrary", "arbitrary"),
                                             vmem_limit_bytes=VMEM_LIMIT),
        name="scan",
    )(at, bt, kt, rt, v, plast, at, bt, kt, rt, v, plast)


def _mix_kernel(dft_ref, rev_ref, z_ref, yf_ref, yb_ref, bonus_ref, gate_ref, x_ref, mod_ref,
                gng_ref, gnb_ref, fg_ref, wout_ref, n2g_ref, ones_ref, x1_ref, h2_ref, zf_ref,
                *, fscale):
    seq = z_ref.shape[1]
    half = seq // 2
    rt = rev_ref.shape[0]

    @pl.when(pl.program_id(1) == 0)
    def _():
        mid = z_ref[0, half:half + 1, :FOURIER_W].astype(F32)
        row0 = lax.broadcasted_iota(jnp.int32, (rt, FOURIER_W), 0) == 0
        for part, sign in ((0, 1.0), (1, -1.0)):
            lanes = slice(part * FOURIER_W, (part + 1) * FOURIER_W)
            for r in range(half // rt):
                lo = seq - rt * (r + 1)
                nxt = (z_ref[0, lo + rt:lo + 2 * rt, lanes] if r > 0
                       else jnp.zeros((rt, FOURIER_W), BF16))
                rev = jnp.dot(rev_ref[...], jnp.concatenate([z_ref[0, lo:lo + rt, lanes], nxt], axis=0),
                              preferred_element_type=F32)
                folded = z_ref[0, r * rt:(r + 1) * rt, lanes].astype(F32) + sign * rev
                if part == 1 and r == 0:
                    folded = jnp.where(row0, mid, folded)
                zf_ref[part * half + r * rt:part * half + (r + 1) * rt, :] = folded.astype(BF16)

    yfour = jnp.dot(dft_ref[...], zf_ref[...], preferred_element_type=F32)
    four = _rms(yfour * fscale, fg_ref[...])

    ones_blk = ones_ref[...]
    y = yf_ref[0] + yb_ref[0]
    mu = _dot_exact_rhs(y, ones_blk) * (1.0 / HEAD)
    dy = y - mu
    var = _dot_exact_rhs(dy * dy, ones_blk) * (1.0 / HEAD)
    yn = dy * lax.rsqrt(var + GN_EPS) * gng_ref[...] + gnb_ref[...]
    rw = (yn + bonus_ref[0]) * gate_ref[0]

    out = _bdot(four, wout_ref[:FOURIER_W, :]) + _bdot(rw, wout_ref[FOURIER_W:, :])
    x1 = x_ref[0] + mod_ref[0, 2:3, :] * out
    x1_ref[0] = x1
    h2 = _rms(x1, n2g_ref[...]) * (1.0 + mod_ref[0, 4:5, :]) + mod_ref[0, 3:4, :]
    h2_ref[0] = h2.astype(BF16)


def _mix_call(dft, rev, z, yf, yb, bonus, gate, x, modx, gn_g, gn_b, f_g, w_out, norm2_g,
              ones_blk):
    b, s, d = x.shape
    tm = MIX_TILE

    def full(arr):
        nd = arr.ndim
        return pl.BlockSpec(arr.shape, lambda bi, m, _n=nd: (0,) * _n)

    tok = lambda w: pl.BlockSpec((1, tm, w), lambda bi, m: (bi, m, 0))
    in_specs = [
        pl.BlockSpec((tm, s), lambda bi, m: (m, 0)),
        full(rev),
        pl.BlockSpec((1, s, 2 * FOURIER_W), lambda bi, m: (bi, 0, 0)),
        tok(RWKV_W), tok(RWKV_W), tok(RWKV_W), tok(RWKV_W), tok(d),
        pl.BlockSpec((1, 6, d), lambda bi, m: (bi, 0, 0)),
        full(gn_g), full(gn_b), full(f_g), full(w_out), full(norm2_g), full(ones_blk),
    ]
    fscale = 1.0 / math.sqrt(s * (FOURIER_W // FOURIER_GROUPS))
    return pl.pallas_call(
        functools.partial(_mix_kernel, fscale=fscale),
        grid=(b, s // tm),
        in_specs=in_specs,
        out_specs=[tok(d), tok(d)],
        out_shape=[jax.ShapeDtypeStruct((b, s, d), F32), jax.ShapeDtypeStruct((b, s, d), BF16)],
        scratch_shapes=[pltpu.VMEM((s, FOURIER_W), BF16)],
        compiler_params=pltpu.CompilerParams(dimension_semantics=("arbitrary", "arbitrary"),
                                             vmem_limit_bytes=VMEM_LIMIT),
        name="mix",
    )(dft, rev, z, yf, yb, bonus, gate, x, modx, gn_g, gn_b, f_g, w_out, norm2_g, ones_blk)


def _ffn_up_kernel(h2_ref, wg_ref, wv_ref, cwg_ref, cwv_ref, cbg_ref, cbv_ref, act_ref):
    t = h2_ref.shape[1]
    rows = FFN_ROWS
    n_blk = t // rows
    w = jnp.concatenate([wg_ref[...], wv_ref[...]], axis=1)
    cw = jnp.concatenate([cwg_ref[...], cwv_ref[...]], axis=1).astype(BF16)
    cb = jnp.concatenate([cbg_ref[...], cbv_ref[...]], axis=1).astype(BF16)
    col = lax.broadcasted_iota(jnp.int32, (rows, 2 * FF_TILE), 0) & (GRID_W - 1)
    has_left = col > 0
    has_right = col < GRID_W - 1
    zeros = jnp.zeros((GRID_W, 2 * FF_TILE), BF16)

    def row_mixes(i):
        u = jnp.dot(h2_ref[0, i * rows:(i + 1) * rows, :], w, preferred_element_type=F32)
        u_l = jnp.where(has_left, pltpu.roll(u, 1, 0), 0.0).astype(BF16)
        u_r = jnp.where(has_right, pltpu.roll(u, rows - 1, 0), 0.0).astype(BF16)
        u_c = u.astype(BF16)
        return [cw[3 * kh:3 * kh + 1] * u_l + cw[3 * kh + 1:3 * kh + 2] * u_c
                + cw[3 * kh + 2:3 * kh + 3] * u_r for kh in range(3)]

    def finish(i, prev, cur, nxt):
        above = zeros if prev is None else prev[0][rows - GRID_W:]
        below = zeros if nxt is None else nxt[2][:GRID_W]
        c = (cur[1] + jnp.concatenate([above, cur[0][:rows - GRID_W]], axis=0)
             + jnp.concatenate([cur[2][GRID_W:], below], axis=0) + cb)
        cg = c[:, :FF_TILE]
        act_ref[0, i * rows:(i + 1) * rows, :] = cg * _sigmoid(cg) * c[:, FF_TILE:]

    mixes = [None] * (n_blk + 1)
    for i in range(n_blk):
        mixes[i] = row_mixes(i)
        if i >= 1:
            finish(i - 1, mixes[i - 2] if i >= 2 else None, mixes[i - 1], mixes[i])
    finish(n_blk - 1, mixes[n_blk - 2] if n_blk >= 2 else None, mixes[n_blk - 1], None)


def _ffn_up_call(h2, w_up, conv_w9, conv_b):
    b, s, d = h2.shape
    n_f = D_FF // FF_TILE
    in_specs = [
        pl.BlockSpec((1, s, d), lambda bi, f: (bi, 0, 0)),
        pl.BlockSpec((d, FF_TILE), lambda bi, f: (0, f)),
        pl.BlockSpec((d, FF_TILE), lambda bi, f: (0, n_f + f)),
        pl.BlockSpec((9, FF_TILE), lambda bi, f: (0, f)),
        pl.BlockSpec((9, FF_TILE), lambda bi, f: (0, n_f + f)),
        pl.BlockSpec((1, FF_TILE), lambda bi, f: (0, f)),
        pl.BlockSpec((1, FF_TILE), lambda bi, f: (0, n_f + f)),
    ]
    return pl.pallas_call(
        _ffn_up_kernel,
        grid=(b, n_f),
        in_specs=in_specs,
        out_specs=pl.BlockSpec((1, s, FF_TILE), lambda bi, f: (bi, 0, f)),
        out_shape=jax.ShapeDtypeStruct((b, s, D_FF), BF16),
        compiler_params=pltpu.CompilerParams(dimension_semantics=("arbitrary", "arbitrary"),
                                             vmem_limit_bytes=VMEM_LIMIT),
        name="ffn_up",
    )(h2, w_up, w_up, conv_w9, conv_w9, conv_b, conv_b)


def _ffn_down_kernel(act_ref, wd_ref, x1_ref, mod_ref, fg_ref, out_ref):
    y = jnp.dot(act_ref[0], wd_ref[...], preferred_element_type=F32)
    out_ref[0] = _rms(x1_ref[0] + mod_ref[0, 5:6, :] * y, fg_ref[...])


def _ffn_down_call(act, w_down, x1, modx, final_g):
    b, s, d = x1.shape
    tm = DOWN_TILE
    in_specs = [
        pl.BlockSpec((1, tm, D_FF), lambda bi, m: (bi, m, 0)),
        pl.BlockSpec((D_FF, d), lambda bi, m: (0, 0)),
        pl.BlockSpec((1, tm, d), lambda bi, m: (bi, m, 0)),
        pl.BlockSpec((1, 6, d), lambda bi, m: (bi, 0, 0)),
        pl.BlockSpec((1, d), lambda bi, m: (0, 0)),
    ]
    return pl.pallas_call(
        _ffn_down_kernel,
        grid=(b, s // tm),
        in_specs=in_specs,
        out_specs=pl.BlockSpec((1, tm, d), lambda bi, m: (bi, m, 0)),
        out_shape=jax.ShapeDtypeStruct((b, s, d), F32),
        compiler_params=pltpu.CompilerParams(dimension_semantics=("arbitrary", "arbitrary"),
                                             vmem_limit_bytes=VMEM_LIMIT),
        name="ffn_down",
    )(act, w_down, x1, modx, final_g)


@functools.lru_cache(maxsize=None)
def _constants(seq):
    gw = FOURIER_W // FOURIER_GROUPS
    nm = np.outer(np.arange(gw), np.arange(gw)) % gw
    ang = 2.0 * np.pi * nm / gw
    cs = np.zeros((FOURIER_W, 2 * FOURIER_W), np.float64)
    for gi in range(FOURIER_GROUPS):
        sl = slice(gi * gw, (gi + 1) * gw)
        cs[sl, sl] = np.cos(ang)
        cs[sl, FOURIER_W + gi * gw:FOURIER_W + (gi + 1) * gw] = np.sin(ang)
    half = seq // 2
    k_idx = np.arange(seq, dtype=np.int64)[:, None]
    ang_c = 2.0 * np.pi * ((k_idx * np.arange(0, half + 1, dtype=np.int64)[None, :]) % seq) / seq
    ang_s = 2.0 * np.pi * ((k_idx * np.arange(1, half, dtype=np.int64)[None, :]) % seq) / seq
    dft = np.concatenate([np.cos(ang_c), -np.sin(ang_s)], axis=1)
    ri = np.arange(REV_TILE)
    rev = (np.arange(2 * REV_TILE)[None, :] == (REV_TILE - ri)[:, None]).astype(np.float64)
    head_id = np.arange(RWKV_W) // HEAD
    ones_blk = (head_id[:, None] == head_id[None, :]).astype(np.float64)
    ti = np.arange(TILE)
    same_chunk = (ti[:, None] // CHUNK) == (ti[None, :] // CHUNK)
    tri = np.stack([same_chunk & (ti[None, :] <= ti[:, None]),
                    same_chunk & (ti[None, :] >= ti[:, None])]).astype(np.float64)
    to_bf16 = lambda arr: np.asarray(arr, dtype=np.float32).astype(BF16)
    return (np.asarray(cs, dtype=np.float32), np.asarray(dft, dtype=np.float32),
            to_bf16(ones_blk), to_bf16(tri), to_bf16(rev))


def _block_diag_dirs(w):
    z = jnp.zeros_like(w[0])
    return jnp.concatenate([jnp.concatenate([w[0], z], axis=1),
                            jnp.concatenate([z, w[1]], axis=1)], axis=0)


def kernel(x, c, ctx, c_ctx, ada_w, ada_b, norm1_g, norm2_g, w_in, rwkv_conv_w, decay_w0, decay_w2, iclr_a0, iclr_a2, gate_g2, k_k, k_a, r_k, gn_g, gn_b, fourier_g, w_out, ffn_w_up, ffn_conv_w, ffn_conv_b, ffn_w_down, final_g):
    b, s, d = x.shape
    assert ada_w.shape[0] == 1, "single-layer configuration"
    assert (b, s, d) == (c.shape[0], s, D_MODEL) and ctx.shape == (b, CTX_LEN, d)
    assert s % TILE == 0 and CTX_LEN == TILE
    cs, dft, ones_blk, tri, rev = (jnp.asarray(t) for t in _constants(s))
    cs, dft = cs.astype(BF16), dft.astype(BF16)

    cc = jnp.concatenate([c, c_ctx[None, :], jnp.zeros((16 - b - 1, d), F32)], axis=0)
    mod = _mod_call(cc, ada_w[0], ada_b[0][None, :])
    modx = mod[:b].reshape(b, 6, d)
    modc = jnp.broadcast_to(mod[b].reshape(1, 6, d)[:, :2], (b, 2, d))
    modsel = jnp.stack([modc, modx[:, :2]], axis=1)

    row = lambda t: t.reshape(1, -1)
    at, bt, kt, rt, v, plast, bonus, gate, z = _prep_call(
        x, ctx, modsel, row(norm1_g[0]), w_in[0].astype(BF16), rwkv_conv_w[0],
        row(decay_w0[0]), _block_diag_dirs(decay_w2[0]).astype(BF16),
        row(iclr_a0[0]), _block_diag_dirs(iclr_a2[0]).astype(BF16),
        gate_g2[0].astype(BF16), row(k_k[0]), row(k_a[0]), row(r_k[0]), cs, ones_blk, tri)

    yf, yb = _scan_call(at, bt, kt, rt, v, plast, s)

    x1, h2 = _mix_call(dft, rev, z, yf, yb, bonus, gate, x, modx, row(gn_g[0]), row(gn_b[0]),
                       row(fourier_g[0]), w_out[0].astype(BF16), row(norm2_g[0]), ones_blk)

    act = _ffn_up_call(h2, ffn_w_up[0].astype(BF16), ffn_conv_w[0].reshape(9, 2 * D_FF),
                       row(ffn_conv_b[0]))
    return _ffn_down_call(act, ffn_w_down[0].astype(BF16), x1, modx, row(final_g))
```

```python
import functools
import math

import numpy as np
import jax
import jax.numpy as jnp
from jax import lax
from jax.experimental import pallas as pl
from jax.experimental.pallas import tpu as pltpu

F32 = jnp.float32
BF16 = jnp.bfloat16

D_MODEL = 1024
CTX_LEN = 256
GRID_W = 64
FOURIER_W = 512
FOURIER_GROUPS = 8
RWKV_W = 512
HEAD = 64
N_HEADS = 8
DECAY_RANK = 64
ICLR_RANK = 64
GATE_RANK = 128
D_FF = 2816
NORM_EPS = 1e-6
GN_EPS = 64e-5
KK_EPS = 1e-12

CHUNK = 64
TILE = 256
MIX_TILE = 512
REV_TILE = 256
CHUNKS_PER_TILE = TILE // CHUNK
PREP_BATCH = 2
PAIR = 2 * HEAD
N_PAIRS = RWKV_W // PAIR
QUAD_HEADS = 4
SCAN_BATCH = 8
SCAN_GROUPS = 4
SCAN_LAG = 3
MOD_TILE = 1536
FF_TILE = 1408
FFN_ROWS = 256
DOWN_TILE = 1024
VMEM_LIMIT = 56 * 1024 * 1024
HEAD_SUM_TERMS = 1
CUMSUM_TERMS = 2
MOD_WEIGHT_TERMS = 2

RKV_LO = FOURIER_W
RKV_HI = FOURIER_W + 3 * RWKV_W
WD_LO = RKV_HI
AD_LO = WD_LO + 2 * DECAY_RANK
GD_LO = AD_LO + 2 * ICLR_RANK
PROJ_W = GD_LO + GATE_RANK


def _bdot(a, b):
    return jnp.dot(a.astype(BF16), b.astype(BF16), preferred_element_type=F32)


def _bdot_nt(a, b):
    return lax.dot_general(a.astype(BF16), b.astype(BF16), (((1,), (1,)), ((), ())),
                           preferred_element_type=F32)


def _bdot_tn(a, b):
    return lax.dot_general(a.astype(BF16), b.astype(BF16), (((0,), (0,)), ((), ())),
                           preferred_element_type=F32)


def _split_terms(a, terms):
    out, rem = [], a
    for _ in range(terms):
        hi = rem.astype(BF16)
        out.append(hi)
        rem = rem - hi.astype(F32)
    return out


def _dot_exact_rhs(a, b_exact, terms=HEAD_SUM_TERMS):
    acc = None
    for piece in _split_terms(a, terms):
        t = jnp.dot(piece, b_exact, preferred_element_type=F32)
        acc = t if acc is None else acc + t
    return acc


def _dot_exact_lhs(a_exact, b, terms=CUMSUM_TERMS):
    acc = None
    for piece in _split_terms(b, terms):
        t = jnp.dot(a_exact, piece, preferred_element_type=F32)
        acc = t if acc is None else acc + t
    return acc


def _sigmoid(x):
    return 1.0 / (1.0 + jnp.exp(-x))


def _rms(x, g):
    return x * lax.rsqrt(jnp.mean(x * x, axis=-1, keepdims=True) + NORM_EPS) * g


def _mod_kernel(c_ref, w_ref, b_ref, o_ref):
    cc = c_ref[...]
    s = cc * _sigmoid(cc)
    rows = s.shape[0]
    s_parts = jnp.concatenate(_split_terms(s, 3), axis=0)
    acc = b_ref[...]
    for wp in _split_terms(w_ref[...], MOD_WEIGHT_TERMS):
        t = jnp.dot(s_parts, wp, preferred_element_type=F32)
        acc = acc + (t[:rows] + t[rows:2 * rows] + t[2 * rows:])
    o_ref[...] = acc


def _mod_call(cc, ada_w, ada_b):
    rows, d = cc.shape
    n = ada_w.shape[1]
    tn = MOD_TILE
    return pl.pallas_call(
        _mod_kernel,
        grid=(n // tn,),
        in_specs=[pl.BlockSpec((rows, d), lambda i: (0, 0)),
                  pl.BlockSpec((d, tn), lambda i: (0, i)),
                  pl.BlockSpec((1, tn), lambda i: (0, i))],
        out_specs=pl.BlockSpec((rows, tn), lambda i: (0, i)),
        out_shape=jax.ShapeDtypeStruct((rows, n), F32),
        compiler_params=pltpu.CompilerParams(dimension_semantics=("arbitrary",),
                                             vmem_limit_bytes=VMEM_LIMIT),
        name="mod",
    )(cc, ada_w, ada_b)


def _prep_kernel(x_ref, xp_ref, xn_ref, ctx_ref, ms_ref, g1_ref, win_ref, cw_ref,
                 w0_ref, w2_ref, a0_ref, a2_ref, g2_ref, kk_ref, ka_ref, rk_ref,
                 cs_ref, ones_ref, tri_ref,
                 at_ref, bt_ref, kt_ref, rt_ref, v_ref, pl_ref, bonus_ref, gate_ref, z_ref):
    j = pl.program_id(1)
    refs = (x_ref, xp_ref, xn_ref, ctx_ref, ms_ref, g1_ref, win_ref, cw_ref,
            w0_ref, w2_ref, a0_ref, a2_ref, kk_ref, ka_ref, ones_ref, tri_ref,
            at_ref, bt_ref, kt_ref, rt_ref, v_ref, pl_ref)
    tiles = _run_interleaved([_prep_tile(bi, *refs) for bi in range(x_ref.shape[0])],
                             groups=PREP_BATCH, lag=2)

    @pl.when(j >= 1)
    def _():
        ones_blk = ones_ref[...]
        for bi, (u_four, r, v, gd, kd_sum) in enumerate(tiles):
            bs = _dot_exact_rhs(r * rk_ref[...] * kd_sum, ones_blk)
            bonus_ref[bi] = bs * v
            gate_ref[bi] = _bdot(_sigmoid(gd), g2_ref[...])
            z_ref[bi] = _bdot(u_four, cs_ref[...]).astype(BF16)


def _prep_tile(bi, x_ref, xp_ref, xn_ref, ctx_ref, ms_ref, g1_ref, win_ref, cw_ref,
               w0_ref, w2_ref, a0_ref, a2_ref, kk_ref, ka_ref, ones_ref, tri_ref,
               at_ref, bt_ref, kt_ref, rt_ref, v_ref, pl_ref):
    j = pl.program_id(1)
    n_j = pl.num_programs(1)
    is_ctx = j == 0
    shift = ms_ref[bi, 0, 0:1, :]
    scale = ms_ref[bi, 0, 1:2, :]
    g = g1_ref[...]

    def norm_mod(xr):
        return _rms(xr, g) * (1.0 + scale) + shift

    xin = jnp.where(is_ctx, ctx_ref[bi], x_ref[bi])
    h = norm_mod(xin).astype(BF16)
    u = jnp.dot(h, win_ref[...], preferred_element_type=F32)

    halo = jnp.concatenate([xp_ref[bi], xn_ref[bi]], axis=0)
    uh = jnp.dot(norm_mod(halo).astype(BF16), win_ref[:, RKV_LO:RKV_HI],
                 preferred_element_type=F32)
    yield
    prev_row = jnp.where(j >= 2, uh[7:8], 0.0)
    next_row = jnp.where(jnp.logical_and(j >= 1, j < n_j - 1), uh[8:9], 0.0)

    rkv = u[:, RKV_LO:RKV_HI]
    row = lax.broadcasted_iota(jnp.int32, rkv.shape, 0)
    u_m1 = jnp.where(row == 0, prev_row, pltpu.roll(rkv, 1, 0))
    u_p1 = jnp.where(row == TILE - 1, next_row, pltpu.roll(rkv, TILE - 1, 0))
    rkv = cw_ref[0:1, :] * u_m1 + cw_ref[1:2, :] * rkv + cw_ref[2:3, :] * u_p1
    r = rkv[:, 0:RWKV_W]
    k = rkv[:, RWKV_W:2 * RWKV_W]
    v = rkv[:, 2 * RWKV_W:3 * RWKV_W]

    wd = u[:, WD_LO:AD_LO]
    ad = u[:, AD_LO:GD_LO]
    gd = u[:, GD_LO:PROJ_W]
    w_lora = w0_ref[...] + _bdot(jnp.tanh(wd), w2_ref[...])
    lw = -math.exp(-0.5) * _sigmoid(w_lora)
    a = _sigmoid(a0_ref[...] + _bdot(ad, a2_ref[...]))

    ones_blk = ones_ref[...]
    kraw = k * kk_ref[...]
    kk = kraw * lax.rsqrt(_dot_exact_rhs(kraw * kraw, ones_blk) + KK_EPS)
    ka = ka_ref[...]
    kd = [k * (1.0 + (a[:, d * RWKV_W:(d + 1) * RWKV_W] - 1.0) * ka) for d in range(2)]
    yield

    v_ref[bi] = v.astype(BF16)
    for d in range(2):
        lwd = lw[:, d * RWKV_W:(d + 1) * RWKV_W]
        ad_ = a[:, d * RWKV_W:(d + 1) * RWKV_W]
        c = _dot_exact_lhs(tri_ref[d], lwd)
        e_pos = jnp.exp(c)
        e_neg = jnp.exp(-c)
        e_prev = jnp.exp(c - lwd)
        at_ref[bi, d] = (kk * e_prev).astype(BF16)
        bt_ref[bi, d] = (kk * ad_ * e_neg).astype(BF16)
        kt_ref[bi, d] = (kd[d] * e_neg).astype(BF16)
        rt_ref[bi, d] = (r * e_pos).astype(BF16)
        for q in range(CHUNKS_PER_TILE):
            last = q * CHUNK + (CHUNK - 1 if d == 0 else 0)
            pl_ref[bi, d, q] = e_pos[last:last + 1, :]
    return u[:, 0:FOURIER_W], r, v, gd, kd[0] + kd[1]


def _prep_call(x, ctx, modsel, norm1_g, w_in, conv_w, w0, w2bd, a0, a2bd, g2, k_k, k_a, r_k,
               cs, ones_blk, tri):
    b, s, d = x.shape
    n_j = 1 + s // TILE
    n_chunks = (CTX_LEN + s) // CHUNK
    t_all = CTX_LEN + s
    c2 = 2 * RWKV_W

    def full(arr):
        nd = arr.ndim
        return pl.BlockSpec(arr.shape, lambda bi, j, _n=nd: (0,) * _n)

    rows8 = TILE // 8
    nb = PREP_BATCH
    in_specs = [
        pl.BlockSpec((nb, TILE, d), lambda bi, j: (bi, jnp.maximum(j - 1, 0), 0)),
        pl.BlockSpec((nb, 8, d), lambda bi, j: (bi, jnp.maximum((j - 1) * rows8 - 1, 0), 0)),
        pl.BlockSpec((nb, 8, d), lambda bi, j: (bi, jnp.minimum(j * rows8, s // 8 - 1), 0)),
        pl.BlockSpec((nb, CTX_LEN, d), lambda bi, j: (bi, 0, 0)),
        pl.BlockSpec((nb, 1, 2, d), lambda bi, j: (bi, jnp.minimum(j, 1), 0, 0)),
        full(norm1_g), full(w_in), full(conv_w), full(w0), full(w2bd), full(a0), full(a2bd),
        full(g2), full(k_k), full(k_a), full(r_k), full(cs), full(ones_blk), full(tri),
    ]
    lat = lambda bi, j: (bi, jnp.maximum(j - 1, 0), 0)
    out_specs = [
        pl.BlockSpec((nb, 2, TILE, RWKV_W), lambda bi, j: (bi, 0, j, 0)),
        pl.BlockSpec((nb, 2, TILE, RWKV_W), lambda bi, j: (bi, 0, j, 0)),
        pl.BlockSpec((nb, 2, TILE, RWKV_W), lambda bi, j: (bi, 0, j, 0)),
        pl.BlockSpec((nb, 2, TILE, RWKV_W), lambda bi, j: (bi, 0, j, 0)),
        pl.BlockSpec((nb, TILE, RWKV_W), lambda bi, j: (bi, j, 0)),
        pl.BlockSpec((nb, 2, CHUNKS_PER_TILE, 1, RWKV_W), lambda bi, j: (bi, 0, j, 0, 0)),
        pl.BlockSpec((nb, TILE, RWKV_W), lat),
        pl.BlockSpec((nb, TILE, RWKV_W), lat),
        pl.BlockSpec((nb, TILE, 2 * FOURIER_W), lat),
    ]
    out_shape = [
        jax.ShapeDtypeStruct((b, 2, t_all, RWKV_W), BF16),
        jax.ShapeDtypeStruct((b, 2, t_all, RWKV_W), BF16),
        jax.ShapeDtypeStruct((b, 2, t_all, RWKV_W), BF16),
        jax.ShapeDtypeStruct((b, 2, t_all, RWKV_W), BF16),
        jax.ShapeDtypeStruct((b, t_all, RWKV_W), BF16),
        jax.ShapeDtypeStruct((b, 2, n_chunks, 1, RWKV_W), F32),
        jax.ShapeDtypeStruct((b, s, RWKV_W), F32),
        jax.ShapeDtypeStruct((b, s, RWKV_W), F32),
        jax.ShapeDtypeStruct((b, s, 2 * FOURIER_W), BF16),
    ]
    return pl.pallas_call(
        _prep_kernel,
        grid=(b // nb, n_j),
        in_specs=in_specs,
        out_specs=out_specs,
        out_shape=out_shape,
        compiler_params=pltpu.CompilerParams(dimension_semantics=("arbitrary", "arbitrary"),
                                             vmem_limit_bytes=VMEM_LIMIT),
        name="prep",
    )(x, x, x, ctx, modsel, norm1_g, w_in, conv_w, w0, w2bd, a0, a2bd, g2, k_k, k_a, r_k,
      cs, ones_blk, tri)


def _quad_chunk(a, bm, km, r, v, plrow, g_states, strict, incl, eye, lane_lo, mask_bd, eye_row,
                level_masks):
    n = PAIR

    def dup(xv):
        z = jnp.zeros_like(xv)
        return jnp.concatenate([jnp.where(lane_lo, xv, z), jnp.where(lane_lo, z, xv)], axis=0)

    def to_row(x2):
        return x2[:CHUNK] + x2[CHUNK:]

    def bdiag(xr):
        return jnp.where(mask_bd, jnp.concatenate([xr] * QUAD_HEADS, axis=0), jnp.zeros((), xr.dtype))

    pairs = []
    for p in range(2):
        sl = slice(p * n, (p + 1) * n)
        a2, b2, k2, r2, v2 = (dup(t[:, sl]) for t in (a, bm, km, r, v))
        rhs = jnp.concatenate([b2, k2], axis=0)
        s = _bdot_nt(jnp.concatenate([a2, r2], axis=0), rhs)
        pairs.append((sl, r2, v2, rhs, s))
    yield

    a_ab = jnp.concatenate([to_row(jnp.where(strict, s[:n, :n], 0.0)) for *_, s in pairs], axis=1)
    a_ak = jnp.concatenate([to_row(jnp.where(strict, s[:n, n:], 0.0)) for *_, s in pairs], axis=1)
    m_rbk = [jnp.concatenate([jnp.where(incl, s[n:, :n], 0.0), jnp.where(incl, s[n:, n:], 0.0)],
                             axis=1).astype(BF16) for *_, s in pairs]

    av = _bdot(a_ak, bdiag(v))
    a_lvl = [jnp.where(m, a_ab, 0.0).astype(BF16) for m in level_masks]
    t_row = jnp.where(eye_row, 1.0, 0.0) - jnp.where(level_masks[0], a_ab, 0.0)
    lvl = 1
    while lvl < len(a_lvl):
        paired = lvl + 1 < len(a_lvl)
        lhs = jnp.concatenate([a_lvl[lvl], a_lvl[lvl + 1]], axis=0) if paired else a_lvl[lvl]
        yc = jnp.dot(lhs, bdiag(t_row.astype(BF16)), preferred_element_type=F32)
        yield
        y_bd = bdiag(yc[:CHUNK].astype(BF16))
        if paired:
            both = jnp.dot(jnp.concatenate([t_row, yc[CHUNK:]], axis=0).astype(BF16), y_bd,
                           preferred_element_type=F32)
            yield
            t_row = t_row - both[:CHUNK]
            y_next = yc[CHUNK:] - both[CHUNK:]
            t_row = t_row - _bdot(t_row, bdiag(y_next.astype(BF16)))
            lvl += 2
        else:
            t_row = t_row - _bdot(t_row, y_bd)
            lvl += 1
        yield
    tx = _bdot(t_row, jnp.concatenate([bdiag(a), bdiag(av.astype(BF16))], axis=1))
    yield
    a_w = tx[:, :2 * n].astype(BF16)
    u_v = (-tx[:, 2 * n:]).astype(BF16)
    stage5 = []
    for (sl, r2, v2, rhs, _), m2 in zip(pairs, m_rbk):
        zmat = jnp.concatenate(
            [jnp.concatenate([dup(a_w[:, sl]), dup(u_v[:, sl])], axis=1),
             jnp.concatenate([jnp.zeros_like(v2), v2], axis=1)], axis=0)
        rhs_end = rhs.astype(F32) * plrow[:, sl]
        lhs = jnp.concatenate([rhs_end.T.astype(BF16), m2], axis=0)
        stage5.append(jnp.dot(lhs, zmat, preferred_element_type=F32))
    yield
    ys, g_new = [], []
    for (sl, r2, v2, rhs, _), o5, g_state in zip(pairs, stage5, g_states):
        phi = jnp.where(eye, plrow[:, sl], 0.0) - o5[:n, :n]
        r_y = r2.astype(F32) - o5[n:, :n]
        o6 = _bdot(jnp.concatenate([r_y, phi], axis=0), g_state)
        ys.append(to_row(o6[:n] + o5[n:, n:]))
        g_new.append(o6[n:] + o5[:n, n:])
    return jnp.concatenate(ys, axis=1), g_new


def _run_interleaved(gens, groups=1, lag=0):
    results = [None] * len(gens)
    active = list(range(len(gens)))
    rnd = 0
    while active:
        for i in list(active):
            if rnd < (i % groups) * lag:
                continue
            try:
                next(gens[i])
            except StopIteration as stop:
                results[i] = stop.value
                active.remove(i)
        rnd += 1
    return results


def _scan_kernel(atf, btf, ktf, rtf, vf, plf, atb, btb, ktb, rtb, vb, plb,
                 yf_ref, yb_ref, g_ref):
    s = pl.program_id(1)

    @pl.when(s == 0)
    def _():
        g_ref[...] = jnp.zeros_like(g_ref)

    ri = lax.broadcasted_iota(jnp.int32, (PAIR, PAIR), 0)
    ci = lax.broadcasted_iota(jnp.int32, (PAIR, PAIR), 1)
    same = (ri >= HEAD) == (ci >= HEAD)
    ti = ri & (HEAD - 1)
    tj = ci & (HEAD - 1)
    eye = ri == ci
    lane_lo = lax.broadcasted_iota(jnp.int32, (CHUNK, PAIR), 1) < HEAD
    masks = [(same & (tj < ti), same & (tj <= ti)),
             (same & (tj > ti), same & (tj >= ti))]
    quad_w = QUAD_HEADS * HEAD
    rq = lax.broadcasted_iota(jnp.int32, (quad_w, quad_w), 0)
    cq = lax.broadcasted_iota(jnp.int32, (quad_w, quad_w), 1)
    mask_bd = (rq // HEAD) == (cq // HEAD)
    t_i = lax.broadcasted_iota(jnp.int32, (CHUNK, quad_w), 0)
    s_i = lax.broadcasted_iota(jnp.int32, (CHUNK, quad_w), 1) & (HEAD - 1)
    eye_row = s_i == t_i
    level_masks = []
    for d in range(2):
        late, early = (t_i, s_i) if d == 0 else (s_i, t_i)
        level_masks.append([
            ((t_i >> (lvl + 1)) == (s_i >> (lvl + 1)))
            & (((late >> lvl) & 1) == 1) & (((early >> lvl) & 1) == 0)
            for lvl in range(int(math.log2(CHUNK)))])
    refs = [(atf, btf, ktf, rtf, vf, plf, yf_ref), (atb, btb, ktb, rtb, vb, plb, yb_ref)]
    gens, dests = [], []
    for bi in range(yf_ref.shape[0]):
        for d in range(2):
            at, bt, kt, rt, vv, plr, y_ref = refs[d]
            strict, incl = masks[d]
            for q in range(RWKV_W // quad_w):
                sl = slice(q * quad_w, (q + 1) * quad_w)
                gens.append(_quad_chunk(at[bi, 0, :, sl], bt[bi, 0, :, sl], kt[bi, 0, :, sl],
                                        rt[bi, 0, :, sl], vv[bi, :, sl], plr[bi, 0, 0, :, sl],
                                        [g_ref[bi, d, 2 * q], g_ref[bi, d, 2 * q + 1]],
                                        strict, incl, eye, lane_lo, mask_bd, eye_row,
                                        level_masks[d]))
                dests.append((y_ref, bi, d, q, sl))
    done = _run_interleaved(gens, groups=SCAN_GROUPS, lag=SCAN_LAG)
    for (y, g_new), (y_ref, bi, d, q, sl) in zip(done, dests):
        g_ref[bi, d, 2 * q] = g_new[0]
        g_ref[bi, d, 2 * q + 1] = g_new[1]
        y_ref[bi, :, sl] = y


def _scan_call(at, bt, kt, rt, v, plast, s_lat):
    b = at.shape[0]
    n_chunks = at.shape[2] // CHUNK
    n_ctx = CTX_LEN // CHUNK
    n_lat = s_lat // CHUNK

    def fwd_c(s):
        return s

    def bwd_c(s):
        return jnp.where(s < n_ctx, n_ctx - 1 - s, n_chunks + n_ctx - 1 - s)

    nb = SCAN_BATCH

    def dir_spec(d, cfun):
        return pl.BlockSpec((nb, 1, CHUNK, RWKV_W), lambda bi, s: (bi, d, cfun(s), 0))

    def v_spec(cfun):
        return pl.BlockSpec((nb, CHUNK, RWKV_W), lambda bi, s: (bi, cfun(s), 0))

    def pl_spec(d, cfun):
        return pl.BlockSpec((nb, 1, 1, 1, RWKV_W), lambda bi, s: (bi, d, cfun(s), 0, 0))

    in_specs = ([dir_spec(0, fwd_c)] * 4 + [v_spec(fwd_c), pl_spec(0, fwd_c)]
                + [dir_spec(1, bwd_c)] * 4 + [v_spec(bwd_c), pl_spec(1, bwd_c)])
    out_specs = [
        pl.BlockSpec((nb, CHUNK, RWKV_W), lambda bi, s: (bi, jnp.maximum(s - n_ctx, 0), 0)),
        pl.BlockSpec((nb, CHUNK, RWKV_W),
                     lambda bi, s: (bi, jnp.minimum(n_chunks - 1 - s, n_lat - 1), 0)),
    ]
    out_shape = [jax.ShapeDtypeStruct((b, s_lat, RWKV_W), F32)] * 2
    return pl.pallas_call(
        _scan_kernel,
        grid=(b // nb, n_chunks),
        in_specs=in_specs,
        out_specs=out_specs,
        out_shape=out_shape,
        scratch_shapes=[pltpu.VMEM((nb, 2, N_PAIRS, PAIR, PAIR), F32)],
        compiler_params=pltpu.CompilerParams(dimension_semantics=("arbitrary", "arbitrary"),
                                             vmem_limit_bytes=VMEM_LIMIT),
        name="scan",
    )(at, bt, kt, rt, v, plast, at, bt, kt, rt, v, plast)


def _mix_kernel(dft_ref, rev_ref, z_ref, yf_ref, yb_ref, bonus_ref, gate_ref, x_ref, mod_ref,
                gng_ref, gnb_ref, fg_ref, wout_ref, n2g_ref, ones_ref, x1_ref, h2_ref, zf_ref,
                *, fscale):
    seq = z_ref.shape[1]
    half = seq // 2
    rt = rev_ref.shape[0]

    @pl.when(pl.program_id(1) == 0)
    def _():
        mid = z_ref[0, half:half + 1, :FOURIER_W].astype(F32)
        row0 = lax.broadcasted_iota(jnp.int32, (rt, FOURIER_W), 0) == 0
        for part, sign in ((0, 1.0), (1, -1.0)):
            lanes = slice(part * FOURIER_W, (part + 1) * FOURIER_W)
            for r in range(half // rt):
                lo = seq - rt * (r + 1)
                nxt = (z_ref[0, lo + rt:lo + 2 * rt, lanes] if r > 0
                       else jnp.zeros((rt, FOURIER_W), BF16))
                rev = jnp.dot(rev_ref[...], jnp.concatenate([z_ref[0, lo:lo + rt, lanes], nxt], axis=0),
                              preferred_element_type=F32)
                folded = z_ref[0, r * rt:(r + 1) * rt, lanes].astype(F32) + sign * rev
                if part == 1 and r == 0:
                    folded = jnp.where(row0, mid, folded)
                zf_ref[part * half + r * rt:part * half + (r + 1) * rt, :] = folded.astype(BF16)

    yfour = jnp.dot(dft_ref[...], zf_ref[...], preferred_element_type=F32)
    four = _rms(yfour * fscale, fg_ref[...])

    ones_blk = ones_ref[...]
    y = yf_ref[0] + yb_ref[0]
    mu = _dot_exact_rhs(y, ones_blk) * (1.0 / HEAD)
    dy = y - mu
    var = _dot_exact_rhs(dy * dy, ones_blk) * (1.0 / HEAD)
    yn = dy * lax.rsqrt(var + GN_EPS) * gng_ref[...] + gnb_ref[...]
    rw = (yn + bonus_ref[0]) * gate_ref[0]

    out = _bdot(four, wout_ref[:FOURIER_W, :]) + _bdot(rw, wout_ref[FOURIER_W:, :])
    x1 = x_ref[0] + mod_ref[0, 2:3, :] * out
    x1_ref[0] = x1
    h2 = _rms(x1, n2g_ref[...]) * (1.0 + mod_ref[0, 4:5, :]) + mod_ref[0, 3:4, :]
    h2_ref[0] = h2.astype(BF16)


def _mix_call(dft, rev, z, yf, yb, bonus, gate, x, modx, gn_g, gn_b, f_g, w_out, norm2_g,
              ones_blk):
    b, s, d = x.shape
    tm = MIX_TILE

    def full(arr):
        nd = arr.ndim
        return pl.BlockSpec(arr.shape, lambda bi, m, _n=nd: (0,) * _n)

    tok = lambda w: pl.BlockSpec((1, tm, w), lambda bi, m: (bi, m, 0))
    in_specs = [
        pl.BlockSpec((tm, s), lambda bi, m: (m, 0)),
        full(rev),
        pl.BlockSpec((1, s, 2 * FOURIER_W), lambda bi, m: (bi, 0, 0)),
        tok(RWKV_W), tok(RWKV_W), tok(RWKV_W), tok(RWKV_W), tok(d),
        pl.BlockSpec((1, 6, d), lambda bi, m: (bi, 0, 0)),
        full(gn_g), full(gn_b), full(f_g), full(w_out), full(norm2_g), full(ones_blk),
    ]
    fscale = 1.0 / math.sqrt(s * (FOURIER_W // FOURIER_GROUPS))
    return pl.pallas_call(
        functools.partial(_mix_kernel, fscale=fscale),
        grid=(b, s // tm),
        in_specs=in_specs,
        out_specs=[tok(d), tok(d)],
        out_shape=[jax.ShapeDtypeStruct((b, s, d), F32), jax.ShapeDtypeStruct((b, s, d), BF16)],
        scratch_shapes=[pltpu.VMEM((s, FOURIER_W), BF16)],
        compiler_params=pltpu.CompilerParams(dimension_semantics=("arbitrary", "arbitrary"),
                                             vmem_limit_bytes=VMEM_LIMIT),
        name="mix",
    )(dft, rev, z, yf, yb, bonus, gate, x, modx, gn_g, gn_b, f_g, w_out, norm2_g, ones_blk)


def _ffn_up_kernel(h2_ref, wg_ref, wv_ref, cwg_ref, cwv_ref, cbg_ref, cbv_ref, act_ref):
    t = h2_ref.shape[1]
    rows = FFN_ROWS
    n_blk = t // rows
    w = jnp.concatenate([wg_ref[...], wv_ref[...]], axis=1)
    cw = jnp.concatenate([cwg_ref[...], cwv_ref[...]], axis=1).astype(BF16)
    cb = jnp.concatenate([cbg_ref[...], cbv_ref[...]], axis=1).astype(BF16)
    col = lax.broadcasted_iota(jnp.int32, (rows, 2 * FF_TILE), 0) & (GRID_W - 1)
    has_left = col > 0
    has_right = col < GRID_W - 1
    zeros = jnp.zeros((GRID_W, 2 * FF_TILE), BF16)

    def row_mixes(i):
        u = jnp.dot(h2_ref[0, i * rows:(i + 1) * rows, :], w, preferred_element_type=F32)
        u_l = jnp.where(has_left, pltpu.roll(u, 1, 0), 0.0).astype(BF16)
        u_r = jnp.where(has_right, pltpu.roll(u, rows - 1, 0), 0.0).astype(BF16)
        u_c = u.astype(BF16)
        return [cw[3 * kh:3 * kh + 1] * u_l + cw[3 * kh + 1:3 * kh + 2] * u_c
                + cw[3 * kh + 2:3 * kh + 3] * u_r for kh in range(3)]

    def finish(i, prev, cur, nxt):
        above = zeros if prev is None else prev[0][rows - GRID_W:]
        below = zeros if nxt is None else nxt[2][:GRID_W]
        c = (cur[1] + jnp.concatenate([above, cur[0][:rows - GRID_W]], axis=0)
             + jnp.concatenate([cur[2][GRID_W:], below], axis=0) + cb)
        cg = c[:, :FF_TILE]
        act_ref[0, i * rows:(i + 1) * rows, :] = cg * _sigmoid(cg) * c[:, FF_TILE:]

    mixes = [None] * (n_blk + 1)
    for i in range(n_blk):
        mixes[i] = row_mixes(i)
        if i >= 1:
            finish(i - 1, mixes[i - 2] if i >= 2 else None, mixes[i - 1], mixes[i])
    finish(n_blk - 1, mixes[n_blk - 2] if n_blk >= 2 else None, mixes[n_blk - 1], None)


def _ffn_up_call(h2, w_up, conv_w9, conv_b):
    b, s, d = h2.shape
    n_f = D_FF // FF_TILE
    in_specs = [
        pl.BlockSpec((1, s, d), lambda bi, f: (bi, 0, 0)),
        pl.BlockSpec((d, FF_TILE), lambda bi, f: (0, f)),
        pl.BlockSpec((d, FF_TILE), lambda bi, f: (0, n_f + f)),
        pl.BlockSpec((9, FF_TILE), lambda bi, f: (0, f)),
        pl.BlockSpec((9, FF_TILE), lambda bi, f: (0, n_f + f)),
        pl.BlockSpec((1, FF_TILE), lambda bi, f: (0, f)),
        pl.BlockSpec((1, FF_TILE), lambda bi, f: (0, n_f + f)),
    ]
    return pl.pallas_call(
        _ffn_up_kernel,
        grid=(b, n_f),
        in_specs=in_specs,
        out_specs=pl.BlockSpec((1, s, FF_TILE), lambda bi, f: (bi, 0, f)),
        out_shape=jax.ShapeDtypeStruct((b, s, D_FF), BF16),
        compiler_params=pltpu.CompilerParams(dimension_semantics=("arbitrary", "arbitrary"),
                                             vmem_limit_bytes=VMEM_LIMIT),
        name="ffn_up",
    )(h2, w_up, w_up, conv_w9, conv_w9, conv_b, conv_b)


def _ffn_down_kernel(act_ref, wd_ref, x1_ref, mod_ref, fg_ref, out_ref):
    y = jnp.dot(act_ref[0], wd_ref[...], preferred_element_type=F32)
    out_ref[0] = _rms(x1_ref[0] + mod_ref[0, 5:6, :] * y, fg_ref[...])


def _ffn_down_call(act, w_down, x1, modx, final_g):
    b, s, d = x1.shape
    tm = DOWN_TILE
    in_specs = [
        pl.BlockSpec((1, tm, D_FF), lambda bi, m: (bi, m, 0)),
        pl.BlockSpec((D_FF, d), lambda bi, m: (0, 0)),
        pl.BlockSpec((1, tm, d), lambda bi, m: (bi, m, 0)),
        pl.BlockSpec((1, 6, d), lambda bi, m: (bi, 0, 0)),
        pl.BlockSpec((1, d), lambda bi, m: (0, 0)),
    ]
    return pl.pallas_call(
        _ffn_down_kernel,
        grid=(b, s // tm),
        in_specs=in_specs,
        out_specs=pl.BlockSpec((1, tm, d), lambda bi, m: (bi, m, 0)),
        out_shape=jax.ShapeDtypeStruct((b, s, d), F32),
        compiler_params=pltpu.CompilerParams(dimension_semantics=("arbitrary", "arbitrary"),
                                             vmem_limit_bytes=VMEM_LIMIT),
        name="ffn_down",
    )(act, w_down, x1, modx, final_g)


@functools.lru_cache(maxsize=None)
def _constants(seq):
    gw = FOURIER_W // FOURIER_GROUPS
    nm = np.outer(np.arange(gw), np.arange(gw)) % gw
    ang = 2.0 * np.pi * nm / gw
    cs = np.zeros((FOURIER_W, 2 * FOURIER_W), np.float64)
    for gi in range(FOURIER_GROUPS):
        sl = slice(gi * gw, (gi + 1) * gw)
        cs[sl, sl] = np.cos(ang)
        cs[sl, FOURIER_W + gi * gw:FOURIER_W + (gi + 1) * gw] = np.sin(ang)
    half = seq // 2
    k_idx = np.arange(seq, dtype=np.int64)[:, None]
    ang_c = 2.0 * np.pi * ((k_idx * np.arange(0, half + 1, dtype=np.int64)[None, :]) % seq) / seq
    ang_s = 2.0 * np.pi * ((k_idx * np.arange(1, half, dtype=np.int64)[None, :]) % seq) / seq
    dft = np.concatenate([np.cos(ang_c), -np.sin(ang_s)], axis=1)
    ri = np.arange(REV_TILE)
    rev = (np.arange(2 * REV_TILE)[None, :] == (REV_TILE - ri)[:, None]).astype(np.float64)
    head_id = np.arange(RWKV_W) // HEAD
    ones_blk = (head_id[:, None] == head_id[None, :]).astype(np.float64)
    ti = np.arange(TILE)
    same_chunk = (ti[:, None] // CHUNK) == (ti[None, :] // CHUNK)
    tri = np.stack([same_chunk & (ti[None, :] <= ti[:, None]),
                    same_chunk & (ti[None, :] >= ti[:, None])]).astype(np.float64)
    to_bf16 = lambda arr: np.asarray(arr, dtype=np.float32).astype(BF16)
    return (np.asarray(cs, dtype=np.float32), np.asarray(dft, dtype=np.float32),
            to_bf16(ones_blk), to_bf16(tri), to_bf16(rev))


def _block_diag_dirs(w):
    z = jnp.zeros_like(w[0])
    return jnp.concatenate([jnp.concatenate([w[0], z], axis=1),
                            jnp.concatenate([z, w[1]], axis=1)], axis=0)


def kernel(x, c, ctx, c_ctx, ada_w, ada_b, norm1_g, norm2_g, w_in, rwkv_conv_w, decay_w0, decay_w2, iclr_a0, iclr_a2, gate_g2, k_k, k_a, r_k, gn_g, gn_b, fourier_g, w_out, ffn_w_up, ffn_conv_w, ffn_conv_b, ffn_w_down, final_g):
    b, s, d = x.shape
    assert ada_w.shape[0] == 1, "single-layer configuration"
    assert (b, s, d) == (c.shape[0], s, D_MODEL) and ctx.shape == (b, CTX_LEN, d)
    assert s % TILE == 0 and CTX_LEN == TILE
    cs, dft, ones_blk, tri, rev = (jnp.asarray(t) for t in _constants(s))
    cs, dft = cs.astype(BF16), dft.astype(BF16)

    cc = jnp.concatenate([c, c_ctx[None, :], jnp.zeros((16 - b - 1, d), F32)], axis=0)
    mod = _mod_call(cc, ada_w[0], ada_b[0][None, :])
    modx = mod[:b].reshape(b, 6, d)
    modc = jnp.broadcast_to(mod[b].reshape(1, 6, d)[:, :2], (b, 2, d))
    modsel = jnp.stack([modc, modx[:, :2]], axis=1)

    row = lambda t: t.reshape(1, -1)
    at, bt, kt, rt, v, plast, bonus, gate, z = _prep_call(
        x, ctx, modsel, row(norm1_g[0]), w_in[0].astype(BF16), rwkv_conv_w[0],
        row(decay_w0[0]), _block_diag_dirs(decay_w2[0]).astype(BF16),
        row(iclr_a0[0]), _block_diag_dirs(iclr_a2[0]).astype(BF16),
        gate_g2[0].astype(BF16), row(k_k[0]), row(k_a[0]), row(r_k[0]), cs, ones_blk, tri)

    yf, yb = _scan_call(at, bt, kt, rt, v, plast, s)

    x1, h2 = _mix_call(dft, rev, z, yf, yb, bonus, gate, x, modx, row(gn_g[0]), row(gn_b[0]),
                       row(fourier_g[0]), w_out[0].astype(BF16), row(norm2_g[0]), ones_blk)

    act = _ffn_up_call(h2, ffn_w_up[0].astype(BF16), ffn_conv_w[0].reshape(9, 2 * D_FF),
                       row(ffn_conv_b[0]))
    return _ffn_down_call(act, ffn_w_down[0].astype(BF16), x1, modx, row(final_g))
```

```python
import functools
import math

import numpy as np
import jax
import jax.numpy as jnp
from jax import lax
from jax.experimental import pallas as pl
from jax.experimental.pallas import tpu as pltpu

F32 = jnp.float32
BF16 = jnp.bfloat16

D_MODEL = 1024
CTX_LEN = 256
GRID_W = 64
FOURIER_W = 512
FOURIER_GROUPS = 8
RWKV_W = 512
HEAD = 64
N_HEADS = 8
DECAY_RANK = 64
ICLR_RANK = 64
GATE_RANK = 128
D_FF = 2816
NORM_EPS = 1e-6
GN_EPS = 64e-5
KK_EPS = 1e-12

CHUNK = 64
TILE = 256
MIX_TILE = 512
REV_TILE = 256
CHUNKS_PER_TILE = TILE // CHUNK
PREP_BATCH = 2
PAIR = 2 * HEAD
N_PAIRS = RWKV_W // PAIR
QUAD_HEADS = 4
SCAN_BATCH = 8
SCAN_GROUPS = 4
SCAN_LAG = 3
MOD_TILE = 1536
FF_TILE = 1408
FFN_ROWS = 256
DOWN_TILE = 1024
VMEM_LIMIT = 56 * 1024 * 1024
HEAD_SUM_TERMS = 1
CUMSUM_TERMS = 2
MOD_WEIGHT_TERMS = 2

RKV_LO = FOURIER_W
RKV_HI = FOURIER_W + 3 * RWKV_W
WD_LO = RKV_HI
AD_LO = WD_LO + 2 * DECAY_RANK
GD_LO = AD_LO + 2 * ICLR_RANK
PROJ_W = GD_LO + GATE_RANK


def _bdot(a, b):
    return jnp.dot(a.astype(BF16), b.astype(BF16), preferred_element_type=F32)


def _bdot_nt(a, b):
    return lax.dot_general(a.astype(BF16), b.astype(BF16), (((1,), (1,)), ((), ())),
                           preferred_element_type=F32)


def _bdot_tn(a, b):
    return lax.dot_general(a.astype(BF16), b.astype(BF16), (((0,), (0,)), ((), ())),
                           preferred_element_type=F32)


def _split_terms(a, terms):
    out, rem = [], a
    for _ in range(terms):
        hi = rem.astype(BF16)
        out.append(hi)
        rem = rem - hi.astype(F32)
    return out


def _dot_exact_rhs(a, b_exact, terms=HEAD_SUM_TERMS):
    acc = None
    for piece in _split_terms(a, terms):
        t = jnp.dot(piece, b_exact, preferred_element_type=F32)
        acc = t if acc is None else acc + t
    return acc


def _dot_exact_lhs(a_exact, b, terms=CUMSUM_TERMS):
    acc = None
    for piece in _split_terms(b, terms):
        t = jnp.dot(a_exact, piece, preferred_element_type=F32)
        acc = t if acc is None else acc + t
    return acc


def _sigmoid(x):
    return 1.0 / (1.0 + jnp.exp(-x))


def _rms(x, g):
    return x * lax.rsqrt(jnp.mean(x * x, axis=-1, keepdims=True) + NORM_EPS) * g


def _mod_kernel(c_ref, w_ref, b_ref, o_ref):
    cc = c_ref[...]
    s = cc * _sigmoid(cc)
    rows = s.shape[0]
    s_parts = jnp.concatenate(_split_terms(s, 3), axis=0)
    acc = b_ref[...]
    for wp in _split_terms(w_ref[...], MOD_WEIGHT_TERMS):
        t = jnp.dot(s_parts, wp, preferred_element_type=F32)
        acc = acc + (t[:rows] + t[rows:2 * rows] + t[2 * rows:])
    o_ref[...] = acc


def _mod_call(cc, ada_w, ada_b):
    rows, d = cc.shape
    n = ada_w.shape[1]
    tn = MOD_TILE
    return pl.pallas_call(
        _mod_kernel,
        grid=(n // tn,),
        in_specs=[pl.BlockSpec((rows, d), lambda i: (0, 0)),
                  pl.BlockSpec((d, tn), lambda i: (0, i)),
                  pl.BlockSpec((1, tn), lambda i: (0, i))],
        out_specs=pl.BlockSpec((rows, tn), lambda i: (0, i)),
        out_shape=jax.ShapeDtypeStruct((rows, n), F32),
        compiler_params=pltpu.CompilerParams(dimension_semantics=("arbitrary",),
                                             vmem_limit_bytes=VMEM_LIMIT),
        name="mod",
    )(cc, ada_w, ada_b)


def _prep_kernel(x_ref, xp_ref, xn_ref, ctx_ref, ms_ref, g1_ref, win_ref, cw_ref,
                 w0_ref, w2_ref, a0_ref, a2_ref, g2_ref, kk_ref, ka_ref, rk_ref,
                 cs_ref, ones_ref, tri_ref,
                 at_ref, bt_ref, kt_ref, rt_ref, v_ref, pl_ref, bonus_ref, gate_ref, z_ref):
    j = pl.program_id(1)
    refs = (x_ref, xp_ref, xn_ref, ctx_ref, ms_ref, g1_ref, win_ref, cw_ref,
            w0_ref, w2_ref, a0_ref, a2_ref, kk_ref, ka_ref, ones_ref, tri_ref,
            at_ref, bt_ref, kt_ref, rt_ref, v_ref, pl_ref)
    tiles = _run_interleaved([_prep_tile(bi, *refs) for bi in range(x_ref.shape[0])],
                             groups=PREP_BATCH, lag=1)

    @pl.when(j >= 1)
    def _():
        ones_blk = ones_ref[...]
        for bi, (u_four, r, v, gd, kd_sum) in enumerate(tiles):
            bs = _dot_exact_rhs(r * rk_ref[...] * kd_sum, ones_blk)
            bonus_ref[bi] = bs * v
            gate_ref[bi] = _bdot(_sigmoid(gd), g2_ref[...])
            z_ref[bi] = _bdot(u_four, cs_ref[...]).astype(BF16)


def _prep_tile(bi, x_ref, xp_ref, xn_ref, ctx_ref, ms_ref, g1_ref, win_ref, cw_ref,
               w0_ref, w2_ref, a0_ref, a2_ref, kk_ref, ka_ref, ones_ref, tri_ref,
               at_ref, bt_ref, kt_ref, rt_ref, v_ref, pl_ref):
    j = pl.program_id(1)
    n_j = pl.num_programs(1)
    is_ctx = j == 0
    shift = ms_ref[bi, 0, 0:1, :]
    scale = ms_ref[bi, 0, 1:2, :]
    g = g1_ref[...]

    def norm_mod(xr):
        return _rms(xr, g) * (1.0 + scale) + shift

    xin = jnp.where(is_ctx, ctx_ref[bi], x_ref[bi])
    h = norm_mod(xin).astype(BF16)
    u = jnp.dot(h, win_ref[...], preferred_element_type=F32)

    halo = jnp.concatenate([xp_ref[bi], xn_ref[bi]], axis=0)
    uh = jnp.dot(norm_mod(halo).astype(BF16), win_ref[:, RKV_LO:RKV_HI],
                 preferred_element_type=F32)
    yield
    prev_row = jnp.where(j >= 2, uh[7:8], 0.0)
    next_row = jnp.where(jnp.logical_and(j >= 1, j < n_j - 1), uh[8:9], 0.0)

    rkv = u[:, RKV_LO:RKV_HI]
    row = lax.broadcasted_iota(jnp.int32, rkv.shape, 0)
    u_m1 = jnp.where(row == 0, prev_row, pltpu.roll(rkv, 1, 0))
    u_p1 = jnp.where(row == TILE - 1, next_row, pltpu.roll(rkv, TILE - 1, 0))
    rkv = cw_ref[0:1, :] * u_m1 + cw_ref[1:2, :] * rkv + cw_ref[2:3, :] * u_p1
    r = rkv[:, 0:RWKV_W]
    k = rkv[:, RWKV_W:2 * RWKV_W]
    v = rkv[:, 2 * RWKV_W:3 * RWKV_W]

    wd = u[:, WD_LO:AD_LO]
    ad = u[:, AD_LO:GD_LO]
    gd = u[:, GD_LO:PROJ_W]
    w_lora = w0_ref[...] + _bdot(jnp.tanh(wd), w2_ref[...])
    lw = -math.exp(-0.5) * _sigmoid(w_lora)
    a = _sigmoid(a0_ref[...] + _bdot(ad, a2_ref[...]))

    ones_blk = ones_ref[...]
    kraw = k * kk_ref[...]
    kk = kraw * lax.rsqrt(_dot_exact_rhs(kraw * kraw, ones_blk) + KK_EPS)
    ka = ka_ref[...]
    kd = [k * (1.0 + (a[:, d * RWKV_W:(d + 1) * RWKV_W] - 1.0) * ka) for d in range(2)]
    yield

    v_ref[bi] = v.astype(BF16)
    for d in range(2):
        lwd = lw[:, d * RWKV_W:(d + 1) * RWKV_W]
        ad_ = a[:, d * RWKV_W:(d + 1) * RWKV_W]
        c = _dot_exact_lhs(tri_ref[d], lwd)
        e_pos = jnp.exp(c)
        e_neg = jnp.exp(-c)
        e_prev = jnp.exp(c - lwd)
        at_ref[bi, d] = (kk * e_prev).astype(BF16)
        bt_ref[bi, d] = (kk * ad_ * e_neg).astype(BF16)
        kt_ref[bi, d] = (kd[d] * e_neg).astype(BF16)
        rt_ref[bi, d] = (r * e_pos).astype(BF16)
        for q in range(CHUNKS_PER_TILE):
            last = q * CHUNK + (CHUNK - 1 if d == 0 else 0)
            pl_ref[bi, d, q] = e_pos[last:last + 1, :]
    return u[:, 0:FOURIER_W], r, v, gd, kd[0] + kd[1]


def _prep_call(x, ctx, modsel, norm1_g, w_in, conv_w, w0, w2bd, a0, a2bd, g2, k_k, k_a, r_k,
               cs, ones_blk, tri):
    b, s, d = x.shape
    n_j = 1 + s // TILE
    n_chunks = (CTX_LEN + s) // CHUNK
    t_all = CTX_LEN + s
    c2 = 2 * RWKV_W

    def full(arr):
        nd = arr.ndim
        return pl.BlockSpec(arr.shape, lambda bi, j, _n=nd: (0,) * _n)

    rows8 = TILE // 8
    nb = PREP_BATCH
    in_specs = [
        pl.BlockSpec((nb, TILE, d), lambda bi, j: (bi, jnp.maximum(j - 1, 0), 0)),
        pl.BlockSpec((nb, 8, d), lambda bi, j: (bi, jnp.maximum((j - 1) * rows8 - 1, 0), 0)),
        pl.BlockSpec((nb, 8, d), lambda bi, j: (bi, jnp.minimum(j * rows8, s // 8 - 1), 0)),
        pl.BlockSpec((nb, CTX_LEN, d), lambda bi, j: (bi, 0, 0)),
        pl.BlockSpec((nb, 1, 2, d), lambda bi, j: (bi, jnp.minimum(j, 1), 0, 0)),
        full(norm1_g), full(w_in), full(conv_w), full(w0), full(w2bd), full(a0), full(a2bd),
        full(g2), full(k_k), full(k_a), full(r_k), full(cs), full(ones_blk), full(tri),
    ]
    lat = lambda bi, j: (bi, jnp.maximum(j - 1, 0), 0)
    out_specs = [
        pl.BlockSpec((nb, 2, TILE, RWKV_W), lambda bi, j: (bi, 0, j, 0)),
        pl.BlockSpec((nb, 2, TILE, RWKV_W), lambda bi, j: (bi, 0, j, 0)),
        pl.BlockSpec((nb, 2, TILE, RWKV_W), lambda bi, j: (bi, 0, j, 0)),
        pl.BlockSpec((nb, 2, TILE, RWKV_W), lambda bi, j: (bi, 0, j, 0)),
        pl.BlockSpec((nb, TILE, RWKV_W), lambda bi, j: (bi, j, 0)),
        pl.BlockSpec((nb, 2, CHUNKS_PER_TILE, 1, RWKV_W), lambda bi, j: (bi, 0, j, 0, 0)),
        pl.BlockSpec((nb, TILE, RWKV_W), lat),
        pl.BlockSpec((nb, TILE, RWKV_W), lat),
        pl.BlockSpec((nb, TILE, 2 * FOURIER_W), lat),
    ]
    out_shape = [
        jax.ShapeDtypeStruct((b, 2, t_all, RWKV_W), BF16),
        jax.ShapeDtypeStruct((b, 2, t_all, RWKV_W), BF16),
        jax.ShapeDtypeStruct((b, 2, t_all, RWKV_W), BF16),
        jax.ShapeDtypeStruct((b, 2, t_all, RWKV_W), BF16),
        jax.ShapeDtypeStruct((b, t_all, RWKV_W), BF16),
        jax.ShapeDtypeStruct((b, 2, n_chunks, 1, RWKV_W), F32),
        jax.ShapeDtypeStruct((b, s, RWKV_W), F32),
        jax.ShapeDtypeStruct((b, s, RWKV_W), F32),
        jax.ShapeDtypeStruct((b, s, 2 * FOURIER_W), BF16),
    ]
    return pl.pallas_call(
        _prep_kernel,
        grid=(b // nb, n_j),
        in_specs=in_specs,
        out_specs=out_specs,
        out_shape=out_shape,
        compiler_params=pltpu.CompilerParams(dimension_semantics=("arbitrary", "arbitrary"),
                                             vmem_limit_bytes=VMEM_LIMIT),
        name="prep",
    )(x, x, x, ctx, modsel, norm1_g, w_in, conv_w, w0, w2bd, a0, a2bd, g2, k_k, k_a, r_k,
      cs, ones_blk, tri)


def _quad_chunk(a, bm, km, r, v, plrow, g_states, strict, incl, eye, lane_lo, mask_bd, eye_row,
                level_masks):
    n = PAIR

    def dup(xv):
        z = jnp.zeros_like(xv)
        return jnp.concatenate([jnp.where(lane_lo, xv, z), jnp.where(lane_lo, z, xv)], axis=0)

    def to_row(x2):
        return x2[:CHUNK] + x2[CHUNK:]

    def bdiag(xr):
        return jnp.where(mask_bd, jnp.concatenate([xr] * QUAD_HEADS, axis=0), jnp.zeros((), xr.dtype))

    pairs = []
    for p in range(2):
        sl = slice(p * n, (p + 1) * n)
        a2, b2, k2, r2, v2 = (dup(t[:, sl]) for t in (a, bm, km, r, v))
        rhs = jnp.concatenate([b2, k2], axis=0)
        s = _bdot_nt(jnp.concatenate([a2, r2], axis=0), rhs)
        pairs.append((sl, r2, v2, rhs, s))
        yield

    a_ab = jnp.concatenate([to_row(jnp.where(strict, s[:n, :n], 0.0)) for *_, s in pairs], axis=1)
    a_ak = jnp.concatenate([to_row(jnp.where(strict, s[:n, n:], 0.0)) for *_, s in pairs], axis=1)
    m_rbk = [jnp.concatenate([jnp.where(incl, s[n:, :n], 0.0), jnp.where(incl, s[n:, n:], 0.0)],
                             axis=1).astype(BF16) for *_, s in pairs]

    av = _bdot(a_ak, bdiag(v))
    a_lvl = [jnp.where(m, a_ab, 0.0).astype(BF16) for m in level_masks]
    t_row = jnp.where(eye_row, 1.0, 0.0) - jnp.where(level_masks[0], a_ab, 0.0)
    lvl = 1
    while lvl < len(a_lvl):
        paired = lvl + 1 < len(a_lvl)
        lhs = jnp.concatenate([a_lvl[lvl], a_lvl[lvl + 1]], axis=0) if paired else a_lvl[lvl]
        yc = jnp.dot(lhs, bdiag(t_row.astype(BF16)), preferred_element_type=F32)
        yield
        y_bd = bdiag(yc[:CHUNK].astype(BF16))
        if paired:
            both = jnp.dot(jnp.concatenate([t_row, yc[CHUNK:]], axis=0).astype(BF16), y_bd,
                           preferred_element_type=F32)
            yield
            t_row = t_row - both[:CHUNK]
            y_next = yc[CHUNK:] - both[CHUNK:]
            t_row = t_row - _bdot(t_row, bdiag(y_next.astype(BF16)))
            lvl += 2
        else:
            t_row = t_row - _bdot(t_row, y_bd)
            lvl += 1
        yield
    tx = _bdot(t_row, jnp.concatenate([bdiag(a), bdiag(av.astype(BF16))], axis=1))
    yield
    a_w = tx[:, :2 * n].astype(BF16)
    u_v = (-tx[:, 2 * n:]).astype(BF16)
    stage5 = []
    for (sl, r2, v2, rhs, _), m2 in zip(pairs, m_rbk):
        zmat = jnp.concatenate(
            [jnp.concatenate([dup(a_w[:, sl]), dup(u_v[:, sl])], axis=1),
             jnp.concatenate([jnp.zeros_like(v2), v2], axis=1)], axis=0)
        rhs_end = rhs.astype(F32) * plrow[:, sl]
        lhs = jnp.concatenate([rhs_end.T.astype(BF16), m2], axis=0)
        stage5.append(jnp.dot(lhs, zmat, preferred_element_type=F32))
    yield
    ys, g_new = [], []
    for (sl, r2, v2, rhs, _), o5, g_state in zip(pairs, stage5, g_states):
        phi = jnp.where(eye, plrow[:, sl], 0.0) - o5[:n, :n]
        r_y = r2.astype(F32) - o5[n:, :n]
        o6 = _bdot(jnp.concatenate([r_y, phi], axis=0), g_state)
        ys.append(to_row(o6[:n] + o5[n:, n:]))
        g_new.append(o6[n:] + o5[:n, n:])
    return jnp.concatenate(ys, axis=1), g_new


def _run_interleaved(gens, groups=1, lag=0):
    results = [None] * len(gens)
    active = list(range(len(gens)))
    rnd = 0
    while active:
        for i in list(active):
            if rnd < (i % groups) * lag:
                continue
            try:
                next(gens[i])
            except StopIteration as stop:
                results[i] = stop.value
                active.remove(i)
        rnd += 1
    return results


def _scan_kernel(atf, btf, ktf, rtf, vf, plf, atb, btb, ktb, rtb, vb, plb,
                 yf_ref, yb_ref, g_ref):
    s = pl.program_id(1)

    @pl.when(s == 0)
    def _():
        g_ref[...] = jnp.zeros_like(g_ref)

    ri = lax.broadcasted_iota(jnp.int32, (PAIR, PAIR), 0)
    ci = lax.broadcasted_iota(jnp.int32, (PAIR, PAIR), 1)
    same = (ri >= HEAD) == (ci >= HEAD)
    ti = ri & (HEAD - 1)
    tj = ci & (HEAD - 1)
    eye = ri == ci
    lane_lo = lax.broadcasted_iota(jnp.int32, (CHUNK, PAIR), 1) < HEAD
    masks = [(same & (tj < ti), same & (tj <= ti)),
             (same & (tj > ti), same & (tj >= ti))]
    quad_w = QUAD_HEADS * HEAD
    rq = lax.broadcasted_iota(jnp.int32, (quad_w, quad_w), 0)
    cq = lax.broadcasted_iota(jnp.int32, (quad_w, quad_w), 1)
    mask_bd = (rq // HEAD) == (cq // HEAD)
    t_i = lax.broadcasted_iota(jnp.int32, (CHUNK, quad_w), 0)
    s_i = lax.broadcasted_iota(jnp.int32, (CHUNK, quad_w), 1) & (HEAD - 1)
    eye_row = s_i == t_i
    level_masks = []
    for d in range(2):
        late, early = (t_i, s_i) if d == 0 else (s_i, t_i)
        level_masks.append([
            ((t_i >> (lvl + 1)) == (s_i >> (lvl + 1)))
            & (((late >> lvl) & 1) == 1) & (((early >> lvl) & 1) == 0)
            for lvl in range(int(math.log2(CHUNK)))])
    refs = [(atf, btf, ktf, rtf, vf, plf, yf_ref), (atb, btb, ktb, rtb, vb, plb, yb_ref)]
    gens, dests = [], []
    for bi in range(yf_ref.shape[0]):
        for d in range(2):
            at, bt, kt, rt, vv, plr, y_ref = refs[d]
            strict, incl = masks[d]
            for q in range(RWKV_W // quad_w):
                sl = slice(q * quad_w, (q + 1) * quad_w)
                gens.append(_quad_chunk(at[bi, 0, :, sl], bt[bi, 0, :, sl], kt[bi, 0, :, sl],
                                        rt[bi, 0, :, sl], vv[bi, :, sl], plr[bi, 0, 0, :, sl],
                                        [g_ref[bi, d, 2 * q], g_ref[bi, d, 2 * q + 1]],
                                        strict, incl, eye, lane_lo, mask_bd, eye_row,
                                        level_masks[d]))
                dests.append((y_ref, bi, d, q, sl))
    done = _run_interleaved(gens, groups=SCAN_GROUPS, lag=SCAN_LAG)
    for (y, g_new), (y_ref, bi, d, q, sl) in zip(done, dests):
        g_ref[bi, d, 2 * q] = g_new[0]
        g_ref[bi, d, 2 * q + 1] = g_new[1]
        y_ref[bi, :, sl] = y


def _scan_call(at, bt, kt, rt, v, plast, s_lat):
    b = at.shape[0]
    n_chunks = at.shape[2] // CHUNK
    n_ctx = CTX_LEN // CHUNK
    n_lat = s_lat // CHUNK

    def fwd_c(s):
        return s

    def bwd_c(s):
        return jnp.where(s < n_ctx, n_ctx - 1 - s, n_chunks + n_ctx - 1 - s)

    nb = SCAN_BATCH

    def dir_spec(d, cfun):
        return pl.BlockSpec((nb, 1, CHUNK, RWKV_W), lambda bi, s: (bi, d, cfun(s), 0))

    def v_spec(cfun):
        return pl.BlockSpec((nb, CHUNK, RWKV_W), lambda bi, s: (bi, cfun(s), 0))

    def pl_spec(d, cfun):
        return pl.BlockSpec((nb, 1, 1, 1, RWKV_W), lambda bi, s: (bi, d, cfun(s), 0, 0))

    in_specs = ([dir_spec(0, fwd_c)] * 4 + [v_spec(fwd_c), pl_spec(0, fwd_c)]
                + [dir_spec(1, bwd_c)] * 4 + [v_spec(bwd_c), pl_spec(1, bwd_c)])
    out_specs = [
        pl.BlockSpec((nb, CHUNK, RWKV_W), lambda bi, s: (bi, jnp.maximum(s - n_ctx, 0), 0)),
        pl.BlockSpec((nb, CHUNK, RWKV_W),
                     lambda bi, s: (bi, jnp.minimum(n_chunks - 1 - s, n_lat - 1), 0)),
    ]
    out_shape = [jax.ShapeDtypeStruct((b, s_lat, RWKV_W), F32)] * 2
    return pl.pallas_call(
        _scan_kernel,
        grid=(b // nb, n_chunks),
        in_specs=in_specs,
        out_specs=out_specs,
        out_shape=out_shape,
        scratch_shapes=[pltpu.VMEM((nb, 2, N_PAIRS, PAIR, PAIR), F32)],
        compiler_params=pltpu.CompilerParams(dimension_semantics=("arbitrary", "arbitrary"),
                                             vmem_limit_bytes=VMEM_LIMIT),
        name="scan",
    )(at, bt, kt, rt, v, plast, at, bt, kt, rt, v, plast)


def _mix_kernel(dft_ref, rev_ref, z_ref, yf_ref, yb_ref, bonus_ref, gate_ref, x_ref, mod_ref,
                gng_ref, gnb_ref, fg_ref, wout_ref, n2g_ref, ones_ref, x1_ref, h2_ref, zf_ref,
                *, fscale):
    seq = z_ref.shape[1]
    half = seq // 2
    rt = rev_ref.shape[0]

    @pl.when(pl.program_id(1) == 0)
    def _():
        mid = z_ref[0, half:half + 1, :FOURIER_W].astype(F32)
        row0 = lax.broadcasted_iota(jnp.int32, (rt, FOURIER_W), 0) == 0
        for part, sign in ((0, 1.0), (1, -1.0)):
            lanes = slice(part * FOURIER_W, (part + 1) * FOURIER_W)
            for r in range(half // rt):
                lo = seq - rt * (r + 1)
                nxt = (z_ref[0, lo + rt:lo + 2 * rt, lanes] if r > 0
                       else jnp.zeros((rt, FOURIER_W), BF16))
                rev = jnp.dot(rev_ref[...], jnp.concatenate([z_ref[0, lo:lo + rt, lanes], nxt], axis=0),
                              preferred_element_type=F32)
                folded = z_ref[0, r * rt:(r + 1) * rt, lanes].astype(F32) + sign * rev
                if part == 1 and r == 0:
                    folded = jnp.where(row0, mid, folded)
                zf_ref[part * half + r * rt:part * half + (r + 1) * rt, :] = folded.astype(BF16)

    yfour = jnp.dot(dft_ref[...], zf_ref[...], preferred_element_type=F32)
    four = _rms(yfour * fscale, fg_ref[...])

    ones_blk = ones_ref[...]
    y = yf_ref[0] + yb_ref[0]
    mu = _dot_exact_rhs(y, ones_blk) * (1.0 / HEAD)
    dy = y - mu
    var = _dot_exact_rhs(dy * dy, ones_blk) * (1.0 / HEAD)
    yn = dy * lax.rsqrt(var + GN_EPS) * gng_ref[...] + gnb_ref[...]
    rw = (yn + bonus_ref[0]) * gate_ref[0]

    out = _bdot(four, wout_ref[:FOURIER_W, :]) + _bdot(rw, wout_ref[FOURIER_W:, :])
    x1 = x_ref[0] + mod_ref[0, 2:3, :] * out
    x1_ref[0] = x1
    h2 = _rms(x1, n2g_ref[...]) * (1.0 + mod_ref[0, 4:5, :]) + mod_ref[0, 3:4, :]
    h2_ref[0] = h2.astype(BF16)


def _mix_call(dft, rev, z, yf, yb, bonus, gate, x, modx, gn_g, gn_b, f_g, w_out, norm2_g,
              ones_blk):
    b, s, d = x.shape
    tm = MIX_TILE

    def full(arr):
        nd = arr.ndim
        return pl.BlockSpec(arr.shape, lambda bi, m, _n=nd: (0,) * _n)

    tok = lambda w: pl.BlockSpec((1, tm, w), lambda bi, m: (bi, m, 0))
    in_specs = [
        pl.BlockSpec((tm, s), lambda bi, m: (m, 0)),
        full(rev),
        pl.BlockSpec((1, s, 2 * FOURIER_W), lambda bi, m: (bi, 0, 0)),
        tok(RWKV_W), tok(RWKV_W), tok(RWKV_W), tok(RWKV_W), tok(d),
        pl.BlockSpec((1, 6, d), lambda bi, m: (bi, 0, 0)),
        full(gn_g), full(gn_b), full(f_g), full(w_out), full(norm2_g), full(ones_blk),
    ]
    fscale = 1.0 / math.sqrt(s * (FOURIER_W // FOURIER_GROUPS))
    return pl.pallas_call(
        functools.partial(_mix_kernel, fscale=fscale),
        grid=(b, s // tm),
        in_specs=in_specs,
        out_specs=[tok(d), tok(d)],
        out_shape=[jax.ShapeDtypeStruct((b, s, d), F32), jax.ShapeDtypeStruct((b, s, d), BF16)],
        scratch_shapes=[pltpu.VMEM((s, FOURIER_W), BF16)],
        compiler_params=pltpu.CompilerParams(dimension_semantics=("arbitrary", "arbitrary"),
                                             vmem_limit_bytes=VMEM_LIMIT),
        name="mix",
    )(dft, rev, z, yf, yb, bonus, gate, x, modx, gn_g, gn_b, f_g, w_out, norm2_g, ones_blk)


def _ffn_up_kernel(h2_ref, wg_ref, wv_ref, cwg_ref, cwv_ref, cbg_ref, cbv_ref, act_ref):
    t = h2_ref.shape[1]
    rows = FFN_ROWS
    n_blk = t // rows
    w = jnp.concatenate([wg_ref[...], wv_ref[...]], axis=1)
    cw = jnp.concatenate([cwg_ref[...], cwv_ref[...]], axis=1).astype(BF16)
    cb = jnp.concatenate([cbg_ref[...], cbv_ref[...]], axis=1).astype(BF16)
    col = lax.broadcasted_iota(jnp.int32, (rows, 2 * FF_TILE), 0) & (GRID_W - 1)
    has_left = col > 0
    has_right = col < GRID_W - 1
    zeros = jnp.zeros((GRID_W, 2 * FF_TILE), BF16)

    def row_mixes(i):
        u = jnp.dot(h2_ref[0, i * rows:(i + 1) * rows, :], w, preferred_element_type=F32)
        u_l = jnp.where(has_left, pltpu.roll(u, 1, 0), 0.0).astype(BF16)
        u_r = jnp.where(has_right, pltpu.roll(u, rows - 1, 0), 0.0).astype(BF16)
        u_c = u.astype(BF16)
        return [cw[3 * kh:3 * kh + 1] * u_l + cw[3 * kh + 1:3 * kh + 2] * u_c
                + cw[3 * kh + 2:3 * kh + 3] * u_r for kh in range(3)]

    def finish(i, prev, cur, nxt):
        above = zeros if prev is None else prev[0][rows - GRID_W:]
        below = zeros if nxt is None else nxt[2][:GRID_W]
        c = (cur[1] + jnp.concatenate([above, cur[0][:rows - GRID_W]], axis=0)
             + jnp.concatenate([cur[2][GRID_W:], below], axis=0) + cb)
        cg = c[:, :FF_TILE]
        act_ref[0, i * rows:(i + 1) * rows, :] = cg * _sigmoid(cg) * c[:, FF_TILE:]

    mixes = [None] * (n_blk + 1)
    for i in range(n_blk):
        mixes[i] = row_mixes(i)
        if i >= 1:
            finish(i - 1, mixes[i - 2] if i >= 2 else None, mixes[i - 1], mixes[i])
    finish(n_blk - 1, mixes[n_blk - 2] if n_blk >= 2 else None, mixes[n_blk - 1], None)


def _ffn_up_call(h2, w_up, conv_w9, conv_b):
    b, s, d = h2.shape
    n_f = D_FF // FF_TILE
    in_specs = [
        pl.BlockSpec((1, s, d), lambda bi, f: (bi, 0, 0)),
        pl.BlockSpec((d, FF_TILE), lambda bi, f: (0, f)),
        pl.BlockSpec((d, FF_TILE), lambda bi, f: (0, n_f + f)),
        pl.BlockSpec((9, FF_TILE), lambda bi, f: (0, f)),
        pl.BlockSpec((9, FF_TILE), lambda bi, f: (0, n_f + f)),
        pl.BlockSpec((1, FF_TILE), lambda bi, f: (0, f)),
        pl.BlockSpec((1, FF_TILE), lambda bi, f: (0, n_f + f)),
    ]
    return pl.pallas_call(
        _ffn_up_kernel,
        grid=(b, n_f),
        in_specs=in_specs,
        out_specs=pl.BlockSpec((1, s, FF_TILE), lambda bi, f: (bi, 0, f)),
        out_shape=jax.ShapeDtypeStruct((b, s, D_FF), BF16),
        compiler_params=pltpu.CompilerParams(dimension_semantics=("arbitrary", "arbitrary"),
                                             vmem_limit_bytes=VMEM_LIMIT),
        name="ffn_up",
    )(h2, w_up, w_up, conv_w9, conv_w9, conv_b, conv_b)


def _ffn_down_kernel(act_ref, wd_ref, x1_ref, mod_ref, fg_ref, out_ref):
    y = jnp.dot(act_ref[0], wd_ref[...], preferred_element_type=F32)
    out_ref[0] = _rms(x1_ref[0] + mod_ref[0, 5:6, :] * y, fg_ref[...])


def _ffn_down_call(act, w_down, x1, modx, final_g):
    b, s, d = x1.shape
    tm = DOWN_TILE
    in_specs = [
        pl.BlockSpec((1, tm, D_FF), lambda bi, m: (bi, m, 0)),
        pl.BlockSpec((D_FF, d), lambda bi, m: (0, 0)),
        pl.BlockSpec((1, tm, d), lambda bi, m: (bi, m, 0)),
        pl.BlockSpec((1, 6, d), lambda bi, m: (bi, 0, 0)),
        pl.BlockSpec((1, d), lambda bi, m: (0, 0)),
    ]
    return pl.pallas_call(
        _ffn_down_kernel,
        grid=(b, s // tm),
        in_specs=in_specs,
        out_specs=pl.BlockSpec((1, tm, d), lambda bi, m: (bi, m, 0)),
        out_shape=jax.ShapeDtypeStruct((b, s, d), F32),
        compiler_params=pltpu.CompilerParams(dimension_semantics=("arbitrary", "arbitrary"),
                                             vmem_limit_bytes=VMEM_LIMIT),
        name="ffn_down",
    )(act, w_down, x1, modx, final_g)


@functools.lru_cache(maxsize=None)
def _constants(seq):
    gw = FOURIER_W // FOURIER_GROUPS
    nm = np.outer(np.arange(gw), np.arange(gw)) % gw
    ang = 2.0 * np.pi * nm / gw
    cs = np.zeros((FOURIER_W, 2 * FOURIER_W), np.float64)
    for gi in range(FOURIER_GROUPS):
        sl = slice(gi * gw, (gi + 1) * gw)
        cs[sl, sl] = np.cos(ang)
        cs[sl, FOURIER_W + gi * gw:FOURIER_W + (gi + 1) * gw] = np.sin(ang)
    half = seq // 2
    k_idx = np.arange(seq, dtype=np.int64)[:, None]
    ang_c = 2.0 * np.pi * ((k_idx * np.arange(0, half + 1, dtype=np.int64)[None, :]) % seq) / seq
    ang_s = 2.0 * np.pi * ((k_idx * np.arange(1, half, dtype=np.int64)[None, :]) % seq) / seq
    dft = np.concatenate([np.cos(ang_c), -np.sin(ang_s)], axis=1)
    ri = np.arange(REV_TILE)
    rev = (np.arange(2 * REV_TILE)[None, :] == (REV_TILE - ri)[:, None]).astype(np.float64)
    head_id = np.arange(RWKV_W) // HEAD
    ones_blk = (head_id[:, None] == head_id[None, :]).astype(np.float64)
    ti = np.arange(TILE)
    same_chunk = (ti[:, None] // CHUNK) == (ti[None, :] // CHUNK)
    tri = np.stack([same_chunk & (ti[None, :] <= ti[:, None]),
                    same_chunk & (ti[None, :] >= ti[:, None])]).astype(np.float64)
    to_bf16 = lambda arr: np.asarray(arr, dtype=np.float32).astype(BF16)
    return (np.asarray(cs, dtype=np.float32), np.asarray(dft, dtype=np.float32),
            to_bf16(ones_blk), to_bf16(tri), to_bf16(rev))


def _block_diag_dirs(w):
    z = jnp.zeros_like(w[0])
    return jnp.concatenate([jnp.concatenate([w[0], z], axis=1),
                            jnp.concatenate([z, w[1]], axis=1)], axis=0)


def kernel(x, c, ctx, c_ctx, ada_w, ada_b, norm1_g, norm2_g, w_in, rwkv_conv_w, decay_w0, decay_w2, iclr_a0, iclr_a2, gate_g2, k_k, k_a, r_k, gn_g, gn_b, fourier_g, w_out, ffn_w_up, ffn_conv_w, ffn_conv_b, ffn_w_down, final_g):
    b, s, d = x.shape
    assert ada_w.shape[0] == 1, "single-layer configuration"
    assert (b, s, d) == (c.shape[0], s, D_MODEL) and ctx.shape == (b, CTX_LEN, d)
    assert s % TILE == 0 and CTX_LEN == TILE
    cs, dft, ones_blk, tri, rev = (jnp.asarray(t) for t in _constants(s))
    cs, dft = cs.astype(BF16), dft.astype(BF16)

    cc = jnp.concatenate([c, c_ctx[None, :], jnp.zeros((16 - b - 1, d), F32)], axis=0)
    mod = _mod_call(cc, ada_w[0], ada_b[0][None, :])
    modx = mod[:b].reshape(b, 6, d)
    modc = jnp.broadcast_to(mod[b].reshape(1, 6, d)[:, :2], (b, 2, d))
    modsel = jnp.stack([modc, modx[:, :2]], axis=1)

    row = lambda t: t.reshape(1, -1)
    at, bt, kt, rt, v, plast, bonus, gate, z = _prep_call(
        x, ctx, modsel, row(norm1_g[0]), w_in[0].astype(BF16), rwkv_conv_w[0],
        row(decay_w0[0]), _block_diag_dirs(decay_w2[0]).astype(BF16),
        row(iclr_a0[0]), _block_diag_dirs(iclr_a2[0]).astype(BF16),
        gate_g2[0].astype(BF16), row(k_k[0]), row(k_a[0]), row(r_k[0]), cs, ones_blk, tri)

    yf, yb = _scan_call(at, bt, kt, rt, v, plast, s)

    x1, h2 = _mix_call(dft, rev, z, yf, yb, bonus, gate, x, modx, row(gn_g[0]), row(gn_b[0]),
                       row(fourier_g[0]), w_out[0].astype(BF16), row(norm2_g[0]), ones_blk)

    act = _ffn_up_call(h2, ffn_w_up[0].astype(BF16), ffn_conv_w[0].reshape(9, 2 * D_FF),
                       row(ffn_conv_b[0]))
    return _ffn_down_call(act, ffn_w_down[0].astype(BF16), x1, modx, row(final_g))
```

```python
import functools
import math

import numpy as np
import jax
import jax.numpy as jnp
from jax import lax
from jax.experimental import pallas as pl
from jax.experimental.pallas import tpu as pltpu

F32 = jnp.float32
BF16 = jnp.bfloat16

D_MODEL = 1024
CTX_LEN = 256
GRID_W = 64
FOURIER_W = 512
FOURIER_GROUPS = 8
RWKV_W = 512
HEAD = 64
N_HEADS = 8
DECAY_RANK = 64
ICLR_RANK = 64
GATE_RANK = 128
D_FF = 2816
NORM_EPS = 1e-6
GN_EPS = 64e-5
KK_EPS = 1e-12

CHUNK = 64
TILE = 256
MIX_TILE = 512
REV_TILE = 256
CHUNKS_PER_TILE = TILE // CHUNK
PREP_BATCH = 2
PAIR = 2 * HEAD
N_PAIRS = RWKV_W // PAIR
QUAD_HEADS = 4
SCAN_BATCH = 8
SCAN_GROUPS = 4
SCAN_LAG = 3
MOD_TILE = 1536
FF_TILE = 1408
FFN_ROWS = 256
DOWN_TILE = 1024
VMEM_LIMIT = 56 * 1024 * 1024
HEAD_SUM_TERMS = 1
CUMSUM_TERMS = 2
MOD_WEIGHT_TERMS = 2

RKV_LO = FOURIER_W
RKV_HI = FOURIER_W + 3 * RWKV_W
WD_LO = RKV_HI
AD_LO = WD_LO + 2 * DECAY_RANK
GD_LO = AD_LO + 2 * ICLR_RANK
PROJ_W = GD_LO + GATE_RANK


def _bdot(a, b):
    return jnp.dot(a.astype(BF16), b.astype(BF16), preferred_element_type=F32)


def _bdot_nt(a, b):
    return lax.dot_general(a.astype(BF16), b.astype(BF16), (((1,), (1,)), ((), ())),
                           preferred_element_type=F32)


def _bdot_tn(a, b):
    return lax.dot_general(a.astype(BF16), b.astype(BF16), (((0,), (0,)), ((), ())),
                           preferred_element_type=F32)


def _split_terms(a, terms):
    out, rem = [], a
    for _ in range(terms):
        hi = rem.astype(BF16)
        out.append(hi)
        rem = rem - hi.astype(F32)
    return out


def _dot_exact_rhs(a, b_exact, terms=HEAD_SUM_TERMS):
    acc = None
    for piece in _split_terms(a, terms):
        t = jnp.dot(piece, b_exact, preferred_element_type=F32)
        acc = t if acc is None else acc + t
    return acc


def _dot_exact_lhs(a_exact, b, terms=CUMSUM_TERMS):
    acc = None
    for piece in _split_terms(b, terms):
        t = jnp.dot(a_exact, piece, preferred_element_type=F32)
        acc = t if acc is None else acc + t
    return acc


def _sigmoid(x):
    return 1.0 / (1.0 + jnp.exp(-x))


def _rms(x, g):
    return x * lax.rsqrt(jnp.mean(x * x, axis=-1, keepdims=True) + NORM_EPS) * g


def _mod_kernel(c_ref, w_ref, b_ref, o_ref):
    cc = c_ref[...]
    s = cc * _sigmoid(cc)
    rows = s.shape[0]
    s_parts = jnp.concatenate(_split_terms(s, 3), axis=0)
    acc = b_ref[...]
    for wp in _split_terms(w_ref[...], MOD_WEIGHT_TERMS):
        t = jnp.dot(s_parts, wp, preferred_element_type=F32)
        acc = acc + (t[:rows] + t[rows:2 * rows] + t[2 * rows:])
    o_ref[...] = acc


def _mod_call(cc, ada_w, ada_b):
    rows, d = cc.shape
    n = ada_w.shape[1]
    tn = MOD_TILE
    return pl.pallas_call(
        _mod_kernel,
        grid=(n // tn,),
        in_specs=[pl.BlockSpec((rows, d), lambda i: (0, 0)),
                  pl.BlockSpec((d, tn), lambda i: (0, i)),
                  pl.BlockSpec((1, tn), lambda i: (0, i))],
        out_specs=pl.BlockSpec((rows, tn), lambda i: (0, i)),
        out_shape=jax.ShapeDtypeStruct((rows, n), F32),
        compiler_params=pltpu.CompilerParams(dimension_semantics=("arbitrary",),
                                             vmem_limit_bytes=VMEM_LIMIT),
        name="mod",
    )(cc, ada_w, ada_b)


def _prep_kernel(x_ref, xp_ref, xn_ref, ctx_ref, ms_ref, g1_ref, win_ref, cw_ref,
                 w0_ref, w2_ref, a0_ref, a2_ref, g2_ref, kk_ref, ka_ref, rk_ref,
                 cs_ref, ones_ref, tri_ref,
                 at_ref, bt_ref, kt_ref, rt_ref, v_ref, pl_ref, bonus_ref, gate_ref, z_ref):
    j = pl.program_id(1)
    refs = (x_ref, xp_ref, xn_ref, ctx_ref, ms_ref, g1_ref, win_ref, cw_ref,
            w0_ref, w2_ref, a0_ref, a2_ref, kk_ref, ka_ref, ones_ref, tri_ref,
            at_ref, bt_ref, kt_ref, rt_ref, v_ref, pl_ref)
    tiles = _run_interleaved([_prep_tile(bi, *refs) for bi in range(x_ref.shape[0])],
                             groups=PREP_BATCH, lag=1)

    @pl.when(j >= 1)
    def _():
        ones_blk = ones_ref[...]
        for bi, (u_four, r, v, gd, kd_sum) in enumerate(tiles):
            bs = _dot_exact_rhs(r * rk_ref[...] * kd_sum, ones_blk)
            bonus_ref[bi] = bs * v
            gate_ref[bi] = _bdot(_sigmoid(gd), g2_ref[...])
            z_ref[bi] = _bdot(u_four, cs_ref[...]).astype(BF16)


def _prep_tile(bi, x_ref, xp_ref, xn_ref, ctx_ref, ms_ref, g1_ref, win_ref, cw_ref,
               w0_ref, w2_ref, a0_ref, a2_ref, kk_ref, ka_ref, ones_ref, tri_ref,
               at_ref, bt_ref, kt_ref, rt_ref, v_ref, pl_ref):
    j = pl.program_id(1)
    n_j = pl.num_programs(1)
    is_ctx = j == 0
    shift = ms_ref[bi, 0, 0:1, :]
    scale = ms_ref[bi, 0, 1:2, :]
    g = g1_ref[...]

    def norm_mod(xr):
        return _rms(xr, g) * (1.0 + scale) + shift

    xin = jnp.where(is_ctx, ctx_ref[bi], x_ref[bi])
    h = norm_mod(xin).astype(BF16)
    u = jnp.dot(h, win_ref[...], preferred_element_type=F32)

    halo = jnp.concatenate([xp_ref[bi], xn_ref[bi]], axis=0)
    uh = jnp.dot(norm_mod(halo).astype(BF16), win_ref[:, RKV_LO:RKV_HI],
                 preferred_element_type=F32)
    yield
    prev_row = jnp.where(j >= 2, uh[7:8], 0.0)
    next_row = jnp.where(jnp.logical_and(j >= 1, j < n_j - 1), uh[8:9], 0.0)

    rkv = u[:, RKV_LO:RKV_HI]
    row = lax.broadcasted_iota(jnp.int32, rkv.shape, 0)
    u_m1 = jnp.where(row == 0, prev_row, pltpu.roll(rkv, 1, 0))
    u_p1 = jnp.where(row == TILE - 1, next_row, pltpu.roll(rkv, TILE - 1, 0))
    rkv = cw_ref[0:1, :] * u_m1 + cw_ref[1:2, :] * rkv + cw_ref[2:3, :] * u_p1
    r = rkv[:, 0:RWKV_W]
    k = rkv[:, RWKV_W:2 * RWKV_W]
    v = rkv[:, 2 * RWKV_W:3 * RWKV_W]

    wd = u[:, WD_LO:AD_LO]
    ad = u[:, AD_LO:GD_LO]
    gd = u[:, GD_LO:PROJ_W]
    w_lora = w0_ref[...] + _bdot(jnp.tanh(wd), w2_ref[...])
    lw = -math.exp(-0.5) * _sigmoid(w_lora)
    a = _sigmoid(a0_ref[...] + _bdot(ad, a2_ref[...]))

    ones_blk = ones_ref[...]
    kraw = k * kk_ref[...]
    kk = kraw * lax.rsqrt(_dot_exact_rhs(kraw * kraw, ones_blk) + KK_EPS)
    ka = ka_ref[...]
    kd = [k * (1.0 + (a[:, d * RWKV_W:(d + 1) * RWKV_W] - 1.0) * ka) for d in range(2)]
    yield

    v_ref[bi] = v.astype(BF16)
    for d in range(2):
        lwd = lw[:, d * RWKV_W:(d + 1) * RWKV_W]
        ad_ = a[:, d * RWKV_W:(d + 1) * RWKV_W]
        c = _dot_exact_lhs(tri_ref[d], lwd)
        e_pos = jnp.exp(c)
        e_neg = jnp.exp(-c)
        e_prev = jnp.exp(c - lwd)
        at_ref[bi, d] = (kk * e_prev).astype(BF16)
        bt_ref[bi, d] = (kk * ad_ * e_neg).astype(BF16)
        kt_ref[bi, d] = (kd[d] * e_neg).astype(BF16)
        rt_ref[bi, d] = (r * e_pos).astype(BF16)
        for q in range(CHUNKS_PER_TILE):
            last = q * CHUNK + (CHUNK - 1 if d == 0 else 0)
            pl_ref[bi, d, q] = e_pos[last:last + 1, :]
    return u[:, 0:FOURIER_W], r, v, gd, kd[0] + kd[1]


def _prep_call(x, ctx, modsel, norm1_g, w_in, conv_w, w0, w2bd, a0, a2bd, g2, k_k, k_a, r_k,
               cs, ones_blk, tri):
    b, s, d = x.shape
    n_j = 1 + s // TILE
    n_chunks = (CTX_LEN + s) // CHUNK
    t_all = CTX_LEN + s
    c2 = 2 * RWKV_W

    def full(arr):
        nd = arr.ndim
        return pl.BlockSpec(arr.shape, lambda bi, j, _n=nd: (0,) * _n)

    rows8 = TILE // 8
    nb = PREP_BATCH
    in_specs = [
        pl.BlockSpec((nb, TILE, d), lambda bi, j: (bi, jnp.maximum(j - 1, 0), 0)),
        pl.BlockSpec((nb, 8, d), lambda bi, j: (bi, jnp.maximum((j - 1) * rows8 - 1, 0), 0)),
        pl.BlockSpec((nb, 8, d), lambda bi, j: (bi, jnp.minimum(j * rows8, s // 8 - 1), 0)),
        pl.BlockSpec((nb, CTX_LEN, d), lambda bi, j: (bi, 0, 0)),
        pl.BlockSpec((nb, 1, 2, d), lambda bi, j: (bi, jnp.minimum(j, 1), 0, 0)),
        full(norm1_g), full(w_in), full(conv_w), full(w0), full(w2bd), full(a0), full(a2bd),
        full(g2), full(k_k), full(k_a), full(r_k), full(cs), full(ones_blk), full(tri),
    ]
    lat = lambda bi, j: (bi, jnp.maximum(j - 1, 0), 0)
    out_specs = [
        pl.BlockSpec((nb, 2, TILE, RWKV_W), lambda bi, j: (bi, 0, j, 0)),
        pl.BlockSpec((nb, 2, TILE, RWKV_W), lambda bi, j: (bi, 0, j, 0)),
        pl.BlockSpec((nb, 2, TILE, RWKV_W), lambda bi, j: (bi, 0, j, 0)),
        pl.BlockSpec((nb, 2, TILE, RWKV_W), lambda bi, j: (bi, 0, j, 0)),
        pl.BlockSpec((nb, TILE, RWKV_W), lambda bi, j: (bi, j, 0)),
        pl.BlockSpec((nb, 2, CHUNKS_PER_TILE, 1, RWKV_W), lambda bi, j: (bi, 0, j, 0, 0)),
        pl.BlockSpec((nb, TILE, RWKV_W), lat),
        pl.BlockSpec((nb, TILE, RWKV_W), lat),
        pl.BlockSpec((nb, TILE, 2 * FOURIER_W), lat),
    ]
    out_shape = [
        jax.ShapeDtypeStruct((b, 2, t_all, RWKV_W), BF16),
        jax.ShapeDtypeStruct((b, 2, t_all, RWKV_W), BF16),
        jax.ShapeDtypeStruct((b, 2, t_all, RWKV_W), BF16),
        jax.ShapeDtypeStruct((b, 2, t_all, RWKV_W), BF16),
        jax.ShapeDtypeStruct((b, t_all, RWKV_W), BF16),
        jax.ShapeDtypeStruct((b, 2, n_chunks, 1, RWKV_W), F32),
        jax.ShapeDtypeStruct((b, s, RWKV_W), F32),
        jax.ShapeDtypeStruct((b, s, RWKV_W), F32),
        jax.ShapeDtypeStruct((b, s, 2 * FOURIER_W), BF16),
    ]
    return pl.pallas_call(
        _prep_kernel,
        grid=(b // nb, n_j),
        in_specs=in_specs,
        out_specs=out_specs,
        out_shape=out_shape,
        compiler_params=pltpu.CompilerParams(dimension_semantics=("arbitrary", "arbitrary"),
                                             vmem_limit_bytes=VMEM_LIMIT),
        name="prep",
    )(x, x, x, ctx, modsel, norm1_g, w_in, conv_w, w0, w2bd, a0, a2bd, g2, k_k, k_a, r_k,
      cs, ones_blk, tri)


def _quad_chunk(a, bm, km, r, v, plrow, g_states, strict, incl, eye, lane_lo, mask_bd, eye_row,
                level_masks):
    n = PAIR

    def dup(xv):
        z = jnp.zeros_like(xv)
        return jnp.concatenate([jnp.where(lane_lo, xv, z), jnp.where(lane_lo, z, xv)], axis=0)

    def to_row(x2):
        return x2[:CHUNK] + x2[CHUNK:]

    def bdiag(xr):
        return jnp.where(mask_bd, jnp.concatenate([xr] * QUAD_HEADS, axis=0), jnp.zeros((), xr.dtype))

    pairs = []
    for p in range(2):
        sl = slice(p * n, (p + 1) * n)
        a2, b2, k2, r2, v2 = (dup(t[:, sl]) for t in (a, bm, km, r, v))
        rhs = jnp.concatenate([b2, k2], axis=0)
        s = _bdot_nt(jnp.concatenate([a2, r2], axis=0), rhs)
        pairs.append((sl, r2, v2, rhs, s))
        yield

    a_ab = jnp.concatenate([to_row(jnp.where(strict, s[:n, :n], 0.0)) for *_, s in pairs], axis=1)
    a_ak = jnp.concatenate([to_row(jnp.where(strict, s[:n, n:], 0.0)) for *_, s in pairs], axis=1)
    m_rbk = [jnp.concatenate([jnp.where(incl, s[n:, :n], 0.0), jnp.where(incl, s[n:, n:], 0.0)],
                             axis=1).astype(BF16) for *_, s in pairs]

    av = _bdot(a_ak, bdiag(v))
    a_lvl = [jnp.where(m, a_ab, 0.0).astype(BF16) for m in level_masks]
    t_row = jnp.where(eye_row, 1.0, 0.0) - jnp.where(level_masks[0], a_ab, 0.0)
    lvl = 1
    while lvl < len(a_lvl):
        paired = lvl + 1 < len(a_lvl)
        lhs = jnp.concatenate([a_lvl[lvl], a_lvl[lvl + 1]], axis=0) if paired else a_lvl[lvl]
        yc = jnp.dot(lhs, bdiag(t_row.astype(BF16)), preferred_element_type=F32)
        yield
        y_bd = bdiag(yc[:CHUNK].astype(BF16))
        if paired:
            both = jnp.dot(jnp.concatenate([t_row, yc[CHUNK:]], axis=0).astype(BF16), y_bd,
                           preferred_element_type=F32)
            yield
            t_row = t_row - both[:CHUNK]
            y_next = yc[CHUNK:] - both[CHUNK:]
            t_row = t_row - _bdot(t_row, bdiag(y_next.astype(BF16)))
            lvl += 2
        else:
            t_row = t_row - _bdot(t_row, y_bd)
            lvl += 1
        yield
    tx = _bdot(t_row, jnp.concatenate([bdiag(a), bdiag(av.astype(BF16))], axis=1))
    yield
    a_w = tx[:, :2 * n].astype(BF16)
    u_v = (-tx[:, 2 * n:]).astype(BF16)
    stage5 = []
    for (sl, r2, v2, rhs, _), m2 in zip(pairs, m_rbk):
        zmat = jnp.concatenate(
            [jnp.concatenate([dup(a_w[:, sl]), dup(u_v[:, sl])], axis=1),
             jnp.concatenate([jnp.zeros_like(v2), v2], axis=1)], axis=0)
        rhs_end = rhs.astype(F32) * plrow[:, sl]
        lhs = jnp.concatenate([rhs_end.T.astype(BF16), m2], axis=0)
        stage5.append(jnp.dot(lhs, zmat, preferred_element_type=F32))
        yield
    ys, g_new = [], []
    for (sl, r2, v2, rhs, _), o5, g_state in zip(pairs, stage5, g_states):
        phi = jnp.where(eye, plrow[:, sl], 0.0) - o5[:n, :n]
        r_y = r2.astype(F32) - o5[n:, :n]
        o6 = _bdot(jnp.concatenate([r_y, phi], axis=0), g_state)
        ys.append(to_row(o6[:n] + o5[n:, n:]))
        g_new.append(o6[n:] + o5[:n, n:])
    return jnp.concatenate(ys, axis=1), g_new


def _run_interleaved(gens, groups=1, lag=0):
    results = [None] * len(gens)
    active = list(range(len(gens)))
    rnd = 0
    while active:
        for i in list(active):
            if rnd < (i % groups) * lag:
                continue
            try:
                next(gens[i])
            except StopIteration as stop:
                results[i] = stop.value
                active.remove(i)
        rnd += 1
    return results


def _scan_kernel(atf, btf, ktf, rtf, vf, plf, atb, btb, ktb, rtb, vb, plb,
                 yf_ref, yb_ref, g_ref):
    s = pl.program_id(1)

    @pl.when(s == 0)
    def _():
        g_ref[...] = jnp.zeros_like(g_ref)

    ri = lax.broadcasted_iota(jnp.int32, (PAIR, PAIR), 0)
    ci = lax.broadcasted_iota(jnp.int32, (PAIR, PAIR), 1)
    same = (ri >= HEAD) == (ci >= HEAD)
    ti = ri & (HEAD - 1)
    tj = ci & (HEAD - 1)
    eye = ri == ci
    lane_lo = lax.broadcasted_iota(jnp.int32, (CHUNK, PAIR), 1) < HEAD
    masks = [(same & (tj < ti), same & (tj <= ti)),
             (same & (tj > ti), same & (tj >= ti))]
    quad_w = QUAD_HEADS * HEAD
    rq = lax.broadcasted_iota(jnp.int32, (quad_w, quad_w), 0)
    cq = lax.broadcasted_iota(jnp.int32, (quad_w, quad_w), 1)
    mask_bd = (rq // HEAD) == (cq // HEAD)
    t_i = lax.broadcasted_iota(jnp.int32, (CHUNK, quad_w), 0)
    s_i = lax.broadcasted_iota(jnp.int32, (CHUNK, quad_w), 1) & (HEAD - 1)
    eye_row = s_i == t_i
    level_masks = []
    for d in range(2):
        late, early = (t_i, s_i) if d == 0 else (s_i, t_i)
        level_masks.append([
            ((t_i >> (lvl + 1)) == (s_i >> (lvl + 1)))
            & (((late >> lvl) & 1) == 1) & (((early >> lvl) & 1) == 0)
            for lvl in range(int(math.log2(CHUNK)))])
    refs = [(atf, btf, ktf, rtf, vf, plf, yf_ref), (atb, btb, ktb, rtb, vb, plb, yb_ref)]
    gens, dests = [], []
    for bi in range(yf_ref.shape[0]):
        for d in range(2):
            at, bt, kt, rt, vv, plr, y_ref = refs[d]
            strict, incl = masks[d]
            for q in range(RWKV_W // quad_w):
                sl = slice(q * quad_w, (q + 1) * quad_w)
                gens.append(_quad_chunk(at[bi, 0, :, sl], bt[bi, 0, :, sl], kt[bi, 0, :, sl],
                                        rt[bi, 0, :, sl], vv[bi, :, sl], plr[bi, 0, 0, :, sl],
                                        [g_ref[bi, d, 2 * q], g_ref[bi, d, 2 * q + 1]],
                                        strict, incl, eye, lane_lo, mask_bd, eye_row,
                                        level_masks[d]))
                dests.append((y_ref, bi, d, q, sl))
    done = _run_interleaved(gens, groups=SCAN_GROUPS, lag=SCAN_LAG)
    for (y, g_new), (y_ref, bi, d, q, sl) in zip(done, dests):
        g_ref[bi, d, 2 * q] = g_new[0]
        g_ref[bi, d, 2 * q + 1] = g_new[1]
        y_ref[bi, :, sl] = y


def _scan_call(at, bt, kt, rt, v, plast, s_lat):
    b = at.shape[0]
    n_chunks = at.shape[2] // CHUNK
    n_ctx = CTX_LEN // CHUNK
    n_lat = s_lat // CHUNK

    def fwd_c(s):
        return s

    def bwd_c(s):
        return jnp.where(s < n_ctx, n_ctx - 1 - s, n_chunks + n_ctx - 1 - s)

    nb = SCAN_BATCH

    def dir_spec(d, cfun):
        return pl.BlockSpec((nb, 1, CHUNK, RWKV_W), lambda bi, s: (bi, d, cfun(s), 0))

    def v_spec(cfun):
        return pl.BlockSpec((nb, CHUNK, RWKV_W), lambda bi, s: (bi, cfun(s), 0))

    def pl_spec(d, cfun):
        return pl.BlockSpec((nb, 1, 1, 1, RWKV_W), lambda bi, s: (bi, d, cfun(s), 0, 0))

    in_specs = ([dir_spec(0, fwd_c)] * 4 + [v_spec(fwd_c), pl_spec(0, fwd_c)]
                + [dir_spec(1, bwd_c)] * 4 + [v_spec(bwd_c), pl_spec(1, bwd_c)])
    out_specs = [
        pl.BlockSpec((nb, CHUNK, RWKV_W), lambda bi, s: (bi, jnp.maximum(s - n_ctx, 0), 0)),
        pl.BlockSpec((nb, CHUNK, RWKV_W),
                     lambda bi, s: (bi, jnp.minimum(n_chunks - 1 - s, n_lat - 1), 0)),
    ]
    out_shape = [jax.ShapeDtypeStruct((b, s_lat, RWKV_W), F32)] * 2
    return pl.pallas_call(
        _scan_kernel,
        grid=(b // nb, n_chunks),
        in_specs=in_specs,
        out_specs=out_specs,
        out_shape=out_shape,
        scratch_shapes=[pltpu.VMEM((nb, 2, N_PAIRS, PAIR, PAIR), F32)],
        compiler_params=pltpu.CompilerParams(dimension_semantics=("arbitrary", "arbitrary"),
                                             vmem_limit_bytes=VMEM_LIMIT),
        name="scan",
    )(at, bt, kt, rt, v, plast, at, bt, kt, rt, v, plast)


def _mix_kernel(dft_ref, rev_ref, z_ref, yf_ref, yb_ref, bonus_ref, gate_ref, x_ref, mod_ref,
                gng_ref, gnb_ref, fg_ref, wout_ref, n2g_ref, ones_ref, x1_ref, h2_ref, zf_ref,
                *, fscale):
    seq = z_ref.shape[1]
    half = seq // 2
    rt = rev_ref.shape[0]

    @pl.when(pl.program_id(1) == 0)
    def _():
        mid = z_ref[0, half:half + 1, :FOURIER_W].astype(F32)
        row0 = lax.broadcasted_iota(jnp.int32, (rt, FOURIER_W), 0) == 0
        for part, sign in ((0, 1.0), (1, -1.0)):
            lanes = slice(part * FOURIER_W, (part + 1) * FOURIER_W)
            for r in range(half // rt):
                lo = seq - rt * (r + 1)
                nxt = (z_ref[0, lo + rt:lo + 2 * rt, lanes] if r > 0
                       else jnp.zeros((rt, FOURIER_W), BF16))
                rev = jnp.dot(rev_ref[...], jnp.concatenate([z_ref[0, lo:lo + rt, lanes], nxt], axis=0),
                              preferred_element_type=F32)
                folded = z_ref[0, r * rt:(r + 1) * rt, lanes].astype(F32) + sign * rev
                if part == 1 and r == 0:
                    folded = jnp.where(row0, mid, folded)
                zf_ref[part * half + r * rt:part * half + (r + 1) * rt, :] = folded.astype(BF16)

    yfour = jnp.dot(dft_ref[...], zf_ref[...], preferred_element_type=F32)
    four = _rms(yfour * fscale, fg_ref[...])

    ones_blk = ones_ref[...]
    y = yf_ref[0] + yb_ref[0]
    mu = _dot_exact_rhs(y, ones_blk) * (1.0 / HEAD)
    dy = y - mu
    var = _dot_exact_rhs(dy * dy, ones_blk) * (1.0 / HEAD)
    yn = dy * lax.rsqrt(var + GN_EPS) * gng_ref[...] + gnb_ref[...]
    rw = (yn + bonus_ref[0]) * gate_ref[0]

    out = _bdot(four, wout_ref[:FOURIER_W, :]) + _bdot(rw, wout_ref[FOURIER_W:, :])
    x1 = x_ref[0] + mod_ref[0, 2:3, :] * out
    x1_ref[0] = x1
    h2 = _rms(x1, n2g_ref[...]) * (1.0 + mod_ref[0, 4:5, :]) + mod_ref[0, 3:4, :]
    h2_ref[0] = h2.astype(BF16)


def _mix_call(dft, rev, z, yf, yb, bonus, gate, x, modx, gn_g, gn_b, f_g, w_out, norm2_g,
              ones_blk):
    b, s, d = x.shape
    tm = MIX_TILE

    def full(arr):
        nd = arr.ndim
        return pl.BlockSpec(arr.shape, lambda bi, m, _n=nd: (0,) * _n)

    tok = lambda w: pl.BlockSpec((1, tm, w), lambda bi, m: (bi, m, 0))
    in_specs = [
        pl.BlockSpec((tm, s), lambda bi, m: (m, 0)),
        full(rev),
        pl.BlockSpec((1, s, 2 * FOURIER_W), lambda bi, m: (bi, 0, 0)),
        tok(RWKV_W), tok(RWKV_W), tok(RWKV_W), tok(RWKV_W), tok(d),
        pl.BlockSpec((1, 6, d), lambda bi, m: (bi, 0, 0)),
        full(gn_g), full(gn_b), full(f_g), full(w_out), full(norm2_g), full(ones_blk),
    ]
    fscale = 1.0 / math.sqrt(s * (FOURIER_W // FOURIER_GROUPS))
    return pl.pallas_call(
        functools.partial(_mix_kernel, fscale=fscale),
        grid=(b, s // tm),
        in_specs=in_specs,
        out_specs=[tok(d), tok(d)],
        out_shape=[jax.ShapeDtypeStruct((b, s, d), F32), jax.ShapeDtypeStruct((b, s, d), BF16)],
        scratch_shapes=[pltpu.VMEM((s, FOURIER_W), BF16)],
        compiler_params=pltpu.CompilerParams(dimension_semantics=("arbitrary", "arbitrary"),
                                             vmem_limit_bytes=VMEM_LIMIT),
        name="mix",
    )(dft, rev, z, yf, yb, bonus, gate, x, modx, gn_g, gn_b, f_g, w_out, norm2_g, ones_blk)


def _ffn_up_kernel(h2_ref, wg_ref, wv_ref, cwg_ref, cwv_ref, cbg_ref, cbv_ref, act_ref):
    t = h2_ref.shape[1]
    rows = FFN_ROWS
    n_blk = t // rows
    w = jnp.concatenate([wg_ref[...], wv_ref[...]], axis=1)
    cw = jnp.concatenate([cwg_ref[...], cwv_ref[...]], axis=1).astype(BF16)
    cb = jnp.concatenate([cbg_ref[...], cbv_ref[...]], axis=1).astype(BF16)
    col = lax.broadcasted_iota(jnp.int32, (rows, 2 * FF_TILE), 0) & (GRID_W - 1)
    has_left = col > 0
    has_right = col < GRID_W - 1
    zeros = jnp.zeros((GRID_W, 2 * FF_TILE), BF16)

    def row_mixes(i):
        u = jnp.dot(h2_ref[0, i * rows:(i + 1) * rows, :], w, preferred_element_type=F32)
        u_l = jnp.where(has_left, pltpu.roll(u, 1, 0), 0.0).astype(BF16)
        u_r = jnp.where(has_right, pltpu.roll(u, rows - 1, 0), 0.0).astype(BF16)
        u_c = u.astype(BF16)
        return [cw[3 * kh:3 * kh + 1] * u_l + cw[3 * kh + 1:3 * kh + 2] * u_c
                + cw[3 * kh + 2:3 * kh + 3] * u_r for kh in range(3)]

    def finish(i, prev, cur, nxt):
        above = zeros if prev is None else prev[0][rows - GRID_W:]
        below = zeros if nxt is None else nxt[2][:GRID_W]
        c = (cur[1] + jnp.concatenate([above, cur[0][:rows - GRID_W]], axis=0)
             + jnp.concatenate([cur[2][GRID_W:], below], axis=0) + cb)
        cg = c[:, :FF_TILE]
        act_ref[0, i * rows:(i + 1) * rows, :] = cg * _sigmoid(cg) * c[:, FF_TILE:]

    mixes = [None] * (n_blk + 1)
    for i in range(n_blk):
        mixes[i] = row_mixes(i)
        if i >= 1:
            finish(i - 1, mixes[i - 2] if i >= 2 else None, mixes[i - 1], mixes[i])
    finish(n_blk - 1, mixes[n_blk - 2] if n_blk >= 2 else None, mixes[n_blk - 1], None)


def _ffn_up_call(h2, w_up, conv_w9, conv_b):
    b, s, d = h2.shape
    n_f = D_FF // FF_TILE
    in_specs = [
        pl.BlockSpec((1, s, d), lambda bi, f: (bi, 0, 0)),
        pl.BlockSpec((d, FF_TILE), lambda bi, f: (0, f)),
        pl.BlockSpec((d, FF_TILE), lambda bi, f: (0, n_f + f)),
        pl.BlockSpec((9, FF_TILE), lambda bi, f: (0, f)),
        pl.BlockSpec((9, FF_TILE), lambda bi, f: (0, n_f + f)),
        pl.BlockSpec((1, FF_TILE), lambda bi, f: (0, f)),
        pl.BlockSpec((1, FF_TILE), lambda bi, f: (0, n_f + f)),
    ]
    return pl.pallas_call(
        _ffn_up_kernel,
        grid=(b, n_f),
        in_specs=in_specs,
        out_specs=pl.BlockSpec((1, s, FF_TILE), lambda bi, f: (bi, 0, f)),
        out_shape=jax.ShapeDtypeStruct((b, s, D_FF), BF16),
        compiler_params=pltpu.CompilerParams(dimension_semantics=("arbitrary", "arbitrary"),
                                             vmem_limit_bytes=VMEM_LIMIT),
        name="ffn_up",
    )(h2, w_up, w_up, conv_w9, conv_w9, conv_b, conv_b)


def _ffn_down_kernel(act_ref, wd_ref, x1_ref, mod_ref, fg_ref, out_ref):
    y = jnp.dot(act_ref[0], wd_ref[...], preferred_element_type=F32)
    out_ref[0] = _rms(x1_ref[0] + mod_ref[0, 5:6, :] * y, fg_ref[...])


def _ffn_down_call(act, w_down, x1, modx, final_g):
    b, s, d = x1.shape
    tm = DOWN_TILE
    in_specs = [
        pl.BlockSpec((1, tm, D_FF), lambda bi, m: (bi, m, 0)),
        pl.BlockSpec((D_FF, d), lambda bi, m: (0, 0)),
        pl.BlockSpec((1, tm, d), lambda bi, m: (bi, m, 0)),
        pl.BlockSpec((1, 6, d), lambda bi, m: (bi, 0, 0)),
        pl.BlockSpec((1, d), lambda bi, m: (0, 0)),
    ]
    return pl.pallas_call(
        _ffn_down_kernel,
        grid=(b, s // tm),
        in_specs=in_specs,
        out_specs=pl.BlockSpec((1, tm, d), lambda bi, m: (bi, m, 0)),
        out_shape=jax.ShapeDtypeStruct((b, s, d), F32),
        compiler_params=pltpu.CompilerParams(dimension_semantics=("arbitrary", "arbitrary"),
                                             vmem_limit_bytes=VMEM_LIMIT),
        name="ffn_down",
    )(act, w_down, x1, modx, final_g)


@functools.lru_cache(maxsize=None)
def _constants(seq):
    gw = FOURIER_W // FOURIER_GROUPS
    nm = np.outer(np.arange(gw), np.arange(gw)) % gw
    ang = 2.0 * np.pi * nm / gw
    cs = np.zeros((FOURIER_W, 2 * FOURIER_W), np.float64)
    for gi in range(FOURIER_GROUPS):
        sl = slice(gi * gw, (gi + 1) * gw)
        cs[sl, sl] = np.cos(ang)
        cs[sl, FOURIER_W + gi * gw:FOURIER_W + (gi + 1) * gw] = np.sin(ang)
    half = seq // 2
    k_idx = np.arange(seq, dtype=np.int64)[:, None]
    ang_c = 2.0 * np.pi * ((k_idx * np.arange(0, half + 1, dtype=np.int64)[None, :]) % seq) / seq
    ang_s = 2.0 * np.pi * ((k_idx * np.arange(1, half, dtype=np.int64)[None, :]) % seq) / seq
    dft = np.concatenate([np.cos(ang_c), -np.sin(ang_s)], axis=1)
    ri = np.arange(REV_TILE)
    rev = (np.arange(2 * REV_TILE)[None, :] == (REV_TILE - ri)[:, None]).astype(np.float64)
    head_id = np.arange(RWKV_W) // HEAD
    ones_blk = (head_id[:, None] == head_id[None, :]).astype(np.float64)
    ti = np.arange(TILE)
    same_chunk = (ti[:, None] // CHUNK) == (ti[None, :] // CHUNK)
    tri = np.stack([same_chunk & (ti[None, :] <= ti[:, None]),
                    same_chunk & (ti[None, :] >= ti[:, None])]).astype(np.float64)
    to_bf16 = lambda arr: np.asarray(arr, dtype=np.float32).astype(BF16)
    return (np.asarray(cs, dtype=np.float32), np.asarray(dft, dtype=np.float32),
            to_bf16(ones_blk), to_bf16(tri), to_bf16(rev))


def _block_diag_dirs(w):
    z = jnp.zeros_like(w[0])
    return jnp.concatenate([jnp.concatenate([w[0], z], axis=1),
                            jnp.concatenate([z, w[1]], axis=1)], axis=0)


def kernel(x, c, ctx, c_ctx, ada_w, ada_b, norm1_g, norm2_g, w_in, rwkv_conv_w, decay_w0, decay_w2, iclr_a0, iclr_a2, gate_g2, k_k, k_a, r_k, gn_g, gn_b, fourier_g, w_out, ffn_w_up, ffn_conv_w, ffn_conv_b, ffn_w_down, final_g):
    b, s, d = x.shape
    assert ada_w.shape[0] == 1, "single-layer configuration"
    assert (b, s, d) == (c.shape[0], s, D_MODEL) and ctx.shape == (b, CTX_LEN, d)
    assert s % TILE == 0 and CTX_LEN == TILE
    cs, dft, ones_blk, tri, rev = (jnp.asarray(t) for t in _constants(s))
    cs, dft = cs.astype(BF16), dft.astype(BF16)

    cc = jnp.concatenate([c, c_ctx[None, :], jnp.zeros((16 - b - 1, d), F32)], axis=0)
    mod = _mod_call(cc, ada_w[0], ada_b[0][None, :])
    modx = mod[:b].reshape(b, 6, d)
    modc = jnp.broadcast_to(mod[b].reshape(1, 6, d)[:, :2], (b, 2, d))
    modsel = jnp.stack([modc, modx[:, :2]], axis=1)

    row = lambda t: t.reshape(1, -1)
    at, bt, kt, rt, v, plast, bonus, gate, z = _prep_call(
        x, ctx, modsel, row(norm1_g[0]), w_in[0].astype(BF16), rwkv_conv_w[0],
        row(decay_w0[0]), _block_diag_dirs(decay_w2[0]).astype(BF16),
        row(iclr_a0[0]), _block_diag_dirs(iclr_a2[0]).astype(BF16),
        gate_g2[0].astype(BF16), row(k_k[0]), row(k_a[0]), row(r_k[0]), cs, ones_blk, tri)

    yf, yb = _scan_call(at, bt, kt, rt, v, plast, s)

    x1, h2 = _mix_call(dft, rev, z, yf, yb, bonus, gate, x, modx, row(gn_g[0]), row(gn_b[0]),
                       row(fourier_g[0]), w_out[0].astype(BF16), row(norm2_g[0]), ones_blk)

    act = _ffn_up_call(h2, ffn_w_up[0].astype(BF16), ffn_conv_w[0].reshape(9, 2 * D_FF),
                       row(ffn_conv_b[0]))
    return _ffn_down_call(act, ffn_w_down[0].astype(BF16), x1, modx, row(final_g))
```

```python
import functools
import math

import numpy as np
import jax
import jax.numpy as jnp
from jax import lax
from jax.experimental import pallas as pl
from jax.experimental.pallas import tpu as pltpu

F32 = jnp.float32
BF16 = jnp.bfloat16

D_MODEL = 1024
CTX_LEN = 256
GRID_W = 64
FOURIER_W = 512
FOURIER_GROUPS = 8
RWKV_W = 512
HEAD = 64
N_HEADS = 8
DECAY_RANK = 64
ICLR_RANK = 64
GATE_RANK = 128
D_FF = 2816
NORM_EPS = 1e-6
GN_EPS = 64e-5
KK_EPS = 1e-12

CHUNK = 64
TILE = 256
MIX_TILE = 512
REV_TILE = 256
CHUNKS_PER_TILE = TILE // CHUNK
PREP_BATCH = 2
PAIR = 2 * HEAD
N_PAIRS = RWKV_W // PAIR
QUAD_HEADS = 4
SCAN_BATCH = 8
SCAN_GROUPS = 4
SCAN_LAG = 4
MOD_TILE = 1536
FF_TILE = 1408
FFN_ROWS = 256
DOWN_TILE = 1024
VMEM_LIMIT = 56 * 1024 * 1024
HEAD_SUM_TERMS = 1
CUMSUM_TERMS = 2
MOD_WEIGHT_TERMS = 2

RKV_LO = FOURIER_W
RKV_HI = FOURIER_W + 3 * RWKV_W
WD_LO = RKV_HI
AD_LO = WD_LO + 2 * DECAY_RANK
GD_LO = AD_LO + 2 * ICLR_RANK
PROJ_W = GD_LO + GATE_RANK


def _bdot(a, b):
    return jnp.dot(a.astype(BF16), b.astype(BF16), preferred_element_type=F32)


def _bdot_nt(a, b):
    return lax.dot_general(a.astype(BF16), b.astype(BF16), (((1,), (1,)), ((), ())),
                           preferred_element_type=F32)


def _bdot_tn(a, b):
    return lax.dot_general(a.astype(BF16), b.astype(BF16), (((0,), (0,)), ((), ())),
                           preferred_element_type=F32)


def _split_terms(a, terms):
    out, rem = [], a
    for _ in range(terms):
        hi = rem.astype(BF16)
        out.append(hi)
        rem = rem - hi.astype(F32)
    return out


def _dot_exact_rhs(a, b_exact, terms=HEAD_SUM_TERMS):
    acc = None
    for piece in _split_terms(a, terms):
        t = jnp.dot(piece, b_exact, preferred_element_type=F32)
        acc = t if acc is None else acc + t
    return acc


def _dot_exact_lhs(a_exact, b, terms=CUMSUM_TERMS):
    acc = None
    for piece in _split_terms(b, terms):
        t = jnp.dot(a_exact, piece, preferred_element_type=F32)
        acc = t if acc is None else acc + t
    return acc


def _sigmoid(x):
    return 1.0 / (1.0 + jnp.exp(-x))


def _rms(x, g):
    return x * lax.rsqrt(jnp.mean(x * x, axis=-1, keepdims=True) + NORM_EPS) * g


def _mod_kernel(c_ref, w_ref, b_ref, o_ref):
    cc = c_ref[...]
    s = cc * _sigmoid(cc)
    rows = s.shape[0]
    s_parts = jnp.concatenate(_split_terms(s, 3), axis=0)
    acc = b_ref[...]
    for wp in _split_terms(w_ref[...], MOD_WEIGHT_TERMS):
        t = jnp.dot(s_parts, wp, preferred_element_type=F32)
        acc = acc + (t[:rows] + t[rows:2 * rows] + t[2 * rows:])
    o_ref[...] = acc


def _mod_call(cc, ada_w, ada_b):
    rows, d = cc.shape
    n = ada_w.shape[1]
    tn = MOD_TILE
    return pl.pallas_call(
        _mod_kernel,
        grid=(n // tn,),
        in_specs=[pl.BlockSpec((rows, d), lambda i: (0, 0)),
                  pl.BlockSpec((d, tn), lambda i: (0, i)),
                  pl.BlockSpec((1, tn), lambda i: (0, i))],
        out_specs=pl.BlockSpec((rows, tn), lambda i: (0, i)),
        out_shape=jax.ShapeDtypeStruct((rows, n), F32),
        compiler_params=pltpu.CompilerParams(dimension_semantics=("arbitrary",),
                                             vmem_limit_bytes=VMEM_LIMIT),
        name="mod",
    )(cc, ada_w, ada_b)


def _prep_kernel(x_ref, xp_ref, xn_ref, ctx_ref, ms_ref, g1_ref, win_ref, cw_ref,
                 w0_ref, w2_ref, a0_ref, a2_ref, g2_ref, kk_ref, ka_ref, rk_ref,
                 cs_ref, ones_ref, tri_ref,
                 at_ref, bt_ref, kt_ref, rt_ref, v_ref, pl_ref, bonus_ref, gate_ref, z_ref):
    j = pl.program_id(1)
    refs = (x_ref, xp_ref, xn_ref, ctx_ref, ms_ref, g1_ref, win_ref, cw_ref,
            w0_ref, w2_ref, a0_ref, a2_ref, kk_ref, ka_ref, ones_ref, tri_ref,
            at_ref, bt_ref, kt_ref, rt_ref, v_ref, pl_ref)
    tiles = _run_interleaved([_prep_tile(bi, *refs) for bi in range(x_ref.shape[0])],
                             groups=PREP_BATCH, lag=1)

    @pl.when(j >= 1)
    def _():
        ones_blk = ones_ref[...]
        for bi, (u_four, r, v, gd, kd_sum) in enumerate(tiles):
            bs = _dot_exact_rhs(r * rk_ref[...] * kd_sum, ones_blk)
            bonus_ref[bi] = bs * v
            gate_ref[bi] = _bdot(_sigmoid(gd), g2_ref[...])
            z_ref[bi] = _bdot(u_four, cs_ref[...]).astype(BF16)


def _prep_tile(bi, x_ref, xp_ref, xn_ref, ctx_ref, ms_ref, g1_ref, win_ref, cw_ref,
               w0_ref, w2_ref, a0_ref, a2_ref, kk_ref, ka_ref, ones_ref, tri_ref,
               at_ref, bt_ref, kt_ref, rt_ref, v_ref, pl_ref):
    j = pl.program_id(1)
    n_j = pl.num_programs(1)
    is_ctx = j == 0
    shift = ms_ref[bi, 0, 0:1, :]
    scale = ms_ref[bi, 0, 1:2, :]
    g = g1_ref[...]

    def norm_mod(xr):
        return _rms(xr, g) * (1.0 + scale) + shift

    xin = jnp.where(is_ctx, ctx_ref[bi], x_ref[bi])
    h = norm_mod(xin).astype(BF16)
    u = jnp.dot(h, win_ref[...], preferred_element_type=F32)

    halo = jnp.concatenate([xp_ref[bi], xn_ref[bi]], axis=0)
    uh = jnp.dot(norm_mod(halo).astype(BF16), win_ref[:, RKV_LO:RKV_HI],
                 preferred_element_type=F32)
    yield
    prev_row = jnp.where(j >= 2, uh[7:8], 0.0)
    next_row = jnp.where(jnp.logical_and(j >= 1, j < n_j - 1), uh[8:9], 0.0)

    rkv = u[:, RKV_LO:RKV_HI]
    row = lax.broadcasted_iota(jnp.int32, rkv.shape, 0)
    u_m1 = jnp.where(row == 0, prev_row, pltpu.roll(rkv, 1, 0))
    u_p1 = jnp.where(row == TILE - 1, next_row, pltpu.roll(rkv, TILE - 1, 0))
    rkv = cw_ref[0:1, :] * u_m1 + cw_ref[1:2, :] * rkv + cw_ref[2:3, :] * u_p1
    r = rkv[:, 0:RWKV_W]
    k = rkv[:, RWKV_W:2 * RWKV_W]
    v = rkv[:, 2 * RWKV_W:3 * RWKV_W]

    wd = u[:, WD_LO:AD_LO]
    ad = u[:, AD_LO:GD_LO]
    gd = u[:, GD_LO:PROJ_W]
    w_lora = w0_ref[...] + _bdot(jnp.tanh(wd), w2_ref[...])
    lw = -math.exp(-0.5) * _sigmoid(w_lora)
    a = _sigmoid(a0_ref[...] + _bdot(ad, a2_ref[...]))

    ones_blk = ones_ref[...]
    kraw = k * kk_ref[...]
    kk = kraw * lax.rsqrt(_dot_exact_rhs(kraw * kraw, ones_blk) + KK_EPS)
    ka = ka_ref[...]
    kd = [k * (1.0 + (a[:, d * RWKV_W:(d + 1) * RWKV_W] - 1.0) * ka) for d in range(2)]
    yield

    v_ref[bi] = v.astype(BF16)
    for d in range(2):
        lwd = lw[:, d * RWKV_W:(d + 1) * RWKV_W]
        ad_ = a[:, d * RWKV_W:(d + 1) * RWKV_W]
        c = _dot_exact_lhs(tri_ref[d], lwd)
        e_pos = jnp.exp(c)
        e_neg = jnp.exp(-c)
        e_prev = jnp.exp(c - lwd)
        at_ref[bi, d] = (kk * e_prev).astype(BF16)
        bt_ref[bi, d] = (kk * ad_ * e_neg).astype(BF16)
        kt_ref[bi, d] = (kd[d] * e_neg).astype(BF16)
        rt_ref[bi, d] = (r * e_pos).astype(BF16)
        for q in range(CHUNKS_PER_TILE):
            last = q * CHUNK + (CHUNK - 1 if d == 0 else 0)
            pl_ref[bi, d, q] = e_pos[last:last + 1, :]
    return u[:, 0:FOURIER_W], r, v, gd, kd[0] + kd[1]


def _prep_call(x, ctx, modsel, norm1_g, w_in, conv_w, w0, w2bd, a0, a2bd, g2, k_k, k_a, r_k,
               cs, ones_blk, tri):
    b, s, d = x.shape
    n_j = 1 + s // TILE
    n_chunks = (CTX_LEN + s) // CHUNK
    t_all = CTX_LEN + s
    c2 = 2 * RWKV_W

    def full(arr):
        nd = arr.ndim
        return pl.BlockSpec(arr.shape, lambda bi, j, _n=nd: (0,) * _n)

    rows8 = TILE // 8
    nb = PREP_BATCH
    in_specs = [
        pl.BlockSpec((nb, TILE, d), lambda bi, j: (bi, jnp.maximum(j - 1, 0), 0)),
        pl.BlockSpec((nb, 8, d), lambda bi, j: (bi, jnp.maximum((j - 1) * rows8 - 1, 0), 0)),
        pl.BlockSpec((nb, 8, d), lambda bi, j: (bi, jnp.minimum(j * rows8, s // 8 - 1), 0)),
        pl.BlockSpec((nb, CTX_LEN, d), lambda bi, j: (bi, 0, 0)),
        pl.BlockSpec((nb, 1, 2, d), lambda bi, j: (bi, jnp.minimum(j, 1), 0, 0)),
        full(norm1_g), full(w_in), full(conv_w), full(w0), full(w2bd), full(a0), full(a2bd),
        full(g2), full(k_k), full(k_a), full(r_k), full(cs), full(ones_blk), full(tri),
    ]
    lat = lambda bi, j: (bi, jnp.maximum(j - 1, 0), 0)
    out_specs = [
        pl.BlockSpec((nb, 2, TILE, RWKV_W), lambda bi, j: (bi, 0, j, 0)),
        pl.BlockSpec((nb, 2, TILE, RWKV_W), lambda bi, j: (bi, 0, j, 0)),
        pl.BlockSpec((nb, 2, TILE, RWKV_W), lambda bi, j: (bi, 0, j, 0)),
        pl.BlockSpec((nb, 2, TILE, RWKV_W), lambda bi, j: (bi, 0, j, 0)),
        pl.BlockSpec((nb, TILE, RWKV_W), lambda bi, j: (bi, j, 0)),
        pl.BlockSpec((nb, 2, CHUNKS_PER_TILE, 1, RWKV_W), lambda bi, j: (bi, 0, j, 0, 0)),
        pl.BlockSpec((nb, TILE, RWKV_W), lat),
        pl.BlockSpec((nb, TILE, RWKV_W), lat),
        pl.BlockSpec((nb, TILE, 2 * FOURIER_W), lat),
    ]
    out_shape = [
        jax.ShapeDtypeStruct((b, 2, t_all, RWKV_W), BF16),
        jax.ShapeDtypeStruct((b, 2, t_all, RWKV_W), BF16),
        jax.ShapeDtypeStruct((b, 2, t_all, RWKV_W), BF16),
        jax.ShapeDtypeStruct((b, 2, t_all, RWKV_W), BF16),
        jax.ShapeDtypeStruct((b, t_all, RWKV_W), BF16),
        jax.ShapeDtypeStruct((b, 2, n_chunks, 1, RWKV_W), F32),
        jax.ShapeDtypeStruct((b, s, RWKV_W), F32),
        jax.ShapeDtypeStruct((b, s, RWKV_W), F32),
        jax.ShapeDtypeStruct((b, s, 2 * FOURIER_W), BF16),
    ]
    return pl.pallas_call(
        _prep_kernel,
        grid=(b // nb, n_j),
        in_specs=in_specs,
        out_specs=out_specs,
        out_shape=out_shape,
        compiler_params=pltpu.CompilerParams(dimension_semantics=("arbitrary", "arbitrary"),
                                             vmem_limit_bytes=VMEM_LIMIT),
        name="prep",
    )(x, x, x, ctx, modsel, norm1_g, w_in, conv_w, w0, w2bd, a0, a2bd, g2, k_k, k_a, r_k,
      cs, ones_blk, tri)


def _quad_chunk(a, bm, km, r, v, plrow, g_states, strict, incl, eye, lane_lo, mask_bd, eye_row,
                level_masks):
    n = PAIR

    def dup(xv):
        z = jnp.zeros_like(xv)
        return jnp.concatenate([jnp.where(lane_lo, xv, z), jnp.where(lane_lo, z, xv)], axis=0)

    def to_row(x2):
        return x2[:CHUNK] + x2[CHUNK:]

    def bdiag(xr):
        return jnp.where(mask_bd, jnp.concatenate([xr] * QUAD_HEADS, axis=0), jnp.zeros((), xr.dtype))

    pairs = []
    for p in range(2):
        sl = slice(p * n, (p + 1) * n)
        a2, b2, k2, r2, v2 = (dup(t[:, sl]) for t in (a, bm, km, r, v))
        rhs = jnp.concatenate([b2, k2], axis=0)
        s = _bdot_nt(jnp.concatenate([a2, r2], axis=0), rhs)
        pairs.append((sl, r2, v2, rhs, s))
        yield

    a_ab = jnp.concatenate([to_row(jnp.where(strict, s[:n, :n], 0.0)) for *_, s in pairs], axis=1)
    a_ak = jnp.concatenate([to_row(jnp.where(strict, s[:n, n:], 0.0)) for *_, s in pairs], axis=1)
    m_rbk = [jnp.concatenate([jnp.where(incl, s[n:, :n], 0.0), jnp.where(incl, s[n:, n:], 0.0)],
                             axis=1).astype(BF16) for *_, s in pairs]

    av = _bdot(a_ak, bdiag(v))
    a_lvl = [jnp.where(m, a_ab, 0.0).astype(BF16) for m in level_masks]
    t_row = jnp.where(eye_row, 1.0, 0.0) - jnp.where(level_masks[0], a_ab, 0.0)
    lvl = 1
    while lvl < len(a_lvl):
        paired = lvl + 1 < len(a_lvl)
        lhs = jnp.concatenate([a_lvl[lvl], a_lvl[lvl + 1]], axis=0) if paired else a_lvl[lvl]
        yc = jnp.dot(lhs, bdiag(t_row.astype(BF16)), preferred_element_type=F32)
        yield
        y_bd = bdiag(yc[:CHUNK].astype(BF16))
        if paired:
            both = jnp.dot(jnp.concatenate([t_row, yc[CHUNK:]], axis=0).astype(BF16), y_bd,
                           preferred_element_type=F32)
            yield
            t_row = t_row - both[:CHUNK]
            y_next = yc[CHUNK:] - both[CHUNK:]
            t_row = t_row - _bdot(t_row, bdiag(y_next.astype(BF16)))
            lvl += 2
        else:
            t_row = t_row - _bdot(t_row, y_bd)
            lvl += 1
        yield
    tx = _bdot(t_row, jnp.concatenate([bdiag(a), bdiag(av.astype(BF16))], axis=1))
    yield
    a_w = tx[:, :2 * n].astype(BF16)
    u_v = (-tx[:, 2 * n:]).astype(BF16)
    stage5 = []
    for (sl, r2, v2, rhs, _), m2 in zip(pairs, m_rbk):
        zmat = jnp.concatenate(
            [jnp.concatenate([dup(a_w[:, sl]), dup(u_v[:, sl])], axis=1),
             jnp.concatenate([jnp.zeros_like(v2), v2], axis=1)], axis=0)
        rhs_end = rhs.astype(F32) * plrow[:, sl]
        lhs = jnp.concatenate([rhs_end.T.astype(BF16), m2], axis=0)
        stage5.append(jnp.dot(lhs, zmat, preferred_element_type=F32))
    yield
    ys, g_new = [], []
    for (sl, r2, v2, rhs, _), o5, g_state in zip(pairs, stage5, g_states):
        phi = jnp.where(eye, plrow[:, sl], 0.0) - o5[:n, :n]
        r_y = r2.astype(F32) - o5[n:, :n]
        o6 = _bdot(jnp.concatenate([r_y, phi], axis=0), g_state)
        ys.append(to_row(o6[:n] + o5[n:, n:]))
        g_new.append(o6[n:] + o5[:n, n:])
    return jnp.concatenate(ys, axis=1), g_new


def _run_interleaved(gens, groups=1, lag=0):
    results = [None] * len(gens)
    active = list(range(len(gens)))
    rnd = 0
    while active:
        for i in list(active):
            if rnd < (i % groups) * lag:
                continue
            try:
                next(gens[i])
            except StopIteration as stop:
                results[i] = stop.value
                active.remove(i)
        rnd += 1
    return results


def _scan_kernel(atf, btf, ktf, rtf, vf, plf, atb, btb, ktb, rtb, vb, plb,
                 yf_ref, yb_ref, g_ref):
    s = pl.program_id(1)

    @pl.when(s == 0)
    def _():
        g_ref[...] = jnp.zeros_like(g_ref)

    ri = lax.broadcasted_iota(jnp.int32, (PAIR, PAIR), 0)
    ci = lax.broadcasted_iota(jnp.int32, (PAIR, PAIR), 1)
    same = (ri >= HEAD) == (ci >= HEAD)
    ti = ri & (HEAD - 1)
    tj = ci & (HEAD - 1)
    eye = ri == ci
    lane_lo = lax.broadcasted_iota(jnp.int32, (CHUNK, PAIR), 1) < HEAD
    masks = [(same & (tj < ti), same & (tj <= ti)),
             (same & (tj > ti), same & (tj >= ti))]
    quad_w = QUAD_HEADS * HEAD
    rq = lax.broadcasted_iota(jnp.int32, (quad_w, quad_w), 0)
    cq = lax.broadcasted_iota(jnp.int32, (quad_w, quad_w), 1)
    mask_bd = (rq // HEAD) == (cq // HEAD)
    t_i = lax.broadcasted_iota(jnp.int32, (CHUNK, quad_w), 0)
    s_i = lax.broadcasted_iota(jnp.int32, (CHUNK, quad_w), 1) & (HEAD - 1)
    eye_row = s_i == t_i
    level_masks = []
    for d in range(2):
        late, early = (t_i, s_i) if d == 0 else (s_i, t_i)
        level_masks.append([
            ((t_i >> (lvl + 1)) == (s_i >> (lvl + 1)))
            & (((late >> lvl) & 1) == 1) & (((early >> lvl) & 1) == 0)
            for lvl in range(int(math.log2(CHUNK)))])
    refs = [(atf, btf, ktf, rtf, vf, plf, yf_ref), (atb, btb, ktb, rtb, vb, plb, yb_ref)]
    gens, dests = [], []
    for bi in range(yf_ref.shape[0]):
        for d in range(2):
            at, bt, kt, rt, vv, plr, y_ref = refs[d]
            strict, incl = masks[d]
            for q in range(RWKV_W // quad_w):
                sl = slice(q * quad_w, (q + 1) * quad_w)
                gens.append(_quad_chunk(at[bi, 0, :, sl], bt[bi, 0, :, sl], kt[bi, 0, :, sl],
                                        rt[bi, 0, :, sl], vv[bi, :, sl], plr[bi, 0, 0, :, sl],
                                        [g_ref[bi, d, 2 * q], g_ref[bi, d, 2 * q + 1]],
                                        strict, incl, eye, lane_lo, mask_bd, eye_row,
                                        level_masks[d]))
                dests.append((y_ref, bi, d, q, sl))
    done = _run_interleaved(gens, groups=SCAN_GROUPS, lag=SCAN_LAG)
    for (y, g_new), (y_ref, bi, d, q, sl) in zip(done, dests):
        g_ref[bi, d, 2 * q] = g_new[0]
        g_ref[bi, d, 2 * q + 1] = g_new[1]
        y_ref[bi, :, sl] = y


def _scan_call(at, bt, kt, rt, v, plast, s_lat):
    b = at.shape[0]
    n_chunks = at.shape[2] // CHUNK
    n_ctx = CTX_LEN // CHUNK
    n_lat = s_lat // CHUNK

    def fwd_c(s):
        return s

    def bwd_c(s):
        return jnp.where(s < n_ctx, n_ctx - 1 - s, n_chunks + n_ctx - 1 - s)

    nb = SCAN_BATCH

    def dir_spec(d, cfun):
        return pl.BlockSpec((nb, 1, CHUNK, RWKV_W), lambda bi, s: (bi, d, cfun(s), 0))

    def v_spec(cfun):
        return pl.BlockSpec((nb, CHUNK, RWKV_W), lambda bi, s: (bi, cfun(s), 0))

    def pl_spec(d, cfun):
        return pl.BlockSpec((nb, 1, 1, 1, RWKV_W), lambda bi, s: (bi, d, cfun(s), 0, 0))

    in_specs = ([dir_spec(0, fwd_c)] * 4 + [v_spec(fwd_c), pl_spec(0, fwd_c)]
                + [dir_spec(1, bwd_c)] * 4 + [v_spec(bwd_c), pl_spec(1, bwd_c)])
    out_specs = [
        pl.BlockSpec((nb, CHUNK, RWKV_W), lambda bi, s: (bi, jnp.maximum(s - n_ctx, 0), 0)),
        pl.BlockSpec((nb, CHUNK, RWKV_W),
                     lambda bi, s: (bi, jnp.minimum(n_chunks - 1 - s, n_lat - 1), 0)),
    ]
    out_shape = [jax.ShapeDtypeStruct((b, s_lat, RWKV_W), F32)] * 2
    return pl.pallas_call(
        _scan_kernel,
        grid=(b // nb, n_chunks),
        in_specs=in_specs,
        out_specs=out_specs,
        out_shape=out_shape,
        scratch_shapes=[pltpu.VMEM((nb, 2, N_PAIRS, PAIR, PAIR), F32)],
        compiler_params=pltpu.CompilerParams(dimension_semantics=("arbitrary", "arbitrary"),
                                             vmem_limit_bytes=VMEM_LIMIT),
        name="scan",
    )(at, bt, kt, rt, v, plast, at, bt, kt, rt, v, plast)


def _mix_kernel(dft_ref, rev_ref, z_ref, yf_ref, yb_ref, bonus_ref, gate_ref, x_ref, mod_ref,
                gng_ref, gnb_ref, fg_ref, wout_ref, n2g_ref, ones_ref, x1_ref, h2_ref, zf_ref,
                *, fscale):
    seq = z_ref.shape[1]
    half = seq // 2
    rt = rev_ref.shape[0]

    @pl.when(pl.program_id(1) == 0)
    def _():
        mid = z_ref[0, half:half + 1, :FOURIER_W].astype(F32)
        row0 = lax.broadcasted_iota(jnp.int32, (rt, FOURIER_W), 0) == 0
        for part, sign in ((0, 1.0), (1, -1.0)):
            lanes = slice(part * FOURIER_W, (part + 1) * FOURIER_W)
            for r in range(half // rt):
                lo = seq - rt * (r + 1)
                nxt = (z_ref[0, lo + rt:lo + 2 * rt, lanes] if r > 0
                       else jnp.zeros((rt, FOURIER_W), BF16))
                rev = jnp.dot(rev_ref[...], jnp.concatenate([z_ref[0, lo:lo + rt, lanes], nxt], axis=0),
                              preferred_element_type=F32)
                folded = z_ref[0, r * rt:(r + 1) * rt, lanes].astype(F32) + sign * rev
                if part == 1 and r == 0:
                    folded = jnp.where(row0, mid, folded)
                zf_ref[part * half + r * rt:part * half + (r + 1) * rt, :] = folded.astype(BF16)

    yfour = jnp.dot(dft_ref[...], zf_ref[...], preferred_element_type=F32)
    four = _rms(yfour * fscale, fg_ref[...])

    ones_blk = ones_ref[...]
    y = yf_ref[0] + yb_ref[0]
    mu = _dot_exact_rhs(y, ones_blk) * (1.0 / HEAD)
    dy = y - mu
    var = _dot_exact_rhs(dy * dy, ones_blk) * (1.0 / HEAD)
    yn = dy * lax.rsqrt(var + GN_EPS) * gng_ref[...] + gnb_ref[...]
    rw = (yn + bonus_ref[0]) * gate_ref[0]

    out = _bdot(four, wout_ref[:FOURIER_W, :]) + _bdot(rw, wout_ref[FOURIER_W:, :])
    x1 = x_ref[0] + mod_ref[0, 2:3, :] * out
    x1_ref[0] = x1
    h2 = _rms(x1, n2g_ref[...]) * (1.0 + mod_ref[0, 4:5, :]) + mod_ref[0, 3:4, :]
    h2_ref[0] = h2.astype(BF16)


def _mix_call(dft, rev, z, yf, yb, bonus, gate, x, modx, gn_g, gn_b, f_g, w_out, norm2_g,
              ones_blk):
    b, s, d = x.shape
    tm = MIX_TILE

    def full(arr):
        nd = arr.ndim
        return pl.BlockSpec(arr.shape, lambda bi, m, _n=nd: (0,) * _n)

    tok = lambda w: pl.BlockSpec((1, tm, w), lambda bi, m: (bi, m, 0))
    in_specs = [
        pl.BlockSpec((tm, s), lambda bi, m: (m, 0)),
        full(rev),
        pl.BlockSpec((1, s, 2 * FOURIER_W), lambda bi, m: (bi, 0, 0)),
        tok(RWKV_W), tok(RWKV_W), tok(RWKV_W), tok(RWKV_W), tok(d),
        pl.BlockSpec((1, 6, d), lambda bi, m: (bi, 0, 0)),
        full(gn_g), full(gn_b), full(f_g), full(w_out), full(norm2_g), full(ones_blk),
    ]
    fscale = 1.0 / math.sqrt(s * (FOURIER_W // FOURIER_GROUPS))
    return pl.pallas_call(
        functools.partial(_mix_kernel, fscale=fscale),
        grid=(b, s // tm),
        in_specs=in_specs,
        out_specs=[tok(d), tok(d)],
        out_shape=[jax.ShapeDtypeStruct((b, s, d), F32), jax.ShapeDtypeStruct((b, s, d), BF16)],
        scratch_shapes=[pltpu.VMEM((s, FOURIER_W), BF16)],
        compiler_params=pltpu.CompilerParams(dimension_semantics=("arbitrary", "arbitrary"),
                                             vmem_limit_bytes=VMEM_LIMIT),
        name="mix",
    )(dft, rev, z, yf, yb, bonus, gate, x, modx, gn_g, gn_b, f_g, w_out, norm2_g, ones_blk)


def _ffn_up_kernel(h2_ref, wg_ref, wv_ref, cwg_ref, cwv_ref, cbg_ref, cbv_ref, act_ref):
    t = h2_ref.shape[1]
    rows = FFN_ROWS
    n_blk = t // rows
    w = jnp.concatenate([wg_ref[...], wv_ref[...]], axis=1)
    cw = jnp.concatenate([cwg_ref[...], cwv_ref[...]], axis=1).astype(BF16)
    cb = jnp.concatenate([cbg_ref[...], cbv_ref[...]], axis=1).astype(BF16)
    col = lax.broadcasted_iota(jnp.int32, (rows, 2 * FF_TILE), 0) & (GRID_W - 1)
    has_left = col > 0
    has_right = col < GRID_W - 1
    zeros = jnp.zeros((GRID_W, 2 * FF_TILE), BF16)

    def row_mixes(i):
        u = jnp.dot(h2_ref[0, i * rows:(i + 1) * rows, :], w, preferred_element_type=F32)
        u_l = jnp.where(has_left, pltpu.roll(u, 1, 0), 0.0).astype(BF16)
        u_r = jnp.where(has_right, pltpu.roll(u, rows - 1, 0), 0.0).astype(BF16)
        u_c = u.astype(BF16)
        return [cw[3 * kh:3 * kh + 1] * u_l + cw[3 * kh + 1:3 * kh + 2] * u_c
                + cw[3 * kh + 2:3 * kh + 3] * u_r for kh in range(3)]

    def finish(i, prev, cur, nxt):
        above = zeros if prev is None else prev[0][rows - GRID_W:]
        below = zeros if nxt is None else nxt[2][:GRID_W]
        c = (cur[1] + jnp.concatenate([above, cur[0][:rows - GRID_W]], axis=0)
             + jnp.concatenate([cur[2][GRID_W:], below], axis=0) + cb)
        cg = c[:, :FF_TILE]
        act_ref[0, i * rows:(i + 1) * rows, :] = cg * _sigmoid(cg) * c[:, FF_TILE:]

    mixes = [None] * (n_blk + 1)
    for i in range(n_blk):
        mixes[i] = row_mixes(i)
        if i >= 1:
            finish(i - 1, mixes[i - 2] if i >= 2 else None, mixes[i - 1], mixes[i])
    finish(n_blk - 1, mixes[n_blk - 2] if n_blk >= 2 else None, mixes[n_blk - 1], None)


def _ffn_up_call(h2, w_up, conv_w9, conv_b):
    b, s, d = h2.shape
    n_f = D_FF // FF_TILE
    in_specs = [
        pl.BlockSpec((1, s, d), lambda bi, f: (bi, 0, 0)),
        pl.BlockSpec((d, FF_TILE), lambda bi, f: (0, f)),
        pl.BlockSpec((d, FF_TILE), lambda bi, f: (0, n_f + f)),
        pl.BlockSpec((9, FF_TILE), lambda bi, f: (0, f)),
        pl.BlockSpec((9, FF_TILE), lambda bi, f: (0, n_f + f)),
        pl.BlockSpec((1, FF_TILE), lambda bi, f: (0, f)),
        pl.BlockSpec((1, FF_TILE), lambda bi, f: (0, n_f + f)),
    ]
    return pl.pallas_call(
        _ffn_up_kernel,
        grid=(b, n_f),
        in_specs=in_specs,
        out_specs=pl.BlockSpec((1, s, FF_TILE), lambda bi, f: (bi, 0, f)),
        out_shape=jax.ShapeDtypeStruct((b, s, D_FF), BF16),
        compiler_params=pltpu.CompilerParams(dimension_semantics=("arbitrary", "arbitrary"),
                                             vmem_limit_bytes=VMEM_LIMIT),
        name="ffn_up",
    )(h2, w_up, w_up, conv_w9, conv_w9, conv_b, conv_b)


def _ffn_down_kernel(act_ref, wd_ref, x1_ref, mod_ref, fg_ref, out_ref):
    y = jnp.dot(act_ref[0], wd_ref[...], preferred_element_type=F32)
    out_ref[0] = _rms(x1_ref[0] + mod_ref[0, 5:6, :] * y, fg_ref[...])


def _ffn_down_call(act, w_down, x1, modx, final_g):
    b, s, d = x1.shape
    tm = DOWN_TILE
    in_specs = [
        pl.BlockSpec((1, tm, D_FF), lambda bi, m: (bi, m, 0)),
        pl.BlockSpec((D_FF, d), lambda bi, m: (0, 0)),
        pl.BlockSpec((1, tm, d), lambda bi, m: (bi, m, 0)),
        pl.BlockSpec((1, 6, d), lambda bi, m: (bi, 0, 0)),
        pl.BlockSpec((1, d), lambda bi, m: (0, 0)),
    ]
    return pl.pallas_call(
        _ffn_down_kernel,
        grid=(b, s // tm),
        in_specs=in_specs,
        out_specs=pl.BlockSpec((1, tm, d), lambda bi, m: (bi, m, 0)),
        out_shape=jax.ShapeDtypeStruct((b, s, d), F32),
        compiler_params=pltpu.CompilerParams(dimension_semantics=("arbitrary", "arbitrary"),
                                             vmem_limit_bytes=VMEM_LIMIT),
        name="ffn_down",
    )(act, w_down, x1, modx, final_g)


@functools.lru_cache(maxsize=None)
def _constants(seq):
    gw = FOURIER_W // FOURIER_GROUPS
    nm = np.outer(np.arange(gw), np.arange(gw)) % gw
    ang = 2.0 * np.pi * nm / gw
    cs = np.zeros((FOURIER_W, 2 * FOURIER_W), np.float64)
    for gi in range(FOURIER_GROUPS):
        sl = slice(gi * gw, (gi + 1) * gw)
        cs[sl, sl] = np.cos(ang)
        cs[sl, FOURIER_W + gi * gw:FOURIER_W + (gi + 1) * gw] = np.sin(ang)
    half = seq // 2
    k_idx = np.arange(seq, dtype=np.int64)[:, None]
    ang_c = 2.0 * np.pi * ((k_idx * np.arange(0, half + 1, dtype=np.int64)[None, :]) % seq) / seq
    ang_s = 2.0 * np.pi * ((k_idx * np.arange(1, half, dtype=np.int64)[None, :]) % seq) / seq
    dft = np.concatenate([np.cos(ang_c), -np.sin(ang_s)], axis=1)
    ri = np.arange(REV_TILE)
    rev = (np.arange(2 * REV_TILE)[None, :] == (REV_TILE - ri)[:, None]).astype(np.float64)
    head_id = np.arange(RWKV_W) // HEAD
    ones_blk = (head_id[:, None] == head_id[None, :]).astype(np.float64)
    ti = np.arange(TILE)
    same_chunk = (ti[:, None] // CHUNK) == (ti[None, :] // CHUNK)
    tri = np.stack([same_chunk & (ti[None, :] <= ti[:, None]),
                    same_chunk & (ti[None, :] >= ti[:, None])]).astype(np.float64)
    to_bf16 = lambda arr: np.asarray(arr, dtype=np.float32).astype(BF16)
    return (np.asarray(cs, dtype=np.float32), np.asarray(dft, dtype=np.float32),
            to_bf16(ones_blk), to_bf16(tri), to_bf16(rev))


def _block_diag_dirs(w):
    z = jnp.zeros_like(w[0])
    return jnp.concatenate([jnp.concatenate([w[0], z], axis=1),
                            jnp.concatenate([z, w[1]], axis=1)], axis=0)


def kernel(x, c, ctx, c_ctx, ada_w, ada_b, norm1_g, norm2_g, w_in, rwkv_conv_w, decay_w0, decay_w2, iclr_a0, iclr_a2, gate_g2, k_k, k_a, r_k, gn_g, gn_b, fourier_g, w_out, ffn_w_up, ffn_conv_w, ffn_conv_b, ffn_w_down, final_g):
    b, s, d = x.shape
    assert ada_w.shape[0] == 1, "single-layer configuration"
    assert (b, s, d) == (c.shape[0], s, D_MODEL) and ctx.shape == (b, CTX_LEN, d)
    assert s % TILE == 0 and CTX_LEN == TILE
    cs, dft, ones_blk, tri, rev = (jnp.asarray(t) for t in _constants(s))
    cs, dft = cs.astype(BF16), dft.astype(BF16)

    cc = jnp.concatenate([c, c_ctx[None, :], jnp.zeros((16 - b - 1, d), F32)], axis=0)
    mod = _mod_call(cc, ada_w[0], ada_b[0][None, :])
    modx = mod[:b].reshape(b, 6, d)
    modc = jnp.broadcast_to(mod[b].reshape(1, 6, d)[:, :2], (b, 2, d))
    modsel = jnp.stack([modc, modx[:, :2]], axis=1)

    row = lambda t: t.reshape(1, -1)
    at, bt, kt, rt, v, plast, bonus, gate, z = _prep_call(
        x, ctx, modsel, row(norm1_g[0]), w_in[0].astype(BF16), rwkv_conv_w[0],
        row(decay_w0[0]), _block_diag_dirs(decay_w2[0]).astype(BF16),
        row(iclr_a0[0]), _block_diag_dirs(iclr_a2[0]).astype(BF16),
        gate_g2[0].astype(BF16), row(k_k[0]), row(k_a[0]), row(r_k[0]), cs, ones_blk, tri)

    yf, yb = _scan_call(at, bt, kt, rt, v, plast, s)

    x1, h2 = _mix_call(dft, rev, z, yf, yb, bonus, gate, x, modx, row(gn_g[0]), row(gn_b[0]),
                       row(fourier_g[0]), w_out[0].astype(BF16), row(norm2_g[0]), ones_blk)

    act = _ffn_up_call(h2, ffn_w_up[0].astype(BF16), ffn_conv_w[0].reshape(9, 2 * D_FF),
                       row(ffn_conv_b[0]))
    return _ffn_down_call(act, ffn_w_down[0].astype(BF16), x1, modx, row(final_g))
```

```python
import functools
import math

import numpy as np
import jax
import jax.numpy as jnp
from jax import lax
from jax.experimental import pallas as pl
from jax.experimental.pallas import tpu as pltpu

F32 = jnp.float32
BF16 = jnp.bfloat16

D_MODEL = 1024
CTX_LEN = 256
GRID_W = 64
FOURIER_W = 512
FOURIER_GROUPS = 8
RWKV_W = 512
HEAD = 64
N_HEADS = 8
DECAY_RANK = 64
ICLR_RANK = 64
GATE_RANK = 128
D_FF = 2816
NORM_EPS = 1e-6
GN_EPS = 64e-5
KK_EPS = 1e-12

CHUNK = 64
TILE = 256
MIX_TILE = 512
REV_TILE = 256
CHUNKS_PER_TILE = TILE // CHUNK
PREP_BATCH = 2
PAIR = 2 * HEAD
N_PAIRS = RWKV_W // PAIR
QUAD_HEADS = 4
SCAN_BATCH = 8
SCAN_GROUPS = 4
SCAN_LAG = 3
MOD_TILE = 1536
FF_TILE = 1408
FFN_ROWS = 256
DOWN_TILE = 1024
VMEM_LIMIT = 56 * 1024 * 1024
HEAD_SUM_TERMS = 1
CUMSUM_TERMS = 2
MOD_WEIGHT_TERMS = 2

RKV_LO = FOURIER_W
RKV_HI = FOURIER_W + 3 * RWKV_W
WD_LO = RKV_HI
AD_LO = WD_LO + 2 * DECAY_RANK
GD_LO = AD_LO + 2 * ICLR_RANK
PROJ_W = GD_LO + GATE_RANK


def _bdot(a, b):
    return jnp.dot(a.astype(BF16), b.astype(BF16), preferred_element_type=F32)


def _bdot_nt(a, b):
    return lax.dot_general(a.astype(BF16), b.astype(BF16), (((1,), (1,)), ((), ())),
                           preferred_element_type=F32)


def _bdot_tn(a, b):
    return lax.dot_general(a.astype(BF16), b.astype(BF16), (((0,), (0,)), ((), ())),
                           preferred_element_type=F32)


def _split_terms(a, terms):
    out, rem = [], a
    for _ in range(terms):
        hi = rem.astype(BF16)
        out.append(hi)
        rem = rem - hi.astype(F32)
    return out


def _dot_exact_rhs(a, b_exact, terms=HEAD_SUM_TERMS):
    acc = None
    for piece in _split_terms(a, terms):
        t = jnp.dot(piece, b_exact, preferred_element_type=F32)
        acc = t if acc is None else acc + t
    return acc


def _dot_exact_lhs(a_exact, b, terms=CUMSUM_TERMS):
    acc = None
    for piece in _split_terms(b, terms):
        t = jnp.dot(a_exact, piece, preferred_element_type=F32)
        acc = t if acc is None else acc + t
    return acc


def _sigmoid(x):
    return 1.0 / (1.0 + jnp.exp(-x))


def _rms(x, g):
    return x * lax.rsqrt(jnp.mean(x * x, axis=-1, keepdims=True) + NORM_EPS) * g


def _mod_kernel(c_ref, w_ref, b_ref, o_ref):
    cc = c_ref[...]
    s = cc * _sigmoid(cc)
    rows = s.shape[0]
    s_parts = jnp.concatenate(_split_terms(s, 3), axis=0)
    acc = b_ref[...]
    for wp in _split_terms(w_ref[...], MOD_WEIGHT_TERMS):
        t = jnp.dot(s_parts, wp, preferred_element_type=F32)
        acc = acc + (t[:rows] + t[rows:2 * rows] + t[2 * rows:])
    o_ref[...] = acc


def _mod_call(cc, ada_w, ada_b):
    rows, d = cc.shape
    n = ada_w.shape[1]
    tn = MOD_TILE
    return pl.pallas_call(
        _mod_kernel,
        grid=(n // tn,),
        in_specs=[pl.BlockSpec((rows, d), lambda i: (0, 0)),
                  pl.BlockSpec((d, tn), lambda i: (0, i)),
                  pl.BlockSpec((1, tn), lambda i: (0, i))],
        out_specs=pl.BlockSpec((rows, tn), lambda i: (0, i)),
        out_shape=jax.ShapeDtypeStruct((rows, n), F32),
        compiler_params=pltpu.CompilerParams(dimension_semantics=("arbitrary",),
                                             vmem_limit_bytes=VMEM_LIMIT),
        name="mod",
    )(cc, ada_w, ada_b)


def _prep_kernel(x_ref, xp_ref, xn_ref, ctx_ref, ms_ref, g1_ref, win_ref, cw_ref,
                 w0_ref, w2_ref, a0_ref, a2_ref, g2_ref, kk_ref, ka_ref, rk_ref,
                 cs_ref, ones_ref, tri_ref,
                 at_ref, bt_ref, kt_ref, rt_ref, v_ref, pl_ref, bonus_ref, gate_ref, z_ref):
    j = pl.program_id(1)
    refs = (x_ref, xp_ref, xn_ref, ctx_ref, ms_ref, g1_ref, win_ref, cw_ref,
            w0_ref, w2_ref, a0_ref, a2_ref, kk_ref, ka_ref, ones_ref, tri_ref,
            at_ref, bt_ref, kt_ref, rt_ref, v_ref, pl_ref)
    tiles = _run_interleaved([_prep_tile(bi, *refs) for bi in range(x_ref.shape[0])],
                             groups=PREP_BATCH, lag=1)

    @pl.when(j >= 1)
    def _():
        ones_blk = ones_ref[...]
        for bi, (u_four, r, v, gd, kd_sum) in enumerate(tiles):
            bs = _dot_exact_rhs(r * rk_ref[...] * kd_sum, ones_blk)
            bonus_ref[bi] = bs * v
            gate_ref[bi] = _bdot(_sigmoid(gd), g2_ref[...])
            z_ref[bi] = _bdot(u_four, cs_ref[...]).astype(BF16)


def _prep_tile(bi, x_ref, xp_ref, xn_ref, ctx_ref, ms_ref, g1_ref, win_ref, cw_ref,
               w0_ref, w2_ref, a0_ref, a2_ref, kk_ref, ka_ref, ones_ref, tri_ref,
               at_ref, bt_ref, kt_ref, rt_ref, v_ref, pl_ref):
    j = pl.program_id(1)
    n_j = pl.num_programs(1)
    is_ctx = j == 0
    shift = ms_ref[bi, 0, 0:1, :]
    scale = ms_ref[bi, 0, 1:2, :]
    g = g1_ref[...]

    def norm_mod(xr):
        return _rms(xr, g) * (1.0 + scale) + shift

    xin = jnp.where(is_ctx, ctx_ref[bi], x_ref[bi])
    h = norm_mod(xin).astype(BF16)
    u = jnp.dot(h, win_ref[...], preferred_element_type=F32)

    halo = jnp.concatenate([xp_ref[bi], xn_ref[bi]], axis=0)
    uh = jnp.dot(norm_mod(halo).astype(BF16), win_ref[:, RKV_LO:RKV_HI],
                 preferred_element_type=F32)
    yield
    prev_row = jnp.where(j >= 2, uh[7:8], 0.0)
    next_row = jnp.where(jnp.logical_and(j >= 1, j < n_j - 1), uh[8:9], 0.0)

    rkv = u[:, RKV_LO:RKV_HI]
    row = lax.broadcasted_iota(jnp.int32, rkv.shape, 0)
    u_m1 = jnp.where(row == 0, prev_row, pltpu.roll(rkv, 1, 0))
    u_p1 = jnp.where(row == TILE - 1, next_row, pltpu.roll(rkv, TILE - 1, 0))
    rkv = cw_ref[0:1, :] * u_m1 + cw_ref[1:2, :] * rkv + cw_ref[2:3, :] * u_p1
    r = rkv[:, 0:RWKV_W]
    k = rkv[:, RWKV_W:2 * RWKV_W]
    v = rkv[:, 2 * RWKV_W:3 * RWKV_W]

    wd = u[:, WD_LO:AD_LO]
    ad = u[:, AD_LO:GD_LO]
    gd = u[:, GD_LO:PROJ_W]
    w_lora = w0_ref[...] + _bdot(jnp.tanh(wd), w2_ref[...])
    lw = -math.exp(-0.5) * _sigmoid(w_lora)
    a = _sigmoid(a0_ref[...] + _bdot(ad, a2_ref[...]))

    ones_blk = ones_ref[...]
    kraw = k * kk_ref[...]
    kk = kraw * lax.rsqrt(_dot_exact_rhs(kraw * kraw, ones_blk) + KK_EPS)
    ka = ka_ref[...]
    kd = [k * (1.0 + (a[:, d * RWKV_W:(d + 1) * RWKV_W] - 1.0) * ka) for d in range(2)]
    yield

    v_ref[bi] = v.astype(BF16)
    for d in range(2):
        lwd = lw[:, d * RWKV_W:(d + 1) * RWKV_W]
        ad_ = a[:, d * RWKV_W:(d + 1) * RWKV_W]
        c = _dot_exact_lhs(tri_ref[d], lwd)
        e_pos = jnp.exp(c)
        e_neg = jnp.exp(-c)
        e_prev = jnp.exp(c - lwd)
        at_ref[bi, d] = (kk * e_prev).astype(BF16)
        bt_ref[bi, d] = (kk * ad_ * e_neg).astype(BF16)
        kt_ref[bi, d] = (kd[d] * e_neg).astype(BF16)
        rt_ref[bi, d] = (r * e_pos).astype(BF16)
        for q in range(CHUNKS_PER_TILE):
            last = q * CHUNK + (CHUNK - 1 if d == 0 else 0)
            pl_ref[bi, d, q] = e_pos[last:last + 1, :]
    return u[:, 0:FOURIER_W], r, v, gd, kd[0] + kd[1]


def _prep_call(x, ctx, modsel, norm1_g, w_in, conv_w, w0, w2bd, a0, a2bd, g2, k_k, k_a, r_k,
               cs, ones_blk, tri):
    b, s, d = x.shape
    n_j = 1 + s // TILE
    n_chunks = (CTX_LEN + s) // CHUNK
    t_all = CTX_LEN + s
    c2 = 2 * RWKV_W

    def full(arr):
        nd = arr.ndim
        return pl.BlockSpec(arr.shape, lambda bi, j, _n=nd: (0,) * _n)

    rows8 = TILE // 8
    nb = PREP_BATCH
    in_specs = [
        pl.BlockSpec((nb, TILE, d), lambda bi, j: (bi, jnp.maximum(j - 1, 0), 0)),
        pl.BlockSpec((nb, 8, d), lambda bi, j: (bi, jnp.maximum((j - 1) * rows8 - 1, 0), 0)),
        pl.BlockSpec((nb, 8, d), lambda bi, j: (bi, jnp.minimum(j * rows8, s // 8 - 1), 0)),
        pl.BlockSpec((nb, CTX_LEN, d), lambda bi, j: (bi, 0, 0)),
        pl.BlockSpec((nb, 1, 2, d), lambda bi, j: (bi, jnp.minimum(j, 1), 0, 0)),
        full(norm1_g), full(w_in), full(conv_w), full(w0), full(w2bd), full(a0), full(a2bd),
        full(g2), full(k_k), full(k_a), full(r_k), full(cs), full(ones_blk), full(tri),
    ]
    lat = lambda bi, j: (bi, jnp.maximum(j - 1, 0), 0)
    out_specs = [
        pl.BlockSpec((nb, 2, TILE, RWKV_W), lambda bi, j: (bi, 0, j, 0)),
        pl.BlockSpec((nb, 2, TILE, RWKV_W), lambda bi, j: (bi, 0, j, 0)),
        pl.BlockSpec((nb, 2, TILE, RWKV_W), lambda bi, j: (bi, 0, j, 0)),
        pl.BlockSpec((nb, 2, TILE, RWKV_W), lambda bi, j: (bi, 0, j, 0)),
        pl.BlockSpec((nb, TILE, RWKV_W), lambda bi, j: (bi, j, 0)),
        pl.BlockSpec((nb, 2, CHUNKS_PER_TILE, 1, RWKV_W), lambda bi, j: (bi, 0, j, 0, 0)),
        pl.BlockSpec((nb, TILE, RWKV_W), lat),
        pl.BlockSpec((nb, TILE, RWKV_W), lat),
        pl.BlockSpec((nb, TILE, 2 * FOURIER_W), lat),
    ]
    out_shape = [
        jax.ShapeDtypeStruct((b, 2, t_all, RWKV_W), BF16),
        jax.ShapeDtypeStruct((b, 2, t_all, RWKV_W), BF16),
        jax.ShapeDtypeStruct((b, 2, t_all, RWKV_W), BF16),
        jax.ShapeDtypeStruct((b, 2, t_all, RWKV_W), BF16),
        jax.ShapeDtypeStruct((b, t_all, RWKV_W), BF16),
        jax.ShapeDtypeStruct((b, 2, n_chunks, 1, RWKV_W), F32),
        jax.ShapeDtypeStruct((b, s, RWKV_W), F32),
        jax.ShapeDtypeStruct((b, s, RWKV_W), F32),
        jax.ShapeDtypeStruct((b, s, 2 * FOURIER_W), BF16),
    ]
    return pl.pallas_call(
        _prep_kernel,
        grid=(b // nb, n_j),
        in_specs=in_specs,
        out_specs=out_specs,
        out_shape=out_shape,
        compiler_params=pltpu.CompilerParams(
            dimension_semantics=("arbitrary", "arbitrary"), vmem_limit_bytes=VMEM_LIMIT,
            allow_input_fusion=[i == 6 for i in range(len(in_specs))]),
        name="prep",
    )(x, x, x, ctx, modsel, norm1_g, w_in, conv_w, w0, w2bd, a0, a2bd, g2, k_k, k_a, r_k,
      cs, ones_blk, tri)


def _quad_chunk(a, bm, km, r, v, plrow, g_states, strict, incl, eye, lane_lo, mask_bd, eye_row,
                level_masks):
    n = PAIR

    def dup(xv):
        z = jnp.zeros_like(xv)
        return jnp.concatenate([jnp.where(lane_lo, xv, z), jnp.where(lane_lo, z, xv)], axis=0)

    def to_row(x2):
        return x2[:CHUNK] + x2[CHUNK:]

    def bdiag(xr):
        return jnp.where(mask_bd, jnp.concatenate([xr] * QUAD_HEADS, axis=0), jnp.zeros((), xr.dtype))

    pairs = []
    for p in range(2):
        sl = slice(p * n, (p + 1) * n)
        a2, b2, k2, r2, v2 = (dup(t[:, sl]) for t in (a, bm, km, r, v))
        rhs = jnp.concatenate([b2, k2], axis=0)
        s = _bdot_nt(jnp.concatenate([a2, r2], axis=0), rhs)
        pairs.append((sl, r2, v2, rhs, s))
        yield

    a_ab = jnp.concatenate([to_row(jnp.where(strict, s[:n, :n], 0.0)) for *_, s in pairs], axis=1)
    a_ak = jnp.concatenate([to_row(jnp.where(strict, s[:n, n:], 0.0)) for *_, s in pairs], axis=1)
    m_rbk = [jnp.concatenate([jnp.where(incl, s[n:, :n], 0.0), jnp.where(incl, s[n:, n:], 0.0)],
                             axis=1).astype(BF16) for *_, s in pairs]

    av = _bdot(a_ak, bdiag(v))
    a_lvl = [jnp.where(m, a_ab, 0.0).astype(BF16) for m in level_masks]
    t_row = jnp.where(eye_row, 1.0, 0.0) - jnp.where(level_masks[0], a_ab, 0.0)
    lvl = 1
    while lvl < len(a_lvl):
        paired = lvl + 1 < len(a_lvl)
        lhs = jnp.concatenate([a_lvl[lvl], a_lvl[lvl + 1]], axis=0) if paired else a_lvl[lvl]
        yc = jnp.dot(lhs, bdiag(t_row.astype(BF16)), preferred_element_type=F32)
        yield
        y_bd = bdiag(yc[:CHUNK].astype(BF16))
        if paired:
            both = jnp.dot(jnp.concatenate([t_row, yc[CHUNK:]], axis=0).astype(BF16), y_bd,
                           preferred_element_type=F32)
            yield
            t_row = t_row - both[:CHUNK]
            y_next = yc[CHUNK:] - both[CHUNK:]
            t_row = t_row - _bdot(t_row, bdiag(y_next.astype(BF16)))
            lvl += 2
        else:
            t_row = t_row - _bdot(t_row, y_bd)
            lvl += 1
        yield
    tx = _bdot(t_row, jnp.concatenate([bdiag(a), bdiag(av.astype(BF16))], axis=1))
    yield
    a_w = tx[:, :2 * n].astype(BF16)
    u_v = (-tx[:, 2 * n:]).astype(BF16)
    stage5 = []
    for (sl, r2, v2, rhs, _), m2 in zip(pairs, m_rbk):
        zmat = jnp.concatenate(
            [jnp.concatenate([dup(a_w[:, sl]), dup(u_v[:, sl])], axis=1),
             jnp.concatenate([jnp.zeros_like(v2), v2], axis=1)], axis=0)
        rhs_end = rhs.astype(F32) * plrow[:, sl]
        lhs = jnp.concatenate([rhs_end.T.astype(BF16), m2], axis=0)
        stage5.append(jnp.dot(lhs, zmat, preferred_element_type=F32))
    yield
    ys, g_new = [], []
    for (sl, r2, v2, rhs, _), o5, g_state in zip(pairs, stage5, g_states):
        phi = jnp.where(eye, plrow[:, sl], 0.0) - o5[:n, :n]
        r_y = r2.astype(F32) - o5[n:, :n]
        o6 = _bdot(jnp.concatenate([r_y, phi], axis=0), g_state)
        ys.append(to_row(o6[:n] + o5[n:, n:]))
        g_new.append(o6[n:] + o5[:n, n:])
    return jnp.concatenate(ys, axis=1), g_new


def _run_interleaved(gens, groups=1, lag=0):
    results = [None] * len(gens)
    active = list(range(len(gens)))
    rnd = 0
    while active:
        for i in list(active):
            if rnd < (i % groups) * lag:
                continue
            try:
                next(gens[i])
            except StopIteration as stop:
                results[i] = stop.value
                active.remove(i)
        rnd += 1
    return results


def _scan_kernel(atf, btf, ktf, rtf, vf, plf, atb, btb, ktb, rtb, vb, plb,
                 yf_ref, yb_ref, g_ref):
    s = pl.program_id(1)

    @pl.when(s == 0)
    def _():
        g_ref[...] = jnp.zeros_like(g_ref)

    ri = lax.broadcasted_iota(jnp.int32, (PAIR, PAIR), 0)
    ci = lax.broadcasted_iota(jnp.int32, (PAIR, PAIR), 1)
    same = (ri >= HEAD) == (ci >= HEAD)
    ti = ri & (HEAD - 1)
    tj = ci & (HEAD - 1)
    eye = ri == ci
    lane_lo = lax.broadcasted_iota(jnp.int32, (CHUNK, PAIR), 1) < HEAD
    masks = [(same & (tj < ti), same & (tj <= ti)),
             (same & (tj > ti), same & (tj >= ti))]
    quad_w = QUAD_HEADS * HEAD
    rq = lax.broadcasted_iota(jnp.int32, (quad_w, quad_w), 0)
    cq = lax.broadcasted_iota(jnp.int32, (quad_w, quad_w), 1)
    mask_bd = (rq // HEAD) == (cq // HEAD)
    t_i = lax.broadcasted_iota(jnp.int32, (CHUNK, quad_w), 0)
    s_i = lax.broadcasted_iota(jnp.int32, (CHUNK, quad_w), 1) & (HEAD - 1)
    eye_row = s_i == t_i
    level_masks = []
    for d in range(2):
        late, early = (t_i, s_i) if d == 0 else (s_i, t_i)
        level_masks.append([
            ((t_i >> (lvl + 1)) == (s_i >> (lvl + 1)))
            & (((late >> lvl) & 1) == 1) & (((early >> lvl) & 1) == 0)
            for lvl in range(int(math.log2(CHUNK)))])
    refs = [(atf, btf, ktf, rtf, vf, plf, yf_ref), (atb, btb, ktb, rtb, vb, plb, yb_ref)]
    gens, dests = [], []
    for bi in range(yf_ref.shape[0]):
        for d in range(2):
            at, bt, kt, rt, vv, plr, y_ref = refs[d]
            strict, incl = masks[d]
            for q in range(RWKV_W // quad_w):
                sl = slice(q * quad_w, (q + 1) * quad_w)
                gens.append(_quad_chunk(at[bi, 0, :, sl], bt[bi, 0, :, sl], kt[bi, 0, :, sl],
                                        rt[bi, 0, :, sl], vv[bi, :, sl], plr[bi, 0, 0, :, sl],
                                        [g_ref[bi, d, 2 * q], g_ref[bi, d, 2 * q + 1]],
                                        strict, incl, eye, lane_lo, mask_bd, eye_row,
                                        level_masks[d]))
                dests.append((y_ref, bi, d, q, sl))
    done = _run_interleaved(gens, groups=SCAN_GROUPS, lag=SCAN_LAG)
    for (y, g_new), (y_ref, bi, d, q, sl) in zip(done, dests):
        g_ref[bi, d, 2 * q] = g_new[0]
        g_ref[bi, d, 2 * q + 1] = g_new[1]
        y_ref[bi, :, sl] = y


def _scan_call(at, bt, kt, rt, v, plast, s_lat):
    b = at.shape[0]
    n_chunks = at.shape[2] // CHUNK
    n_ctx = CTX_LEN // CHUNK
    n_lat = s_lat // CHUNK

    def fwd_c(s):
        return s

    def bwd_c(s):
        return jnp.where(s < n_ctx, n_ctx - 1 - s, n_chunks + n_ctx - 1 - s)

    nb = SCAN_BATCH

    def dir_spec(d, cfun):
        return pl.BlockSpec((nb, 1, CHUNK, RWKV_W), lambda bi, s: (bi, d, cfun(s), 0))

    def v_spec(cfun):
        return pl.BlockSpec((nb, CHUNK, RWKV_W), lambda bi, s: (bi, cfun(s), 0))

    def pl_spec(d, cfun):
        return pl.BlockSpec((nb, 1, 1, 1, RWKV_W), lambda bi, s: (bi, d, cfun(s), 0, 0))

    in_specs = ([dir_spec(0, fwd_c)] * 4 + [v_spec(fwd_c), pl_spec(0, fwd_c)]
                + [dir_spec(1, bwd_c)] * 4 + [v_spec(bwd_c), pl_spec(1, bwd_c)])
    out_specs = [
        pl.BlockSpec((nb, CHUNK, RWKV_W), lambda bi, s: (bi, jnp.maximum(s - n_ctx, 0), 0)),
        pl.BlockSpec((nb, CHUNK, RWKV_W),
                     lambda bi, s: (bi, jnp.minimum(n_chunks - 1 - s, n_lat - 1), 0)),
    ]
    out_shape = [jax.ShapeDtypeStruct((b, s_lat, RWKV_W), F32)] * 2
    return pl.pallas_call(
        _scan_kernel,
        grid=(b // nb, n_chunks),
        in_specs=in_specs,
        out_specs=out_specs,
        out_shape=out_shape,
        scratch_shapes=[pltpu.VMEM((nb, 2, N_PAIRS, PAIR, PAIR), F32)],
        compiler_params=pltpu.CompilerParams(dimension_semantics=("arbitrary", "arbitrary"),
                                             vmem_limit_bytes=VMEM_LIMIT),
        name="scan",
    )(at, bt, kt, rt, v, plast, at, bt, kt, rt, v, plast)


def _mix_kernel(dft_ref, rev_ref, z_ref, yf_ref, yb_ref, bonus_ref, gate_ref, x_ref, mod_ref,
                gng_ref, gnb_ref, fg_ref, wout_ref, n2g_ref, ones_ref, x1_ref, h2_ref, zf_ref,
                *, fscale):
    seq = z_ref.shape[1]
    half = seq // 2
    rt = rev_ref.shape[0]

    @pl.when(pl.program_id(1) == 0)
    def _():
        mid = z_ref[0, half:half + 1, :FOURIER_W].astype(F32)
        row0 = lax.broadcasted_iota(jnp.int32, (rt, FOURIER_W), 0) == 0
        for part, sign in ((0, 1.0), (1, -1.0)):
            lanes = slice(part * FOURIER_W, (part + 1) * FOURIER_W)
            for r in range(half // rt):
                lo = seq - rt * (r + 1)
                nxt = (z_ref[0, lo + rt:lo + 2 * rt, lanes] if r > 0
                       else jnp.zeros((rt, FOURIER_W), BF16))
                rev = jnp.dot(rev_ref[...], jnp.concatenate([z_ref[0, lo:lo + rt, lanes], nxt], axis=0),
                              preferred_element_type=F32)
                folded = z_ref[0, r * rt:(r + 1) * rt, lanes].astype(F32) + sign * rev
                if part == 1 and r == 0:
                    folded = jnp.where(row0, mid, folded)
                zf_ref[part * half + r * rt:part * half + (r + 1) * rt, :] = folded.astype(BF16)

    yfour = jnp.dot(dft_ref[...], zf_ref[...], preferred_element_type=F32)
    four = _rms(yfour * fscale, fg_ref[...])

    ones_blk = ones_ref[...]
    y = yf_ref[0] + yb_ref[0]
    mu = _dot_exact_rhs(y, ones_blk) * (1.0 / HEAD)
    dy = y - mu
    var = _dot_exact_rhs(dy * dy, ones_blk) * (1.0 / HEAD)
    yn = dy * lax.rsqrt(var + GN_EPS) * gng_ref[...] + gnb_ref[...]
    rw = (yn + bonus_ref[0]) * gate_ref[0]

    out = _bdot(four, wout_ref[:FOURIER_W, :]) + _bdot(rw, wout_ref[FOURIER_W:, :])
    x1 = x_ref[0] + mod_ref[0, 2:3, :] * out
    x1_ref[0] = x1
    h2 = _rms(x1, n2g_ref[...]) * (1.0 + mod_ref[0, 4:5, :]) + mod_ref[0, 3:4, :]
    h2_ref[0] = h2.astype(BF16)


def _mix_call(dft, rev, z, yf, yb, bonus, gate, x, modx, gn_g, gn_b, f_g, w_out, norm2_g,
              ones_blk):
    b, s, d = x.shape
    tm = MIX_TILE

    def full(arr):
        nd = arr.ndim
        return pl.BlockSpec(arr.shape, lambda bi, m, _n=nd: (0,) * _n)

    tok = lambda w: pl.BlockSpec((1, tm, w), lambda bi, m: (bi, m, 0))
    in_specs = [
        pl.BlockSpec((tm, s), lambda bi, m: (m, 0)),
        full(rev),
        pl.BlockSpec((1, s, 2 * FOURIER_W), lambda bi, m: (bi, 0, 0)),
        tok(RWKV_W), tok(RWKV_W), tok(RWKV_W), tok(RWKV_W), tok(d),
        pl.BlockSpec((1, 6, d), lambda bi, m: (bi, 0, 0)),
        full(gn_g), full(gn_b), full(f_g), full(w_out), full(norm2_g), full(ones_blk),
    ]
    fscale = 1.0 / math.sqrt(s * (FOURIER_W // FOURIER_GROUPS))
    return pl.pallas_call(
        functools.partial(_mix_kernel, fscale=fscale),
        grid=(b, s // tm),
        in_specs=in_specs,
        out_specs=[tok(d), tok(d)],
        out_shape=[jax.ShapeDtypeStruct((b, s, d), F32), jax.ShapeDtypeStruct((b, s, d), BF16)],
        scratch_shapes=[pltpu.VMEM((s, FOURIER_W), BF16)],
        compiler_params=pltpu.CompilerParams(dimension_semantics=("arbitrary", "arbitrary"),
                                             vmem_limit_bytes=VMEM_LIMIT),
        name="mix",
    )(dft, rev, z, yf, yb, bonus, gate, x, modx, gn_g, gn_b, f_g, w_out, norm2_g, ones_blk)


def _ffn_up_kernel(h2_ref, wg_ref, wv_ref, cwg_ref, cwv_ref, cbg_ref, cbv_ref, act_ref):
    t = h2_ref.shape[1]
    rows = FFN_ROWS
    n_blk = t // rows
    w = jnp.concatenate([wg_ref[...], wv_ref[...]], axis=1)
    cw = jnp.concatenate([cwg_ref[...], cwv_ref[...]], axis=1).astype(BF16)
    cb = jnp.concatenate([cbg_ref[...], cbv_ref[...]], axis=1).astype(BF16)
    col = lax.broadcasted_iota(jnp.int32, (rows, 2 * FF_TILE), 0) & (GRID_W - 1)
    has_left = col > 0
    has_right = col < GRID_W - 1
    zeros = jnp.zeros((GRID_W, 2 * FF_TILE), BF16)

    def row_mixes(i):
        u = jnp.dot(h2_ref[0, i * rows:(i + 1) * rows, :], w, preferred_element_type=F32)
        u_l = jnp.where(has_left, pltpu.roll(u, 1, 0), 0.0).astype(BF16)
        u_r = jnp.where(has_right, pltpu.roll(u, rows - 1, 0), 0.0).astype(BF16)
        u_c = u.astype(BF16)
        return [cw[3 * kh:3 * kh + 1] * u_l + cw[3 * kh + 1:3 * kh + 2] * u_c
                + cw[3 * kh + 2:3 * kh + 3] * u_r for kh in range(3)]

    def finish(i, prev, cur, nxt):
        above = zeros if prev is None else prev[0][rows - GRID_W:]
        below = zeros if nxt is None else nxt[2][:GRID_W]
        c = (cur[1] + jnp.concatenate([above, cur[0][:rows - GRID_W]], axis=0)
             + jnp.concatenate([cur[2][GRID_W:], below], axis=0) + cb)
        cg = c[:, :FF_TILE]
        act_ref[0, i * rows:(i + 1) * rows, :] = cg * _sigmoid(cg) * c[:, FF_TILE:]

    mixes = [None] * (n_blk + 1)
    for i in range(n_blk):
        mixes[i] = row_mixes(i)
        if i >= 1:
            finish(i - 1, mixes[i - 2] if i >= 2 else None, mixes[i - 1], mixes[i])
    finish(n_blk - 1, mixes[n_blk - 2] if n_blk >= 2 else None, mixes[n_blk - 1], None)


def _ffn_up_call(h2, w_up, conv_w9, conv_b):
    b, s, d = h2.shape
    n_f = D_FF // FF_TILE
    in_specs = [
        pl.BlockSpec((1, s, d), lambda bi, f: (bi, 0, 0)),
        pl.BlockSpec((d, FF_TILE), lambda bi, f: (0, f)),
        pl.BlockSpec((d, FF_TILE), lambda bi, f: (0, n_f + f)),
        pl.BlockSpec((9, FF_TILE), lambda bi, f: (0, f)),
        pl.BlockSpec((9, FF_TILE), lambda bi, f: (0, n_f + f)),
        pl.BlockSpec((1, FF_TILE), lambda bi, f: (0, f)),
        pl.BlockSpec((1, FF_TILE), lambda bi, f: (0, n_f + f)),
    ]
    return pl.pallas_call(
        _ffn_up_kernel,
        grid=(b, n_f),
        in_specs=in_specs,
        out_specs=pl.BlockSpec((1, s, FF_TILE), lambda bi, f: (bi, 0, f)),
        out_shape=jax.ShapeDtypeStruct((b, s, D_FF), BF16),
        compiler_params=pltpu.CompilerParams(
            dimension_semantics=("arbitrary", "arbitrary"), vmem_limit_bytes=VMEM_LIMIT,
            allow_input_fusion=[i in (1, 2) for i in range(len(in_specs))]),
        name="ffn_up",
    )(h2, w_up, w_up, conv_w9, conv_w9, conv_b, conv_b)


def _ffn_down_kernel(act_ref, wd_ref, x1_ref, mod_ref, fg_ref, out_ref):
    y = jnp.dot(act_ref[0], wd_ref[...], preferred_element_type=F32)
    out_ref[0] = _rms(x1_ref[0] + mod_ref[0, 5:6, :] * y, fg_ref[...])


def _ffn_down_call(act, w_down, x1, modx, final_g):
    b, s, d = x1.shape
    tm = DOWN_TILE
    in_specs = [
        pl.BlockSpec((1, tm, D_FF), lambda bi, m: (bi, m, 0)),
        pl.BlockSpec((D_FF, d), lambda bi, m: (0, 0)),
        pl.BlockSpec((1, tm, d), lambda bi, m: (bi, m, 0)),
        pl.BlockSpec((1, 6, d), lambda bi, m: (bi, 0, 0)),
        pl.BlockSpec((1, d), lambda bi, m: (0, 0)),
    ]
    return pl.pallas_call(
        _ffn_down_kernel,
        grid=(b, s // tm),
        in_specs=in_specs,
        out_specs=pl.BlockSpec((1, tm, d), lambda bi, m: (bi, m, 0)),
        out_shape=jax.ShapeDtypeStruct((b, s, d), F32),
        compiler_params=pltpu.CompilerParams(
            dimension_semantics=("arbitrary", "arbitrary"), vmem_limit_bytes=VMEM_LIMIT,
            allow_input_fusion=[i == 1 for i in range(len(in_specs))]),
        name="ffn_down",
    )(act, w_down, x1, modx, final_g)


@functools.lru_cache(maxsize=None)
def _constants(seq):
    gw = FOURIER_W // FOURIER_GROUPS
    nm = np.outer(np.arange(gw), np.arange(gw)) % gw
    ang = 2.0 * np.pi * nm / gw
    cs = np.zeros((FOURIER_W, 2 * FOURIER_W), np.float64)
    for gi in range(FOURIER_GROUPS):
        sl = slice(gi * gw, (gi + 1) * gw)
        cs[sl, sl] = np.cos(ang)
        cs[sl, FOURIER_W + gi * gw:FOURIER_W + (gi + 1) * gw] = np.sin(ang)
    half = seq // 2
    k_idx = np.arange(seq, dtype=np.int64)[:, None]
    ang_c = 2.0 * np.pi * ((k_idx * np.arange(0, half + 1, dtype=np.int64)[None, :]) % seq) / seq
    ang_s = 2.0 * np.pi * ((k_idx * np.arange(1, half, dtype=np.int64)[None, :]) % seq) / seq
    dft = np.concatenate([np.cos(ang_c), -np.sin(ang_s)], axis=1)
    ri = np.arange(REV_TILE)
    rev = (np.arange(2 * REV_TILE)[None, :] == (REV_TILE - ri)[:, None]).astype(np.float64)
    head_id = np.arange(RWKV_W) // HEAD
    ones_blk = (head_id[:, None] == head_id[None, :]).astype(np.float64)
    ti = np.arange(TILE)
    same_chunk = (ti[:, None] // CHUNK) == (ti[None, :] // CHUNK)
    tri = np.stack([same_chunk & (ti[None, :] <= ti[:, None]),
                    same_chunk & (ti[None, :] >= ti[:, None])]).astype(np.float64)
    to_bf16 = lambda arr: np.asarray(arr, dtype=np.float32).astype(BF16)
    return (np.asarray(cs, dtype=np.float32), np.asarray(dft, dtype=np.float32),
            to_bf16(ones_blk), to_bf16(tri), to_bf16(rev))


def _block_diag_dirs(w):
    z = jnp.zeros_like(w[0])
    return jnp.concatenate([jnp.concatenate([w[0], z], axis=1),
                            jnp.concatenate([z, w[1]], axis=1)], axis=0)


def kernel(x, c, ctx, c_ctx, ada_w, ada_b, norm1_g, norm2_g, w_in, rwkv_conv_w, decay_w0, decay_w2, iclr_a0, iclr_a2, gate_g2, k_k, k_a, r_k, gn_g, gn_b, fourier_g, w_out, ffn_w_up, ffn_conv_w, ffn_conv_b, ffn_w_down, final_g):
    b, s, d = x.shape
    assert ada_w.shape[0] == 1, "single-layer configuration"
    assert (b, s, d) == (c.shape[0], s, D_MODEL) and ctx.shape == (b, CTX_LEN, d)
    assert s % TILE == 0 and CTX_LEN == TILE
    cs, dft, ones_blk, tri, rev = (jnp.asarray(t) for t in _constants(s))
    cs, dft = cs.astype(BF16), dft.astype(BF16)

    cc = jnp.concatenate([c, c_ctx[None, :], jnp.zeros((16 - b - 1, d), F32)], axis=0)
    mod = _mod_call(cc, ada_w[0], ada_b[0][None, :])
    modx = mod[:b].reshape(b, 6, d)
    modc = jnp.broadcast_to(mod[b].reshape(1, 6, d)[:, :2], (b, 2, d))
    modsel = jnp.stack([modc, modx[:, :2]], axis=1)

    row = lambda t: t.reshape(1, -1)
    at, bt, kt, rt, v, plast, bonus, gate, z = _prep_call(
        x, ctx, modsel, row(norm1_g[0]), w_in[0].astype(BF16), rwkv_conv_w[0],
        row(decay_w0[0]), _block_diag_dirs(decay_w2[0]).astype(BF16),
        row(iclr_a0[0]), _block_diag_dirs(iclr_a2[0]).astype(BF16),
        gate_g2[0].astype(BF16), row(k_k[0]), row(k_a[0]), row(r_k[0]), cs, ones_blk, tri)

    yf, yb = _scan_call(at, bt, kt, rt, v, plast, s)

    x1, h2 = _mix_call(dft, rev, z, yf, yb, bonus, gate, x, modx, row(gn_g[0]), row(gn_b[0]),
                       row(fourier_g[0]), w_out[0].astype(BF16), row(norm2_g[0]), ones_blk)

    act = _ffn_up_call(h2, ffn_w_up[0].astype(BF16), ffn_conv_w[0].reshape(9, 2 * D_FF),
                       row(ffn_conv_b[0]))
    return _ffn_down_call(act, ffn_w_down[0].astype(BF16), x1, modx, row(final_g))
```
